```python
import math
import jax
import jax.numpy as jnp
from jax import lax
import numpy as np

D_MODEL = 1024
BATCH = 2
SEQ = 8192
DEPTH = 2

GRID_W = 64
CTX_LEN = 256
EPS = 1e-6
NEG_INF = -1e30
ROPE_THETA = 10000.0
Q_BLOCK = 128

DA_HEADS = 4
DA_HEAD_DIM = 64
DA_QK_WIDTH = DA_HEADS * 2 * DA_HEAD_DIM
DA_V_WIDTH = DA_HEADS * 2 * DA_HEAD_DIM
ROPE_AXIS = DA_HEAD_DIM // 2
ROPE_HALF = ROPE_AXIS // 2

SG_CHUNK = 128
SG_GROUPS = 4
SG_WIDTH = 512
SG_GROUP_DIM = SG_WIDTH // SG_GROUPS

NA_HEADS = 8
NA_HEAD_DIM = 64
NA_WIDTH = NA_HEADS * NA_HEAD_DIM
NA_ROWS_MAX = 8
NA_COLS = 16

N_BRANCH = 3
BRANCH_WIDTH = 512

OFF_KA = 0
OFF_VA = OFF_KA + DA_QK_WIDTH
OFF_KC = OFF_VA + DA_V_WIDTH
OFF_VC = OFF_KC + NA_WIDTH
KV_COLS = OFF_VC + NA_WIDTH
OFF_QA = KV_COLS
OFF_QC = OFF_QA + DA_QK_WIDTH
OFF_ZB = OFF_QC + NA_WIDTH
OFF_GATE = OFF_ZB + 2 * SG_WIDTH
IN_COLS = OFF_GATE + N_BRANCH * D_MODEL

N_GROUPS = 4
EXPERTS_PER_GROUP = 8
N_EXPERTS = N_GROUPS * EXPERTS_PER_GROUP
TOP_K = 2
D_EXPERT = 512
MOE_BLOCK = 128

kernel_name = 'hybrid_diffattn_gmlp_natten_hmoe_dit'


def rms_norm(x, g):
    xf = x.astype(jnp.float32)
    y = xf * lax.rsqrt(jnp.mean(xf * xf, axis=-1, keepdims=True) + EPS)
    return (y * g.astype(jnp.float32)).astype(x.dtype)


def layer_norm(x, g, b):
    xf = x.astype(jnp.float32)
    mu = jnp.mean(xf, axis=-1, keepdims=True)
    var = jnp.mean(jnp.square(xf - mu), axis=-1, keepdims=True)
    y = (xf - mu) * lax.rsqrt(var + EPS)
    return (y * g.astype(jnp.float32) + b.astype(jnp.float32)).astype(x.dtype)


def ada_modulation(cond, w, b):
    m = cond @ w + b
    return jnp.split(m[:, None, :], 6, axis=-1)


def modulate(xn, shift, scale):
    return xn * (1.0 + scale) + shift


def da_qk_heads(t):
    return t.reshape(*t.shape[:-1], DA_HEADS, 2, DA_HEAD_DIM)


def da_v_heads(t):
    return t.reshape(*t.shape[:-1], DA_HEADS, 2 * DA_HEAD_DIM)


def na_heads(t):
    return t.reshape(*t.shape[:-1], NA_HEADS, NA_HEAD_DIM)


def axial_rope_tables(n_tok):
    t = jnp.arange(n_tok, dtype=jnp.int32)
    row = (t // GRID_W).astype(jnp.float32)
    col = (t % GRID_W).astype(jnp.float32)
    inv = ROPE_THETA ** (-jnp.arange(ROPE_HALF, dtype=jnp.float32) / ROPE_HALF)
    ar = row[:, None] * inv
    ac = col[:, None] * inv
    ang = jnp.concatenate([ar, ar, ac, ac], axis=-1)
    return jnp.cos(ang), jnp.sin(ang)


def apply_axial_rope(x, cos, sin):
    x1, x2, x3, x4 = jnp.split(x, 4, axis=-1)
    rot = jnp.concatenate([-x2, x1, -x4, x3], axis=-1)
    cos = cos[None, :, None, None, :].astype(x.dtype)
    sin = sin[None, :, None, None, :].astype(x.dtype)
    return x * cos + rot * sin


def diff_softmax_mix(q, k, v, lam):
    s = jnp.einsum('bqhmd,bkhmd->bhmqk', q, k).astype(jnp.float32) * (DA_HEAD_DIM ** -0.5)
    p = jax.nn.softmax(s, axis=-1)
    w = (p[:, :, 0] - lam * p[:, :, 1]).astype(v.dtype)
    return jnp.einsum('bhqk,bkhe->bqhe', w, v)


def diff_attention_latent(q, k, v, k_ctx, v_ctx, lam):
    b, n = q.shape[:2]
    k_all = jnp.concatenate([k_ctx, k], axis=1)
    v_all = jnp.concatenate([v_ctx, v], axis=1)
    qb = q.reshape(b, n // Q_BLOCK, Q_BLOCK, DA_HEADS, 2, DA_HEAD_DIM).swapaxes(0, 1)
    o = lax.map(lambda qq: diff_softmax_mix(qq, k_all, v_all, lam), qb)
    return o.swapaxes(0, 1).reshape(b, n, DA_HEADS, 2 * DA_HEAD_DIM)


def diff_post(o, g, lam_init):
    y = rms_norm(o, g) * (1.0 - lam_init)
    return y.reshape(*o.shape[:2], DA_V_WIDTH)


def spatial_gating(z, ln_g, ln_b, w_s, b_s):
    z = jax.nn.gelu(z)
    u, vv = jnp.split(z, 2, axis=-1)
    vv = layer_norm(vv, ln_g, ln_b)
    b, n, _ = vv.shape
    vv = vv.reshape(b, n // SG_CHUNK, SG_CHUNK, SG_GROUPS, SG_GROUP_DIM)
    s = jnp.einsum('gpq,bnqgc->bnpgc', w_s, vv) + b_s.T[None, None, :, :, None]
    return u * s.reshape(b, n, SG_WIDTH)


def neighbourhood_attention(q, k, v, k_ctx, v_ctx, rpb, rows):
    b, n, nh, dh = q.shape
    kr = min(NA_ROWS_MAX, rows)
    n_cb = GRID_W // NA_COLS
    half = NA_COLS // 2
    band_w = 2 * NA_COLS
    scale = dh ** -0.5

    def grid_padded(t):
        t = t.reshape(b, rows, GRID_W, nh, dh)
        return jnp.pad(t, ((0, 0), (0, 0), (half, half), (0, 0), (0, 0)))

    kg, vg = grid_padded(k), grid_padded(v)
    qcol = np.arange(GRID_W).reshape(n_cb, NA_COLS)
    kcol = np.arange(n_cb)[:, None] * NA_COLS - half + np.arange(band_w)[None, :]
    cstart = np.clip(qcol - half, 0, GRID_W - NA_COLS)
    col_ok = jnp.asarray((kcol[:, None, :] >= cstart[..., None]) & (kcol[:, None, :] < cstart[..., None] + NA_COLS))
    dc_idx = np.clip(kcol[:, None, :] - qcol[:, :, None] + NA_COLS - 1, 0, 2 * NA_COLS - 2)
    rpb_c = rpb.astype(jnp.float32)[:, :, dc_idx]

    def row_step(args):
        r, qr = args
        rs = jnp.clip(r - kr // 2, 0, rows - kr)

        def band(t):
            t = lax.dynamic_slice_in_dim(t, rs, kr, axis=1).reshape(b, kr, n_cb + 1, NA_COLS, nh, dh)
            return jnp.concatenate([t[:, :, :-1], t[:, :, 1:]], axis=3)

        kb, vb = band(kg), band(vg)
        qb = qr.reshape(b, n_cb, NA_COLS, nh, dh)
        s_win = jnp.einsum('bjqhd,brjkhd->bhjqrk', qb, kb).astype(jnp.float32) * scale
        dr_idx = rs + jnp.arange(kr) - r + NA_ROWS_MAX - 1
        bias = jnp.take(rpb_c, dr_idx, axis=1).transpose(0, 2, 3, 1, 4)
        s_win = jnp.where(col_ok[:, :, None, :], s_win + bias[None], NEG_INF)
        s_ctx = jnp.einsum('bjqhd,bkhd->bhjqk', qb, k_ctx).astype(jnp.float32) * scale
        nw = kr * band_w
        s = jnp.concatenate([s_win.reshape(b, nh, n_cb, NA_COLS, nw), s_ctx], axis=-1)
        p = jax.nn.softmax(s, axis=-1).astype(v.dtype)
        p_win = p[..., :nw].reshape(b, nh, n_cb, NA_COLS, kr, band_w)
        o = (jnp.einsum('bhjqrk,brjkhd->bjqhd', p_win, vb)
             + jnp.einsum('bhjqk,bkhd->bjqhd', p[..., nw:], v_ctx))
        return o.reshape(b, GRID_W, nh, dh)

    q_rows = q.reshape(b, rows, GRID_W, nh, dh).swapaxes(0, 1)
    o = lax.map(row_step, (jnp.arange(rows, dtype=jnp.int32), q_rows))
    return o.swapaxes(0, 1).reshape(b, n, nh * dh)


def ctx_attention(q, k, v):
    s = jnp.einsum('bqhd,bkhd->bhqk', q, k).astype(jnp.float32) * (q.shape[-1] ** -0.5)
    p = jax.nn.softmax(s, axis=-1).astype(v.dtype)
    o = jnp.einsum('bhqk,bkhd->bqhd', p, v)
    return o.reshape(*o.shape[:2], -1)


def merge_branches(p_gate, y_a, y_b, y_c, w_branch, w_out):
    ys = jnp.stack([y_a, y_b, y_c], axis=2)
    yb = jnp.einsum('bnie,ied->bnid', ys, w_branch)
    g = jax.nn.sigmoid(p_gate.reshape(*p_gate.shape[:-1], N_BRANCH, D_MODEL))
    return jnp.sum(g * yb, axis=2) @ w_out


def hier_moe(x, w_group, b_group, w_router, b_router, w_gate, w_up, w_down):
    t = x.shape[0]
    g_logits = (x @ w_group).astype(jnp.float32) + b_group.astype(jnp.float32)
    g_prob = jax.nn.softmax(g_logits, axis=-1)
    g_idx = jnp.argmax(g_logits, axis=-1)
    g_w = jnp.take_along_axis(g_prob, g_idx[:, None], axis=1)
    e_logits = ((x @ w_router).astype(jnp.float32) + b_router.astype(jnp.float32)).reshape(t, N_GROUPS, EXPERTS_PER_GROUP)
    e_logits = e_logits[jnp.arange(t), g_idx]
    top_l, top_i = lax.top_k(e_logits, TOP_K)
    top_w = jax.nn.softmax(top_l, axis=-1) * g_w
    expert = (g_idx[:, None] * EXPERTS_PER_GROUP + top_i).astype(jnp.int32)

    a = t * TOP_K
    flat_e = expert.reshape(a)
    order = jnp.argsort(flat_e).astype(jnp.int32)
    sorted_e = flat_e[order]
    counts = jnp.bincount(flat_e, length=N_EXPERTS)
    padded = (counts + MOE_BLOCK - 1) // MOE_BLOCK * MOE_BLOCK
    pad_end = jnp.cumsum(padded)
    pad_start = pad_end - padded
    start = jnp.cumsum(counts) - counts
    dest = pad_start[sorted_e] + jnp.arange(a, dtype=jnp.int32) - start[sorted_e]
    n_blocks = -(-(a + N_EXPERTS * (MOE_BLOCK - 1)) // MOE_BLOCK)
    n_slots = n_blocks * MOE_BLOCK
    slot_token = jnp.full((n_slots,), t, jnp.int32).at[dest].set(order // TOP_K)
    x_pad = jnp.concatenate([x, jnp.zeros((1, x.shape[1]), x.dtype)], axis=0)
    xs = x_pad[slot_token].reshape(n_blocks, MOE_BLOCK, x.shape[1])
    block_expert = jnp.minimum(jnp.searchsorted(pad_end, jnp.arange(n_blocks) * MOE_BLOCK, side='right'), N_EXPERTS - 1)

    def expert_block(args):
        xb, e = args
        hdn = jax.nn.silu(xb @ w_gate[e]) * (xb @ w_up[e])
        return hdn @ w_down[e]

    ys = lax.map(expert_block, (xs, block_expert)).reshape(n_slots, -1)
    w_sorted = top_w.reshape(a)[order].astype(ys.dtype)
    return jax.ops.segment_sum(ys[dest] * w_sorted[:, None], order // TOP_K, num_segments=t)


def setup_inputs(seed: int = 0) -> dict:
    key = jax.random.key(seed)
    ks = jax.random.split(key, 32)
    f32 = jnp.float32
    D = D_MODEL

    def nrm(k, shape, scale):
        return jax.random.normal(k, shape, f32) * scale

    return {
        'x': nrm(ks[0], (BATCH, SEQ, D), 1.0),
        'c': nrm(ks[1], (BATCH, D), 1.0),
        'ctx': nrm(ks[2], (BATCH, CTX_LEN, D), 1.0),
        'c_ctx': nrm(ks[3], (D,), 1.0),
        'w_ada': nrm(ks[4], (DEPTH, D, 6 * D), 0.5 * D ** -0.5),
        'b_ada': nrm(ks[5], (DEPTH, 6 * D), 0.02),
        'g_norm_mix': 1.0 + nrm(ks[6], (DEPTH, D), 0.02),
        'g_norm_ffn': 1.0 + nrm(ks[7], (DEPTH, D), 0.02),
        'w_in': nrm(ks[8], (DEPTH, D, IN_COLS), D ** -0.5),
        'da_lambda': nrm(ks[9], (DEPTH, 4, DA_HEAD_DIM), 0.1),
        'da_subln_g': 1.0 + nrm(ks[10], (DEPTH, 2 * DA_HEAD_DIM), 0.02),
        'sg_ln_g': 1.0 + nrm(ks[11], (DEPTH, SG_WIDTH), 0.02),
        'sg_ln_b': nrm(ks[12], (DEPTH, SG_WIDTH), 0.02),
        'sg_w': nrm(ks[13], (DEPTH, SG_GROUPS, SG_CHUNK, SG_CHUNK), SG_CHUNK ** -0.5),
        'sg_b': 1.0 + nrm(ks[14], (DEPTH, SG_GROUPS, SG_CHUNK), 0.02),
        'na_rpb': nrm(ks[15], (DEPTH, NA_HEADS, 2 * NA_ROWS_MAX - 1, 2 * NA_COLS - 1), 0.1),
        'w_branch': nrm(ks[16], (DEPTH, N_BRANCH, BRANCH_WIDTH, D), BRANCH_WIDTH ** -0.5),
        'w_out': nrm(ks[17], (DEPTH, D, D), D ** -0.5),
        'moe_w_group': nrm(ks[18], (DEPTH, D, N_GROUPS), D ** -0.5),
        'moe_b_group': nrm(ks[19], (DEPTH, N_GROUPS), 0.01),
        'moe_w_router': nrm(ks[20], (DEPTH, D, N_EXPERTS), D ** -0.5),
        'moe_b_router': nrm(ks[21], (DEPTH, N_EXPERTS), 0.01),
        'moe_w_gate': nrm(ks[22], (DEPTH, N_EXPERTS, D, D_EXPERT), D ** -0.5),
        'moe_w_up': nrm(ks[23], (DEPTH, N_EXPERTS, D, D_EXPERT), D ** -0.5),
        'moe_w_down': nrm(ks[24], (DEPTH, N_EXPERTS, D_EXPERT, D), D_EXPERT ** -0.5),
        'g_final': 1.0 + nrm(ks[25], (D,), 0.02),
    }


def reference(x, c, ctx, c_ctx, w_ada, b_ada, g_norm_mix, g_norm_ffn, w_in, da_lambda, da_subln_g,
              sg_ln_g, sg_ln_b, sg_w, sg_b, na_rpb, w_branch, w_out, moe_w_group, moe_b_group,
              moe_w_router, moe_b_router, moe_w_gate, moe_w_up, moe_w_down, g_final):
    b, n_lat, _ = x.shape
    n_ctx = ctx.shape[1]
    rows = n_lat // GRID_W
    cos, sin = axial_rope_tables(n_lat)
    cond_lat = jax.nn.silu(c)
    cond_ctx = jax.nn.silu(c_ctx)[None]
    h, hc = x, ctx
    for l in range(DEPTH):
        last = l == DEPTH - 1
        lam_init = 0.8 - 0.6 * math.exp(-0.3 * l)
        lq1, lk1, lq2, lk2 = da_lambda[l].astype(jnp.float32)
        lam = jnp.exp(jnp.sum(lq1 * lk1)) - jnp.exp(jnp.sum(lq2 * lk2)) + lam_init
        sh1, sc1, gt1, sh2, sc2, gt2 = ada_modulation(cond_lat, w_ada[l], b_ada[l])
        csh1, csc1, cgt1, csh2, csc2, cgt2 = ada_modulation(cond_ctx, w_ada[l], b_ada[l])

        xn = modulate(rms_norm(h, g_norm_mix[l]), sh1, sc1)
        xc = modulate(rms_norm(hc, g_norm_mix[l]), csh1, csc1)
        p = xn @ w_in[l]
        pc = xc @ (w_in[l][:, :KV_COLS] if last else w_in[l])
        ka_c = da_qk_heads(pc[..., OFF_KA:OFF_VA])
        va_c = da_v_heads(pc[..., OFF_VA:OFF_KC])
        kc_c = na_heads(pc[..., OFF_KC:OFF_VC])
        vc_c = na_heads(pc[..., OFF_VC:KV_COLS])

        qa = apply_axial_rope(da_qk_heads(p[..., OFF_QA:OFF_QC]), cos, sin)
        ka = apply_axial_rope(da_qk_heads(p[..., OFF_KA:OFF_VA]), cos, sin)
        va = da_v_heads(p[..., OFF_VA:OFF_KC])
        y_a = diff_post(diff_attention_latent(qa, ka, va, ka_c, va_c, lam), da_subln_g[l], lam_init)
        y_b = spatial_gating(p[..., OFF_ZB:OFF_GATE], sg_ln_g[l], sg_ln_b[l], sg_w[l], sg_b[l])
        y_c = neighbourhood_attention(na_heads(p[..., OFF_QC:OFF_ZB]), na_heads(p[..., OFF_KC:OFF_VC]),
                                      na_heads(p[..., OFF_VC:KV_COLS]), kc_c, vc_c, na_rpb[l], rows)
        h = h + gt1 * merge_branches(p[..., OFF_GATE:], y_a, y_b, y_c, w_branch[l], w_out[l])
        if not last:
            ya_c = diff_post(diff_softmax_mix(da_qk_heads(pc[..., OFF_QA:OFF_QC]), ka_c, va_c, lam), da_subln_g[l], lam_init)
            yb_c = spatial_gating(pc[..., OFF_ZB:OFF_GATE], sg_ln_g[l], sg_ln_b[l], sg_w[l], sg_b[l])
            yc_c = ctx_attention(na_heads(pc[..., OFF_QC:OFF_ZB]), kc_c, vc_c)
            hc = hc + cgt1 * merge_branches(pc[..., OFF_GATE:], ya_c, yb_c, yc_c, w_branch[l], w_out[l])

        xn2 = modulate(rms_norm(h, g_norm_ffn[l]), sh2, sc2).reshape(b * n_lat, D_MODEL)
        if last:
            y = hier_moe(xn2, moe_w_group[l], moe_b_group[l], moe_w_router[l], moe_b_router[l],
                         moe_w_gate[l], moe_w_up[l], moe_w_down[l])
            h = h + gt2 * y.reshape(b, n_lat, D_MODEL)
        else:
            xc2 = modulate(rms_norm(hc, g_norm_ffn[l]), csh2, csc2).reshape(b * n_ctx, D_MODEL)
            y = hier_moe(jnp.concatenate([xn2, xc2], axis=0), moe_w_group[l], moe_b_group[l],
                         moe_w_router[l], moe_b_router[l], moe_w_gate[l], moe_w_up[l], moe_w_down[l])
            h = h + gt2 * y[:b * n_lat].reshape(b, n_lat, D_MODEL)
            hc = hc + cgt2 * y[b * n_lat:].reshape(b, n_ctx, D_MODEL)
    return rms_norm(h, g_final)
```

```python
import functools
import math

import numpy as np
import jax
import jax.numpy as jnp
from jax import lax
from jax.experimental import pallas as pl
from jax.experimental.pallas import tpu as pltpu

F32 = jnp.float32
BF16 = jnp.bfloat16

D_MODEL = 1024
GRID_W = 64
EPS = 1e-6
NEG_INF = -1e30
ROPE_THETA = 10000.0

DA_HEADS = 4
DA_HEAD_DIM = 64
NA_HEADS = 8
NA_ROWS = 8
NA_COLS = 16
SG_CHUNK = 128
SG_GROUPS = 4
SG_WIDTH = 512
BRANCH_WIDTH = 512
N_BRANCH = 3

OFF_KA = 0
OFF_VA = 512
OFF_KC = 1024
OFF_VC = 1536
KV_COLS = 2048
OFF_QA = 2048
OFF_QC = 2560
OFF_ZB = 3072
OFF_GATE = 4096
IN_COLS = 7168

N_GROUPS = 4
EXPERTS_PER_GROUP = 8
N_EXPERTS = 32
D_EXPERT = 512

LANES = 128
HEAD_PAIR = LANES
VMEM_LIMIT = 56 * 1024 * 1024

MOE_ROWS = 256
DMA_CHUNK = 256

_CONTRACT_LAST = (((1,), (1,)), ((), ()))


def _params(sem, vmem=VMEM_LIMIT):
    return pltpu.CompilerParams(dimension_semantics=sem, vmem_limit_bytes=vmem)


def _ada_kernel(cond_ref, w_ref, b_ref, o_ref):
    c = cond_ref[...]
    c = c * jax.nn.sigmoid(c)
    o_ref[0] = jnp.dot(c, w_ref[0], preferred_element_type=F32, precision=lax.Precision.HIGHEST) + b_ref[0]


def _ada(cond, w_ada, b_ada):
    n_layers, d, d6 = w_ada.shape
    tn = 1024
    return pl.pallas_call(
        _ada_kernel,
        grid=(n_layers, d6 // tn),
        in_specs=[
            pl.BlockSpec((8, d), lambda l, j: (0, 0)),
            pl.BlockSpec((1, d, tn), lambda l, j: (l, 0, j)),
            pl.BlockSpec((1, 1, tn), lambda l, j: (l, 0, j)),
        ],
        out_specs=pl.BlockSpec((1, 8, tn), lambda l, j: (l, 0, j)),
        out_shape=jax.ShapeDtypeStruct((n_layers, 8, d6), F32),
        compiler_params=_params(("parallel", "parallel")),
        name="ada_mod",
    )(cond, w_ada, b_ada)


def _rms_mod(x, g, shift, scale):
    y = x * lax.rsqrt(jnp.mean(x * x, axis=-1, keepdims=True) + EPS)
    return (y * g) * (1.0 + scale) + shift


def _norm_proj_kernel(h_ref, g_ref, sh_ref, sc_ref, w_ref, o_ref, xn_ref):
    @pl.when(pl.program_id(2) == 0)
    def _():
        xn_ref[...] = _rms_mod(h_ref[0], g_ref[...], sh_ref[0], sc_ref[0]).astype(BF16)

    o_ref[0] = jnp.dot(xn_ref[...], w_ref[...], preferred_element_type=F32).astype(o_ref.dtype)


def _norm_proj(h, g, shift, scale, w, n_cols, tm):
    b, n, d = h.shape
    tn = 1024
    return pl.pallas_call(
        _norm_proj_kernel,
        grid=(b, n // tm, n_cols // tn),
        in_specs=[
            pl.BlockSpec((1, tm, d), lambda bi, i, j: (bi, i, 0)),
            pl.BlockSpec((1, d), lambda bi, i, j: (0, 0)),
            pl.BlockSpec((1, 1, d), lambda bi, i, j: (bi, 0, 0)),
            pl.BlockSpec((1, 1, d), lambda bi, i, j: (bi, 0, 0)),
            pl.BlockSpec((d, tn), lambda bi, i, j: (0, j)),
        ],
        out_specs=pl.BlockSpec((1, tm, tn), lambda bi, i, j: (bi, i, j)),
        out_shape=jax.ShapeDtypeStruct((b, n, n_cols), BF16),
        scratch_shapes=[pltpu.VMEM((tm, d), BF16)],
        compiler_params=_params(("parallel", "parallel", "arbitrary")),
        name="norm_proj",
    )(h, g, shift, scale, w)


def _rope_tables(n_tok):
    t = jnp.arange(n_tok, dtype=jnp.int32)
    row = (t // GRID_W).astype(F32)
    col = (t % GRID_W).astype(F32)
    half = DA_HEAD_DIM // 4
    inv = ROPE_THETA ** (-jnp.arange(half, dtype=F32) / half)
    ar = row[:, None] * inv
    ac = col[:, None] * inv
    ang = jnp.concatenate([ar, ar, ac, ac], axis=-1)
    sign = np.tile(np.concatenate([-np.ones(half), np.ones(half)]), 2).astype(np.float32)
    cos = jnp.cos(ang)
    sin = jnp.sin(ang) * sign
    return jnp.concatenate([cos, cos], axis=-1), jnp.concatenate([sin, sin], axis=-1)


def _rope_kernel(q_ref, k_ref, v_ref, cos_ref, sin_ref, qo_ref, ko_ref, vo_ref):
    cos = cos_ref[...]
    sin = sin_ref[...]
    lane = lax.broadcasted_iota(jnp.int32, cos.shape, 1)
    first = (lane % (DA_HEAD_DIM // 2)) < (DA_HEAD_DIM // 4)
    seg = DA_HEAD_DIM // 4

    def rope(x):
        partner = jnp.where(first, pltpu.roll(x, LANES - seg, 1), pltpu.roll(x, seg, 1))
        return x * cos + partner * sin

    for hd in range(DA_HEADS):
        sl = slice(hd * HEAD_PAIR, (hd + 1) * HEAD_PAIR)
        q = q_ref[0, :, sl].astype(F32)
        k = k_ref[0, :, sl].astype(F32)
        qo_ref[0, hd] = (rope(q) * (DA_HEAD_DIM ** -0.5)).astype(BF16)
        ko_ref[0, hd] = rope(k).astype(BF16)
        vo_ref[0, hd] = v_ref[0, :, sl]


def _rope(p, cos, sin, tm):
    b, n, _ = p.shape
    w = DA_HEADS * HEAD_PAIR
    hm = jax.ShapeDtypeStruct((b, DA_HEADS, n, HEAD_PAIR), BF16)
    hm_spec = pl.BlockSpec((1, DA_HEADS, tm, HEAD_PAIR), lambda bi, i: (bi, 0, i, 0))
    return pl.pallas_call(
        _rope_kernel,
        grid=(b, n // tm),
        in_specs=[
            pl.BlockSpec((1, tm, w), lambda bi, i: (bi, i, OFF_QA // w)),
            pl.BlockSpec((1, tm, w), lambda bi, i: (bi, i, OFF_KA // w)),
            pl.BlockSpec((1, tm, w), lambda bi, i: (bi, i, OFF_VA // w)),
            pl.BlockSpec((tm, LANES), lambda bi, i: (i, 0)),
            pl.BlockSpec((tm, LANES), lambda bi, i: (i, 0)),
        ],
        out_specs=[hm_spec, hm_spec, hm_spec],
        out_shape=[hm, hm, hm],
        compiler_params=_params(("parallel", "parallel")),
        name="rope_heads",
    )(p, p, p, cos, sin)


def _heads_major(t):
    b, n, _ = t.shape
    return t.reshape(b, n, DA_HEADS, HEAD_PAIR).transpose(0, 2, 1, 3)


def _split_pair(q):
    lo = lax.broadcasted_iota(jnp.int32, q.shape, 1) < (HEAD_PAIR // 2)
    zero = jnp.zeros_like(q)
    return jnp.concatenate([jnp.where(lo, q, zero), jnp.where(lo, zero, q)], axis=0)


def _diff_attn_kernel(lam_ref, g_ref, q_ref, k_ref, v_ref, o_ref, qs_ref, m_ref, l_ref, acc_ref, *, tq, lam_init):
    ki = pl.program_id(3)

    @pl.when(ki == 0)
    def _():
        qs_ref[...] = _split_pair(q_ref[0, 0])
        m_ref[...] = jnp.full(m_ref.shape, NEG_INF, F32)
        l_ref[...] = jnp.zeros(l_ref.shape, F32)
        acc_ref[...] = jnp.zeros(acc_ref.shape, F32)

    s = lax.dot_general(qs_ref[...], k_ref[0, 0], _CONTRACT_LAST, preferred_element_type=F32)
    m_prev = m_ref[...]
    m_new = jnp.maximum(m_prev, jnp.max(s, axis=-1, keepdims=True))
    alpha = jnp.exp(m_prev - m_new)
    p = jnp.exp(s - m_new)
    l_ref[...] = alpha * l_ref[...] + jnp.sum(p, axis=-1, keepdims=True)
    acc_ref[...] = alpha * acc_ref[...] + jnp.dot(p.astype(BF16), v_ref[0, 0], preferred_element_type=F32)
    m_ref[...] = m_new

    @pl.when(ki == pl.num_programs(3) - 1)
    def _():
        o = acc_ref[...] / l_ref[...]
        lm = lam_ref[...]
        lam = (jnp.exp(jnp.sum(lm[0:1] * lm[1:2], axis=-1, keepdims=True))
               - jnp.exp(jnp.sum(lm[2:3] * lm[3:4], axis=-1, keepdims=True)) + lam_init)
        d = o[:tq] - lam * o[tq:]
        y = d * lax.rsqrt(jnp.mean(d * d, axis=-1, keepdims=True) + EPS)
        o_ref[0] = ((y * g_ref[...]) * (1.0 - lam_init)).astype(o_ref.dtype)


def _pick(n, options):
    for o in options:
        if n % o == 0:
            return o
    raise ValueError(f"no tile in {options} divides {n}")


def _diff_attn(q, k, v, lam_params, g, lam_init):
    b, nh, nq, _ = q.shape
    nk = k.shape[2]
    tq = _pick(nq, (512, 256))
    tk = _pick(nk, (768, 512, 256))
    kern = functools.partial(_diff_attn_kernel, tq=tq, lam_init=lam_init)
    return pl.pallas_call(
        kern,
        grid=(b, nh, nq // tq, nk // tk),
        in_specs=[
            pl.BlockSpec((4, DA_HEAD_DIM), lambda bi, h, i, j: (0, 0)),
            pl.BlockSpec((1, HEAD_PAIR), lambda bi, h, i, j: (0, 0)),
            pl.BlockSpec((1, 1, tq, HEAD_PAIR), lambda bi, h, i, j: (bi, h, i, 0)),
            pl.BlockSpec((1, 1, tk, HEAD_PAIR), lambda bi, h, i, j: (bi, h, j, 0)),
            pl.BlockSpec((1, 1, tk, HEAD_PAIR), lambda bi, h, i, j: (bi, h, j, 0)),
        ],
        out_specs=pl.BlockSpec((1, tq, HEAD_PAIR), lambda bi, h, i, j: (bi, i, h)),
        out_shape=jax.ShapeDtypeStruct((b, nq, nh * HEAD_PAIR), BF16),
        scratch_shapes=[
            pltpu.VMEM((2 * tq, HEAD_PAIR), BF16),
            pltpu.VMEM((2 * tq, 1), F32),
            pltpu.VMEM((2 * tq, 1), F32),
            pltpu.VMEM((2 * tq, HEAD_PAIR), F32),
        ],
        compiler_params=_params(("parallel", "parallel", "parallel", "arbitrary")),
        name="diff_attn",
    )(lam_params, g, q, k, v)


NA_QROWS = 8
NA_QTOK = NA_QROWS * GRID_W
NA_KBLK = 4 * GRID_W
NA_PIECES = 4
NA_WIN = NA_PIECES * NA_KBLK


def _na_bias_tables(rpb, rows):
    n_r = rows // NA_QROWS
    qi = np.arange(NA_QROWS)
    kj = np.arange(NA_PIECES * 4)
    c = np.arange(GRID_W)
    half = NA_COLS // 2
    cstart = np.clip(c - half, 0, GRID_W - NA_COLS)
    col_ok = (c[None, :] >= cstart[:, None]) & (c[None, :] < cstart[:, None] + NA_COLS)
    dc = np.clip(c[None, :] - c[:, None] + NA_COLS - 1, 0, 2 * NA_COLS - 2)
    tabs = []
    for r_grp in (0, min(1, n_r - 1), n_r - 1):
        r = NA_QROWS * r_grp + qi
        rs = np.clip(r - NA_ROWS // 2, 0, rows - NA_ROWS)
        krow = NA_QROWS * r_grp - 4 + kj
        row_ok = ((krow[None, :] >= rs[:, None]) & (krow[None, :] < rs[:, None] + NA_ROWS)
                  & (krow[None, :] >= 0) & (krow[None, :] < rows))
        dr = np.clip(krow[None, :] - r[:, None] + NA_ROWS - 1, 0, 2 * NA_ROWS - 2)
        ok = row_ok[:, None, :, None] & col_ok[None, :, None, :]
        dr_b = np.broadcast_to(dr[:, None, :, None], ok.shape)
        dc_b = np.broadcast_to(dc[None, :, None, :], ok.shape)
        bias = rpb[:, dr_b, dc_b]
        bias = jnp.where(jnp.asarray(ok)[None], bias, NEG_INF)
        tabs.append(bias.reshape(rpb.shape[0], NA_QTOK, NA_WIN))
    return jnp.stack(tabs)


def _pair_softmax_pv(qs, k_list, v_list, bias_list):
    s_list = []
    for k, bias in zip(k_list, bias_list):
        s = lax.dot_general(qs, k, _CONTRACT_LAST, preferred_element_type=F32)
        s_list.append(s if bias is None else s + bias)
    m = functools.reduce(jnp.maximum, [jnp.max(s, axis=-1, keepdims=True) for s in s_list])
    p_list = [jnp.exp(s - m) for s in s_list]
    l = functools.reduce(jnp.add, [jnp.sum(p, axis=-1, keepdims=True) for p in p_list])
    o = functools.reduce(jnp.add, [jnp.dot(p.astype(BF16), v, preferred_element_type=F32)
                                   for p, v in zip(p_list, v_list)])
    o = o / l
    t = qs.shape[0] // 2
    lo = lax.broadcasted_iota(jnp.int32, (t, HEAD_PAIR), 1) < (HEAD_PAIR // 2)
    return jnp.where(lo, o[:t], o[t:])


def _scaled_pair(q, head_dim):
    return _split_pair((q.astype(F32) * (head_dim ** -0.5)).astype(BF16))


def _na_kernel(q_ref, k0, k1, k2, k3, v0, v1, v2, v3, kc_ref, vc_ref, b_ref, o_ref):
    qs = _scaled_pair(q_ref[0], HEAD_PAIR // 2)
    k_win = jnp.concatenate([k0[0], k1[0], k2[0], k3[0]], axis=0)
    v_win = jnp.concatenate([v0[0], v1[0], v2[0], v3[0]], axis=0)
    bias = b_ref[0].reshape(2 * NA_QTOK, NA_WIN)
    o = _pair_softmax_pv(qs, [k_win, kc_ref[0]], [v_win, vc_ref[0]], [bias, None])
    o_ref[0] = o.astype(o_ref.dtype)


def _na_attn(p, pc, bias_tabs):
    b, n, _ = p.shape
    n_ctx = pc.shape[1]
    n_r = n // NA_QTOK
    n_kb = n // NA_KBLK
    n_hp = NA_HEADS // 2

    def case(r):
        return jnp.where(r == 0, 0, jnp.where(r == n_r - 1, 2, 1))

    def kv_spec(off, piece):
        return pl.BlockSpec(
            (1, NA_KBLK, HEAD_PAIR),
            lambda bi, hp, r: (bi, jnp.clip(2 * r - 1 + piece, 0, n_kb - 1), off // HEAD_PAIR + hp))

    in_specs = [pl.BlockSpec((1, NA_QTOK, HEAD_PAIR), lambda bi, hp, r: (bi, r, OFF_QC // HEAD_PAIR + hp))]
    in_specs += [kv_spec(OFF_KC, i) for i in range(NA_PIECES)]
    in_specs += [kv_spec(OFF_VC, i) for i in range(NA_PIECES)]
    in_specs += [
        pl.BlockSpec((1, n_ctx, HEAD_PAIR), lambda bi, hp, r: (bi, 0, OFF_KC // HEAD_PAIR + hp)),
        pl.BlockSpec((1, n_ctx, HEAD_PAIR), lambda bi, hp, r: (bi, 0, OFF_VC // HEAD_PAIR + hp)),
        pl.BlockSpec((1, 2, NA_QTOK, NA_WIN), lambda bi, hp, r: (case(r), hp, 0, 0)),
    ]
    return pl.pallas_call(
        _na_kernel,
        grid=(b, n_hp, n_r),
        in_specs=in_specs,
        out_specs=pl.BlockSpec((1, NA_QTOK, HEAD_PAIR), lambda bi, hp, r: (bi, r, hp)),
        out_shape=jax.ShapeDtypeStruct((b, n, NA_HEADS * HEAD_PAIR // 2), BF16),
        compiler_params=_params(("parallel", "parallel", "arbitrary")),
        name="na_attn",
    )(p, *([p] * (2 * NA_PIECES)), pc, pc, bias_tabs)


def _ctx_mha_kernel(q_ref, k_ref, v_ref, o_ref):
    qs = _scaled_pair(q_ref[0], HEAD_PAIR // 2)
    o_ref[0] = _pair_softmax_pv(qs, [k_ref[0]], [v_ref[0]], [None]).astype(o_ref.dtype)


def _ctx_mha(pc):
    b, n_ctx, _ = pc.shape
    n_hp = NA_HEADS // 2

    def spec(off):
        return pl.BlockSpec((1, n_ctx, HEAD_PAIR), lambda bi, hp: (bi, 0, off // HEAD_PAIR + hp))

    return pl.pallas_call(
        _ctx_mha_kernel,
        grid=(b, n_hp),
        in_specs=[spec(OFF_QC), spec(OFF_KC), spec(OFF_VC)],
        out_specs=pl.BlockSpec((1, n_ctx, HEAD_PAIR), lambda bi, hp: (bi, 0, hp)),
        out_shape=jax.ShapeDtypeStruct((b, n_ctx, NA_HEADS * HEAD_PAIR // 2), BF16),
        compiler_params=_params(("parallel", "parallel")),
        name="ctx_mha",
    )(pc, pc, pc)


def _gelu_tanh(x):
    return 0.5 * x * (1.0 + jnp.tanh(math.sqrt(2.0 / math.pi) * (x + 0.044715 * (x * x * x))))


def _merge_kernel(h_ref, ya_ref, z_ref, yc_ref, g0_ref, g1_ref, g2_ref, wb_ref, wo_ref, sgw_ref, sgb_ref,
                  lng_ref, lnb_ref, gt_ref, gf_ref, sh_ref, sc_ref, ho_ref, xo_ref, *, tm):
    z = _gelu_tanh(z_ref[0].astype(F32))
    u = z[:, :SG_WIDTH]
    vv = z[:, SG_WIDTH:]
    mu = jnp.mean(vv, axis=-1, keepdims=True)
    var = jnp.mean(jnp.square(vv - mu), axis=-1, keepdims=True)
    vv = ((vv - mu) * lax.rsqrt(var + EPS)) * lng_ref[...] + lnb_ref[...]
    vv = vv.astype(BF16)
    gd = SG_WIDTH // SG_GROUPS
    chunks = []
    for c in range(tm // SG_CHUNK):
        rows = slice(c * SG_CHUNK, (c + 1) * SG_CHUNK)
        groups = []
        for g in range(SG_GROUPS):
            s = jnp.dot(sgw_ref[g], vv[rows, g * gd:(g + 1) * gd], preferred_element_type=F32) + sgb_ref[g]
            groups.append(s)
        chunks.append(jnp.concatenate(groups, axis=1))
    y_b = (u * jnp.concatenate(chunks, axis=0)).astype(BF16)

    merged = None
    for y, gate_ref, i in ((ya_ref[0], g0_ref, 0), (y_b, g1_ref, 1), (yc_ref[0], g2_ref, 2)):
        t = jax.nn.sigmoid(gate_ref[0].astype(F32)) * jnp.dot(y, wb_ref[i], preferred_element_type=F32)
        merged = t if merged is None else merged + t
    out = jnp.dot(merged.astype(BF16), wo_ref[...], preferred_element_type=F32)
    h_new = h_ref[0] + gt_ref[0] * out
    ho_ref[0] = h_new
    xo_ref[...] = _rms_mod(h_new, gf_ref[...], sh_ref[0], sc_ref[0])


def _merge(h, y_a, p, y_c, wb, wo, sgw, sgb, lng, lnb, gt1, g_ffn, sh2, sc2, xn2_buf, xn2_rows, row_off, tm):
    b, n, d = h.shape
    n_i = n // tm
    off_blk = row_off // tm
    vec = pl.BlockSpec((1, 1, d), lambda bi, i: (bi, 0, 0))

    def col(width, off):
        return pl.BlockSpec((1, tm, width), lambda bi, i: (bi, i, off // width))

    kern = functools.partial(_merge_kernel, tm=tm)
    args = [h, y_a, p, y_c, p, p, p, wb, wo, sgw, sgb, lng, lnb, gt1, g_ffn, sh2, sc2]
    in_specs = [
        pl.BlockSpec((1, tm, d), lambda bi, i: (bi, i, 0)),
        col(BRANCH_WIDTH, 0),
        col(2 * SG_WIDTH, OFF_ZB),
        col(BRANCH_WIDTH, 0),
        col(d, OFF_GATE), col(d, OFF_GATE + d), col(d, OFF_GATE + 2 * d),
        pl.BlockSpec((N_BRANCH, BRANCH_WIDTH, d), lambda bi, i: (0, 0, 0)),
        pl.BlockSpec((d, d), lambda bi, i: (0, 0)),
        pl.BlockSpec((SG_GROUPS, SG_CHUNK, SG_CHUNK), lambda bi, i: (0, 0, 0)),
        pl.BlockSpec((SG_GROUPS, SG_CHUNK, SG_CHUNK), lambda bi, i: (0, 0, 0)),
        pl.BlockSpec((1, SG_WIDTH), lambda bi, i: (0, 0)),
        pl.BlockSpec((1, SG_WIDTH), lambda bi, i: (0, 0)),
        vec,
        pl.BlockSpec((1, d), lambda bi, i: (0, 0)),
        vec, vec,
    ]
    aliases = {}
    if xn2_buf is not None:
        args.append(xn2_buf)
        in_specs.append(pl.BlockSpec(memory_space=pl.ANY))
        aliases = {len(args) - 1: 1}
    xn2_shape = jax.ShapeDtypeStruct((xn2_rows, d), F32)

    def body(*refs):
        n_in = 17
        kern(*refs[:n_in], *refs[len(refs) - 2:])

    return pl.pallas_call(
        body,
        grid=(b, n_i),
        in_specs=in_specs,
        out_specs=[
            pl.BlockSpec((1, tm, d), lambda bi, i: (bi, i, 0)),
            pl.BlockSpec((tm, d), lambda bi, i: (off_blk + bi * n_i + i, 0)),
        ],
        out_shape=[jax.ShapeDtypeStruct((b, n, d), F32), xn2_shape],
        input_output_aliases=aliases,
        compiler_params=_params(("parallel", "parallel")),
        name="merge_branches",
    )(*args)


R_E1, R_E2, R_W1, R_W2, R_RANK1, R_RANK2 = range(6)


def _router_kernel(x_ref, w_ref, b_ref, o_ref, cnt_ref, run_ref, *, tm):
    @pl.when(pl.program_id(0) == 0)
    def _():
        run_ref[...] = jnp.zeros(run_ref.shape, F32)

    logits = jnp.dot(x_ref[...], w_ref[...], preferred_element_type=F32,
                     precision=lax.Precision.HIGHEST) + b_ref[...]
    lane = lax.broadcasted_iota(jnp.int32, logits.shape, 1)
    lane_f = lane.astype(F32)
    far = jnp.float32(1e9)

    def first_lane(mask):
        return jnp.min(jnp.where(mask, lane_f, far), axis=-1, keepdims=True)

    is_g = lane < N_GROUPS
    gl = jnp.where(is_g, logits, NEG_INF)
    g_max = jnp.max(gl, axis=-1, keepdims=True)
    g_idx = first_lane(is_g & (gl == g_max))
    g_sum = jnp.sum(jnp.where(is_g, jnp.exp(gl - g_max), 0.0), axis=-1, keepdims=True)
    g_w = 1.0 / g_sum

    e_lane = lane - N_GROUPS
    in_grp = (e_lane >= 0) & (e_lane < N_EXPERTS) & \
        (jnp.right_shift(e_lane, 3).astype(F32) == g_idx)
    el = jnp.where(in_grp, logits, NEG_INF)
    l1 = jnp.max(el, axis=-1, keepdims=True)
    i1 = first_lane(in_grp & (el == l1))
    rest = in_grp & (lane_f != i1)
    el2 = jnp.where(rest, logits, NEG_INF)
    l2 = jnp.max(el2, axis=-1, keepdims=True)
    i2 = first_lane(rest & (el2 == l2))
    t = jnp.exp(l2 - l1)
    w1 = g_w / (1.0 + t)
    w2 = g_w * t / (1.0 + t)

    oh1 = lane_f == i1
    oh2 = lane_f == i2
    oh = jnp.where(oh1 | oh2, 1.0, 0.0)
    row = lax.broadcasted_iota(jnp.int32, (tm, tm), 0)
    colm = lax.broadcasted_iota(jnp.int32, (tm, tm), 1)
    before = jnp.where(colm < row, 1.0, 0.0).astype(BF16)
    prior = jnp.dot(before, oh.astype(BF16), preferred_element_type=F32) + run_ref[...]
    rank1 = jnp.sum(jnp.where(oh1, prior, 0.0), axis=-1, keepdims=True)
    rank2 = jnp.sum(jnp.where(oh2, prior, 0.0), axis=-1, keepdims=True)
    run_new = run_ref[...] + jnp.sum(oh, axis=0, keepdims=True)
    run_ref[...] = run_new
    cnt_ref[...] = run_new

    slab = jnp.zeros(logits.shape, F32)
    for ln, val in ((R_E1, i1 - N_GROUPS), (R_E2, i2 - N_GROUPS), (R_W1, w1), (R_W2, w2),
                    (R_RANK1, rank1), (R_RANK2, rank2)):
        slab = jnp.where(lane == ln, val, slab)
    o_ref[...] = slab


def _router(xn2, w_rt, b_rt):
    t_tok, d = xn2.shape
    tm = 512
    kern = functools.partial(_router_kernel, tm=tm)
    return pl.pallas_call(
        kern,
        grid=(t_tok // tm,),
        in_specs=[
            pl.BlockSpec((tm, d), lambda i: (i, 0)),
            pl.BlockSpec((d, LANES), lambda i: (0, 0)),
            pl.BlockSpec((1, LANES), lambda i: (0, 0)),
        ],
        out_specs=[pl.BlockSpec((tm, LANES), lambda i: (i, 0)), pl.BlockSpec((1, LANES), lambda i: (0, 0))],
        out_shape=[jax.ShapeDtypeStruct((t_tok, LANES), F32), jax.ShapeDtypeStruct((1, LANES), F32)],
        scratch_shapes=[pltpu.VMEM((1, LANES), F32)],
        compiler_params=_params(("arbitrary",)),
        name="moe_router",
    )(xn2, w_rt, b_rt)


def _row_copy(src, dst, src_row, dst_row, dst_col, d, sem):
    return pltpu.make_async_copy(src.at[pl.ds(src_row, 1), pl.ds(0, d)],
                                 dst.at[pl.ds(dst_row, 1), pl.ds(dst_col, d)], sem)


def _chunk_wait(src, dst, d, sem):
    pltpu.make_async_copy(src.at[pl.ds(0, 2 * DMA_CHUNK), pl.ds(0, d)],
                          dst.at[pl.ds(0, 2 * DMA_CHUNK), pl.ds(0, d)], sem).wait()


def _dispatch_kernel(dest_ref, x_ref, init_ref, xs_ref, sems, *, d):
    del init_ref
    step = pl.program_id(0)
    slot = step % 2

    def issue(i, carry):
        tok = step * DMA_CHUNK + i
        _row_copy(x_ref, xs_ref, tok, dest_ref[2 * tok], 0, d, sems.at[slot]).start()
        _row_copy(x_ref, xs_ref, tok, dest_ref[2 * tok + 1], 0, d, sems.at[slot]).start()
        return carry

    lax.fori_loop(0, DMA_CHUNK, issue, 0)

    @pl.when(step > 0)
    def _():
        _chunk_wait(x_ref, xs_ref, d, sems.at[1 - slot])

    @pl.when(step == pl.num_programs(0) - 1)
    def _():
        _chunk_wait(x_ref, xs_ref, d, sems.at[slot])


def _dispatch(dest, xn2, n_slots):
    t_tok, d = xn2.shape
    kern = functools.partial(_dispatch_kernel, d=d)
    grid_spec = pltpu.PrefetchScalarGridSpec(
        num_scalar_prefetch=1,
        grid=(t_tok // DMA_CHUNK,),
        in_specs=[pl.BlockSpec(memory_space=pl.ANY), pl.BlockSpec(memory_space=pl.ANY)],
        out_specs=pl.BlockSpec(memory_space=pl.ANY),
        scratch_shapes=[pltpu.SemaphoreType.DMA((2,))],
    )
    return pl.pallas_call(
        kern,
        grid_spec=grid_spec,
        out_shape=jax.ShapeDtypeStruct((n_slots, d), F32),
        input_output_aliases={2: 0},
        compiler_params=_params(("arbitrary",)),
        name="moe_dispatch",
    )(dest, xn2, jnp.zeros((n_slots, d), F32))


def _collect_kernel(dest_ref, ys_ref, g_ref, sems, *, d):
    step = pl.program_id(0)
    slot = step % 2

    def issue(i, carry):
        tok = step * DMA_CHUNK + i
        _row_copy(ys_ref, g_ref, dest_ref[2 * tok], tok, 0, d, sems.at[slot]).start()
        _row_copy(ys_ref, g_ref, dest_ref[2 * tok + 1], tok, d, d, sems.at[slot]).start()
        return carry

    lax.fori_loop(0, DMA_CHUNK, issue, 0)

    @pl.when(step > 0)
    def _():
        _chunk_wait(ys_ref, g_ref, d, sems.at[1 - slot])

    @pl.when(step == pl.num_programs(0) - 1)
    def _():
        _chunk_wait(ys_ref, g_ref, d, sems.at[slot])


def _collect(dest, ys, t_tok):
    d = ys.shape[1]
    kern = functools.partial(_collect_kernel, d=d)
    grid_spec = pltpu.PrefetchScalarGridSpec(
        num_scalar_prefetch=1,
        grid=(t_tok // DMA_CHUNK,),
        in_specs=[pl.BlockSpec(memory_space=pl.ANY)],
        out_specs=pl.BlockSpec(memory_space=pl.ANY),
        scratch_shapes=[pltpu.SemaphoreType.DMA((2,))],
    )
    return pl.pallas_call(
        kern,
        grid_spec=grid_spec,
        out_shape=jax.ShapeDtypeStruct((t_tok, 2 * d), F32),
        compiler_params=_params(("arbitrary",)),
        name="moe_collect",
    )(dest, ys)


def _expert_kernel(be_ref, nu_ref, x_ref, wg_ref, wu_ref, wd_ref, o_ref):
    i = pl.program_id(0)

    @pl.when(i < nu_ref[0])
    def _():
        x = x_ref[...].astype(BF16)
        gate = jnp.dot(x, wg_ref[0].astype(BF16), preferred_element_type=F32)
        up = jnp.dot(x, wu_ref[0].astype(BF16), preferred_element_type=F32)
        hdn = (gate * jax.nn.sigmoid(gate)) * up
        o_ref[...] = jnp.dot(hdn.astype(BF16), wd_ref[0].astype(BF16), preferred_element_type=F32)

    @pl.when(i >= nu_ref[0])
    def _():
        o_ref[...] = jnp.zeros(o_ref.shape, F32)


def _experts(block_expert, n_used, xs, w_gate, w_up, w_down):
    n_slots, d = xs.shape
    n_blocks = n_slots // MOE_ROWS
    de = w_gate.shape[2]
    grid_spec = pltpu.PrefetchScalarGridSpec(
        num_scalar_prefetch=2,
        grid=(n_blocks,),
        in_specs=[
            pl.BlockSpec((MOE_ROWS, d), lambda i, be, nu: (i, 0)),
            pl.BlockSpec((1, d, de), lambda i, be, nu: (be[i], 0, 0)),
            pl.BlockSpec((1, d, de), lambda i, be, nu: (be[i], 0, 0)),
            pl.BlockSpec((1, de, d), lambda i, be, nu: (be[i], 0, 0)),
        ],
        out_specs=pl.BlockSpec((MOE_ROWS, d), lambda i, be, nu: (i, 0)),
    )
    return pl.pallas_call(
        _expert_kernel,
        grid_spec=grid_spec,
        out_shape=jax.ShapeDtypeStruct((n_slots, d), F32),
        compiler_params=_params(("arbitrary",)),
        name="moe_experts",
    )(block_expert, n_used, xs, w_gate, w_up, w_down)


def _residual_kernel(h_ref, g_ref, r_ref, gt_ref, gf_ref, o_ref, *, d, final):
    r = r_ref[...]
    w1 = r[:, R_W1:R_W1 + 1]
    w2 = r[:, R_W2:R_W2 + 1]
    y = g_ref[:, :d] * w1 + g_ref[:, d:] * w2
    h_new = h_ref[0] + gt_ref[0] * y
    if final:
        h_new = (h_new * lax.rsqrt(jnp.mean(h_new * h_new, axis=-1, keepdims=True) + EPS)) * gf_ref[...]
    o_ref[0] = h_new


def _residual(h, g, route, gt2, g_final, row_off, tm, final):
    b, n, d = h.shape
    n_i = n // tm
    off_blk = row_off // tm
    kern = functools.partial(_residual_kernel, d=d, final=final)
    return pl.pallas_call(
        kern,
        grid=(b, n_i),
        in_specs=[
            pl.BlockSpec((1, tm, d), lambda bi, i: (bi, i, 0)),
            pl.BlockSpec((tm, 2 * d), lambda bi, i: (off_blk + bi * n_i + i, 0)),
            pl.BlockSpec((tm, LANES), lambda bi, i: (off_blk + bi * n_i + i, 0)),
            pl.BlockSpec((1, 1, d), lambda bi, i: (bi, 0, 0)),
            pl.BlockSpec((1, d), lambda bi, i: (0, 0)),
        ],
        out_specs=pl.BlockSpec((1, tm, d), lambda bi, i: (bi, i, 0)),
        out_shape=jax.ShapeDtypeStruct((b, n, d), F32),
        compiler_params=_params(("parallel", "parallel")),
        name="moe_residual",
    )(h, g, route, gt2, g_final)


def _moe(xn2, w_rt, b_rt, w_gate, w_up, w_down):
    t_tok = xn2.shape[0]
    route, counts = _router(xn2, w_rt, b_rt)
    cnt = counts[0, N_GROUPS:N_GROUPS + N_EXPERTS].astype(jnp.int32)
    padded = (cnt + MOE_ROWS - 1) // MOE_ROWS * MOE_ROWS
    pad_end = jnp.cumsum(padded)
    pad_start = pad_end - padded
    expert = route[:, R_E1:R_E2 + 1].astype(jnp.int32)
    rank = route[:, R_RANK1:R_RANK2 + 1].astype(jnp.int32)
    dest = (pad_start[expert] + rank).reshape(2 * t_tok)
    n_blocks = -(-(2 * t_tok + N_EXPERTS * (MOE_ROWS - 1)) // MOE_ROWS)
    n_slots = n_blocks * MOE_ROWS
    starts = jnp.arange(n_blocks, dtype=jnp.int32) * MOE_ROWS
    block_expert = jnp.minimum(jnp.sum(starts[:, None] >= pad_end[None, :], axis=1), N_EXPERTS - 1).astype(jnp.int32)
    n_used = (pad_end[-1:] // MOE_ROWS).astype(jnp.int32)

    xs = _dispatch(dest, xn2, n_slots)
    ys = _experts(block_expert, n_used, xs, w_gate, w_up, w_down)
    return route, _collect(dest, ys, t_tok)


def kernel(x, c, ctx, c_ctx, w_ada, b_ada, g_norm_mix, g_norm_ffn, w_in, da_lambda, da_subln_g, sg_ln_g, sg_ln_b, sg_w, sg_b, na_rpb, w_branch, w_out, moe_w_group, moe_b_group, moe_w_router, moe_b_router, moe_w_gate, moe_w_up, moe_w_down, g_final):
    b, n_lat, d = x.shape
    n_ctx = ctx.shape[1]
    depth = w_in.shape[0]
    rows = n_lat // GRID_W
    assert d == D_MODEL and n_lat % NA_QTOK == 0 and rows >= 2 * NA_QROWS and n_ctx % 256 == 0 and b <= 7
    tm_lat = 1024 if n_lat % 1024 == 0 else 512
    tm_mrg = 512
    tm_ctx = 256

    cos, sin = _rope_tables(n_lat)
    cond = jnp.zeros((8, d), F32).at[:b].set(c).at[b].set(c_ctx)
    mods = _ada(cond, w_ada, b_ada.reshape(depth, 1, 6 * d))

    h, hc = x, ctx
    for l in range(depth):
        last = l == depth - 1
        lam_init = 0.8 - 0.6 * math.exp(-0.3 * l)
        m_lat = mods[l, :b].reshape(b, 1, 6, d)
        m_ctx = jnp.broadcast_to(mods[l, b].reshape(1, 1, 6, d), (b, 1, 6, d))
        sh1, sc1, gt1, sh2, sc2, gt2 = (m_lat[:, :, i] for i in range(6))
        csh1, csc1, cgt1, csh2, csc2, cgt2 = (m_ctx[:, :, i] for i in range(6))

        w_in_l = w_in[l].astype(BF16)
        g_mix = g_norm_mix[l].reshape(1, d)
        g_ffn = g_norm_ffn[l].reshape(1, d)
        p = _norm_proj(h, g_mix, sh1, sc1, w_in_l, IN_COLS, tm_lat)
        pc = _norm_proj(hc, g_mix, csh1, csc1, w_in_l, KV_COLS if last else IN_COLS, tm_ctx)

        q_hm, k_hm, v_hm = _rope(p, cos, sin, tm_mrg)
        kc_hm = _heads_major(pc[..., OFF_KA:OFF_VA])
        vc_hm = _heads_major(pc[..., OFF_VA:OFF_KC])
        k_all = jnp.concatenate([kc_hm, k_hm], axis=2)
        v_all = jnp.concatenate([vc_hm, v_hm], axis=2)
        g_sub = da_subln_g[l].reshape(1, 2 * DA_HEAD_DIM)
        y_a = _diff_attn(q_hm, k_all, v_all, da_lambda[l], g_sub, lam_init)

        y_c = _na_attn(p, pc, _na_bias_tables(na_rpb[l], rows))

        wb = w_branch[l].astype(BF16)
        wo = w_out[l].astype(BF16)
        sgw = sg_w[l].astype(BF16)
        sgb = jnp.broadcast_to(sg_b[l][:, :, None], (SG_GROUPS, SG_CHUNK, SG_CHUNK))
        lng = sg_ln_g[l].reshape(1, SG_WIDTH)
        lnb = sg_ln_b[l].reshape(1, SG_WIDTH)
        t_lat = b * n_lat
        t_tok = t_lat if last else t_lat + b * n_ctx
        xn2_buf = None if last else jnp.zeros((t_tok, d), F32)
        h, xn2 = _merge(h, y_a, p, y_c, wb, wo, sgw, sgb, lng, lnb, gt1, g_ffn, sh2, sc2, xn2_buf, t_tok, 0, tm_mrg)
        if not last:
            qc_hm = (_heads_major(pc[..., OFF_QA:OFF_QC]).astype(F32) * (DA_HEAD_DIM ** -0.5)).astype(BF16)
            ya_c = _diff_attn(qc_hm, kc_hm, vc_hm, da_lambda[l], g_sub, lam_init)
            yc_c = _ctx_mha(pc)
            hc, xn2 = _merge(hc, ya_c, pc, yc_c, wb, wo, sgw, sgb, lng, lnb, cgt1, g_ffn, csh2, csc2,
                             xn2, t_tok, t_lat, tm_ctx)

        w_rt = jnp.zeros((d, LANES), F32).at[:, :N_GROUPS].set(moe_w_group[l]) \
            .at[:, N_GROUPS:N_GROUPS + N_EXPERTS].set(moe_w_router[l])
        b_rt = jnp.zeros((1, LANES), F32).at[0, :N_GROUPS].set(moe_b_group[l]) \
            .at[0, N_GROUPS:N_GROUPS + N_EXPERTS].set(moe_b_router[l])
        route, g = _moe(xn2, w_rt, b_rt, moe_w_gate[l], moe_w_up[l], moe_w_down[l])
        h = _residual(h, g, route, gt2, g_final.reshape(1, d), 0, tm_mrg, last)
        if not last:
            hc = _residual(hc, g, route, cgt2, g_final.reshape(1, d), t_lat, tm_ctx, False)
    return h
```

```python
import functools
import math

import numpy as np
import jax
import jax.numpy as jnp
from jax import lax
from jax.experimental import pallas as pl
from jax.experimental.pallas import tpu as pltpu

F32 = jnp.float32
BF16 = jnp.bfloat16

D_MODEL = 1024
GRID_W = 64
EPS = 1e-6
NEG_INF = -1e30
ROPE_THETA = 10000.0

DA_HEADS = 4
DA_HEAD_DIM = 64
NA_HEADS = 8
NA_ROWS = 8
NA_COLS = 16
SG_CHUNK = 128
SG_GROUPS = 4
SG_WIDTH = 512
BRANCH_WIDTH = 512
N_BRANCH = 3

OFF_KA = 0
OFF_VA = 512
OFF_KC = 1024
OFF_VC = 1536
KV_COLS = 2048
OFF_QA = 2048
OFF_QC = 2560
OFF_ZB = 3072
OFF_GATE = 4096
IN_COLS = 7168

N_GROUPS = 4
EXPERTS_PER_GROUP = 8
N_EXPERTS = 32
D_EXPERT = 512

LANES = 128
HEAD_PAIR = LANES
VMEM_LIMIT = 56 * 1024 * 1024

MOE_ROWS = 256
DMA_CHUNK = 512

_CONTRACT_LAST = (((1,), (1,)), ((), ()))


def _params(sem, vmem=VMEM_LIMIT):
    return pltpu.CompilerParams(dimension_semantics=sem, vmem_limit_bytes=vmem)


def _ada_kernel(cond_ref, w_ref, b_ref, o_ref):
    c = cond_ref[...]
    c = c * jax.nn.sigmoid(c)
    o_ref[0] = jnp.dot(c, w_ref[0], preferred_element_type=F32, precision=lax.Precision.HIGHEST) + b_ref[0]


def _ada(cond, w_ada, b_ada):
    n_layers, d, d6 = w_ada.shape
    tn = 1024
    return pl.pallas_call(
        _ada_kernel,
        grid=(n_layers, d6 // tn),
        in_specs=[
            pl.BlockSpec((8, d), lambda l, j: (0, 0)),
            pl.BlockSpec((1, d, tn), lambda l, j: (l, 0, j)),
            pl.BlockSpec((1, 1, tn), lambda l, j: (l, 0, j)),
        ],
        out_specs=pl.BlockSpec((1, 8, tn), lambda l, j: (l, 0, j)),
        out_shape=jax.ShapeDtypeStruct((n_layers, 8, d6), F32),
        compiler_params=_params(("parallel", "parallel")),
        name="ada_mod",
    )(cond, w_ada, b_ada)


def _rms_mod(x, g, shift, scale):
    y = x * lax.rsqrt(jnp.mean(x * x, axis=-1, keepdims=True) + EPS)
    return (y * g) * (1.0 + scale) + shift


def _norm_proj_kernel(h_ref, g_ref, sh_ref, sc_ref, w_ref, o_ref, xn_ref):
    @pl.when(pl.program_id(2) == 0)
    def _():
        xn_ref[...] = _rms_mod(h_ref[0], g_ref[...], sh_ref[0], sc_ref[0]).astype(BF16)

    o_ref[0] = jnp.dot(xn_ref[...], w_ref[...], preferred_element_type=F32).astype(o_ref.dtype)


def _norm_proj(h, g, shift, scale, w, n_cols, tm):
    b, n, d = h.shape
    tn = 1024
    return pl.pallas_call(
        _norm_proj_kernel,
        grid=(b, n // tm, n_cols // tn),
        in_specs=[
            pl.BlockSpec((1, tm, d), lambda bi, i, j: (bi, i, 0)),
            pl.BlockSpec((1, d), lambda bi, i, j: (0, 0)),
            pl.BlockSpec((1, 1, d), lambda bi, i, j: (bi, 0, 0)),
            pl.BlockSpec((1, 1, d), lambda bi, i, j: (bi, 0, 0)),
            pl.BlockSpec((d, tn), lambda bi, i, j: (0, j)),
        ],
        out_specs=pl.BlockSpec((1, tm, tn), lambda bi, i, j: (bi, i, j)),
        out_shape=jax.ShapeDtypeStruct((b, n, n_cols), BF16),
        scratch_shapes=[pltpu.VMEM((tm, d), BF16)],
        compiler_params=_params(("parallel", "parallel", "arbitrary")),
        name="norm_proj",
    )(h, g, shift, scale, w)


def _rope_tables(n_tok):
    t = jnp.arange(n_tok, dtype=jnp.int32)
    row = (t // GRID_W).astype(F32)
    col = (t % GRID_W).astype(F32)
    half = DA_HEAD_DIM // 4
    inv = ROPE_THETA ** (-jnp.arange(half, dtype=F32) / half)
    ar = row[:, None] * inv
    ac = col[:, None] * inv
    ang = jnp.concatenate([ar, ar, ac, ac], axis=-1)
    sign = np.tile(np.concatenate([-np.ones(half), np.ones(half)]), 2).astype(np.float32)
    cos = jnp.cos(ang)
    sin = jnp.sin(ang) * sign
    return jnp.concatenate([cos, cos], axis=-1), jnp.concatenate([sin, sin], axis=-1)


def _rope_kernel(q_ref, k_ref, v_ref, cos_ref, sin_ref, qo_ref, ko_ref, vo_ref):
    cos = cos_ref[...]
    sin = sin_ref[...]
    lane = lax.broadcasted_iota(jnp.int32, cos.shape, 1)
    first = (lane % (DA_HEAD_DIM // 2)) < (DA_HEAD_DIM // 4)
    seg = DA_HEAD_DIM // 4

    def rope(x):
        partner = jnp.where(first, pltpu.roll(x, LANES - seg, 1), pltpu.roll(x, seg, 1))
        return x * cos + partner * sin

    for hd in range(DA_HEADS):
        sl = slice(hd * HEAD_PAIR, (hd + 1) * HEAD_PAIR)
        q = q_ref[0, :, sl].astype(F32)
        k = k_ref[0, :, sl].astype(F32)
        qo_ref[0, hd] = (rope(q) * (DA_HEAD_DIM ** -0.5)).astype(BF16)
        ko_ref[0, hd] = rope(k).astype(BF16)
        vo_ref[0, hd] = v_ref[0, :, sl]


def _rope(p, cos, sin, tm):
    b, n, _ = p.shape
    w = DA_HEADS * HEAD_PAIR
    hm = jax.ShapeDtypeStruct((b, DA_HEADS, n, HEAD_PAIR), BF16)
    hm_spec = pl.BlockSpec((1, DA_HEADS, tm, HEAD_PAIR), lambda bi, i: (bi, 0, i, 0))
    return pl.pallas_call(
        _rope_kernel,
        grid=(b, n // tm),
        in_specs=[
            pl.BlockSpec((1, tm, w), lambda bi, i: (bi, i, OFF_QA // w)),
            pl.BlockSpec((1, tm, w), lambda bi, i: (bi, i, OFF_KA // w)),
            pl.BlockSpec((1, tm, w), lambda bi, i: (bi, i, OFF_VA // w)),
            pl.BlockSpec((tm, LANES), lambda bi, i: (i, 0)),
            pl.BlockSpec((tm, LANES), lambda bi, i: (i, 0)),
        ],
        out_specs=[hm_spec, hm_spec, hm_spec],
        out_shape=[hm, hm, hm],
        compiler_params=_params(("parallel", "parallel")),
        name="rope_heads",
    )(p, p, p, cos, sin)


def _heads_major(t):
    b, n, _ = t.shape
    return t.reshape(b, n, DA_HEADS, HEAD_PAIR).transpose(0, 2, 1, 3)


def _split_pair(q):
    lo = lax.broadcasted_iota(jnp.int32, q.shape, 1) < (HEAD_PAIR // 2)
    zero = jnp.zeros_like(q)
    return jnp.concatenate([jnp.where(lo, q, zero), jnp.where(lo, zero, q)], axis=0)


DA_ROW_GROUP = 256


def _diff_attn_kernel(lam_ref, g_ref, q_ref, k_ref, v_ref, o_ref, qs_ref, m_ref, l_ref, acc_ref, *, tq, lam_init):
    ki = pl.program_id(3)

    @pl.when(ki == 0)
    def _():
        qs_ref[...] = _split_pair(q_ref[0, 0])
        m_ref[...] = jnp.full(m_ref.shape, NEG_INF, F32)
        l_ref[...] = jnp.zeros(l_ref.shape, F32)
        acc_ref[...] = jnp.zeros(acc_ref.shape, F32)

    k = k_ref[0, 0]
    v = v_ref[0, 0]
    for r0 in range(0, 2 * tq, DA_ROW_GROUP):
        rows = slice(r0, r0 + DA_ROW_GROUP)
        s = lax.dot_general(qs_ref[rows], k, _CONTRACT_LAST, preferred_element_type=F32)
        m_prev = m_ref[rows]
        m_new = jnp.maximum(m_prev, jnp.max(s, axis=-1, keepdims=True))
        alpha = jnp.exp(m_prev - m_new)
        p = jnp.exp(s - m_new)
        l_ref[rows] = alpha * l_ref[rows] + jnp.sum(p, axis=-1, keepdims=True)
        acc_ref[rows] = alpha * acc_ref[rows] + jnp.dot(p.astype(BF16), v, preferred_element_type=F32)
        m_ref[rows] = m_new

    @pl.when(ki == pl.num_programs(3) - 1)
    def _():
        o = acc_ref[...] / l_ref[...]
        lm = lam_ref[...]
        lam = (jnp.exp(jnp.sum(lm[0:1] * lm[1:2], axis=-1, keepdims=True))
               - jnp.exp(jnp.sum(lm[2:3] * lm[3:4], axis=-1, keepdims=True)) + lam_init)
        d = o[:tq] - lam * o[tq:]
        y = d * lax.rsqrt(jnp.mean(d * d, axis=-1, keepdims=True) + EPS)
        o_ref[0] = ((y * g_ref[...]) * (1.0 - lam_init)).astype(o_ref.dtype)


def _pick(n, options):
    for o in options:
        if n % o == 0:
            return o
    raise ValueError(f"no tile in {options} divides {n}")


def _diff_attn(q, k, v, lam_params, g, lam_init):
    b, nh, nq, _ = q.shape
    nk = k.shape[2]
    tq = _pick(nq, (512, 256))
    tk = _pick(nk, (768, 512, 256))
    kern = functools.partial(_diff_attn_kernel, tq=tq, lam_init=lam_init)
    return pl.pallas_call(
        kern,
        grid=(b, nh, nq // tq, nk // tk),
        in_specs=[
            pl.BlockSpec((4, DA_HEAD_DIM), lambda bi, h, i, j: (0, 0)),
            pl.BlockSpec((1, HEAD_PAIR), lambda bi, h, i, j: (0, 0)),
            pl.BlockSpec((1, 1, tq, HEAD_PAIR), lambda bi, h, i, j: (bi, h, i, 0)),
            pl.BlockSpec((1, 1, tk, HEAD_PAIR), lambda bi, h, i, j: (bi, h, j, 0)),
            pl.BlockSpec((1, 1, tk, HEAD_PAIR), lambda bi, h, i, j: (bi, h, j, 0)),
        ],
        out_specs=pl.BlockSpec((1, tq, HEAD_PAIR), lambda bi, h, i, j: (bi, i, h)),
        out_shape=jax.ShapeDtypeStruct((b, nq, nh * HEAD_PAIR), BF16),
        scratch_shapes=[
            pltpu.VMEM((2 * tq, HEAD_PAIR), BF16),
            pltpu.VMEM((2 * tq, 1), F32),
            pltpu.VMEM((2 * tq, 1), F32),
            pltpu.VMEM((2 * tq, HEAD_PAIR), F32),
        ],
        compiler_params=_params(("parallel", "parallel", "parallel", "arbitrary")),
        name="diff_attn",
    )(lam_params, g, q, k, v)


NA_QROWS = 8
NA_QTOK = NA_QROWS * GRID_W
NA_KBLK = 4 * GRID_W
NA_PIECES = 4
NA_WIN = NA_PIECES * NA_KBLK


def _na_bias_tables(rpb, rows):
    n_h = rpb.shape[0]
    n_kj = NA_PIECES * 4
    pad_r = n_kj - NA_ROWS
    pad_c = GRID_W - NA_COLS
    rp = jnp.pad(rpb, ((0, 0), (pad_r, pad_r), (pad_c, pad_c)))
    a = jnp.stack([rp[:, :, GRID_W - 1 - qc:2 * GRID_W - 1 - qc] for qc in range(GRID_W)], axis=2)
    bias = jnp.stack([a[:, 3 + pad_r - qi:3 + pad_r - qi + n_kj].transpose(0, 2, 1, 3) for qi in range(NA_QROWS)],
                     axis=1).reshape(n_h, NA_QTOK, NA_WIN)

    n_r = rows // NA_QROWS
    qi = np.arange(NA_QROWS)
    kj = np.arange(n_kj)
    c = np.arange(GRID_W)
    cstart = np.clip(c - NA_COLS // 2, 0, GRID_W - NA_COLS)
    col_ok = (c[None, :] >= cstart[:, None]) & (c[None, :] < cstart[:, None] + NA_COLS)
    row_ok = []
    for r_grp in (0, min(1, n_r - 1), n_r - 1):
        r = NA_QROWS * r_grp + qi
        rs = np.clip(r - NA_ROWS // 2, 0, rows - NA_ROWS)
        krow = NA_QROWS * r_grp - 4 + kj
        row_ok.append((krow[None, :] >= rs[:, None]) & (krow[None, :] < rs[:, None] + NA_ROWS)
                      & (krow[None, :] >= 0) & (krow[None, :] < rows))
    ok = jnp.asarray(np.stack(row_ok))[:, :, None, :, None] & jnp.asarray(col_ok)[None, None, :, None, :]
    mask = jnp.where(ok, 0.0, NEG_INF).astype(F32).reshape(3, NA_QTOK, NA_WIN)
    return bias, mask


def _pair_softmax_pv(qs, k_list, v_list, bias_list):
    s_list = []
    for k, bias in zip(k_list, bias_list):
        s = lax.dot_general(qs, k, _CONTRACT_LAST, preferred_element_type=F32)
        s_list.append(s if bias is None else s + bias)
    m = functools.reduce(jnp.maximum, [jnp.max(s, axis=-1, keepdims=True) for s in s_list])
    p_list = [jnp.exp(s - m) for s in s_list]
    l = functools.reduce(jnp.add, [jnp.sum(p, axis=-1, keepdims=True) for p in p_list])
    o = functools.reduce(jnp.add, [jnp.dot(p.astype(BF16), v, preferred_element_type=F32)
                                   for p, v in zip(p_list, v_list)])
    o = o / l
    t = qs.shape[0] // 2
    lo = lax.broadcasted_iota(jnp.int32, (t, HEAD_PAIR), 1) < (HEAD_PAIR // 2)
    return jnp.where(lo, o[:t], o[t:])


def _scaled_pair(q, head_dim):
    return _split_pair((q.astype(F32) * (head_dim ** -0.5)).astype(BF16))


def _na_kernel(q_ref, k0, k1, k2, k3, v0, v1, v2, v3, kc_ref, vc_ref, b_ref, mk_ref, o_ref):
    qs = _scaled_pair(q_ref[0], HEAD_PAIR // 2)
    k_win = jnp.concatenate([k0[0], k1[0], k2[0], k3[0]], axis=0)
    v_win = jnp.concatenate([v0[0], v1[0], v2[0], v3[0]], axis=0)
    bias = (b_ref[...] + mk_ref[...]).reshape(2 * NA_QTOK, NA_WIN)
    o = _pair_softmax_pv(qs, [k_win, kc_ref[0]], [v_win, vc_ref[0]], [bias, None])
    o_ref[0] = o.astype(o_ref.dtype)


def _na_attn(p, pc, bias, mask):
    b, n, _ = p.shape
    n_ctx = pc.shape[1]
    n_r = n // NA_QTOK
    n_kb = n // NA_KBLK
    n_hp = NA_HEADS // 2

    def case(r):
        return jnp.where(r == 0, 0, jnp.where(r == n_r - 1, 2, 1))

    def kv_spec(off, piece):
        return pl.BlockSpec(
            (1, NA_KBLK, HEAD_PAIR),
            lambda bi, hp, r: (bi, jnp.clip(2 * r - 1 + piece, 0, n_kb - 1), off // HEAD_PAIR + hp))

    in_specs = [pl.BlockSpec((1, NA_QTOK, HEAD_PAIR), lambda bi, hp, r: (bi, r, OFF_QC // HEAD_PAIR + hp))]
    in_specs += [kv_spec(OFF_KC, i) for i in range(NA_PIECES)]
    in_specs += [kv_spec(OFF_VC, i) for i in range(NA_PIECES)]
    in_specs += [
        pl.BlockSpec((1, n_ctx, HEAD_PAIR), lambda bi, hp, r: (bi, 0, OFF_KC // HEAD_PAIR + hp)),
        pl.BlockSpec((1, n_ctx, HEAD_PAIR), lambda bi, hp, r: (bi, 0, OFF_VC // HEAD_PAIR + hp)),
        pl.BlockSpec((2, NA_QTOK, NA_WIN), lambda bi, hp, r: (hp, 0, 0)),
        pl.BlockSpec((1, NA_QTOK, NA_WIN), lambda bi, hp, r: (case(r), 0, 0)),
    ]
    return pl.pallas_call(
        _na_kernel,
        grid=(b, n_hp, n_r),
        in_specs=in_specs,
        out_specs=pl.BlockSpec((1, NA_QTOK, HEAD_PAIR), lambda bi, hp, r: (bi, r, hp)),
        out_shape=jax.ShapeDtypeStruct((b, n, NA_HEADS * HEAD_PAIR // 2), BF16),
        compiler_params=_params(("parallel", "parallel", "arbitrary")),
        name="na_attn",
    )(p, *([p] * (2 * NA_PIECES)), pc, pc, bias, mask)


def _ctx_mha_kernel(q_ref, k_ref, v_ref, o_ref):
    qs = _scaled_pair(q_ref[0], HEAD_PAIR // 2)
    o_ref[0] = _pair_softmax_pv(qs, [k_ref[0]], [v_ref[0]], [None]).astype(o_ref.dtype)


def _ctx_mha(pc):
    b, n_ctx, _ = pc.shape
    n_hp = NA_HEADS // 2

    def spec(off):
        return pl.BlockSpec((1, n_ctx, HEAD_PAIR), lambda bi, hp: (bi, 0, off // HEAD_PAIR + hp))

    return pl.pallas_call(
        _ctx_mha_kernel,
        grid=(b, n_hp),
        in_specs=[spec(OFF_QC), spec(OFF_KC), spec(OFF_VC)],
        out_specs=pl.BlockSpec((1, n_ctx, HEAD_PAIR), lambda bi, hp: (bi, 0, hp)),
        out_shape=jax.ShapeDtypeStruct((b, n_ctx, NA_HEADS * HEAD_PAIR // 2), BF16),
        compiler_params=_params(("parallel", "parallel")),
        name="ctx_mha",
    )(pc, pc, pc)


def _gelu_tanh(x):
    return 0.5 * x * (1.0 + jnp.tanh(math.sqrt(2.0 / math.pi) * (x + 0.044715 * (x * x * x))))


SUBLANES = 8


def _store_token_tiles(ref, x):
    t = x.shape[0]
    for j in range(SUBLANES):
        ref[pl.ds(j, t, stride=SUBLANES), :] = x[:, j * LANES:(j + 1) * LANES]


def _load_token_tiles(ref, t):
    return jnp.concatenate([ref[pl.ds(j, t, stride=SUBLANES), :] for j in range(SUBLANES)], axis=1)


def _merge_kernel(h_ref, ya_ref, z_ref, yc_ref, g0_ref, g1_ref, g2_ref, wb_ref, wo_ref, sgw_ref, sgb_ref,
                  lng_ref, lnb_ref, gt_ref, gf_ref, sh_ref, sc_ref, ho_ref, xo_ref, *, tm):
    z = _gelu_tanh(z_ref[0].astype(F32))
    u = z[:, :SG_WIDTH]
    vv = z[:, SG_WIDTH:]
    mu = jnp.mean(vv, axis=-1, keepdims=True)
    var = jnp.mean(jnp.square(vv - mu), axis=-1, keepdims=True)
    vv = ((vv - mu) * lax.rsqrt(var + EPS)) * lng_ref[...] + lnb_ref[...]
    vv = vv.astype(BF16)
    gd = SG_WIDTH // SG_GROUPS
    chunks = []
    for c in range(tm // SG_CHUNK):
        rows = slice(c * SG_CHUNK, (c + 1) * SG_CHUNK)
        groups = []
        for g in range(SG_GROUPS):
            s = jnp.dot(sgw_ref[g], vv[rows, g * gd:(g + 1) * gd], preferred_element_type=F32) + sgb_ref[g]
            groups.append(s)
        chunks.append(jnp.concatenate(groups, axis=1))
    y_b = (u * jnp.concatenate(chunks, axis=0)).astype(BF16)

    merged = None
    for y, gate_ref, i in ((ya_ref[0], g0_ref, 0), (y_b, g1_ref, 1), (yc_ref[0], g2_ref, 2)):
        t = jax.nn.sigmoid(gate_ref[0].astype(F32)) * jnp.dot(y, wb_ref[i], preferred_element_type=F32)
        merged = t if merged is None else merged + t
    out = jnp.dot(merged.astype(BF16), wo_ref[...], preferred_element_type=F32)
    h_new = h_ref[0] + gt_ref[0] * out
    ho_ref[0] = h_new
    _store_token_tiles(xo_ref, _rms_mod(h_new, gf_ref[...], sh_ref[0], sc_ref[0]))


def _merge(h, y_a, p, y_c, wb, wo, sgw, sgb, lng, lnb, gt1, g_ffn, sh2, sc2, xn2_buf, xn2_rows, row_off, tm):
    b, n, d = h.shape
    n_i = n // tm
    off_blk = row_off // tm
    vec = pl.BlockSpec((1, 1, d), lambda bi, i: (bi, 0, 0))

    def col(width, off):
        return pl.BlockSpec((1, tm, width), lambda bi, i: (bi, i, off // width))

    kern = functools.partial(_merge_kernel, tm=tm)
    args = [h, y_a, p, y_c, p, p, p, wb, wo, sgw, sgb, lng, lnb, gt1, g_ffn, sh2, sc2]
    in_specs = [
        pl.BlockSpec((1, tm, d), lambda bi, i: (bi, i, 0)),
        col(BRANCH_WIDTH, 0),
        col(2 * SG_WIDTH, OFF_ZB),
        col(BRANCH_WIDTH, 0),
        col(d, OFF_GATE), col(d, OFF_GATE + d), col(d, OFF_GATE + 2 * d),
        pl.BlockSpec((N_BRANCH, BRANCH_WIDTH, d), lambda bi, i: (0, 0, 0)),
        pl.BlockSpec((d, d), lambda bi, i: (0, 0)),
        pl.BlockSpec((SG_GROUPS, SG_CHUNK, SG_CHUNK), lambda bi, i: (0, 0, 0)),
        pl.BlockSpec((SG_GROUPS, SG_CHUNK, SG_CHUNK), lambda bi, i: (0, 0, 0)),
        pl.BlockSpec((1, SG_WIDTH), lambda bi, i: (0, 0)),
        pl.BlockSpec((1, SG_WIDTH), lambda bi, i: (0, 0)),
        vec,
        pl.BlockSpec((1, d), lambda bi, i: (0, 0)),
        vec, vec,
    ]
    aliases = {}
    if xn2_buf is not None:
        args.append(xn2_buf)
        in_specs.append(pl.BlockSpec(memory_space=pl.ANY))
        aliases = {len(args) - 1: 1}
    assert d == SUBLANES * LANES
    xn2_shape = jax.ShapeDtypeStruct((xn2_rows * SUBLANES, LANES), F32)

    def body(*refs):
        n_in = 17
        kern(*refs[:n_in], *refs[len(refs) - 2:])

    return pl.pallas_call(
        body,
        grid=(b, n_i),
        in_specs=in_specs,
        out_specs=[
            pl.BlockSpec((1, tm, d), lambda bi, i: (bi, i, 0)),
            pl.BlockSpec((tm * SUBLANES, LANES), lambda bi, i: (off_blk + bi * n_i + i, 0)),
        ],
        out_shape=[jax.ShapeDtypeStruct((b, n, d), F32), xn2_shape],
        input_output_aliases=aliases,
        compiler_params=_params(("parallel", "parallel")),
        name="merge_branches",
    )(*args)


R_E1, R_E2, R_W1, R_W2, R_RANK1, R_RANK2 = range(6)


def _router_kernel(x_ref, w_ref, b_ref, o_ref, cnt_ref, run_ref, *, tm):
    @pl.when(pl.program_id(0) == 0)
    def _():
        run_ref[...] = jnp.zeros(run_ref.shape, F32)

    logits = jnp.dot(_load_token_tiles(x_ref, tm), w_ref[...], preferred_element_type=F32,
                     precision=lax.Precision.HIGHEST) + b_ref[...]
    lane = lax.broadcasted_iota(jnp.int32, logits.shape, 1)
    lane_f = lane.astype(F32)
    far = jnp.float32(1e9)

    def first_lane(mask):
        return jnp.min(jnp.where(mask, lane_f, far), axis=-1, keepdims=True)

    is_g = lane < N_GROUPS
    gl = jnp.where(is_g, logits, NEG_INF)
    g_max = jnp.max(gl, axis=-1, keepdims=True)
    g_idx = first_lane(is_g & (gl == g_max))
    g_sum = jnp.sum(jnp.where(is_g, jnp.exp(gl - g_max), 0.0), axis=-1, keepdims=True)
    g_w = 1.0 / g_sum

    e_lane = lane - N_GROUPS
    in_grp = (e_lane >= 0) & (e_lane < N_EXPERTS) & \
        (jnp.right_shift(e_lane, 3).astype(F32) == g_idx)
    el = jnp.where(in_grp, logits, NEG_INF)
    l1 = jnp.max(el, axis=-1, keepdims=True)
    i1 = first_lane(in_grp & (el == l1))
    rest = in_grp & (lane_f != i1)
    el2 = jnp.where(rest, logits, NEG_INF)
    l2 = jnp.max(el2, axis=-1, keepdims=True)
    i2 = first_lane(rest & (el2 == l2))
    t = jnp.exp(l2 - l1)
    w1 = g_w / (1.0 + t)
    w2 = g_w * t / (1.0 + t)

    oh1 = lane_f == i1
    oh2 = lane_f == i2
    oh = jnp.where(oh1 | oh2, 1.0, 0.0)
    row = lax.broadcasted_iota(jnp.int32, (tm, tm), 0)
    colm = lax.broadcasted_iota(jnp.int32, (tm, tm), 1)
    before = jnp.where(colm < row, 1.0, 0.0).astype(BF16)
    prior = jnp.dot(before, oh.astype(BF16), preferred_element_type=F32) + run_ref[...]
    rank1 = jnp.sum(jnp.where(oh1, prior, 0.0), axis=-1, keepdims=True)
    rank2 = jnp.sum(jnp.where(oh2, prior, 0.0), axis=-1, keepdims=True)
    run_new = run_ref[...] + jnp.sum(oh, axis=0, keepdims=True)
    run_ref[...] = run_new
    cnt_ref[...] = run_new

    slab = jnp.zeros(logits.shape, F32)
    for ln, val in ((R_E1, i1 - N_GROUPS), (R_E2, i2 - N_GROUPS), (R_W1, w1), (R_W2, w2),
                    (R_RANK1, rank1), (R_RANK2, rank2)):
        slab = jnp.where(lane == ln, val, slab)
    o_ref[...] = slab


def _router(xn2, w_rt, b_rt):
    t_tok = xn2.shape[0] // SUBLANES
    d = w_rt.shape[0]
    tm = 512
    kern = functools.partial(_router_kernel, tm=tm)
    return pl.pallas_call(
        kern,
        grid=(t_tok // tm,),
        in_specs=[
            pl.BlockSpec((tm * SUBLANES, LANES), lambda i: (i, 0)),
            pl.BlockSpec((d, LANES), lambda i: (0, 0)),
            pl.BlockSpec((1, LANES), lambda i: (0, 0)),
        ],
        out_specs=[pl.BlockSpec((tm, LANES), lambda i: (i, 0)), pl.BlockSpec((1, LANES), lambda i: (0, 0))],
        out_shape=[jax.ShapeDtypeStruct((t_tok, LANES), F32), jax.ShapeDtypeStruct((1, LANES), F32)],
        scratch_shapes=[pltpu.VMEM((1, LANES), F32)],
        compiler_params=_params(("arbitrary",)),
        name="moe_router",
    )(xn2, w_rt, b_rt)


def _tile_rows(index):
    return pl.ds(pl.multiple_of(index * SUBLANES, SUBLANES), SUBLANES)


def _dispatch_kernel(dest_ref, x_ref, init_ref, xs_ref, sem):
    del init_ref
    step = pl.program_id(0)

    def issue(i, carry):
        tok = step * DMA_CHUNK + i
        for k in range(2):
            pltpu.make_async_copy(x_ref.at[_tile_rows(i), :], xs_ref.at[_tile_rows(dest_ref[2 * tok + k]), :],
                                  sem.at[0]).start()
        return carry

    lax.fori_loop(0, DMA_CHUNK, issue, 0)
    for _ in range(2):
        pltpu.make_async_copy(x_ref, xs_ref.at[pl.ds(0, DMA_CHUNK * SUBLANES), :], sem.at[0]).wait()


def _dispatch(dest, xn2, n_slots):
    t_tok = xn2.shape[0] // SUBLANES
    grid_spec = pltpu.PrefetchScalarGridSpec(
        num_scalar_prefetch=1,
        grid=(t_tok // DMA_CHUNK,),
        in_specs=[pl.BlockSpec((DMA_CHUNK * SUBLANES, LANES), lambda i, dest: (i, 0)),
                  pl.BlockSpec(memory_space=pl.ANY)],
        out_specs=pl.BlockSpec(memory_space=pl.ANY),
        scratch_shapes=[pltpu.SemaphoreType.DMA((1,))],
    )
    return pl.pallas_call(
        _dispatch_kernel,
        grid_spec=grid_spec,
        out_shape=jax.ShapeDtypeStruct((n_slots * SUBLANES, LANES), F32),
        input_output_aliases={2: 0},
        compiler_params=_params(("arbitrary",)),
        name="moe_dispatch",
    )(dest, xn2, jnp.zeros((n_slots * SUBLANES, LANES), F32))


def _expert_kernel(be_ref, nu_ref, x_ref, wg_ref, wu_ref, wd_ref, o_ref):
    i = pl.program_id(0)

    @pl.when(i < nu_ref[0])
    def _():
        x = _load_token_tiles(x_ref, MOE_ROWS).astype(BF16)
        gate = jnp.dot(x, wg_ref[0].astype(BF16), preferred_element_type=F32)
        up = jnp.dot(x, wu_ref[0].astype(BF16), preferred_element_type=F32)
        hdn = (gate * jax.nn.sigmoid(gate)) * up
        _store_token_tiles(o_ref, jnp.dot(hdn.astype(BF16), wd_ref[0].astype(BF16), preferred_element_type=F32))

    @pl.when(i >= nu_ref[0])
    def _():
        o_ref[...] = jnp.zeros(o_ref.shape, F32)


def _experts(block_expert, n_used, xs, w_gate, w_up, w_down):
    n_blocks = xs.shape[0] // (MOE_ROWS * SUBLANES)
    _, d, de = w_gate.shape
    blk = pl.BlockSpec((MOE_ROWS * SUBLANES, LANES), lambda i, be, nu: (i, 0))
    grid_spec = pltpu.PrefetchScalarGridSpec(
        num_scalar_prefetch=2,
        grid=(n_blocks,),
        in_specs=[
            blk,
            pl.BlockSpec((1, d, de), lambda i, be, nu: (be[i], 0, 0)),
            pl.BlockSpec((1, d, de), lambda i, be, nu: (be[i], 0, 0)),
            pl.BlockSpec((1, de, d), lambda i, be, nu: (be[i], 0, 0)),
        ],
        out_specs=blk,
    )
    return pl.pallas_call(
        _expert_kernel,
        grid_spec=grid_spec,
        out_shape=jax.ShapeDtypeStruct(xs.shape, F32),
        compiler_params=_params(("arbitrary",)),
        name="moe_experts",
    )(block_expert, n_used, xs, w_gate, w_up, w_down)


def _residual_kernel(dest_ref, h_ref, r_ref, gt_ref, gf_ref, ys_ref, o_ref, buf_ref, sem, *, tm, n_i, row_off, final):
    base = row_off + (pl.program_id(0) * n_i + pl.program_id(1)) * tm

    def issue(t, carry):
        for k in range(2):
            pltpu.make_async_copy(ys_ref.at[_tile_rows(dest_ref[2 * (base + t) + k]), :],
                                  buf_ref.at[k, _tile_rows(t), :], sem.at[0]).start()
        return carry

    lax.fori_loop(0, tm, issue, 0)
    for k in range(2):
        pltpu.make_async_copy(ys_ref.at[pl.ds(0, tm * SUBLANES), :], buf_ref.at[k], sem.at[0]).wait()

    r = r_ref[...]
    y = (_load_token_tiles(buf_ref.at[0], tm) * r[:, R_W1:R_W1 + 1]
         + _load_token_tiles(buf_ref.at[1], tm) * r[:, R_W2:R_W2 + 1])
    h_new = h_ref[0] + gt_ref[0] * y
    if final:
        h_new = (h_new * lax.rsqrt(jnp.mean(h_new * h_new, axis=-1, keepdims=True) + EPS)) * gf_ref[...]
    o_ref[0] = h_new


def _residual(dest, h, ys, route, gt2, g_final, row_off, final):
    b, n, d = h.shape
    tm = 256
    n_i = n // tm
    off_blk = row_off // tm
    kern = functools.partial(_residual_kernel, tm=tm, n_i=n_i, row_off=row_off, final=final)
    grid_spec = pltpu.PrefetchScalarGridSpec(
        num_scalar_prefetch=1,
        grid=(b, n_i),
        in_specs=[
            pl.BlockSpec((1, tm, d), lambda bi, i, dest: (bi, i, 0)),
            pl.BlockSpec((tm, LANES), lambda bi, i, dest: (off_blk + bi * n_i + i, 0)),
            pl.BlockSpec((1, 1, d), lambda bi, i, dest: (bi, 0, 0)),
            pl.BlockSpec((1, d), lambda bi, i, dest: (0, 0)),
            pl.BlockSpec(memory_space=pl.ANY),
        ],
        out_specs=pl.BlockSpec((1, tm, d), lambda bi, i, dest: (bi, i, 0)),
        scratch_shapes=[pltpu.VMEM((2, tm * SUBLANES, LANES), F32), pltpu.SemaphoreType.DMA((1,))],
    )
    return pl.pallas_call(
        kern,
        grid_spec=grid_spec,
        out_shape=jax.ShapeDtypeStruct((b, n, d), F32),
        compiler_params=_params(("arbitrary", "arbitrary")),
        name="moe_residual",
    )(dest, h, route, gt2, g_final, ys)


def _moe(xn2, w_rt, b_rt, w_gate, w_up, w_down):
    t_tok = xn2.shape[0] // SUBLANES
    route, counts = _router(xn2, w_rt, b_rt)
    cnt = counts[0, N_GROUPS:N_GROUPS + N_EXPERTS].astype(jnp.int32)
    padded = (cnt + MOE_ROWS - 1) // MOE_ROWS * MOE_ROWS
    pad_end = jnp.cumsum(padded)
    pad_start = pad_end - padded
    expert = route[:, R_E1:R_E2 + 1].astype(jnp.int32)
    rank = route[:, R_RANK1:R_RANK2 + 1].astype(jnp.int32)
    dest = (pad_start[expert] + rank).reshape(2 * t_tok)
    n_blocks = -(-(2 * t_tok + N_EXPERTS * (MOE_ROWS - 1)) // MOE_ROWS)
    n_slots = n_blocks * MOE_ROWS
    starts = jnp.arange(n_blocks, dtype=jnp.int32) * MOE_ROWS
    block_expert = jnp.minimum(jnp.sum(starts[:, None] >= pad_end[None, :], axis=1), N_EXPERTS - 1).astype(jnp.int32)
    n_used = (pad_end[-1:] // MOE_ROWS).astype(jnp.int32)

    xs = _dispatch(dest, xn2, n_slots)
    return route, dest, _experts(block_expert, n_used, xs, w_gate, w_up, w_down)


def kernel(x, c, ctx, c_ctx, w_ada, b_ada, g_norm_mix, g_norm_ffn, w_in, da_lambda, da_subln_g, sg_ln_g, sg_ln_b, sg_w, sg_b, na_rpb, w_branch, w_out, moe_w_group, moe_b_group, moe_w_router, moe_b_router, moe_w_gate, moe_w_up, moe_w_down, g_final):
    b, n_lat, d = x.shape
    n_ctx = ctx.shape[1]
    depth = w_in.shape[0]
    rows = n_lat // GRID_W
    assert d == D_MODEL and n_lat % NA_QTOK == 0 and rows >= 2 * NA_QROWS and n_ctx % 256 == 0 and b <= 7
    tm_lat = 1024 if n_lat % 1024 == 0 else 512
    tm_mrg = 512
    tm_ctx = 256

    cos, sin = _rope_tables(n_lat)
    cond = jnp.zeros((8, d), F32).at[:b].set(c).at[b].set(c_ctx)
    mods = _ada(cond, w_ada, b_ada.reshape(depth, 1, 6 * d))

    h, hc = x, ctx
    for l in range(depth):
        last = l == depth - 1
        lam_init = 0.8 - 0.6 * math.exp(-0.3 * l)
        m_lat = mods[l, :b].reshape(b, 1, 6, d)
        m_ctx = jnp.broadcast_to(mods[l, b].reshape(1, 1, 6, d), (b, 1, 6, d))
        sh1, sc1, gt1, sh2, sc2, gt2 = (m_lat[:, :, i] for i in range(6))
        csh1, csc1, cgt1, csh2, csc2, cgt2 = (m_ctx[:, :, i] for i in range(6))

        w_in_l = w_in[l].astype(BF16)
        g_mix = g_norm_mix[l].reshape(1, d)
        g_ffn = g_norm_ffn[l].reshape(1, d)
        p = _norm_proj(h, g_mix, sh1, sc1, w_in_l, IN_COLS, tm_lat)
        pc = _norm_proj(hc, g_mix, csh1, csc1, w_in_l, KV_COLS if last else IN_COLS, tm_ctx)

        q_hm, k_hm, v_hm = _rope(p, cos, sin, tm_mrg)
        kc_hm = _heads_major(pc[..., OFF_KA:OFF_VA])
        vc_hm = _heads_major(pc[..., OFF_VA:OFF_KC])
        k_all = jnp.concatenate([kc_hm, k_hm], axis=2)
        v_all = jnp.concatenate([vc_hm, v_hm], axis=2)
        g_sub = da_subln_g[l].reshape(1, 2 * DA_HEAD_DIM)
        y_a = _diff_attn(q_hm, k_all, v_all, da_lambda[l], g_sub, lam_init)

        y_c = _na_attn(p, pc, *_na_bias_tables(na_rpb[l], rows))

        wb = w_branch[l].astype(BF16)
        wo = w_out[l].astype(BF16)
        sgw = sg_w[l].astype(BF16)
        sgb = jnp.broadcast_to(sg_b[l][:, :, None], (SG_GROUPS, SG_CHUNK, SG_CHUNK))
        lng = sg_ln_g[l].reshape(1, SG_WIDTH)
        lnb = sg_ln_b[l].reshape(1, SG_WIDTH)
        t_lat = b * n_lat
        t_tok = t_lat if last else t_lat + b * n_ctx
        xn2_buf = None if last else jnp.zeros((t_tok * SUBLANES, LANES), F32)
        h, xn2 = _merge(h, y_a, p, y_c, wb, wo, sgw, sgb, lng, lnb, gt1, g_ffn, sh2, sc2, xn2_buf, t_tok, 0, tm_mrg)
        if not last:
            qc_hm = (_heads_major(pc[..., OFF_QA:OFF_QC]).astype(F32) * (DA_HEAD_DIM ** -0.5)).astype(BF16)
            ya_c = _diff_attn(qc_hm, kc_hm, vc_hm, da_lambda[l], g_sub, lam_init)
            yc_c = _ctx_mha(pc)
            hc, xn2 = _merge(hc, ya_c, pc, yc_c, wb, wo, sgw, sgb, lng, lnb, cgt1, g_ffn, csh2, csc2,
                             xn2, t_tok, t_lat, tm_ctx)

        w_rt = jnp.zeros((d, LANES), F32).at[:, :N_GROUPS].set(moe_w_group[l]) \
            .at[:, N_GROUPS:N_GROUPS + N_EXPERTS].set(moe_w_router[l])
        b_rt = jnp.zeros((1, LANES), F32).at[0, :N_GROUPS].set(moe_b_group[l]) \
            .at[0, N_GROUPS:N_GROUPS + N_EXPERTS].set(moe_b_router[l])
        route, dest, ys = _moe(xn2, w_rt, b_rt, moe_w_gate[l], moe_w_up[l], moe_w_down[l])
        h = _residual(dest, h, ys, route, gt2, g_final.reshape(1, d), 0, last)
        if not last:
            hc = _residual(dest, hc, ys, route, cgt2, g_final.reshape(1, d), t_lat, False)
    return h
```

```python
import functools
import math

import numpy as np
import jax
import jax.numpy as jnp
from jax import lax
from jax.experimental import pallas as pl
from jax.experimental.pallas import tpu as pltpu

F32 = jnp.float32
BF16 = jnp.bfloat16

D_MODEL = 1024
GRID_W = 64
EPS = 1e-6
NEG_INF = -1e30
ROPE_THETA = 10000.0

DA_HEADS = 4
DA_HEAD_DIM = 64
NA_HEADS = 8
NA_ROWS = 8
NA_COLS = 16
SG_CHUNK = 128
SG_GROUPS = 4
SG_WIDTH = 512
BRANCH_WIDTH = 512
N_BRANCH = 3

OFF_KA = 0
OFF_VA = 512
OFF_KC = 1024
OFF_VC = 1536
KV_COLS = 2048
OFF_QA = 2048
OFF_QC = 2560
OFF_ZB = 3072
OFF_GATE = 4096
IN_COLS = 7168

N_GROUPS = 4
EXPERTS_PER_GROUP = 8
N_EXPERTS = 32
D_EXPERT = 512

LANES = 128
HEAD_PAIR = LANES
VMEM_LIMIT = 56 * 1024 * 1024

MOE_ROWS = 256
DMA_CHUNK = 512

_CONTRACT_LAST = (((1,), (1,)), ((), ()))


def _params(sem, vmem=VMEM_LIMIT):
    return pltpu.CompilerParams(dimension_semantics=sem, vmem_limit_bytes=vmem)


def _ada_kernel(cond_ref, w_ref, b_ref, o_ref):
    c = cond_ref[...]
    c = c * jax.nn.sigmoid(c)
    o_ref[0] = jnp.dot(c, w_ref[0], preferred_element_type=F32, precision=lax.Precision.HIGHEST) + b_ref[0]


def _ada(cond, w_ada, b_ada):
    n_layers, d, d6 = w_ada.shape
    tn = 1024
    return pl.pallas_call(
        _ada_kernel,
        grid=(n_layers, d6 // tn),
        in_specs=[
            pl.BlockSpec((8, d), lambda l, j: (0, 0)),
            pl.BlockSpec((1, d, tn), lambda l, j: (l, 0, j)),
            pl.BlockSpec((1, 1, tn), lambda l, j: (l, 0, j)),
        ],
        out_specs=pl.BlockSpec((1, 8, tn), lambda l, j: (l, 0, j)),
        out_shape=jax.ShapeDtypeStruct((n_layers, 8, d6), F32),
        compiler_params=_params(("parallel", "parallel")),
        name="ada_mod",
    )(cond, w_ada, b_ada)


def _rms_mod(x, g, shift, scale):
    y = x * lax.rsqrt(jnp.mean(x * x, axis=-1, keepdims=True) + EPS)
    return (y * g) * (1.0 + scale) + shift


def _norm_proj_kernel(h_ref, g_ref, sh_ref, sc_ref, w_ref, o_ref, xn_ref):
    @pl.when(pl.program_id(2) == 0)
    def _():
        xn_ref[...] = _rms_mod(h_ref[0], g_ref[...], sh_ref[0], sc_ref[0]).astype(BF16)

    o_ref[0] = jnp.dot(xn_ref[...], w_ref[...], preferred_element_type=F32).astype(o_ref.dtype)


def _norm_proj(h, g, shift, scale, w, n_cols, tm):
    b, n, d = h.shape
    tn = 1024
    return pl.pallas_call(
        _norm_proj_kernel,
        grid=(b, n // tm, n_cols // tn),
        in_specs=[
            pl.BlockSpec((1, tm, d), lambda bi, i, j: (bi, i, 0)),
            pl.BlockSpec((1, d), lambda bi, i, j: (0, 0)),
            pl.BlockSpec((1, 1, d), lambda bi, i, j: (bi, 0, 0)),
            pl.BlockSpec((1, 1, d), lambda bi, i, j: (bi, 0, 0)),
            pl.BlockSpec((d, tn), lambda bi, i, j: (0, j)),
        ],
        out_specs=pl.BlockSpec((1, tm, tn), lambda bi, i, j: (bi, i, j)),
        out_shape=jax.ShapeDtypeStruct((b, n, n_cols), BF16),
        scratch_shapes=[pltpu.VMEM((tm, d), BF16)],
        compiler_params=_params(("parallel", "parallel", "arbitrary")),
        name="norm_proj",
    )(h, g, shift, scale, w)


def _rope_tables(n_tok):
    t = jnp.arange(n_tok, dtype=jnp.int32)
    row = (t // GRID_W).astype(F32)
    col = (t % GRID_W).astype(F32)
    half = DA_HEAD_DIM // 4
    inv = ROPE_THETA ** (-jnp.arange(half, dtype=F32) / half)
    ar = row[:, None] * inv
    ac = col[:, None] * inv
    ang = jnp.concatenate([ar, ar, ac, ac], axis=-1)
    sign = np.tile(np.concatenate([-np.ones(half), np.ones(half)]), 2).astype(np.float32)
    cos = jnp.cos(ang)
    sin = jnp.sin(ang) * sign
    return jnp.concatenate([cos, cos], axis=-1), jnp.concatenate([sin, sin], axis=-1)


DA_Q_SCALE = DA_HEAD_DIM ** -0.5 * math.log2(math.e)


def _rope_kernel(q_ref, k_ref, v_ref, cos_ref, sin_ref, qo_ref, ko_ref, vo_ref):
    cos = cos_ref[...]
    sin = sin_ref[...]
    lane = lax.broadcasted_iota(jnp.int32, cos.shape, 1)
    first = (lane % (DA_HEAD_DIM // 2)) < (DA_HEAD_DIM // 4)
    seg = DA_HEAD_DIM // 4

    def rope(x):
        partner = jnp.where(first, pltpu.roll(x, LANES - seg, 1), pltpu.roll(x, seg, 1))
        return x * cos + partner * sin

    for hd in range(DA_HEADS):
        sl = slice(hd * HEAD_PAIR, (hd + 1) * HEAD_PAIR)
        q = q_ref[0, :, sl].astype(F32)
        k = k_ref[0, :, sl].astype(F32)
        qo_ref[0, hd] = (rope(q) * DA_Q_SCALE).astype(BF16)
        ko_ref[0, hd] = rope(k).astype(BF16)
        vo_ref[0, hd] = v_ref[0, :, sl]


def _rope(p, cos, sin, tm):
    b, n, _ = p.shape
    w = DA_HEADS * HEAD_PAIR
    hm = jax.ShapeDtypeStruct((b, DA_HEADS, n, HEAD_PAIR), BF16)
    hm_spec = pl.BlockSpec((1, DA_HEADS, tm, HEAD_PAIR), lambda bi, i: (bi, 0, i, 0))
    return pl.pallas_call(
        _rope_kernel,
        grid=(b, n // tm),
        in_specs=[
            pl.BlockSpec((1, tm, w), lambda bi, i: (bi, i, OFF_QA // w)),
            pl.BlockSpec((1, tm, w), lambda bi, i: (bi, i, OFF_KA // w)),
            pl.BlockSpec((1, tm, w), lambda bi, i: (bi, i, OFF_VA // w)),
            pl.BlockSpec((tm, LANES), lambda bi, i: (i, 0)),
            pl.BlockSpec((tm, LANES), lambda bi, i: (i, 0)),
        ],
        out_specs=[hm_spec, hm_spec, hm_spec],
        out_shape=[hm, hm, hm],
        compiler_params=_params(("parallel", "parallel")),
        name="rope_heads",
    )(p, p, p, cos, sin)


def _heads_major(t):
    b, n, _ = t.shape
    return t.reshape(b, n, DA_HEADS, HEAD_PAIR).transpose(0, 2, 1, 3)


def _split_pair(q):
    lo = lax.broadcasted_iota(jnp.int32, q.shape, 1) < (HEAD_PAIR // 2)
    zero = jnp.zeros_like(q)
    return jnp.concatenate([jnp.where(lo, q, zero), jnp.where(lo, zero, q)], axis=0)


DA_DOT_ROWS = 256


def _diff_attn_kernel(lam_ref, g_ref, q_ref, k_ref, v_ref, o_ref, qs_ref, m_ref, l_ref, acc_ref, *, tq, lam_init):
    ki = pl.program_id(3)

    @pl.when(ki == 0)
    def _():
        qs_ref[...] = _split_pair(q_ref[0, 0])
        m_ref[...] = jnp.full(m_ref.shape, NEG_INF, F32)
        l_ref[...] = jnp.zeros(l_ref.shape, F32)
        acc_ref[...] = jnp.zeros(acc_ref.shape, F32)

    k = k_ref[0, 0]
    v = v_ref[0, 0]
    n_lt = k.shape[0] // LANES
    chunks = [slice(r0, r0 + DA_DOT_ROWS) for r0 in range(0, 2 * tq, DA_DOT_ROWS)]
    s = jnp.concatenate([lax.dot_general(qs_ref[c], k, _CONTRACT_LAST, preferred_element_type=F32)
                         for c in chunks], axis=0)
    tiles = [s[:, j * LANES:(j + 1) * LANES] for j in range(n_lt)]
    m_prev = m_ref[...]
    m_new = jnp.maximum(m_prev, jnp.max(functools.reduce(jnp.maximum, tiles), axis=-1, keepdims=True))
    alpha = jnp.exp2(m_prev - m_new)
    p_tiles = [jnp.exp2(t - m_new) for t in tiles]
    l_ref[...] = alpha * l_ref[...] + functools.reduce(jnp.add, p_tiles)
    p = jnp.concatenate(p_tiles, axis=1).astype(BF16)
    pv = jnp.concatenate([jnp.dot(p[c], v, preferred_element_type=F32) for c in chunks], axis=0)
    acc_ref[...] = alpha * acc_ref[...] + pv
    m_ref[...] = m_new

    @pl.when(ki == pl.num_programs(3) - 1)
    def _():
        o = acc_ref[...] / jnp.sum(l_ref[...], axis=-1, keepdims=True)
        lm = lam_ref[...]
        lam = (jnp.exp(jnp.sum(lm[0:1] * lm[1:2], axis=-1, keepdims=True))
               - jnp.exp(jnp.sum(lm[2:3] * lm[3:4], axis=-1, keepdims=True)) + lam_init)
        d = o[:tq] - lam * o[tq:]
        y = d * lax.rsqrt(jnp.mean(d * d, axis=-1, keepdims=True) + EPS)
        o_ref[0] = ((y * g_ref[...]) * (1.0 - lam_init)).astype(o_ref.dtype)


def _pick(n, options):
    for o in options:
        if n % o == 0:
            return o
    raise ValueError(f"no tile in {options} divides {n}")


def _diff_attn(q, k, v, lam_params, g, lam_init):
    b, nh, nq, _ = q.shape
    nk = k.shape[2]
    tq = _pick(nq, (512, 256))
    tk = _pick(nk, (768, 512, 256))
    kern = functools.partial(_diff_attn_kernel, tq=tq, lam_init=lam_init)
    return pl.pallas_call(
        kern,
        grid=(b, nh, nq // tq, nk // tk),
        in_specs=[
            pl.BlockSpec((4, DA_HEAD_DIM), lambda bi, h, i, j: (0, 0)),
            pl.BlockSpec((1, HEAD_PAIR), lambda bi, h, i, j: (0, 0)),
            pl.BlockSpec((1, 1, tq, HEAD_PAIR), lambda bi, h, i, j: (bi, h, i, 0)),
            pl.BlockSpec((1, 1, tk, HEAD_PAIR), lambda bi, h, i, j: (bi, h, j, 0)),
            pl.BlockSpec((1, 1, tk, HEAD_PAIR), lambda bi, h, i, j: (bi, h, j, 0)),
        ],
        out_specs=pl.BlockSpec((1, tq, HEAD_PAIR), lambda bi, h, i, j: (bi, i, h)),
        out_shape=jax.ShapeDtypeStruct((b, nq, nh * HEAD_PAIR), BF16),
        scratch_shapes=[
            pltpu.VMEM((2 * tq, HEAD_PAIR), BF16),
            pltpu.VMEM((2 * tq, LANES), F32),
            pltpu.VMEM((2 * tq, LANES), F32),
            pltpu.VMEM((2 * tq, HEAD_PAIR), F32),
        ],
        compiler_params=_params(("parallel", "parallel", "parallel", "arbitrary")),
        name="diff_attn",
    )(lam_params, g, q, k, v)


NA_QROWS = 8
NA_QTOK = NA_QROWS * GRID_W
NA_KBLK = 4 * GRID_W
NA_PIECES = 4
NA_WIN = NA_PIECES * NA_KBLK


def _na_bias_tables(rpb, rows):
    n_h = rpb.shape[0]
    n_kj = NA_PIECES * 4
    pad_r = n_kj - NA_ROWS
    pad_c = GRID_W - NA_COLS
    rp = jnp.pad(rpb * math.log2(math.e), ((0, 0), (pad_r, pad_r), (pad_c, pad_c)))
    a = jnp.stack([rp[:, :, GRID_W - 1 - qc:2 * GRID_W - 1 - qc] for qc in range(GRID_W)], axis=2)
    bias = jnp.stack([a[:, 3 + pad_r - qi:3 + pad_r - qi + n_kj].transpose(0, 2, 1, 3) for qi in range(NA_QROWS)],
                     axis=1).reshape(n_h, NA_QTOK, NA_WIN)

    n_r = rows // NA_QROWS
    qi = np.arange(NA_QROWS)
    kj = np.arange(n_kj)
    c = np.arange(GRID_W)
    cstart = np.clip(c - NA_COLS // 2, 0, GRID_W - NA_COLS)
    col_ok = (c[None, :] >= cstart[:, None]) & (c[None, :] < cstart[:, None] + NA_COLS)
    row_ok = []
    for r_grp in (0, min(1, n_r - 1), n_r - 1):
        r = NA_QROWS * r_grp + qi
        rs = np.clip(r - NA_ROWS // 2, 0, rows - NA_ROWS)
        krow = NA_QROWS * r_grp - 4 + kj
        row_ok.append((krow[None, :] >= rs[:, None]) & (krow[None, :] < rs[:, None] + NA_ROWS)
                      & (krow[None, :] >= 0) & (krow[None, :] < rows))
    ok = jnp.asarray(np.stack(row_ok))[:, :, None, :, None] & jnp.asarray(col_ok)[None, None, :, None, :]
    mask = jnp.where(ok, 0.0, NEG_INF).astype(F32).reshape(3, NA_QTOK, NA_WIN)
    return bias, mask


def _pair_softmax_pv(qs, k_list, v_list, bias_list):
    n_rows = qs.shape[0]
    chunks = [slice(r0, r0 + DA_DOT_ROWS) for r0 in range(0, n_rows, DA_DOT_ROWS)]
    tiles_list = []
    for k, bias in zip(k_list, bias_list):
        s = jnp.concatenate([lax.dot_general(qs[c], k, _CONTRACT_LAST, preferred_element_type=F32)
                             for c in chunks], axis=0)
        s = s if bias is None else s + bias
        tiles_list.append([s[:, j * LANES:(j + 1) * LANES] for j in range(k.shape[0] // LANES)])
    all_tiles = [t for tiles in tiles_list for t in tiles]
    m = jnp.max(functools.reduce(jnp.maximum, all_tiles), axis=-1, keepdims=True)
    p_list = [[jnp.exp2(t - m) for t in tiles] for tiles in tiles_list]
    l = jnp.sum(functools.reduce(jnp.add, [t for tiles in p_list for t in tiles]), axis=-1, keepdims=True)
    o = None
    for tiles, v in zip(p_list, v_list):
        p = jnp.concatenate(tiles, axis=1).astype(BF16)
        pv = jnp.concatenate([jnp.dot(p[c], v, preferred_element_type=F32) for c in chunks], axis=0)
        o = pv if o is None else o + pv
    o = o / l
    t = qs.shape[0] // 2
    lo = lax.broadcasted_iota(jnp.int32, (t, HEAD_PAIR), 1) < (HEAD_PAIR // 2)
    return jnp.where(lo, o[:t], o[t:])


def _scaled_pair(q, head_dim):
    return _split_pair((q.astype(F32) * (head_dim ** -0.5 * math.log2(math.e))).astype(BF16))


def _na_kernel(q_ref, k0, k1, k2, k3, v0, v1, v2, v3, kc_ref, vc_ref, b_ref, mk_ref, o_ref):
    qs = _scaled_pair(q_ref[0], HEAD_PAIR // 2)
    k_win = jnp.concatenate([k0[0], k1[0], k2[0], k3[0]], axis=0)
    v_win = jnp.concatenate([v0[0], v1[0], v2[0], v3[0]], axis=0)
    bias = (b_ref[...] + mk_ref[...]).reshape(2 * NA_QTOK, NA_WIN)
    o = _pair_softmax_pv(qs, [k_win, kc_ref[0]], [v_win, vc_ref[0]], [bias, None])
    o_ref[0] = o.astype(o_ref.dtype)


def _na_attn(p, pc, bias, mask):
    b, n, _ = p.shape
    n_ctx = pc.shape[1]
    n_r = n // NA_QTOK
    n_kb = n // NA_KBLK
    n_hp = NA_HEADS // 2

    def case(r):
        return jnp.where(r == 0, 0, jnp.where(r == n_r - 1, 2, 1))

    def kv_spec(off, piece):
        return pl.BlockSpec(
            (1, NA_KBLK, HEAD_PAIR),
            lambda bi, hp, r: (bi, jnp.clip(2 * r - 1 + piece, 0, n_kb - 1), off // HEAD_PAIR + hp))

    in_specs = [pl.BlockSpec((1, NA_QTOK, HEAD_PAIR), lambda bi, hp, r: (bi, r, OFF_QC // HEAD_PAIR + hp))]
    in_specs += [kv_spec(OFF_KC, i) for i in range(NA_PIECES)]
    in_specs += [kv_spec(OFF_VC, i) for i in range(NA_PIECES)]
    in_specs += [
        pl.BlockSpec((1, n_ctx, HEAD_PAIR), lambda bi, hp, r: (bi, 0, OFF_KC // HEAD_PAIR + hp)),
        pl.BlockSpec((1, n_ctx, HEAD_PAIR), lambda bi, hp, r: (bi, 0, OFF_VC // HEAD_PAIR + hp)),
        pl.BlockSpec((2, NA_QTOK, NA_WIN), lambda bi, hp, r: (hp, 0, 0)),
        pl.BlockSpec((1, NA_QTOK, NA_WIN), lambda bi, hp, r: (case(r), 0, 0)),
    ]
    return pl.pallas_call(
        _na_kernel,
        grid=(b, n_hp, n_r),
        in_specs=in_specs,
        out_specs=pl.BlockSpec((1, NA_QTOK, HEAD_PAIR), lambda bi, hp, r: (bi, r, hp)),
        out_shape=jax.ShapeDtypeStruct((b, n, NA_HEADS * HEAD_PAIR // 2), BF16),
        compiler_params=_params(("parallel", "parallel", "arbitrary")),
        name="na_attn",
    )(p, *([p] * (2 * NA_PIECES)), pc, pc, bias, mask)


def _ctx_mha_kernel(q_ref, k_ref, v_ref, o_ref):
    qs = _scaled_pair(q_ref[0], HEAD_PAIR // 2)
    o_ref[0] = _pair_softmax_pv(qs, [k_ref[0]], [v_ref[0]], [None]).astype(o_ref.dtype)


def _ctx_mha(pc):
    b, n_ctx, _ = pc.shape
    n_hp = NA_HEADS // 2

    def spec(off):
        return pl.BlockSpec((1, n_ctx, HEAD_PAIR), lambda bi, hp: (bi, 0, off // HEAD_PAIR + hp))

    return pl.pallas_call(
        _ctx_mha_kernel,
        grid=(b, n_hp),
        in_specs=[spec(OFF_QC), spec(OFF_KC), spec(OFF_VC)],
        out_specs=pl.BlockSpec((1, n_ctx, HEAD_PAIR), lambda bi, hp: (bi, 0, hp)),
        out_shape=jax.ShapeDtypeStruct((b, n_ctx, NA_HEADS * HEAD_PAIR // 2), BF16),
        compiler_params=_params(("parallel", "parallel")),
        name="ctx_mha",
    )(pc, pc, pc)


def _gelu_tanh(x):
    return 0.5 * x * (1.0 + jnp.tanh(math.sqrt(2.0 / math.pi) * (x + 0.044715 * (x * x * x))))


SUBLANES = 8


def _store_token_tiles(ref, x):
    t = x.shape[0]
    for j in range(SUBLANES):
        ref[pl.ds(j, t, stride=SUBLANES), :] = x[:, j * LANES:(j + 1) * LANES]


def _load_token_tiles(ref, t):
    return jnp.concatenate([ref[pl.ds(j, t, stride=SUBLANES), :] for j in range(SUBLANES)], axis=1)


def _merge_kernel(h_ref, ya_ref, z_ref, yc_ref, g0_ref, g1_ref, g2_ref, wb_ref, wo_ref, sgw_ref, sgb_ref,
                  lng_ref, lnb_ref, gt_ref, gf_ref, sh_ref, sc_ref, ho_ref, xo_ref, *, tm):
    z = _gelu_tanh(z_ref[0].astype(F32))
    u = z[:, :SG_WIDTH]
    vv = z[:, SG_WIDTH:]
    mu = jnp.mean(vv, axis=-1, keepdims=True)
    var = jnp.mean(jnp.square(vv - mu), axis=-1, keepdims=True)
    vv = ((vv - mu) * lax.rsqrt(var + EPS)) * lng_ref[...] + lnb_ref[...]
    vv = vv.astype(BF16)
    gd = SG_WIDTH // SG_GROUPS
    chunks = []
    for c in range(tm // SG_CHUNK):
        rows = slice(c * SG_CHUNK, (c + 1) * SG_CHUNK)
        groups = []
        for g in range(SG_GROUPS):
            s = jnp.dot(sgw_ref[g], vv[rows, g * gd:(g + 1) * gd], preferred_element_type=F32) + sgb_ref[g]
            groups.append(s)
        chunks.append(jnp.concatenate(groups, axis=1))
    y_b = (u * jnp.concatenate(chunks, axis=0)).astype(BF16)

    merged = None
    for y, gate_ref, i in ((ya_ref[0], g0_ref, 0), (y_b, g1_ref, 1), (yc_ref[0], g2_ref, 2)):
        t = jax.nn.sigmoid(gate_ref[0].astype(F32)) * jnp.dot(y, wb_ref[i], preferred_element_type=F32)
        merged = t if merged is None else merged + t
    out = jnp.dot(merged.astype(BF16), wo_ref[...], preferred_element_type=F32)
    h_new = h_ref[0] + gt_ref[0] * out
    ho_ref[0] = h_new
    _store_token_tiles(xo_ref, _rms_mod(h_new, gf_ref[...], sh_ref[0], sc_ref[0]))


def _merge(h, y_a, p, y_c, wb, wo, sgw, sgb, lng, lnb, gt1, g_ffn, sh2, sc2, xn2_buf, xn2_rows, row_off, tm):
    b, n, d = h.shape
    n_i = n // tm
    off_blk = row_off // tm
    vec = pl.BlockSpec((1, 1, d), lambda bi, i: (bi, 0, 0))

    def col(width, off):
        return pl.BlockSpec((1, tm, width), lambda bi, i: (bi, i, off // width))

    kern = functools.partial(_merge_kernel, tm=tm)
    args = [h, y_a, p, y_c, p, p, p, wb, wo, sgw, sgb, lng, lnb, gt1, g_ffn, sh2, sc2]
    in_specs = [
        pl.BlockSpec((1, tm, d), lambda bi, i: (bi, i, 0)),
        col(BRANCH_WIDTH, 0),
        col(2 * SG_WIDTH, OFF_ZB),
        col(BRANCH_WIDTH, 0),
        col(d, OFF_GATE), col(d, OFF_GATE + d), col(d, OFF_GATE + 2 * d),
        pl.BlockSpec((N_BRANCH, BRANCH_WIDTH, d), lambda bi, i: (0, 0, 0)),
        pl.BlockSpec((d, d), lambda bi, i: (0, 0)),
        pl.BlockSpec((SG_GROUPS, SG_CHUNK, SG_CHUNK), lambda bi, i: (0, 0, 0)),
        pl.BlockSpec((SG_GROUPS, SG_CHUNK, SG_CHUNK), lambda bi, i: (0, 0, 0)),
        pl.BlockSpec((1, SG_WIDTH), lambda bi, i: (0, 0)),
        pl.BlockSpec((1, SG_WIDTH), lambda bi, i: (0, 0)),
        vec,
        pl.BlockSpec((1, d), lambda bi, i: (0, 0)),
        vec, vec,
    ]
    aliases = {}
    if xn2_buf is not None:
        args.append(xn2_buf)
        in_specs.append(pl.BlockSpec(memory_space=pl.ANY))
        aliases = {len(args) - 1: 1}
    assert d == SUBLANES * LANES
    xn2_shape = jax.ShapeDtypeStruct((xn2_rows * SUBLANES, LANES), F32)

    def body(*refs):
        n_in = 17
        kern(*refs[:n_in], *refs[len(refs) - 2:])

    return pl.pallas_call(
        body,
        grid=(b, n_i),
        in_specs=in_specs,
        out_specs=[
            pl.BlockSpec((1, tm, d), lambda bi, i: (bi, i, 0)),
            pl.BlockSpec((tm * SUBLANES, LANES), lambda bi, i: (off_blk + bi * n_i + i, 0)),
        ],
        out_shape=[jax.ShapeDtypeStruct((b, n, d), F32), xn2_shape],
        input_output_aliases=aliases,
        compiler_params=_params(("parallel", "parallel")),
        name="merge_branches",
    )(*args)


R_E1, R_E2, R_W1, R_W2, R_RANK1, R_RANK2 = range(6)


def _router_kernel(x_ref, w_ref, b_ref, o_ref, cnt_ref, run_ref, *, tm):
    @pl.when(pl.program_id(0) == 0)
    def _():
        run_ref[...] = jnp.zeros(run_ref.shape, F32)

    logits = jnp.dot(_load_token_tiles(x_ref, tm), w_ref[...], preferred_element_type=F32,
                     precision=lax.Precision.HIGHEST) + b_ref[...]
    lane = lax.broadcasted_iota(jnp.int32, logits.shape, 1)
    lane_f = lane.astype(F32)
    far = jnp.float32(1e9)

    def first_lane(mask):
        return jnp.min(jnp.where(mask, lane_f, far), axis=-1, keepdims=True)

    is_g = lane < N_GROUPS
    gl = jnp.where(is_g, logits, NEG_INF)
    g_max = jnp.max(gl, axis=-1, keepdims=True)
    g_idx = first_lane(is_g & (gl == g_max))
    g_sum = jnp.sum(jnp.where(is_g, jnp.exp(gl - g_max), 0.0), axis=-1, keepdims=True)
    g_w = 1.0 / g_sum

    e_lane = lane - N_GROUPS
    in_grp = (e_lane >= 0) & (e_lane < N_EXPERTS) & \
        (jnp.right_shift(e_lane, 3).astype(F32) == g_idx)
    el = jnp.where(in_grp, logits, NEG_INF)
    l1 = jnp.max(el, axis=-1, keepdims=True)
    i1 = first_lane(in_grp & (el == l1))
    rest = in_grp & (lane_f != i1)
    el2 = jnp.where(rest, logits, NEG_INF)
    l2 = jnp.max(el2, axis=-1, keepdims=True)
    i2 = first_lane(rest & (el2 == l2))
    t = jnp.exp(l2 - l1)
    w1 = g_w / (1.0 + t)
    w2 = g_w * t / (1.0 + t)

    oh1 = lane_f == i1
    oh2 = lane_f == i2
    oh = jnp.where(oh1 | oh2, 1.0, 0.0)
    row = lax.broadcasted_iota(jnp.int32, (tm, tm), 0)
    colm = lax.broadcasted_iota(jnp.int32, (tm, tm), 1)
    before = jnp.where(colm < row, 1.0, 0.0).astype(BF16)
    prior = jnp.dot(before, oh.astype(BF16), preferred_element_type=F32) + run_ref[...]
    rank1 = jnp.sum(jnp.where(oh1, prior, 0.0), axis=-1, keepdims=True)
    rank2 = jnp.sum(jnp.where(oh2, prior, 0.0), axis=-1, keepdims=True)
    run_new = run_ref[...] + jnp.sum(oh, axis=0, keepdims=True)
    run_ref[...] = run_new
    cnt_ref[...] = run_new

    slab = jnp.zeros(logits.shape, F32)
    for ln, val in ((R_E1, i1 - N_GROUPS), (R_E2, i2 - N_GROUPS), (R_W1, w1), (R_W2, w2),
                    (R_RANK1, rank1), (R_RANK2, rank2)):
        slab = jnp.where(lane == ln, val, slab)
    o_ref[...] = slab


def _router(xn2, w_rt, b_rt):
    t_tok = xn2.shape[0] // SUBLANES
    d = w_rt.shape[0]
    tm = 512
    kern = functools.partial(_router_kernel, tm=tm)
    return pl.pallas_call(
        kern,
        grid=(t_tok // tm,),
        in_specs=[
            pl.BlockSpec((tm * SUBLANES, LANES), lambda i: (i, 0)),
            pl.BlockSpec((d, LANES), lambda i: (0, 0)),
            pl.BlockSpec((1, LANES), lambda i: (0, 0)),
        ],
        out_specs=[pl.BlockSpec((tm, LANES), lambda i: (i, 0)), pl.BlockSpec((1, LANES), lambda i: (0, 0))],
        out_shape=[jax.ShapeDtypeStruct((t_tok, LANES), F32), jax.ShapeDtypeStruct((1, LANES), F32)],
        scratch_shapes=[pltpu.VMEM((1, LANES), F32)],
        compiler_params=_params(("arbitrary",)),
        name="moe_router",
    )(xn2, w_rt, b_rt)


def _tile_rows(index):
    return pl.ds(pl.multiple_of(index * SUBLANES, SUBLANES), SUBLANES)


def _dispatch_kernel(dest_ref, x_ref, init_ref, xs_ref, sem):
    del init_ref
    step = pl.program_id(0)

    def issue(i, carry):
        tok = step * DMA_CHUNK + i
        for k in range(2):
            pltpu.make_async_copy(x_ref.at[_tile_rows(i), :], xs_ref.at[_tile_rows(dest_ref[2 * tok + k]), :],
                                  sem.at[0]).start()
        return carry

    lax.fori_loop(0, DMA_CHUNK, issue, 0)
    for _ in range(2):
        pltpu.make_async_copy(x_ref, xs_ref.at[pl.ds(0, DMA_CHUNK * SUBLANES), :], sem.at[0]).wait()


def _dispatch(dest, xn2, n_slots):
    t_tok = xn2.shape[0] // SUBLANES
    grid_spec = pltpu.PrefetchScalarGridSpec(
        num_scalar_prefetch=1,
        grid=(t_tok // DMA_CHUNK,),
        in_specs=[pl.BlockSpec((DMA_CHUNK * SUBLANES, LANES), lambda i, dest: (i, 0)),
                  pl.BlockSpec(memory_space=pl.ANY)],
        out_specs=pl.BlockSpec(memory_space=pl.ANY),
        scratch_shapes=[pltpu.SemaphoreType.DMA((1,))],
    )
    return pl.pallas_call(
        _dispatch_kernel,
        grid_spec=grid_spec,
        out_shape=jax.ShapeDtypeStruct((n_slots * SUBLANES, LANES), F32),
        input_output_aliases={2: 0},
        compiler_params=_params(("arbitrary",)),
        name="moe_dispatch",
    )(dest, xn2, jnp.zeros((n_slots * SUBLANES, LANES), F32))


def _expert_kernel(be_ref, nu_ref, x_ref, wg_ref, wu_ref, wd_ref, o_ref):
    i = pl.program_id(0)

    @pl.when(i < nu_ref[0])
    def _():
        x = _load_token_tiles(x_ref, MOE_ROWS).astype(BF16)
        gate = jnp.dot(x, wg_ref[0].astype(BF16), preferred_element_type=F32)
        up = jnp.dot(x, wu_ref[0].astype(BF16), preferred_element_type=F32)
        hdn = (gate * jax.nn.sigmoid(gate)) * up
        _store_token_tiles(o_ref, jnp.dot(hdn.astype(BF16), wd_ref[0].astype(BF16), preferred_element_type=F32))

    @pl.when(i >= nu_ref[0])
    def _():
        o_ref[...] = jnp.zeros(o_ref.shape, F32)


def _experts(block_expert, n_used, xs, w_gate, w_up, w_down):
    n_blocks = xs.shape[0] // (MOE_ROWS * SUBLANES)
    _, d, de = w_gate.shape
    blk = pl.BlockSpec((MOE_ROWS * SUBLANES, LANES), lambda i, be, nu: (i, 0))
    grid_spec = pltpu.PrefetchScalarGridSpec(
        num_scalar_prefetch=2,
        grid=(n_blocks,),
        in_specs=[
            blk,
            pl.BlockSpec((1, d, de), lambda i, be, nu: (be[i], 0, 0)),
            pl.BlockSpec((1, d, de), lambda i, be, nu: (be[i], 0, 0)),
            pl.BlockSpec((1, de, d), lambda i, be, nu: (be[i], 0, 0)),
        ],
        out_specs=blk,
    )
    return pl.pallas_call(
        _expert_kernel,
        grid_spec=grid_spec,
        out_shape=jax.ShapeDtypeStruct(xs.shape, F32),
        compiler_params=_params(("arbitrary",)),
        name="moe_experts",
    )(block_expert, n_used, xs, w_gate, w_up, w_down)


def _residual_kernel(dest_ref, h_ref, r_ref, gt_ref, gf_ref, ys_ref, o_ref, buf_ref, sem, *, tm, n_i, row_off, final):
    base = row_off + (pl.program_id(0) * n_i + pl.program_id(1)) * tm

    def issue(t, carry):
        for k in range(2):
            pltpu.make_async_copy(ys_ref.at[_tile_rows(dest_ref[2 * (base + t) + k]), :],
                                  buf_ref.at[k, _tile_rows(t), :], sem.at[0]).start()
        return carry

    lax.fori_loop(0, tm, issue, 0)
    for k in range(2):
        pltpu.make_async_copy(ys_ref.at[pl.ds(0, tm * SUBLANES), :], buf_ref.at[k], sem.at[0]).wait()

    r = r_ref[...]
    y = (_load_token_tiles(buf_ref.at[0], tm) * r[:, R_W1:R_W1 + 1]
         + _load_token_tiles(buf_ref.at[1], tm) * r[:, R_W2:R_W2 + 1])
    h_new = h_ref[0] + gt_ref[0] * y
    if final:
        h_new = (h_new * lax.rsqrt(jnp.mean(h_new * h_new, axis=-1, keepdims=True) + EPS)) * gf_ref[...]
    o_ref[0] = h_new


def _residual(dest, h, ys, route, gt2, g_final, row_off, final):
    b, n, d = h.shape
    tm = 256
    n_i = n // tm
    off_blk = row_off // tm
    kern = functools.partial(_residual_kernel, tm=tm, n_i=n_i, row_off=row_off, final=final)
    grid_spec = pltpu.PrefetchScalarGridSpec(
        num_scalar_prefetch=1,
        grid=(b, n_i),
        in_specs=[
            pl.BlockSpec((1, tm, d), lambda bi, i, dest: (bi, i, 0)),
            pl.BlockSpec((tm, LANES), lambda bi, i, dest: (off_blk + bi * n_i + i, 0)),
            pl.BlockSpec((1, 1, d), lambda bi, i, dest: (bi, 0, 0)),
            pl.BlockSpec((1, d), lambda bi, i, dest: (0, 0)),
            pl.BlockSpec(memory_space=pl.ANY),
        ],
        out_specs=pl.BlockSpec((1, tm, d), lambda bi, i, dest: (bi, i, 0)),
        scratch_shapes=[pltpu.VMEM((2, tm * SUBLANES, LANES), F32), pltpu.SemaphoreType.DMA((1,))],
    )
    return pl.pallas_call(
        kern,
        grid_spec=grid_spec,
        out_shape=jax.ShapeDtypeStruct((b, n, d), F32),
        compiler_params=_params(("arbitrary", "arbitrary")),
        name="moe_residual",
    )(dest, h, route, gt2, g_final, ys)


def _moe(xn2, w_rt, b_rt, w_gate, w_up, w_down):
    t_tok = xn2.shape[0] // SUBLANES
    route, counts = _router(xn2, w_rt, b_rt)
    cnt = counts[0, N_GROUPS:N_GROUPS + N_EXPERTS].astype(jnp.int32)
    padded = (cnt + MOE_ROWS - 1) // MOE_ROWS * MOE_ROWS
    pad_end = jnp.cumsum(padded)
    pad_start = pad_end - padded
    expert = route[:, R_E1:R_E2 + 1].astype(jnp.int32)
    rank = route[:, R_RANK1:R_RANK2 + 1].astype(jnp.int32)
    dest = (pad_start[expert] + rank).reshape(2 * t_tok)
    n_blocks = -(-(2 * t_tok + N_EXPERTS * (MOE_ROWS - 1)) // MOE_ROWS)
    n_slots = n_blocks * MOE_ROWS
    starts = jnp.arange(n_blocks, dtype=jnp.int32) * MOE_ROWS
    block_expert = jnp.minimum(jnp.sum(starts[:, None] >= pad_end[None, :], axis=1), N_EXPERTS - 1).astype(jnp.int32)
    n_used = (pad_end[-1:] // MOE_ROWS).astype(jnp.int32)

    xs = _dispatch(dest, xn2, n_slots)
    return route, dest, _experts(block_expert, n_used, xs, w_gate, w_up, w_down)


def kernel(x, c, ctx, c_ctx, w_ada, b_ada, g_norm_mix, g_norm_ffn, w_in, da_lambda, da_subln_g, sg_ln_g, sg_ln_b, sg_w, sg_b, na_rpb, w_branch, w_out, moe_w_group, moe_b_group, moe_w_router, moe_b_router, moe_w_gate, moe_w_up, moe_w_down, g_final):
    b, n_lat, d = x.shape
    n_ctx = ctx.shape[1]
    depth = w_in.shape[0]
    rows = n_lat // GRID_W
    assert d == D_MODEL and n_lat % NA_QTOK == 0 and rows >= 2 * NA_QROWS and n_ctx % 256 == 0 and b <= 7
    tm_lat = 1024 if n_lat % 1024 == 0 else 512
    tm_mrg = 512
    tm_ctx = 256

    cos, sin = _rope_tables(n_lat)
    cond = jnp.zeros((8, d), F32).at[:b].set(c).at[b].set(c_ctx)
    mods = _ada(cond, w_ada, b_ada.reshape(depth, 1, 6 * d))

    h, hc = x, ctx
    for l in range(depth):
        last = l == depth - 1
        lam_init = 0.8 - 0.6 * math.exp(-0.3 * l)
        m_lat = mods[l, :b].reshape(b, 1, 6, d)
        m_ctx = jnp.broadcast_to(mods[l, b].reshape(1, 1, 6, d), (b, 1, 6, d))
        sh1, sc1, gt1, sh2, sc2, gt2 = (m_lat[:, :, i] for i in range(6))
        csh1, csc1, cgt1, csh2, csc2, cgt2 = (m_ctx[:, :, i] for i in range(6))

        w_in_l = w_in[l].astype(BF16)
        g_mix = g_norm_mix[l].reshape(1, d)
        g_ffn = g_norm_ffn[l].reshape(1, d)
        p = _norm_proj(h, g_mix, sh1, sc1, w_in_l, IN_COLS, tm_lat)
        pc = _norm_proj(hc, g_mix, csh1, csc1, w_in_l, KV_COLS if last else IN_COLS, tm_ctx)

        q_hm, k_hm, v_hm = _rope(p, cos, sin, tm_mrg)
        kc_hm = _heads_major(pc[..., OFF_KA:OFF_VA])
        vc_hm = _heads_major(pc[..., OFF_VA:OFF_KC])
        k_all = jnp.concatenate([kc_hm, k_hm], axis=2)
        v_all = jnp.concatenate([vc_hm, v_hm], axis=2)
        g_sub = da_subln_g[l].reshape(1, 2 * DA_HEAD_DIM)
        y_a = _diff_attn(q_hm, k_all, v_all, da_lambda[l], g_sub, lam_init)

        y_c = _na_attn(p, pc, *_na_bias_tables(na_rpb[l], rows))

        wb = w_branch[l].astype(BF16)
        wo = w_out[l].astype(BF16)
        sgw = sg_w[l].astype(BF16)
        sgb = jnp.broadcast_to(sg_b[l][:, :, None], (SG_GROUPS, SG_CHUNK, SG_CHUNK))
        lng = sg_ln_g[l].reshape(1, SG_WIDTH)
        lnb = sg_ln_b[l].reshape(1, SG_WIDTH)
        t_lat = b * n_lat
        t_tok = t_lat if last else t_lat + b * n_ctx
        xn2_buf = None if last else jnp.zeros((t_tok * SUBLANES, LANES), F32)
        h, xn2 = _merge(h, y_a, p, y_c, wb, wo, sgw, sgb, lng, lnb, gt1, g_ffn, sh2, sc2, xn2_buf, t_tok, 0, tm_mrg)
        if not last:
            qc_hm = (_heads_major(pc[..., OFF_QA:OFF_QC]).astype(F32) * DA_Q_SCALE).astype(BF16)
            ya_c = _diff_attn(qc_hm, kc_hm, vc_hm, da_lambda[l], g_sub, lam_init)
            yc_c = _ctx_mha(pc)
            hc, xn2 = _merge(hc, ya_c, pc, yc_c, wb, wo, sgw, sgb, lng, lnb, cgt1, g_ffn, csh2, csc2,
                             xn2, t_tok, t_lat, tm_ctx)

        w_rt = jnp.zeros((d, LANES), F32).at[:, :N_GROUPS].set(moe_w_group[l]) \
            .at[:, N_GROUPS:N_GROUPS + N_EXPERTS].set(moe_w_router[l])
        b_rt = jnp.zeros((1, LANES), F32).at[0, :N_GROUPS].set(moe_b_group[l]) \
            .at[0, N_GROUPS:N_GROUPS + N_EXPERTS].set(moe_b_router[l])
        route, dest, ys = _moe(xn2, w_rt, b_rt, moe_w_gate[l], moe_w_up[l], moe_w_down[l])
        h = _residual(dest, h, ys, route, gt2, g_final.reshape(1, d), 0, last)
        if not last:
            hc = _residual(dest, hc, ys, route, cgt2, g_final.reshape(1, d), t_lat, False)
    return h
```

```python
import functools
import math

import numpy as np
import jax
import jax.numpy as jnp
from jax import lax
from jax.experimental import pallas as pl
from jax.experimental.pallas import tpu as pltpu

F32 = jnp.float32
BF16 = jnp.bfloat16

D_MODEL = 1024
GRID_W = 64
EPS = 1e-6
NEG_INF = -1e30
ROPE_THETA = 10000.0

DA_HEADS = 4
DA_HEAD_DIM = 64
NA_HEADS = 8
NA_ROWS = 8
NA_COLS = 16
SG_CHUNK = 128
SG_GROUPS = 4
SG_WIDTH = 512
BRANCH_WIDTH = 512
N_BRANCH = 3

OFF_KA = 0
OFF_VA = 512
OFF_KC = 1024
OFF_VC = 1536
KV_COLS = 2048
OFF_QA = 2048
OFF_QC = 2560
OFF_ZB = 3072
OFF_GATE = 4096
IN_COLS = 7168

N_GROUPS = 4
EXPERTS_PER_GROUP = 8
N_EXPERTS = 32
D_EXPERT = 512

LANES = 128
HEAD_PAIR = LANES
VMEM_LIMIT = 56 * 1024 * 1024

MOE_ROWS = 256
DMA_CHUNK = 512

_CONTRACT_LAST = (((1,), (1,)), ((), ()))


def _params(sem, vmem=VMEM_LIMIT, flags=None):
    return pltpu.CompilerParams(dimension_semantics=sem, vmem_limit_bytes=vmem, flags=flags)


def _ada_kernel(cond_ref, w_ref, b_ref, o_ref):
    c = cond_ref[...]
    c = c * jax.nn.sigmoid(c)
    o_ref[0] = jnp.dot(c, w_ref[0], preferred_element_type=F32, precision=lax.Precision.HIGHEST) + b_ref[0]


def _ada(cond, w_ada, b_ada):
    n_layers, d, d6 = w_ada.shape
    tn = 1024
    return pl.pallas_call(
        _ada_kernel,
        grid=(n_layers, d6 // tn),
        in_specs=[
            pl.BlockSpec((8, d), lambda l, j: (0, 0)),
            pl.BlockSpec((1, d, tn), lambda l, j: (l, 0, j)),
            pl.BlockSpec((1, 1, tn), lambda l, j: (l, 0, j)),
        ],
        out_specs=pl.BlockSpec((1, 8, tn), lambda l, j: (l, 0, j)),
        out_shape=jax.ShapeDtypeStruct((n_layers, 8, d6), F32),
        compiler_params=_params(("parallel", "parallel")),
        name="ada_mod",
    )(cond, w_ada, b_ada)


def _rms_mod(x, g, shift, scale):
    y = x * lax.rsqrt(jnp.mean(x * x, axis=-1, keepdims=True) + EPS)
    return (y * g) * (1.0 + scale) + shift


def _norm_proj_kernel(h_ref, g_ref, sh_ref, sc_ref, w_ref, o_ref, xn_ref):
    @pl.when(pl.program_id(2) == 0)
    def _():
        xn_ref[...] = _rms_mod(h_ref[0], g_ref[...], sh_ref[0], sc_ref[0]).astype(BF16)

    o_ref[0] = jnp.dot(xn_ref[...], w_ref[...], preferred_element_type=F32).astype(o_ref.dtype)


def _norm_proj(h, g, shift, scale, w, n_cols, tm):
    b, n, d = h.shape
    tn = 1024
    return pl.pallas_call(
        _norm_proj_kernel,
        grid=(b, n // tm, n_cols // tn),
        in_specs=[
            pl.BlockSpec((1, tm, d), lambda bi, i, j: (bi, i, 0)),
            pl.BlockSpec((1, d), lambda bi, i, j: (0, 0)),
            pl.BlockSpec((1, 1, d), lambda bi, i, j: (bi, 0, 0)),
            pl.BlockSpec((1, 1, d), lambda bi, i, j: (bi, 0, 0)),
            pl.BlockSpec((d, tn), lambda bi, i, j: (0, j)),
        ],
        out_specs=pl.BlockSpec((1, tm, tn), lambda bi, i, j: (bi, i, j)),
        out_shape=jax.ShapeDtypeStruct((b, n, n_cols), BF16),
        scratch_shapes=[pltpu.VMEM((tm, d), BF16)],
        compiler_params=_params(("parallel", "parallel", "arbitrary")),
        name="norm_proj",
    )(h, g, shift, scale, w)


def _rope_tables(n_tok):
    t = jnp.arange(n_tok, dtype=jnp.int32)
    row = (t // GRID_W).astype(F32)
    col = (t % GRID_W).astype(F32)
    half = DA_HEAD_DIM // 4
    inv = ROPE_THETA ** (-jnp.arange(half, dtype=F32) / half)
    ar = row[:, None] * inv
    ac = col[:, None] * inv
    ang = jnp.concatenate([ar, ar, ac, ac], axis=-1)
    sign = np.tile(np.concatenate([-np.ones(half), np.ones(half)]), 2).astype(np.float32)
    cos = jnp.cos(ang)
    sin = jnp.sin(ang) * sign
    return jnp.concatenate([cos, cos], axis=-1), jnp.concatenate([sin, sin], axis=-1)


DA_Q_SCALE = DA_HEAD_DIM ** -0.5 * math.log2(math.e)


def _rope_kernel(q_ref, k_ref, v_ref, cos_ref, sin_ref, qo_ref, ko_ref, vo_ref):
    cos = cos_ref[...]
    sin = sin_ref[...]
    lane = lax.broadcasted_iota(jnp.int32, cos.shape, 1)
    first = (lane % (DA_HEAD_DIM // 2)) < (DA_HEAD_DIM // 4)
    seg = DA_HEAD_DIM // 4

    def rope(x):
        partner = jnp.where(first, pltpu.roll(x, LANES - seg, 1), pltpu.roll(x, seg, 1))
        return x * cos + partner * sin

    for hd in range(DA_HEADS):
        sl = slice(hd * HEAD_PAIR, (hd + 1) * HEAD_PAIR)
        q = q_ref[0, :, sl].astype(F32)
        k = k_ref[0, :, sl].astype(F32)
        qo_ref[0, hd] = (rope(q) * DA_Q_SCALE).astype(BF16)
        ko_ref[0, hd] = rope(k).astype(BF16)
        vo_ref[0, hd] = v_ref[0, :, sl]


def _rope(p, cos, sin, tm):
    b, n, _ = p.shape
    w = DA_HEADS * HEAD_PAIR
    hm = jax.ShapeDtypeStruct((b, DA_HEADS, n, HEAD_PAIR), BF16)
    hm_spec = pl.BlockSpec((1, DA_HEADS, tm, HEAD_PAIR), lambda bi, i: (bi, 0, i, 0))
    return pl.pallas_call(
        _rope_kernel,
        grid=(b, n // tm),
        in_specs=[
            pl.BlockSpec((1, tm, w), lambda bi, i: (bi, i, OFF_QA // w)),
            pl.BlockSpec((1, tm, w), lambda bi, i: (bi, i, OFF_KA // w)),
            pl.BlockSpec((1, tm, w), lambda bi, i: (bi, i, OFF_VA // w)),
            pl.BlockSpec((tm, LANES), lambda bi, i: (i, 0)),
            pl.BlockSpec((tm, LANES), lambda bi, i: (i, 0)),
        ],
        out_specs=[hm_spec, hm_spec, hm_spec],
        out_shape=[hm, hm, hm],
        compiler_params=_params(("parallel", "parallel")),
        name="rope_heads",
    )(p, p, p, cos, sin)


def _heads_major(t):
    b, n, _ = t.shape
    return t.reshape(b, n, DA_HEADS, HEAD_PAIR).transpose(0, 2, 1, 3)


def _split_pair(q):
    lo = lax.broadcasted_iota(jnp.int32, q.shape, 1) < (HEAD_PAIR // 2)
    zero = jnp.zeros_like(q)
    return jnp.concatenate([jnp.where(lo, q, zero), jnp.where(lo, zero, q)], axis=0)


DA_DOT_ROWS = 256


def _diff_attn_kernel(lam_ref, g_ref, q_ref, k_ref, v_ref, o_ref, qs_ref, m_ref, l_ref, acc_ref, *, tq, lam_init):
    ki = pl.program_id(3)

    @pl.when(ki == 0)
    def _():
        qs_ref[...] = _split_pair(q_ref[0, 0])
        m_ref[...] = jnp.full(m_ref.shape, NEG_INF, F32)
        l_ref[...] = jnp.zeros(l_ref.shape, F32)
        acc_ref[...] = jnp.zeros(acc_ref.shape, F32)

    k = k_ref[0, 0]
    v = v_ref[0, 0]
    n_lt = k.shape[0] // LANES
    chunks = [slice(r0, r0 + DA_DOT_ROWS) for r0 in range(0, 2 * tq, DA_DOT_ROWS)]
    s = jnp.concatenate([lax.dot_general(qs_ref[c], k, _CONTRACT_LAST, preferred_element_type=F32)
                         for c in chunks], axis=0)
    tiles = [s[:, j * LANES:(j + 1) * LANES] for j in range(n_lt)]
    m_prev = m_ref[...]
    m_new = jnp.maximum(m_prev, jnp.max(functools.reduce(jnp.maximum, tiles), axis=-1, keepdims=True))
    alpha = jnp.exp2(m_prev - m_new)
    p_tiles = [jnp.exp2(t - m_new) for t in tiles]
    l_ref[...] = alpha * l_ref[...] + functools.reduce(jnp.add, p_tiles)
    p = jnp.concatenate(p_tiles, axis=1).astype(BF16)
    pv = jnp.concatenate([jnp.dot(p[c], v, preferred_element_type=F32) for c in chunks], axis=0)
    acc_ref[...] = alpha * acc_ref[...] + pv
    m_ref[...] = m_new

    @pl.when(ki == pl.num_programs(3) - 1)
    def _():
        o = acc_ref[...] / jnp.sum(l_ref[...], axis=-1, keepdims=True)
        lm = lam_ref[...]
        lam = (jnp.exp(jnp.sum(lm[0:1] * lm[1:2], axis=-1, keepdims=True))
               - jnp.exp(jnp.sum(lm[2:3] * lm[3:4], axis=-1, keepdims=True)) + lam_init)
        d = o[:tq] - lam * o[tq:]
        y = d * lax.rsqrt(jnp.mean(d * d, axis=-1, keepdims=True) + EPS)
        o_ref[0] = ((y * g_ref[...]) * (1.0 - lam_init)).astype(o_ref.dtype)


def _pick(n, options):
    for o in options:
        if n % o == 0:
            return o
    raise ValueError(f"no tile in {options} divides {n}")


def _diff_attn(q, k, v, lam_params, g, lam_init):
    b, nh, nq, _ = q.shape
    nk = k.shape[2]
    tq = _pick(nq, (512, 256))
    tk = _pick(nk, (1408, 768, 512, 256))
    kern = functools.partial(_diff_attn_kernel, tq=tq, lam_init=lam_init)
    return pl.pallas_call(
        kern,
        grid=(b, nh, nq // tq, nk // tk),
        in_specs=[
            pl.BlockSpec((4, DA_HEAD_DIM), lambda bi, h, i, j: (0, 0)),
            pl.BlockSpec((1, HEAD_PAIR), lambda bi, h, i, j: (0, 0)),
            pl.BlockSpec((1, 1, tq, HEAD_PAIR), lambda bi, h, i, j: (bi, h, i, 0)),
            pl.BlockSpec((1, 1, tk, HEAD_PAIR), lambda bi, h, i, j: (bi, h, j, 0)),
            pl.BlockSpec((1, 1, tk, HEAD_PAIR), lambda bi, h, i, j: (bi, h, j, 0)),
        ],
        out_specs=pl.BlockSpec((1, tq, HEAD_PAIR), lambda bi, h, i, j: (bi, i, h)),
        out_shape=jax.ShapeDtypeStruct((b, nq, nh * HEAD_PAIR), BF16),
        scratch_shapes=[
            pltpu.VMEM((2 * tq, HEAD_PAIR), BF16),
            pltpu.VMEM((2 * tq, LANES), F32),
            pltpu.VMEM((2 * tq, LANES), F32),
            pltpu.VMEM((2 * tq, HEAD_PAIR), F32),
        ],
        compiler_params=_params(("parallel", "parallel", "parallel", "arbitrary")),
        name="diff_attn",
    )(lam_params, g, q, k, v)


NA_QROWS = 8
NA_QTOK = NA_QROWS * GRID_W
NA_KBLK = 4 * GRID_W
NA_PIECES = 4
NA_WIN = NA_PIECES * NA_KBLK


def _na_bias_tables(rpb, rows):
    n_h = rpb.shape[0]
    n_kj = NA_PIECES * 4
    pad_r = n_kj - NA_ROWS
    pad_c = GRID_W - NA_COLS
    rp = jnp.pad(rpb * math.log2(math.e), ((0, 0), (pad_r, pad_r), (pad_c, pad_c)))
    a = jnp.stack([rp[:, :, GRID_W - 1 - qc:2 * GRID_W - 1 - qc] for qc in range(GRID_W)], axis=2)
    bias = jnp.stack([a[:, 3 + pad_r - qi:3 + pad_r - qi + n_kj].transpose(0, 2, 1, 3) for qi in range(NA_QROWS)],
                     axis=1).reshape(n_h, NA_QTOK, NA_WIN)

    n_r = rows // NA_QROWS
    qi = np.arange(NA_QROWS)
    kj = np.arange(n_kj)
    c = np.arange(GRID_W)
    cstart = np.clip(c - NA_COLS // 2, 0, GRID_W - NA_COLS)
    col_ok = (c[None, :] >= cstart[:, None]) & (c[None, :] < cstart[:, None] + NA_COLS)
    row_ok = []
    for r_grp in (0, min(1, n_r - 1), n_r - 1):
        r = NA_QROWS * r_grp + qi
        rs = np.clip(r - NA_ROWS // 2, 0, rows - NA_ROWS)
        krow = NA_QROWS * r_grp - 4 + kj
        row_ok.append((krow[None, :] >= rs[:, None]) & (krow[None, :] < rs[:, None] + NA_ROWS)
                      & (krow[None, :] >= 0) & (krow[None, :] < rows))
    ok = jnp.asarray(np.stack(row_ok))[:, :, None, :, None] & jnp.asarray(col_ok)[None, None, :, None, :]
    mask = jnp.where(ok, 0.0, NEG_INF).astype(F32).reshape(3, NA_QTOK, NA_WIN)
    return bias, mask


def _pair_softmax_pv(qs, k_list, v_list, bias_list):
    n_rows = qs.shape[0]
    chunks = [slice(r0, r0 + DA_DOT_ROWS) for r0 in range(0, n_rows, DA_DOT_ROWS)]
    tiles_list = []
    for k, bias in zip(k_list, bias_list):
        s = jnp.concatenate([lax.dot_general(qs[c], k, _CONTRACT_LAST, preferred_element_type=F32)
                             for c in chunks], axis=0)
        s = s if bias is None else s + bias
        tiles_list.append([s[:, j * LANES:(j + 1) * LANES] for j in range(k.shape[0] // LANES)])
    all_tiles = [t for tiles in tiles_list for t in tiles]
    m = jnp.max(functools.reduce(jnp.maximum, all_tiles), axis=-1, keepdims=True)
    p_list = [[jnp.exp2(t - m) for t in tiles] for tiles in tiles_list]
    l = jnp.sum(functools.reduce(jnp.add, [t for tiles in p_list for t in tiles]), axis=-1, keepdims=True)
    o = None
    for tiles, v in zip(p_list, v_list):
        p = jnp.concatenate(tiles, axis=1).astype(BF16)
        pv = jnp.concatenate([jnp.dot(p[c], v, preferred_element_type=F32) for c in chunks], axis=0)
        o = pv if o is None else o + pv
    o = o / l
    t = qs.shape[0] // 2
    lo = lax.broadcasted_iota(jnp.int32, (t, HEAD_PAIR), 1) < (HEAD_PAIR // 2)
    return jnp.where(lo, o[:t], o[t:])


def _scaled_pair(q, head_dim):
    return _split_pair((q.astype(F32) * (head_dim ** -0.5 * math.log2(math.e))).astype(BF16))


def _na_kernel(q_ref, k0, k1, k2, k3, v0, v1, v2, v3, kc_ref, vc_ref, b_ref, mk_ref, o_ref):
    qs = _scaled_pair(q_ref[0], HEAD_PAIR // 2)
    k_win = jnp.concatenate([k0[0], k1[0], k2[0], k3[0]], axis=0)
    v_win = jnp.concatenate([v0[0], v1[0], v2[0], v3[0]], axis=0)
    bias = (b_ref[...] + mk_ref[...]).reshape(2 * NA_QTOK, NA_WIN)
    o = _pair_softmax_pv(qs, [k_win, kc_ref[0]], [v_win, vc_ref[0]], [bias, None])
    o_ref[0] = o.astype(o_ref.dtype)


def _na_attn(p, pc, bias, mask):
    b, n, _ = p.shape
    n_ctx = pc.shape[1]
    n_r = n // NA_QTOK
    n_kb = n // NA_KBLK
    n_hp = NA_HEADS // 2

    def case(r):
        return jnp.where(r == 0, 0, jnp.where(r == n_r - 1, 2, 1))

    def kv_spec(off, piece):
        return pl.BlockSpec(
            (1, NA_KBLK, HEAD_PAIR),
            lambda bi, hp, r: (bi, jnp.clip(2 * r - 1 + piece, 0, n_kb - 1), off // HEAD_PAIR + hp))

    in_specs = [pl.BlockSpec((1, NA_QTOK, HEAD_PAIR), lambda bi, hp, r: (bi, r, OFF_QC // HEAD_PAIR + hp))]
    in_specs += [kv_spec(OFF_KC, i) for i in range(NA_PIECES)]
    in_specs += [kv_spec(OFF_VC, i) for i in range(NA_PIECES)]
    in_specs += [
        pl.BlockSpec((1, n_ctx, HEAD_PAIR), lambda bi, hp, r: (bi, 0, OFF_KC // HEAD_PAIR + hp)),
        pl.BlockSpec((1, n_ctx, HEAD_PAIR), lambda bi, hp, r: (bi, 0, OFF_VC // HEAD_PAIR + hp)),
        pl.BlockSpec((2, NA_QTOK, NA_WIN), lambda bi, hp, r: (hp, 0, 0)),
        pl.BlockSpec((1, NA_QTOK, NA_WIN), lambda bi, hp, r: (case(r), 0, 0)),
    ]
    return pl.pallas_call(
        _na_kernel,
        grid=(b, n_hp, n_r),
        in_specs=in_specs,
        out_specs=pl.BlockSpec((1, NA_QTOK, HEAD_PAIR), lambda bi, hp, r: (bi, r, hp)),
        out_shape=jax.ShapeDtypeStruct((b, n, NA_HEADS * HEAD_PAIR // 2), BF16),
        compiler_params=_params(("parallel", "parallel", "arbitrary")),
        name="na_attn",
    )(p, *([p] * (2 * NA_PIECES)), pc, pc, bias, mask)


def _ctx_mha_kernel(q_ref, k_ref, v_ref, o_ref):
    qs = _scaled_pair(q_ref[0], HEAD_PAIR // 2)
    o_ref[0] = _pair_softmax_pv(qs, [k_ref[0]], [v_ref[0]], [None]).astype(o_ref.dtype)


def _ctx_mha(pc):
    b, n_ctx, _ = pc.shape
    n_hp = NA_HEADS // 2

    def spec(off):
        return pl.BlockSpec((1, n_ctx, HEAD_PAIR), lambda bi, hp: (bi, 0, off // HEAD_PAIR + hp))

    return pl.pallas_call(
        _ctx_mha_kernel,
        grid=(b, n_hp),
        in_specs=[spec(OFF_QC), spec(OFF_KC), spec(OFF_VC)],
        out_specs=pl.BlockSpec((1, n_ctx, HEAD_PAIR), lambda bi, hp: (bi, 0, hp)),
        out_shape=jax.ShapeDtypeStruct((b, n_ctx, NA_HEADS * HEAD_PAIR // 2), BF16),
        compiler_params=_params(("parallel", "parallel")),
        name="ctx_mha",
    )(pc, pc, pc)


def _gelu_tanh(x):
    return 0.5 * x * (1.0 + jnp.tanh(math.sqrt(2.0 / math.pi) * (x + 0.044715 * (x * x * x))))


SUBLANES = 8


def _store_token_tiles(ref, x):
    t = x.shape[0]
    for j in range(SUBLANES):
        ref[pl.ds(j, t, stride=SUBLANES), :] = x[:, j * LANES:(j + 1) * LANES]


def _load_token_tiles(ref, t):
    return jnp.concatenate([ref[pl.ds(j, t, stride=SUBLANES), :] for j in range(SUBLANES)], axis=1)


def _merge_kernel(h_ref, ya_ref, z_ref, yc_ref, g0_ref, g1_ref, g2_ref, wb_ref, wo_ref, sgw_ref, sgb_ref,
                  lng_ref, lnb_ref, gt_ref, gf_ref, sh_ref, sc_ref, ho_ref, xo_ref, *, tm):
    z = _gelu_tanh(z_ref[0].astype(F32))
    u = z[:, :SG_WIDTH]
    vv = z[:, SG_WIDTH:]
    mu = jnp.mean(vv, axis=-1, keepdims=True)
    var = jnp.mean(jnp.square(vv - mu), axis=-1, keepdims=True)
    vv = ((vv - mu) * lax.rsqrt(var + EPS)) * lng_ref[...] + lnb_ref[...]
    vv = vv.astype(BF16)
    gd = SG_WIDTH // SG_GROUPS
    chunks = []
    for c in range(tm // SG_CHUNK):
        rows = slice(c * SG_CHUNK, (c + 1) * SG_CHUNK)
        groups = []
        for g in range(SG_GROUPS):
            s = jnp.dot(sgw_ref[g], vv[rows, g * gd:(g + 1) * gd], preferred_element_type=F32) + sgb_ref[g]
            groups.append(s)
        chunks.append(jnp.concatenate(groups, axis=1))
    y_b = (u * jnp.concatenate(chunks, axis=0)).astype(BF16)

    merged = None
    for y, gate_ref, i in ((ya_ref[0], g0_ref, 0), (y_b, g1_ref, 1), (yc_ref[0], g2_ref, 2)):
        t = jax.nn.sigmoid(gate_ref[0].astype(F32)) * jnp.dot(y, wb_ref[i], preferred_element_type=F32)
        merged = t if merged is None else merged + t
    out = jnp.dot(merged.astype(BF16), wo_ref[...], preferred_element_type=F32)
    h_new = h_ref[0] + gt_ref[0] * out
    ho_ref[0] = h_new
    _store_token_tiles(xo_ref, _rms_mod(h_new, gf_ref[...], sh_ref[0], sc_ref[0]))


def _merge(h, y_a, p, y_c, wb, wo, sgw, sgb, lng, lnb, gt1, g_ffn, sh2, sc2, xn2_buf, xn2_rows, row_off, tm):
    b, n, d = h.shape
    n_i = n // tm
    off_blk = row_off // tm
    vec = pl.BlockSpec((1, 1, d), lambda bi, i: (bi, 0, 0))

    def col(width, off):
        return pl.BlockSpec((1, tm, width), lambda bi, i: (bi, i, off // width))

    kern = functools.partial(_merge_kernel, tm=tm)
    args = [h, y_a, p, y_c, p, p, p, wb, wo, sgw, sgb, lng, lnb, gt1, g_ffn, sh2, sc2]
    in_specs = [
        pl.BlockSpec((1, tm, d), lambda bi, i: (bi, i, 0)),
        col(BRANCH_WIDTH, 0),
        col(2 * SG_WIDTH, OFF_ZB),
        col(BRANCH_WIDTH, 0),
        col(d, OFF_GATE), col(d, OFF_GATE + d), col(d, OFF_GATE + 2 * d),
        pl.BlockSpec((N_BRANCH, BRANCH_WIDTH, d), lambda bi, i: (0, 0, 0)),
        pl.BlockSpec((d, d), lambda bi, i: (0, 0)),
        pl.BlockSpec((SG_GROUPS, SG_CHUNK, SG_CHUNK), lambda bi, i: (0, 0, 0)),
        pl.BlockSpec((SG_GROUPS, SG_CHUNK, SG_CHUNK), lambda bi, i: (0, 0, 0)),
        pl.BlockSpec((1, SG_WIDTH), lambda bi, i: (0, 0)),
        pl.BlockSpec((1, SG_WIDTH), lambda bi, i: (0, 0)),
        vec,
        pl.BlockSpec((1, d), lambda bi, i: (0, 0)),
        vec, vec,
    ]
    aliases = {}
    if xn2_buf is not None:
        args.append(xn2_buf)
        in_specs.append(pl.BlockSpec(memory_space=pl.ANY))
        aliases = {len(args) - 1: 1}
    assert d == SUBLANES * LANES
    xn2_shape = jax.ShapeDtypeStruct((xn2_rows * SUBLANES, LANES), F32)

    def body(*refs):
        n_in = 17
        kern(*refs[:n_in], *refs[len(refs) - 2:])

    return pl.pallas_call(
        body,
        grid=(b, n_i),
        in_specs=in_specs,
        out_specs=[
            pl.BlockSpec((1, tm, d), lambda bi, i: (bi, i, 0)),
            pl.BlockSpec((tm * SUBLANES, LANES), lambda bi, i: (off_blk + bi * n_i + i, 0)),
        ],
        out_shape=[jax.ShapeDtypeStruct((b, n, d), F32), xn2_shape],
        input_output_aliases=aliases,
        compiler_params=_params(("parallel", "parallel")),
        name="merge_branches",
    )(*args)


R_E1, R_E2, R_W1, R_W2, R_RANK1, R_RANK2 = range(6)


def _router_kernel(x_ref, w_ref, b_ref, o_ref, cnt_ref, run_ref, *, tm):
    @pl.when(pl.program_id(0) == 0)
    def _():
        run_ref[...] = jnp.zeros(run_ref.shape, F32)

    logits = jnp.dot(_load_token_tiles(x_ref, tm), w_ref[...], preferred_element_type=F32,
                     precision=lax.Precision.HIGHEST) + b_ref[...]
    lane = lax.broadcasted_iota(jnp.int32, logits.shape, 1)
    lane_f = lane.astype(F32)
    far = jnp.float32(1e9)

    def first_lane(mask):
        return jnp.min(jnp.where(mask, lane_f, far), axis=-1, keepdims=True)

    is_g = lane < N_GROUPS
    gl = jnp.where(is_g, logits, NEG_INF)
    g_max = jnp.max(gl, axis=-1, keepdims=True)
    g_idx = first_lane(is_g & (gl == g_max))
    g_sum = jnp.sum(jnp.where(is_g, jnp.exp(gl - g_max), 0.0), axis=-1, keepdims=True)
    g_w = 1.0 / g_sum

    e_lane = lane - N_GROUPS
    in_grp = (e_lane >= 0) & (e_lane < N_EXPERTS) & \
        (jnp.right_shift(e_lane, 3).astype(F32) == g_idx)
    el = jnp.where(in_grp, logits, NEG_INF)
    l1 = jnp.max(el, axis=-1, keepdims=True)
    i1 = first_lane(in_grp & (el == l1))
    rest = in_grp & (lane_f != i1)
    el2 = jnp.where(rest, logits, NEG_INF)
    l2 = jnp.max(el2, axis=-1, keepdims=True)
    i2 = first_lane(rest & (el2 == l2))
    t = jnp.exp(l2 - l1)
    w1 = g_w / (1.0 + t)
    w2 = g_w * t / (1.0 + t)

    oh1 = lane_f == i1
    oh2 = lane_f == i2
    oh = jnp.where(oh1 | oh2, 1.0, 0.0)
    row = lax.broadcasted_iota(jnp.int32, (tm, tm), 0)
    colm = lax.broadcasted_iota(jnp.int32, (tm, tm), 1)
    before = jnp.where(colm < row, 1.0, 0.0).astype(BF16)
    prior = jnp.dot(before, oh.astype(BF16), preferred_element_type=F32) + run_ref[...]
    rank1 = jnp.sum(jnp.where(oh1, prior, 0.0), axis=-1, keepdims=True)
    rank2 = jnp.sum(jnp.where(oh2, prior, 0.0), axis=-1, keepdims=True)
    run_new = run_ref[...] + jnp.sum(oh, axis=0, keepdims=True)
    run_ref[...] = run_new
    cnt_ref[...] = run_new

    slab = jnp.zeros(logits.shape, F32)
    for ln, val in ((R_E1, i1 - N_GROUPS), (R_E2, i2 - N_GROUPS), (R_W1, w1), (R_W2, w2),
                    (R_RANK1, rank1), (R_RANK2, rank2)):
        slab = jnp.where(lane == ln, val, slab)
    o_ref[...] = slab


def _router(xn2, w_rt, b_rt):
    t_tok = xn2.shape[0] // SUBLANES
    d = w_rt.shape[0]
    tm = 512
    kern = functools.partial(_router_kernel, tm=tm)
    return pl.pallas_call(
        kern,
        grid=(t_tok // tm,),
        in_specs=[
            pl.BlockSpec((tm * SUBLANES, LANES), lambda i: (i, 0)),
            pl.BlockSpec((d, LANES), lambda i: (0, 0)),
            pl.BlockSpec((1, LANES), lambda i: (0, 0)),
        ],
        out_specs=[pl.BlockSpec((tm, LANES), lambda i: (i, 0)), pl.BlockSpec((1, LANES), lambda i: (0, 0))],
        out_shape=[jax.ShapeDtypeStruct((t_tok, LANES), F32), jax.ShapeDtypeStruct((1, LANES), F32)],
        scratch_shapes=[pltpu.VMEM((1, LANES), F32)],
        compiler_params=_params(("arbitrary",)),
        name="moe_router",
    )(xn2, w_rt, b_rt)


def _tile_rows(index):
    return pl.ds(pl.multiple_of(index * SUBLANES, SUBLANES), SUBLANES)


def _dispatch_kernel(dest_ref, x_ref, init_ref, xs_ref, sem):
    del init_ref
    step = pl.program_id(0)

    def issue(i, carry):
        tok = step * DMA_CHUNK + i
        for k in range(2):
            pltpu.make_async_copy(x_ref.at[_tile_rows(i), :], xs_ref.at[_tile_rows(dest_ref[2 * tok + k]), :],
                                  sem.at[0]).start()
        return carry

    lax.fori_loop(0, DMA_CHUNK, issue, 0)
    for _ in range(2):
        pltpu.make_async_copy(x_ref, xs_ref.at[pl.ds(0, DMA_CHUNK * SUBLANES), :], sem.at[0]).wait()


def _dispatch(dest, xn2, n_slots):
    t_tok = xn2.shape[0] // SUBLANES
    grid_spec = pltpu.PrefetchScalarGridSpec(
        num_scalar_prefetch=1,
        grid=(t_tok // DMA_CHUNK,),
        in_specs=[pl.BlockSpec((DMA_CHUNK * SUBLANES, LANES), lambda i, dest: (i, 0)),
                  pl.BlockSpec(memory_space=pl.ANY)],
        out_specs=pl.BlockSpec(memory_space=pl.ANY),
        scratch_shapes=[pltpu.SemaphoreType.DMA((1,))],
    )
    return pl.pallas_call(
        _dispatch_kernel,
        grid_spec=grid_spec,
        out_shape=jax.ShapeDtypeStruct((n_slots * SUBLANES, LANES), F32),
        input_output_aliases={2: 0},
        compiler_params=_params(("arbitrary",)),
        name="moe_dispatch",
    )(dest, xn2, jnp.zeros((n_slots * SUBLANES, LANES), F32))


def _expert_kernel(be_ref, nu_ref, x_ref, wg_ref, wu_ref, wd_ref, o_ref):
    i = pl.program_id(0)

    @pl.when(i < nu_ref[0])
    def _():
        x = _load_token_tiles(x_ref, MOE_ROWS).astype(BF16)
        gate = jnp.dot(x, wg_ref[0].astype(BF16), preferred_element_type=F32)
        up = jnp.dot(x, wu_ref[0].astype(BF16), preferred_element_type=F32)
        hdn = (gate * jax.nn.sigmoid(gate)) * up
        _store_token_tiles(o_ref, jnp.dot(hdn.astype(BF16), wd_ref[0].astype(BF16), preferred_element_type=F32))

    @pl.when(i >= nu_ref[0])
    def _():
        o_ref[...] = jnp.zeros(o_ref.shape, F32)


def _experts(block_expert, n_used, xs, layer, w_gate, w_up, w_down):
    n_blocks = xs.shape[0] // (MOE_ROWS * SUBLANES)
    n_layers, n_e, d, de = w_gate.shape
    w_gate, w_up, w_down = (w.reshape(n_layers * n_e, *w.shape[2:]) for w in (w_gate, w_up, w_down))
    blk = pl.BlockSpec((MOE_ROWS * SUBLANES, LANES), lambda i, be, nu: (i, 0))
    grid_spec = pltpu.PrefetchScalarGridSpec(
        num_scalar_prefetch=2,
        grid=(n_blocks,),
        in_specs=[
            blk,
            pl.BlockSpec((1, d, de), lambda i, be, nu: (layer * n_e + be[i], 0, 0)),
            pl.BlockSpec((1, d, de), lambda i, be, nu: (layer * n_e + be[i], 0, 0)),
            pl.BlockSpec((1, de, d), lambda i, be, nu: (layer * n_e + be[i], 0, 0)),
        ],
        out_specs=blk,
    )
    return pl.pallas_call(
        _expert_kernel,
        grid_spec=grid_spec,
        out_shape=jax.ShapeDtypeStruct(xs.shape, F32),
        compiler_params=_params(("arbitrary",)),
        name="moe_experts",
    )(block_expert, n_used, xs, w_gate, w_up, w_down)


def _residual_kernel(dest_ref, h_ref, r_ref, gt_ref, gf_ref, ys_ref, o_ref, buf_ref, sem, *, tm, n_i, row_off, final):
    base = row_off + (pl.program_id(0) * n_i + pl.program_id(1)) * tm

    def issue(t, carry):
        for k in range(2):
            pltpu.make_async_copy(ys_ref.at[_tile_rows(dest_ref[2 * (base + t) + k]), :],
                                  buf_ref.at[k, _tile_rows(t), :], sem.at[0]).start()
        return carry

    lax.fori_loop(0, tm, issue, 0)
    for k in range(2):
        pltpu.make_async_copy(ys_ref.at[pl.ds(0, tm * SUBLANES), :], buf_ref.at[k], sem.at[0]).wait()

    r = r_ref[...]
    y = (_load_token_tiles(buf_ref.at[0], tm) * r[:, R_W1:R_W1 + 1]
         + _load_token_tiles(buf_ref.at[1], tm) * r[:, R_W2:R_W2 + 1])
    h_new = h_ref[0] + gt_ref[0] * y
    if final:
        h_new = (h_new * lax.rsqrt(jnp.mean(h_new * h_new, axis=-1, keepdims=True) + EPS)) * gf_ref[...]
    o_ref[0] = h_new


def _residual(dest, h, ys, route, gt2, g_final, row_off, final):
    b, n, d = h.shape
    tm = 256
    n_i = n // tm
    off_blk = row_off // tm
    kern = functools.partial(_residual_kernel, tm=tm, n_i=n_i, row_off=row_off, final=final)
    grid_spec = pltpu.PrefetchScalarGridSpec(
        num_scalar_prefetch=1,
        grid=(b, n_i),
        in_specs=[
            pl.BlockSpec((1, tm, d), lambda bi, i, dest: (bi, i, 0)),
            pl.BlockSpec((tm, LANES), lambda bi, i, dest: (off_blk + bi * n_i + i, 0)),
            pl.BlockSpec((1, 1, d), lambda bi, i, dest: (bi, 0, 0)),
            pl.BlockSpec((1, d), lambda bi, i, dest: (0, 0)),
            pl.BlockSpec(memory_space=pl.ANY),
        ],
        out_specs=pl.BlockSpec((1, tm, d), lambda bi, i, dest: (bi, i, 0)),
        scratch_shapes=[pltpu.VMEM((2, tm * SUBLANES, LANES), F32), pltpu.SemaphoreType.DMA((1,))],
    )
    return pl.pallas_call(
        kern,
        grid_spec=grid_spec,
        out_shape=jax.ShapeDtypeStruct((b, n, d), F32),
        compiler_params=_params(("arbitrary", "arbitrary")),
        name="moe_residual",
    )(dest, h, route, gt2, g_final, ys)


def _moe(xn2, w_rt, b_rt, layer, w_gate, w_up, w_down):
    t_tok = xn2.shape[0] // SUBLANES
    route, counts = _router(xn2, w_rt, b_rt)
    cnt = counts[0, N_GROUPS:N_GROUPS + N_EXPERTS].astype(jnp.int32)
    padded = (cnt + MOE_ROWS - 1) // MOE_ROWS * MOE_ROWS
    pad_end = jnp.cumsum(padded)
    pad_start = pad_end - padded
    expert = route[:, R_E1:R_E2 + 1].astype(jnp.int32)
    rank = route[:, R_RANK1:R_RANK2 + 1].astype(jnp.int32)
    dest = (pad_start[expert] + rank).reshape(2 * t_tok)
    n_blocks = -(-(2 * t_tok + N_EXPERTS * (MOE_ROWS - 1)) // MOE_ROWS)
    n_slots = n_blocks * MOE_ROWS
    starts = jnp.arange(n_blocks, dtype=jnp.int32) * MOE_ROWS
    block_expert = jnp.minimum(jnp.sum(starts[:, None] >= pad_end[None, :], axis=1), N_EXPERTS - 1).astype(jnp.int32)
    n_used = (pad_end[-1:] // MOE_ROWS).astype(jnp.int32)

    xs = _dispatch(dest, xn2, n_slots)
    return route, dest, _experts(block_expert, n_used, xs, layer, w_gate, w_up, w_down)


def kernel(x, c, ctx, c_ctx, w_ada, b_ada, g_norm_mix, g_norm_ffn, w_in, da_lambda, da_subln_g, sg_ln_g, sg_ln_b, sg_w, sg_b, na_rpb, w_branch, w_out, moe_w_group, moe_b_group, moe_w_router, moe_b_router, moe_w_gate, moe_w_up, moe_w_down, g_final):
    b, n_lat, d = x.shape
    n_ctx = ctx.shape[1]
    depth = w_in.shape[0]
    rows = n_lat // GRID_W
    assert d == D_MODEL and n_lat % NA_QTOK == 0 and rows >= 2 * NA_QROWS and n_ctx % 256 == 0 and b <= 7
    tm_lat = 1024 if n_lat % 1024 == 0 else 512
    tm_mrg = 512
    tm_ctx = 256

    cos, sin = _rope_tables(n_lat)
    cond = jnp.zeros((8, d), F32).at[:b].set(c).at[b].set(c_ctx)
    mods = _ada(cond, w_ada, b_ada.reshape(depth, 1, 6 * d))

    h, hc = x, ctx
    for l in range(depth):
        last = l == depth - 1
        lam_init = 0.8 - 0.6 * math.exp(-0.3 * l)
        m_lat = mods[l, :b].reshape(b, 1, 6, d)
        m_ctx = jnp.broadcast_to(mods[l, b].reshape(1, 1, 6, d), (b, 1, 6, d))
        sh1, sc1, gt1, sh2, sc2, gt2 = (m_lat[:, :, i] for i in range(6))
        csh1, csc1, cgt1, csh2, csc2, cgt2 = (m_ctx[:, :, i] for i in range(6))

        w_in_l = w_in[l].astype(BF16)
        g_mix = g_norm_mix[l].reshape(1, d)
        g_ffn = g_norm_ffn[l].reshape(1, d)
        p = _norm_proj(h, g_mix, sh1, sc1, w_in_l, IN_COLS, tm_lat)
        pc = _norm_proj(hc, g_mix, csh1, csc1, w_in_l, KV_COLS if last else IN_COLS, tm_ctx)

        q_hm, k_hm, v_hm = _rope(p, cos, sin, tm_mrg)
        kc_hm = _heads_major(pc[..., OFF_KA:OFF_VA])
        vc_hm = _heads_major(pc[..., OFF_VA:OFF_KC])
        k_all = jnp.concatenate([kc_hm, k_hm], axis=2)
        v_all = jnp.concatenate([vc_hm, v_hm], axis=2)
        g_sub = da_subln_g[l].reshape(1, 2 * DA_HEAD_DIM)
        y_a = _diff_attn(q_hm, k_all, v_all, da_lambda[l], g_sub, lam_init)

        y_c = _na_attn(p, pc, *_na_bias_tables(na_rpb[l], rows))

        wb = w_branch[l].astype(BF16)
        wo = w_out[l].astype(BF16)
        sgw = sg_w[l].astype(BF16)
        sgb = jnp.broadcast_to(sg_b[l][:, :, None], (SG_GROUPS, SG_CHUNK, SG_CHUNK))
        lng = sg_ln_g[l].reshape(1, SG_WIDTH)
        lnb = sg_ln_b[l].reshape(1, SG_WIDTH)
        t_lat = b * n_lat
        t_tok = t_lat if last else t_lat + b * n_ctx
        xn2_buf = None if last else jnp.zeros((t_tok * SUBLANES, LANES), F32)
        h, xn2 = _merge(h, y_a, p, y_c, wb, wo, sgw, sgb, lng, lnb, gt1, g_ffn, sh2, sc2, xn2_buf, t_tok, 0, tm_mrg)
        if not last:
            qc_hm = (_heads_major(pc[..., OFF_QA:OFF_QC]).astype(F32) * DA_Q_SCALE).astype(BF16)
            ya_c = _diff_attn(qc_hm, kc_hm, vc_hm, da_lambda[l], g_sub, lam_init)
            yc_c = _ctx_mha(pc)
            hc, xn2 = _merge(hc, ya_c, pc, yc_c, wb, wo, sgw, sgb, lng, lnb, cgt1, g_ffn, csh2, csc2,
                             xn2, t_tok, t_lat, tm_ctx)

        w_rt = jnp.zeros((d, LANES), F32).at[:, :N_GROUPS].set(moe_w_group[l]) \
            .at[:, N_GROUPS:N_GROUPS + N_EXPERTS].set(moe_w_router[l])
        b_rt = jnp.zeros((1, LANES), F32).at[0, :N_GROUPS].set(moe_b_group[l]) \
            .at[0, N_GROUPS:N_GROUPS + N_EXPERTS].set(moe_b_router[l])
        route, dest, ys = _moe(xn2, w_rt, b_rt, l, moe_w_gate, moe_w_up, moe_w_down)
        h = _residual(dest, h, ys, route, gt2, g_final.reshape(1, d), 0, last)
        if not last:
            hc = _residual(dest, hc, ys, route, cgt2, g_final.reshape(1, d), t_lat, False)
    return h
```

```python
import functools
import math

import numpy as np
import jax
import jax.numpy as jnp
from jax import lax
from jax.experimental import pallas as pl
from jax.experimental.pallas import tpu as pltpu

F32 = jnp.float32
BF16 = jnp.bfloat16

D_MODEL = 1024
GRID_W = 64
EPS = 1e-6
NEG_INF = -1e30
ROPE_THETA = 10000.0

DA_HEADS = 4
DA_HEAD_DIM = 64
NA_HEADS = 8
NA_ROWS = 8
NA_COLS = 16
SG_CHUNK = 128
SG_GROUPS = 4
SG_WIDTH = 512
BRANCH_WIDTH = 512
N_BRANCH = 3

OFF_KA = 0
OFF_VA = 512
OFF_KC = 1024
OFF_VC = 1536
KV_COLS = 2048
OFF_QA = 2048
OFF_QC = 2560
OFF_ZB = 3072
OFF_GATE = 4096
IN_COLS = 7168

N_GROUPS = 4
EXPERTS_PER_GROUP = 8
N_EXPERTS = 32
D_EXPERT = 512

LANES = 128
HEAD_PAIR = LANES
VMEM_LIMIT = 56 * 1024 * 1024

MOE_ROWS = 256
DMA_CHUNK = 512

_CONTRACT_LAST = (((1,), (1,)), ((), ()))


def _params(sem, vmem=VMEM_LIMIT, flags=None):
    return pltpu.CompilerParams(dimension_semantics=sem, vmem_limit_bytes=vmem, flags=flags)


DOT_ROWS = 256


def _dot_rows(a, b, contract_last=False):
    dims = _CONTRACT_LAST if contract_last else (((1,), (0,)), ((), ()))
    n = a.shape[0]
    if n <= DOT_ROWS:
        return lax.dot_general(a, b, dims, preferred_element_type=F32)
    return jnp.concatenate([lax.dot_general(a[r0:r0 + DOT_ROWS], b, dims, preferred_element_type=F32)
                            for r0 in range(0, n, DOT_ROWS)], axis=0)


def _ada_kernel(cond_ref, w_ref, b_ref, o_ref):
    c = cond_ref[...]
    c = c * jax.nn.sigmoid(c)
    o_ref[0] = jnp.dot(c, w_ref[0], preferred_element_type=F32, precision=lax.Precision.HIGHEST) + b_ref[0]


def _ada(cond, w_ada, b_ada):
    n_layers, d, d6 = w_ada.shape
    tn = 1024
    return pl.pallas_call(
        _ada_kernel,
        grid=(n_layers, d6 // tn),
        in_specs=[
            pl.BlockSpec((8, d), lambda l, j: (0, 0)),
            pl.BlockSpec((1, d, tn), lambda l, j: (l, 0, j)),
            pl.BlockSpec((1, 1, tn), lambda l, j: (l, 0, j)),
        ],
        out_specs=pl.BlockSpec((1, 8, tn), lambda l, j: (l, 0, j)),
        out_shape=jax.ShapeDtypeStruct((n_layers, 8, d6), F32),
        compiler_params=_params(("parallel", "parallel")),
        name="ada_mod",
    )(cond, w_ada, b_ada)


def _rms_mod(x, g, shift, scale):
    y = x * lax.rsqrt(jnp.mean(x * x, axis=-1, keepdims=True) + EPS)
    return (y * g) * (1.0 + scale) + shift


def _norm_proj_kernel(h_ref, g_ref, sh_ref, sc_ref, w_ref, o_ref, xn_ref):
    @pl.when(pl.program_id(2) == 0)
    def _():
        xn_ref[...] = _rms_mod(h_ref[0], g_ref[...], sh_ref[0], sc_ref[0]).astype(BF16)

    o_ref[0] = _dot_rows(xn_ref[...], w_ref[...]).astype(o_ref.dtype)


def _norm_proj(h, g, shift, scale, w, n_cols, tm):
    b, n, d = h.shape
    tn = 1024
    return pl.pallas_call(
        _norm_proj_kernel,
        grid=(b, n // tm, n_cols // tn),
        in_specs=[
            pl.BlockSpec((1, tm, d), lambda bi, i, j: (bi, i, 0)),
            pl.BlockSpec((1, d), lambda bi, i, j: (0, 0)),
            pl.BlockSpec((1, 1, d), lambda bi, i, j: (bi, 0, 0)),
            pl.BlockSpec((1, 1, d), lambda bi, i, j: (bi, 0, 0)),
            pl.BlockSpec((d, tn), lambda bi, i, j: (0, j)),
        ],
        out_specs=pl.BlockSpec((1, tm, tn), lambda bi, i, j: (bi, i, j)),
        out_shape=jax.ShapeDtypeStruct((b, n, n_cols), BF16),
        scratch_shapes=[pltpu.VMEM((tm, d), BF16)],
        compiler_params=_params(("parallel", "parallel", "arbitrary")),
        name="norm_proj",
    )(h, g, shift, scale, w)


def _rope_tables(n_tok):
    t = jnp.arange(n_tok, dtype=jnp.int32)
    row = (t // GRID_W).astype(F32)
    col = (t % GRID_W).astype(F32)
    half = DA_HEAD_DIM // 4
    inv = ROPE_THETA ** (-jnp.arange(half, dtype=F32) / half)
    ar = row[:, None] * inv
    ac = col[:, None] * inv
    ang = jnp.concatenate([ar, ar, ac, ac], axis=-1)
    sign = np.tile(np.concatenate([-np.ones(half), np.ones(half)]), 2).astype(np.float32)
    cos = jnp.cos(ang)
    sin = jnp.sin(ang) * sign
    return jnp.concatenate([cos, cos], axis=-1), jnp.concatenate([sin, sin], axis=-1)


DA_Q_SCALE = DA_HEAD_DIM ** -0.5 * math.log2(math.e)


def _rope_kernel(q_ref, k_ref, v_ref, cos_ref, sin_ref, qo_ref, ko_ref, vo_ref):
    cos = cos_ref[...]
    sin = sin_ref[...]
    lane = lax.broadcasted_iota(jnp.int32, cos.shape, 1)
    first = (lane % (DA_HEAD_DIM // 2)) < (DA_HEAD_DIM // 4)
    seg = DA_HEAD_DIM // 4

    def rope(x):
        partner = jnp.where(first, pltpu.roll(x, LANES - seg, 1), pltpu.roll(x, seg, 1))
        return x * cos + partner * sin

    for hd in range(DA_HEADS):
        sl = slice(hd * HEAD_PAIR, (hd + 1) * HEAD_PAIR)
        q = q_ref[0, :, sl].astype(F32)
        k = k_ref[0, :, sl].astype(F32)
        qo_ref[0, hd] = (rope(q) * DA_Q_SCALE).astype(BF16)
        ko_ref[0, hd] = rope(k).astype(BF16)
        vo_ref[0, hd] = v_ref[0, :, sl]


def _rope(p, cos, sin, tm):
    b, n, _ = p.shape
    w = DA_HEADS * HEAD_PAIR
    hm = jax.ShapeDtypeStruct((b, DA_HEADS, n, HEAD_PAIR), BF16)
    hm_spec = pl.BlockSpec((1, DA_HEADS, tm, HEAD_PAIR), lambda bi, i: (bi, 0, i, 0))
    return pl.pallas_call(
        _rope_kernel,
        grid=(b, n // tm),
        in_specs=[
            pl.BlockSpec((1, tm, w), lambda bi, i: (bi, i, OFF_QA // w)),
            pl.BlockSpec((1, tm, w), lambda bi, i: (bi, i, OFF_KA // w)),
            pl.BlockSpec((1, tm, w), lambda bi, i: (bi, i, OFF_VA // w)),
            pl.BlockSpec((tm, LANES), lambda bi, i: (i, 0)),
            pl.BlockSpec((tm, LANES), lambda bi, i: (i, 0)),
        ],
        out_specs=[hm_spec, hm_spec, hm_spec],
        out_shape=[hm, hm, hm],
        compiler_params=_params(("parallel", "parallel")),
        name="rope_heads",
    )(p, p, p, cos, sin)


def _heads_major(t):
    b, n, _ = t.shape
    return t.reshape(b, n, DA_HEADS, HEAD_PAIR).transpose(0, 2, 1, 3)


def _split_pair(q):
    lo = lax.broadcasted_iota(jnp.int32, q.shape, 1) < (HEAD_PAIR // 2)
    zero = jnp.zeros_like(q)
    return jnp.concatenate([jnp.where(lo, q, zero), jnp.where(lo, zero, q)], axis=0)


DA_DOT_ROWS = 256


def _diff_attn_kernel(lam_ref, g_ref, q_ref, k_ref, v_ref, o_ref, qs_ref, m_ref, l_ref, acc_ref, *, tq, lam_init):
    ki = pl.program_id(3)

    @pl.when(ki == 0)
    def _():
        qs_ref[...] = _split_pair(q_ref[0, 0])
        m_ref[...] = jnp.full(m_ref.shape, NEG_INF, F32)
        l_ref[...] = jnp.zeros(l_ref.shape, F32)
        acc_ref[...] = jnp.zeros(acc_ref.shape, F32)

    k = k_ref[0, 0]
    v = v_ref[0, 0]
    n_lt = k.shape[0] // LANES
    chunks = [slice(r0, r0 + DA_DOT_ROWS) for r0 in range(0, 2 * tq, DA_DOT_ROWS)]
    s = jnp.concatenate([lax.dot_general(qs_ref[c], k, _CONTRACT_LAST, preferred_element_type=F32)
                         for c in chunks], axis=0)
    tiles = [s[:, j * LANES:(j + 1) * LANES] for j in range(n_lt)]
    m_prev = m_ref[...]
    m_new = jnp.maximum(m_prev, jnp.max(functools.reduce(jnp.maximum, tiles), axis=-1, keepdims=True))
    alpha = jnp.exp2(m_prev - m_new)
    p_tiles = [jnp.exp2(t - m_new) for t in tiles]
    l_ref[...] = alpha * l_ref[...] + functools.reduce(jnp.add, p_tiles)
    p = jnp.concatenate(p_tiles, axis=1).astype(BF16)
    pv = jnp.concatenate([jnp.dot(p[c], v, preferred_element_type=F32) for c in chunks], axis=0)
    acc_ref[...] = alpha * acc_ref[...] + pv
    m_ref[...] = m_new

    @pl.when(ki == pl.num_programs(3) - 1)
    def _():
        o = acc_ref[...] / jnp.sum(l_ref[...], axis=-1, keepdims=True)
        lm = lam_ref[...]
        lam = (jnp.exp(jnp.sum(lm[0:1] * lm[1:2], axis=-1, keepdims=True))
               - jnp.exp(jnp.sum(lm[2:3] * lm[3:4], axis=-1, keepdims=True)) + lam_init)
        d = o[:tq] - lam * o[tq:]
        y = d * lax.rsqrt(jnp.mean(d * d, axis=-1, keepdims=True) + EPS)
        o_ref[0] = ((y * g_ref[...]) * (1.0 - lam_init)).astype(o_ref.dtype)


def _pick(n, options):
    for o in options:
        if n % o == 0:
            return o
    raise ValueError(f"no tile in {options} divides {n}")


def _diff_attn(q, k, v, lam_params, g, lam_init):
    b, nh, nq, _ = q.shape
    nk = k.shape[2]
    tq = _pick(nq, (512, 256))
    tk = _pick(nk, (2816, 1408, 768, 512, 256))
    kern = functools.partial(_diff_attn_kernel, tq=tq, lam_init=lam_init)
    return pl.pallas_call(
        kern,
        grid=(b, nh, nq // tq, nk // tk),
        in_specs=[
            pl.BlockSpec((4, DA_HEAD_DIM), lambda bi, h, i, j: (0, 0)),
            pl.BlockSpec((1, HEAD_PAIR), lambda bi, h, i, j: (0, 0)),
            pl.BlockSpec((1, 1, tq, HEAD_PAIR), lambda bi, h, i, j: (bi, h, i, 0)),
            pl.BlockSpec((1, 1, tk, HEAD_PAIR), lambda bi, h, i, j: (bi, h, j, 0)),
            pl.BlockSpec((1, 1, tk, HEAD_PAIR), lambda bi, h, i, j: (bi, h, j, 0)),
        ],
        out_specs=pl.BlockSpec((1, tq, HEAD_PAIR), lambda bi, h, i, j: (bi, i, h)),
        out_shape=jax.ShapeDtypeStruct((b, nq, nh * HEAD_PAIR), BF16),
        scratch_shapes=[
            pltpu.VMEM((2 * tq, HEAD_PAIR), BF16),
            pltpu.VMEM((2 * tq, LANES), F32),
            pltpu.VMEM((2 * tq, LANES), F32),
            pltpu.VMEM((2 * tq, HEAD_PAIR), F32),
        ],
        compiler_params=_params(("parallel", "parallel", "parallel", "arbitrary")),
        name="diff_attn",
    )(lam_params, g, q, k, v)


NA_QROWS = 8
NA_QTOK = NA_QROWS * GRID_W
NA_KBLK = 4 * GRID_W
NA_PIECES = 4
NA_WIN = NA_PIECES * NA_KBLK


def _na_bias_tables(rpb, rows):
    n_h = rpb.shape[0]
    n_kj = NA_PIECES * 4
    pad_r = n_kj - NA_ROWS
    pad_c = GRID_W - NA_COLS
    rp = jnp.pad(rpb * math.log2(math.e), ((0, 0), (pad_r, pad_r), (pad_c, pad_c)))
    a = jnp.stack([rp[:, :, GRID_W - 1 - qc:2 * GRID_W - 1 - qc] for qc in range(GRID_W)], axis=2)
    bias = jnp.stack([a[:, 3 + pad_r - qi:3 + pad_r - qi + n_kj].transpose(0, 2, 1, 3) for qi in range(NA_QROWS)],
                     axis=1).reshape(n_h, NA_QTOK, NA_WIN)

    n_r = rows // NA_QROWS
    qi = np.arange(NA_QROWS)
    kj = np.arange(n_kj)
    c = np.arange(GRID_W)
    cstart = np.clip(c - NA_COLS // 2, 0, GRID_W - NA_COLS)
    col_ok = (c[None, :] >= cstart[:, None]) & (c[None, :] < cstart[:, None] + NA_COLS)
    row_ok = []
    for r_grp in (0, min(1, n_r - 1), n_r - 1):
        r = NA_QROWS * r_grp + qi
        rs = np.clip(r - NA_ROWS // 2, 0, rows - NA_ROWS)
        krow = NA_QROWS * r_grp - 4 + kj
        row_ok.append((krow[None, :] >= rs[:, None]) & (krow[None, :] < rs[:, None] + NA_ROWS)
                      & (krow[None, :] >= 0) & (krow[None, :] < rows))
    ok = jnp.asarray(np.stack(row_ok))[:, :, None, :, None] & jnp.asarray(col_ok)[None, None, :, None, :]
    mask = jnp.where(ok, 0.0, NEG_INF).astype(F32).reshape(3, NA_QTOK, NA_WIN)
    return bias, mask


def _pair_softmax_pv(qs, k_list, v_list, bias_list):
    n_rows = qs.shape[0]
    chunks = [slice(r0, r0 + DA_DOT_ROWS) for r0 in range(0, n_rows, DA_DOT_ROWS)]
    tiles_list = []
    for k, bias in zip(k_list, bias_list):
        s = jnp.concatenate([lax.dot_general(qs[c], k, _CONTRACT_LAST, preferred_element_type=F32)
                             for c in chunks], axis=0)
        s = s if bias is None else s + bias
        tiles_list.append([s[:, j * LANES:(j + 1) * LANES] for j in range(k.shape[0] // LANES)])
    all_tiles = [t for tiles in tiles_list for t in tiles]
    m = jnp.max(functools.reduce(jnp.maximum, all_tiles), axis=-1, keepdims=True)
    p_list = [[jnp.exp2(t - m) for t in tiles] for tiles in tiles_list]
    l = jnp.sum(functools.reduce(jnp.add, [t for tiles in p_list for t in tiles]), axis=-1, keepdims=True)
    o = None
    for tiles, v in zip(p_list, v_list):
        p = jnp.concatenate(tiles, axis=1).astype(BF16)
        pv = jnp.concatenate([jnp.dot(p[c], v, preferred_element_type=F32) for c in chunks], axis=0)
        o = pv if o is None else o + pv
    o = o / l
    t = qs.shape[0] // 2
    lo = lax.broadcasted_iota(jnp.int32, (t, HEAD_PAIR), 1) < (HEAD_PAIR // 2)
    return jnp.where(lo, o[:t], o[t:])


def _scaled_pair(q, head_dim):
    return _split_pair((q.astype(F32) * (head_dim ** -0.5 * math.log2(math.e))).astype(BF16))


def _na_kernel(q_ref, k0, k1, k2, k3, v0, v1, v2, v3, kc_ref, vc_ref, b_ref, mk_ref, o_ref):
    qs = _scaled_pair(q_ref[0], HEAD_PAIR // 2)
    k_win = jnp.concatenate([k0[0], k1[0], k2[0], k3[0]], axis=0)
    v_win = jnp.concatenate([v0[0], v1[0], v2[0], v3[0]], axis=0)
    bias = (b_ref[...] + mk_ref[...]).reshape(2 * NA_QTOK, NA_WIN)
    o = _pair_softmax_pv(qs, [k_win, kc_ref[0]], [v_win, vc_ref[0]], [bias, None])
    o_ref[0] = o.astype(o_ref.dtype)


def _na_attn(p, pc, bias, mask):
    b, n, _ = p.shape
    n_ctx = pc.shape[1]
    n_r = n // NA_QTOK
    n_kb = n // NA_KBLK
    n_hp = NA_HEADS // 2

    def case(r):
        return jnp.where(r == 0, 0, jnp.where(r == n_r - 1, 2, 1))

    def kv_spec(off, piece):
        return pl.BlockSpec(
            (1, NA_KBLK, HEAD_PAIR),
            lambda bi, hp, r: (bi, jnp.clip(2 * r - 1 + piece, 0, n_kb - 1), off // HEAD_PAIR + hp))

    in_specs = [pl.BlockSpec((1, NA_QTOK, HEAD_PAIR), lambda bi, hp, r: (bi, r, OFF_QC // HEAD_PAIR + hp))]
    in_specs += [kv_spec(OFF_KC, i) for i in range(NA_PIECES)]
    in_specs += [kv_spec(OFF_VC, i) for i in range(NA_PIECES)]
    in_specs += [
        pl.BlockSpec((1, n_ctx, HEAD_PAIR), lambda bi, hp, r: (bi, 0, OFF_KC // HEAD_PAIR + hp)),
        pl.BlockSpec((1, n_ctx, HEAD_PAIR), lambda bi, hp, r: (bi, 0, OFF_VC // HEAD_PAIR + hp)),
        pl.BlockSpec((2, NA_QTOK, NA_WIN), lambda bi, hp, r: (hp, 0, 0)),
        pl.BlockSpec((1, NA_QTOK, NA_WIN), lambda bi, hp, r: (case(r), 0, 0)),
    ]
    return pl.pallas_call(
        _na_kernel,
        grid=(b, n_hp, n_r),
        in_specs=in_specs,
        out_specs=pl.BlockSpec((1, NA_QTOK, HEAD_PAIR), lambda bi, hp, r: (bi, r, hp)),
        out_shape=jax.ShapeDtypeStruct((b, n, NA_HEADS * HEAD_PAIR // 2), BF16),
        compiler_params=_params(("parallel", "parallel", "arbitrary")),
        name="na_attn",
    )(p, *([p] * (2 * NA_PIECES)), pc, pc, bias, mask)


def _ctx_mha_kernel(q_ref, k_ref, v_ref, o_ref):
    qs = _scaled_pair(q_ref[0], HEAD_PAIR // 2)
    o_ref[0] = _pair_softmax_pv(qs, [k_ref[0]], [v_ref[0]], [None]).astype(o_ref.dtype)


def _ctx_mha(pc):
    b, n_ctx, _ = pc.shape
    n_hp = NA_HEADS // 2

    def spec(off):
        return pl.BlockSpec((1, n_ctx, HEAD_PAIR), lambda bi, hp: (bi, 0, off // HEAD_PAIR + hp))

    return pl.pallas_call(
        _ctx_mha_kernel,
        grid=(b, n_hp),
        in_specs=[spec(OFF_QC), spec(OFF_KC), spec(OFF_VC)],
        out_specs=pl.BlockSpec((1, n_ctx, HEAD_PAIR), lambda bi, hp: (bi, 0, hp)),
        out_shape=jax.ShapeDtypeStruct((b, n_ctx, NA_HEADS * HEAD_PAIR // 2), BF16),
        compiler_params=_params(("parallel", "parallel")),
        name="ctx_mha",
    )(pc, pc, pc)


def _gelu_tanh(x):
    return 0.5 * x * (1.0 + jnp.tanh(math.sqrt(2.0 / math.pi) * (x + 0.044715 * (x * x * x))))


SUBLANES = 8


def _store_token_tiles(ref, x):
    t = x.shape[0]
    for j in range(SUBLANES):
        ref[pl.ds(j, t, stride=SUBLANES), :] = x[:, j * LANES:(j + 1) * LANES]


def _load_token_tiles(ref, t):
    return jnp.concatenate([ref[pl.ds(j, t, stride=SUBLANES), :] for j in range(SUBLANES)], axis=1)


def _merge_kernel(h_ref, ya_ref, z_ref, yc_ref, g0_ref, g1_ref, g2_ref, wb_ref, wo_ref, sgw_ref, sgb_ref,
                  lng_ref, lnb_ref, gt_ref, gf_ref, sh_ref, sc_ref, ho_ref, xo_ref, *, tm):
    z = _gelu_tanh(z_ref[0].astype(F32))
    u = z[:, :SG_WIDTH]
    vv = z[:, SG_WIDTH:]
    mu = jnp.mean(vv, axis=-1, keepdims=True)
    var = jnp.mean(jnp.square(vv - mu), axis=-1, keepdims=True)
    vv = ((vv - mu) * lax.rsqrt(var + EPS)) * lng_ref[...] + lnb_ref[...]
    vv = vv.astype(BF16)
    gd = SG_WIDTH // SG_GROUPS
    chunks = []
    for c in range(tm // SG_CHUNK):
        rows = slice(c * SG_CHUNK, (c + 1) * SG_CHUNK)
        groups = []
        for g in range(SG_GROUPS):
            s = jnp.dot(sgw_ref[g], vv[rows, g * gd:(g + 1) * gd], preferred_element_type=F32) + sgb_ref[g]
            groups.append(s)
        chunks.append(jnp.concatenate(groups, axis=1))
    y_b = (u * jnp.concatenate(chunks, axis=0)).astype(BF16)

    merged = None
    for y, gate_ref, i in ((ya_ref[0], g0_ref, 0), (y_b, g1_ref, 1), (yc_ref[0], g2_ref, 2)):
        t = jax.nn.sigmoid(gate_ref[0].astype(F32)) * _dot_rows(y, wb_ref[i])
        merged = t if merged is None else merged + t
    out = _dot_rows(merged.astype(BF16), wo_ref[...])
    h_new = h_ref[0] + gt_ref[0] * out
    ho_ref[0] = h_new
    _store_token_tiles(xo_ref, _rms_mod(h_new, gf_ref[...], sh_ref[0], sc_ref[0]))


def _merge(h, y_a, p, y_c, wb, wo, sgw, sgb, lng, lnb, gt1, g_ffn, sh2, sc2, xn2_buf, xn2_rows, row_off, tm):
    b, n, d = h.shape
    n_i = n // tm
    off_blk = row_off // tm
    vec = pl.BlockSpec((1, 1, d), lambda bi, i: (bi, 0, 0))

    def col(width, off):
        return pl.BlockSpec((1, tm, width), lambda bi, i: (bi, i, off // width))

    kern = functools.partial(_merge_kernel, tm=tm)
    args = [h, y_a, p, y_c, p, p, p, wb, wo, sgw, sgb, lng, lnb, gt1, g_ffn, sh2, sc2]
    in_specs = [
        pl.BlockSpec((1, tm, d), lambda bi, i: (bi, i, 0)),
        col(BRANCH_WIDTH, 0),
        col(2 * SG_WIDTH, OFF_ZB),
        col(BRANCH_WIDTH, 0),
        col(d, OFF_GATE), col(d, OFF_GATE + d), col(d, OFF_GATE + 2 * d),
        pl.BlockSpec((N_BRANCH, BRANCH_WIDTH, d), lambda bi, i: (0, 0, 0)),
        pl.BlockSpec((d, d), lambda bi, i: (0, 0)),
        pl.BlockSpec((SG_GROUPS, SG_CHUNK, SG_CHUNK), lambda bi, i: (0, 0, 0)),
        pl.BlockSpec((SG_GROUPS, SG_CHUNK, SG_CHUNK), lambda bi, i: (0, 0, 0)),
        pl.BlockSpec((1, SG_WIDTH), lambda bi, i: (0, 0)),
        pl.BlockSpec((1, SG_WIDTH), lambda bi, i: (0, 0)),
        vec,
        pl.BlockSpec((1, d), lambda bi, i: (0, 0)),
        vec, vec,
    ]
    aliases = {}
    if xn2_buf is not None:
        args.append(xn2_buf)
        in_specs.append(pl.BlockSpec(memory_space=pl.ANY))
        aliases = {len(args) - 1: 1}
    assert d == SUBLANES * LANES
    xn2_shape = jax.ShapeDtypeStruct((xn2_rows * SUBLANES, LANES), F32)

    def body(*refs):
        n_in = 17
        kern(*refs[:n_in], *refs[len(refs) - 2:])

    return pl.pallas_call(
        body,
        grid=(b, n_i),
        in_specs=in_specs,
        out_specs=[
            pl.BlockSpec((1, tm, d), lambda bi, i: (bi, i, 0)),
            pl.BlockSpec((tm * SUBLANES, LANES), lambda bi, i: (off_blk + bi * n_i + i, 0)),
        ],
        out_shape=[jax.ShapeDtypeStruct((b, n, d), F32), xn2_shape],
        input_output_aliases=aliases,
        compiler_params=_params(("parallel", "parallel")),
        name="merge_branches",
    )(*args)


R_E1, R_E2, R_W1, R_W2, R_RANK1, R_RANK2 = range(6)


def _router_kernel(x_ref, w_ref, b_ref, o_ref, cnt_ref, run_ref, *, tm):
    @pl.when(pl.program_id(0) == 0)
    def _():
        run_ref[...] = jnp.zeros(run_ref.shape, F32)

    logits = jnp.dot(_load_token_tiles(x_ref, tm), w_ref[...], preferred_element_type=F32,
                     precision=lax.Precision.HIGHEST) + b_ref[...]
    lane = lax.broadcasted_iota(jnp.int32, logits.shape, 1)
    lane_f = lane.astype(F32)
    far = jnp.float32(1e9)

    def first_lane(mask):
        return jnp.min(jnp.where(mask, lane_f, far), axis=-1, keepdims=True)

    is_g = lane < N_GROUPS
    gl = jnp.where(is_g, logits, NEG_INF)
    g_max = jnp.max(gl, axis=-1, keepdims=True)
    g_idx = first_lane(is_g & (gl == g_max))
    g_sum = jnp.sum(jnp.where(is_g, jnp.exp(gl - g_max), 0.0), axis=-1, keepdims=True)
    g_w = 1.0 / g_sum

    e_lane = lane - N_GROUPS
    in_grp = (e_lane >= 0) & (e_lane < N_EXPERTS) & \
        (jnp.right_shift(e_lane, 3).astype(F32) == g_idx)
    el = jnp.where(in_grp, logits, NEG_INF)
    l1 = jnp.max(el, axis=-1, keepdims=True)
    i1 = first_lane(in_grp & (el == l1))
    rest = in_grp & (lane_f != i1)
    el2 = jnp.where(rest, logits, NEG_INF)
    l2 = jnp.max(el2, axis=-1, keepdims=True)
    i2 = first_lane(rest & (el2 == l2))
    t = jnp.exp(l2 - l1)
    w1 = g_w / (1.0 + t)
    w2 = g_w * t / (1.0 + t)

    oh1 = lane_f == i1
    oh2 = lane_f == i2
    oh = jnp.where(oh1 | oh2, 1.0, 0.0)
    row = lax.broadcasted_iota(jnp.int32, (tm, tm), 0)
    colm = lax.broadcasted_iota(jnp.int32, (tm, tm), 1)
    before = jnp.where(colm < row, 1.0, 0.0).astype(BF16)
    prior = jnp.dot(before, oh.astype(BF16), preferred_element_type=F32) + run_ref[...]
    rank1 = jnp.sum(jnp.where(oh1, prior, 0.0), axis=-1, keepdims=True)
    rank2 = jnp.sum(jnp.where(oh2, prior, 0.0), axis=-1, keepdims=True)
    run_new = run_ref[...] + jnp.sum(oh, axis=0, keepdims=True)
    run_ref[...] = run_new
    cnt_ref[...] = run_new

    slab = jnp.zeros(logits.shape, F32)
    for ln, val in ((R_E1, i1 - N_GROUPS), (R_E2, i2 - N_GROUPS), (R_W1, w1), (R_W2, w2),
                    (R_RANK1, rank1), (R_RANK2, rank2)):
        slab = jnp.where(lane == ln, val, slab)
    o_ref[...] = slab


def _router(xn2, w_rt, b_rt):
    t_tok = xn2.shape[0] // SUBLANES
    d = w_rt.shape[0]
    tm = 512
    kern = functools.partial(_router_kernel, tm=tm)
    return pl.pallas_call(
        kern,
        grid=(t_tok // tm,),
        in_specs=[
            pl.BlockSpec((tm * SUBLANES, LANES), lambda i: (i, 0)),
            pl.BlockSpec((d, LANES), lambda i: (0, 0)),
            pl.BlockSpec((1, LANES), lambda i: (0, 0)),
        ],
        out_specs=[pl.BlockSpec((tm, LANES), lambda i: (i, 0)), pl.BlockSpec((1, LANES), lambda i: (0, 0))],
        out_shape=[jax.ShapeDtypeStruct((t_tok, LANES), F32), jax.ShapeDtypeStruct((1, LANES), F32)],
        scratch_shapes=[pltpu.VMEM((1, LANES), F32)],
        compiler_params=_params(("arbitrary",)),
        name="moe_router",
    )(xn2, w_rt, b_rt)


def _tile_rows(index):
    return pl.ds(pl.multiple_of(index * SUBLANES, SUBLANES), SUBLANES)


RANK_BITS = 20


def _slot(code_ref, start_ref, a):
    code = code_ref[a]
    return start_ref[jnp.right_shift(code, RANK_BITS)] + jnp.bitwise_and(code, (1 << RANK_BITS) - 1)


def _dispatch_kernel(code_ref, start_ref, end_ref, x_ref, xs_ref, zero_ref, sem, zsem, *, n_blocks):
    step = pl.program_id(0)
    blk_rows = MOE_ROWS * SUBLANES

    @pl.when(step == 0)
    def _():
        zero_ref[...] = jnp.zeros(zero_ref.shape, F32)

        def fill(blk):
            return pltpu.make_async_copy(zero_ref, xs_ref.at[pl.ds(pl.multiple_of(blk * blk_rows, blk_rows),
                                                                  blk_rows), :], zsem.at[0])

        def expert_tail(op):
            def body(e, carry):
                @pl.when(end_ref[e] > start_ref[e])
                def _():
                    op(fill(end_ref[e] // MOE_ROWS - 1))
                return carry
            lax.fori_loop(0, N_EXPERTS, body, 0)

        def unused(op):
            def body(blk, carry):
                @pl.when(blk * MOE_ROWS >= end_ref[N_EXPERTS - 1])
                def _():
                    op(fill(blk))
                return carry
            lax.fori_loop(0, n_blocks, body, 0)

        for phase in (lambda c: c.start(), lambda c: c.wait()):
            expert_tail(phase)
            unused(phase)

    def issue(i, carry):
        tok = step * DMA_CHUNK + i
        for k in range(2):
            pltpu.make_async_copy(x_ref.at[_tile_rows(i), :],
                                  xs_ref.at[_tile_rows(_slot(code_ref, start_ref, 2 * tok + k)), :],
                                  sem.at[0]).start()
        return carry

    lax.fori_loop(0, DMA_CHUNK, issue, 0, unroll=4)
    for _ in range(2):
        pltpu.make_async_copy(x_ref, xs_ref.at[pl.ds(0, DMA_CHUNK * SUBLANES), :], sem.at[0]).wait()


def _dispatch(code, pad_start, pad_end, xn2, n_slots):
    t_tok = xn2.shape[0] // SUBLANES
    kern = functools.partial(_dispatch_kernel, n_blocks=n_slots // MOE_ROWS)
    grid_spec = pltpu.PrefetchScalarGridSpec(
        num_scalar_prefetch=3,
        grid=(t_tok // DMA_CHUNK,),
        in_specs=[pl.BlockSpec((DMA_CHUNK * SUBLANES, LANES), lambda i, *_: (i, 0))],
        out_specs=pl.BlockSpec(memory_space=pl.ANY),
        scratch_shapes=[pltpu.VMEM((MOE_ROWS * SUBLANES, LANES), F32), pltpu.SemaphoreType.DMA((1,)),
                        pltpu.SemaphoreType.DMA((1,))],
    )
    return pl.pallas_call(
        kern,
        grid_spec=grid_spec,
        out_shape=jax.ShapeDtypeStruct((n_slots * SUBLANES, LANES), F32),
        compiler_params=_params(("arbitrary",)),
        name="moe_dispatch",
    )(code, pad_start, pad_end, xn2)


def _expert_kernel(be_ref, nu_ref, x_ref, wg_ref, wu_ref, wd_ref, o_ref):
    i = pl.program_id(0)

    @pl.when(i < nu_ref[0])
    def _():
        x = _load_token_tiles(x_ref, MOE_ROWS).astype(BF16)
        gate = jnp.dot(x, wg_ref[0].astype(BF16), preferred_element_type=F32)
        up = jnp.dot(x, wu_ref[0].astype(BF16), preferred_element_type=F32)
        hdn = (gate * jax.nn.sigmoid(gate)) * up
        _store_token_tiles(o_ref, jnp.dot(hdn.astype(BF16), wd_ref[0].astype(BF16), preferred_element_type=F32))

    @pl.when(i >= nu_ref[0])
    def _():
        o_ref[...] = jnp.zeros(o_ref.shape, F32)


def _experts(block_expert, n_used, xs, layer, w_gate, w_up, w_down):
    n_blocks = xs.shape[0] // (MOE_ROWS * SUBLANES)
    n_layers, n_e, d, de = w_gate.shape
    w_gate, w_up, w_down = (w.reshape(n_layers * n_e, *w.shape[2:]) for w in (w_gate, w_up, w_down))
    blk = pl.BlockSpec((MOE_ROWS * SUBLANES, LANES), lambda i, be, nu: (i, 0))
    grid_spec = pltpu.PrefetchScalarGridSpec(
        num_scalar_prefetch=2,
        grid=(n_blocks,),
        in_specs=[
            blk,
            pl.BlockSpec((1, d, de), lambda i, be, nu: (layer * n_e + be[i], 0, 0)),
            pl.BlockSpec((1, d, de), lambda i, be, nu: (layer * n_e + be[i], 0, 0)),
            pl.BlockSpec((1, de, d), lambda i, be, nu: (layer * n_e + be[i], 0, 0)),
        ],
        out_specs=blk,
    )
    return pl.pallas_call(
        _expert_kernel,
        grid_spec=grid_spec,
        out_shape=jax.ShapeDtypeStruct(xs.shape, F32),
        compiler_params=_params(("arbitrary",)),
        name="moe_experts",
    )(block_expert, n_used, xs, w_gate, w_up, w_down)


def _residual_kernel(code_ref, start_ref, h_ref, r_ref, gt_ref, gf_ref, ys_ref, o_ref, buf_ref, sem,
                     *, tm, n_i, row_off, final):
    base = row_off + (pl.program_id(0) * n_i + pl.program_id(1)) * tm

    def issue(t, carry):
        for k in range(2):
            pltpu.make_async_copy(ys_ref.at[_tile_rows(_slot(code_ref, start_ref, 2 * (base + t) + k)), :],
                                  buf_ref.at[k, _tile_rows(t), :], sem.at[0]).start()
        return carry

    lax.fori_loop(0, tm, issue, 0, unroll=4)
    for k in range(2):
        pltpu.make_async_copy(ys_ref.at[pl.ds(0, tm * SUBLANES), :], buf_ref.at[k], sem.at[0]).wait()

    r = r_ref[...]
    y = (_load_token_tiles(buf_ref.at[0], tm) * r[:, R_W1:R_W1 + 1]
         + _load_token_tiles(buf_ref.at[1], tm) * r[:, R_W2:R_W2 + 1])
    h_new = h_ref[0] + gt_ref[0] * y
    if final:
        h_new = (h_new * lax.rsqrt(jnp.mean(h_new * h_new, axis=-1, keepdims=True) + EPS)) * gf_ref[...]
    o_ref[0] = h_new


def _residual(code, pad_start, h, ys, route, gt2, g_final, row_off, final):
    b, n, d = h.shape
    tm = 256
    n_i = n // tm
    off_blk = row_off // tm
    kern = functools.partial(_residual_kernel, tm=tm, n_i=n_i, row_off=row_off, final=final)
    grid_spec = pltpu.PrefetchScalarGridSpec(
        num_scalar_prefetch=2,
        grid=(b, n_i),
        in_specs=[
            pl.BlockSpec((1, tm, d), lambda bi, i, *_: (bi, i, 0)),
            pl.BlockSpec((tm, LANES), lambda bi, i, *_: (off_blk + bi * n_i + i, 0)),
            pl.BlockSpec((1, 1, d), lambda bi, i, *_: (bi, 0, 0)),
            pl.BlockSpec((1, d), lambda bi, i, *_: (0, 0)),
            pl.BlockSpec(memory_space=pl.ANY),
        ],
        out_specs=pl.BlockSpec((1, tm, d), lambda bi, i, *_: (bi, i, 0)),
        scratch_shapes=[pltpu.VMEM((2, tm * SUBLANES, LANES), F32), pltpu.SemaphoreType.DMA((1,))],
    )
    return pl.pallas_call(
        kern,
        grid_spec=grid_spec,
        out_shape=jax.ShapeDtypeStruct((b, n, d), F32),
        compiler_params=_params(("arbitrary", "arbitrary")),
        name="moe_residual",
    )(code, pad_start, h, route, gt2, g_final, ys)


def _moe(xn2, w_rt, b_rt, layer, w_gate, w_up, w_down):
    t_tok = xn2.shape[0] // SUBLANES
    assert 2 * t_tok < (1 << RANK_BITS)
    route, counts = _router(xn2, w_rt, b_rt)
    cnt = counts[0, N_GROUPS:N_GROUPS + N_EXPERTS].astype(jnp.int32)
    padded = (cnt + MOE_ROWS - 1) // MOE_ROWS * MOE_ROWS
    pad_end = jnp.cumsum(padded)
    pad_start = pad_end - padded
    ids = route[:, R_E1:R_RANK2 + 1].astype(jnp.int32)
    code = (jnp.left_shift(ids[:, R_E1:R_E2 + 1], RANK_BITS) + ids[:, R_RANK1:R_RANK2 + 1]).reshape(2 * t_tok)
    n_blocks = -(-(2 * t_tok + N_EXPERTS * (MOE_ROWS - 1)) // MOE_ROWS)
    n_slots = n_blocks * MOE_ROWS
    starts = jnp.arange(n_blocks, dtype=jnp.int32) * MOE_ROWS
    block_expert = jnp.minimum(jnp.sum(starts[:, None] >= pad_end[None, :], axis=1), N_EXPERTS - 1).astype(jnp.int32)
    n_used = (pad_end[-1:] // MOE_ROWS).astype(jnp.int32)

    pad_start = pad_start.astype(jnp.int32)
    xs = _dispatch(code, pad_start, pad_end.astype(jnp.int32), xn2, n_slots)
    return route, code, pad_start, _experts(block_expert, n_used, xs, layer, w_gate, w_up, w_down)


def kernel(x, c, ctx, c_ctx, w_ada, b_ada, g_norm_mix, g_norm_ffn, w_in, da_lambda, da_subln_g, sg_ln_g, sg_ln_b, sg_w, sg_b, na_rpb, w_branch, w_out, moe_w_group, moe_b_group, moe_w_router, moe_b_router, moe_w_gate, moe_w_up, moe_w_down, g_final):
    b, n_lat, d = x.shape
    n_ctx = ctx.shape[1]
    depth = w_in.shape[0]
    rows = n_lat // GRID_W
    assert d == D_MODEL and n_lat % NA_QTOK == 0 and rows >= 2 * NA_QROWS and n_ctx % 256 == 0 and b <= 7
    tm_lat = 1024 if n_lat % 1024 == 0 else 512
    tm_mrg = 512
    tm_ctx = 256

    cos, sin = _rope_tables(n_lat)
    cond = jnp.zeros((8, d), F32).at[:b].set(c).at[b].set(c_ctx)
    mods = _ada(cond, w_ada, b_ada.reshape(depth, 1, 6 * d))

    h, hc = x, ctx
    for l in range(depth):
        last = l == depth - 1
        lam_init = 0.8 - 0.6 * math.exp(-0.3 * l)
        m_lat = mods[l, :b].reshape(b, 1, 6, d)
        m_ctx = jnp.broadcast_to(mods[l, b].reshape(1, 1, 6, d), (b, 1, 6, d))
        sh1, sc1, gt1, sh2, sc2, gt2 = (m_lat[:, :, i] for i in range(6))
        csh1, csc1, cgt1, csh2, csc2, cgt2 = (m_ctx[:, :, i] for i in range(6))

        w_in_l = w_in[l].astype(BF16)
        g_mix = g_norm_mix[l].reshape(1, d)
        g_ffn = g_norm_ffn[l].reshape(1, d)
        p = _norm_proj(h, g_mix, sh1, sc1, w_in_l, IN_COLS, tm_lat)
        pc = _norm_proj(hc, g_mix, csh1, csc1, w_in_l, KV_COLS if last else IN_COLS, tm_ctx)

        q_hm, k_hm, v_hm = _rope(p, cos, sin, tm_mrg)
        kc_hm = _heads_major(pc[..., OFF_KA:OFF_VA])
        vc_hm = _heads_major(pc[..., OFF_VA:OFF_KC])
        k_all = jnp.concatenate([kc_hm, k_hm], axis=2)
        v_all = jnp.concatenate([vc_hm, v_hm], axis=2)
        g_sub = da_subln_g[l].reshape(1, 2 * DA_HEAD_DIM)
        y_a = _diff_attn(q_hm, k_all, v_all, da_lambda[l], g_sub, lam_init)

        y_c = _na_attn(p, pc, *_na_bias_tables(na_rpb[l], rows))

        wb = w_branch[l].astype(BF16)
        wo = w_out[l].astype(BF16)
        sgw = sg_w[l].astype(BF16)
        sgb = jnp.broadcast_to(sg_b[l][:, :, None], (SG_GROUPS, SG_CHUNK, SG_CHUNK))
        lng = sg_ln_g[l].reshape(1, SG_WIDTH)
        lnb = sg_ln_b[l].reshape(1, SG_WIDTH)
        t_lat = b * n_lat
        t_tok = t_lat if last else t_lat + b * n_ctx
        xn2_buf = None if last else jnp.zeros((t_tok * SUBLANES, LANES), F32)
        h, xn2 = _merge(h, y_a, p, y_c, wb, wo, sgw, sgb, lng, lnb, gt1, g_ffn, sh2, sc2, xn2_buf, t_tok, 0, tm_mrg)
        if not last:
            qc_hm = (_heads_major(pc[..., OFF_QA:OFF_QC]).astype(F32) * DA_Q_SCALE).astype(BF16)
            ya_c = _diff_attn(qc_hm, kc_hm, vc_hm, da_lambda[l], g_sub, lam_init)
            yc_c = _ctx_mha(pc)
            hc, xn2 = _merge(hc, ya_c, pc, yc_c, wb, wo, sgw, sgb, lng, lnb, cgt1, g_ffn, csh2, csc2,
                             xn2, t_tok, t_lat, tm_ctx)

        w_rt = jnp.zeros((d, LANES), F32).at[:, :N_GROUPS].set(moe_w_group[l]) \
            .at[:, N_GROUPS:N_GROUPS + N_EXPERTS].set(moe_w_router[l])
        b_rt = jnp.zeros((1, LANES), F32).at[0, :N_GROUPS].set(moe_b_group[l]) \
            .at[0, N_GROUPS:N_GROUPS + N_EXPERTS].set(moe_b_router[l])
        route, code, pad_start, ys = _moe(xn2, w_rt, b_rt, l, moe_w_gate, moe_w_up, moe_w_down)
        h = _residual(code, pad_start, h, ys, route, gt2, g_final.reshape(1, d), 0, last)
        if not last:
            hc = _residual(code, pad_start, hc, ys, route, cgt2, g_final.reshape(1, d), t_lat, False)
    return h
```

```python
import functools
import math

import numpy as np
import jax
import jax.numpy as jnp
from jax import lax
from jax.experimental import pallas as pl
from jax.experimental.pallas import tpu as pltpu

F32 = jnp.float32
BF16 = jnp.bfloat16

D_MODEL = 1024
GRID_W = 64
EPS = 1e-6
NEG_INF = -1e30
ROPE_THETA = 10000.0

DA_HEADS = 4
DA_HEAD_DIM = 64
NA_HEADS = 8
NA_ROWS = 8
NA_COLS = 16
SG_CHUNK = 128
SG_GROUPS = 4
SG_WIDTH = 512
BRANCH_WIDTH = 512
N_BRANCH = 3

OFF_KA = 0
OFF_VA = 512
OFF_KC = 1024
OFF_VC = 1536
KV_COLS = 2048
OFF_QA = 2048
OFF_QC = 2560
OFF_ZB = 3072
OFF_GATE = 4096
IN_COLS = 7168

N_GROUPS = 4
EXPERTS_PER_GROUP = 8
N_EXPERTS = 32
D_EXPERT = 512

LANES = 128
HEAD_PAIR = LANES
VMEM_LIMIT = 56 * 1024 * 1024

MOE_ROWS = 256
DMA_CHUNK = 512

_CONTRACT_LAST = (((1,), (1,)), ((), ()))


def _params(sem, vmem=VMEM_LIMIT, flags=None):
    return pltpu.CompilerParams(dimension_semantics=sem, vmem_limit_bytes=vmem, flags=flags)


DOT_ROWS = 256


def _dot_rows(a, b, contract_last=False):
    dims = _CONTRACT_LAST if contract_last else (((1,), (0,)), ((), ()))
    n = a.shape[0]
    if n <= DOT_ROWS:
        return lax.dot_general(a, b, dims, preferred_element_type=F32)
    return jnp.concatenate([lax.dot_general(a[r0:r0 + DOT_ROWS], b, dims, preferred_element_type=F32)
                            for r0 in range(0, n, DOT_ROWS)], axis=0)


def _ada_kernel(cond_ref, w_ref, b_ref, o_ref):
    c = cond_ref[...]
    c = c * jax.nn.sigmoid(c)
    o_ref[0] = jnp.dot(c, w_ref[0], preferred_element_type=F32, precision=lax.Precision.HIGHEST) + b_ref[0]


def _ada(cond, w_ada, b_ada):
    n_layers, d, d6 = w_ada.shape
    tn = 1024
    return pl.pallas_call(
        _ada_kernel,
        grid=(n_layers, d6 // tn),
        in_specs=[
            pl.BlockSpec((8, d), lambda l, j: (0, 0)),
            pl.BlockSpec((1, d, tn), lambda l, j: (l, 0, j)),
            pl.BlockSpec((1, 1, tn), lambda l, j: (l, 0, j)),
        ],
        out_specs=pl.BlockSpec((1, 8, tn), lambda l, j: (l, 0, j)),
        out_shape=jax.ShapeDtypeStruct((n_layers, 8, d6), F32),
        compiler_params=_params(("parallel", "parallel")),
        name="ada_mod",
    )(cond, w_ada, b_ada)


def _rms_mod(x, g, shift, scale):
    y = x * lax.rsqrt(jnp.mean(x * x, axis=-1, keepdims=True) + EPS)
    return (y * g) * (1.0 + scale) + shift


def _norm_proj_kernel(h_ref, g_ref, sh_ref, sc_ref, w_ref, o_ref, xn_ref):
    @pl.when(pl.program_id(2) == 0)
    def _():
        xn_ref[...] = _rms_mod(h_ref[0], g_ref[...], sh_ref[0], sc_ref[0]).astype(BF16)

    o_ref[0] = _dot_rows(xn_ref[...], w_ref[...]).astype(o_ref.dtype)


def _norm_proj(h, g, shift, scale, w, n_cols, tm):
    b, n, d = h.shape
    tn = 1024
    return pl.pallas_call(
        _norm_proj_kernel,
        grid=(b, n // tm, n_cols // tn),
        in_specs=[
            pl.BlockSpec((1, tm, d), lambda bi, i, j: (bi, i, 0)),
            pl.BlockSpec((1, d), lambda bi, i, j: (0, 0)),
            pl.BlockSpec((1, 1, d), lambda bi, i, j: (bi, 0, 0)),
            pl.BlockSpec((1, 1, d), lambda bi, i, j: (bi, 0, 0)),
            pl.BlockSpec((d, tn), lambda bi, i, j: (0, j)),
        ],
        out_specs=pl.BlockSpec((1, tm, tn), lambda bi, i, j: (bi, i, j)),
        out_shape=jax.ShapeDtypeStruct((b, n, n_cols), BF16),
        scratch_shapes=[pltpu.VMEM((tm, d), BF16)],
        compiler_params=_params(("parallel", "parallel", "arbitrary")),
        name="norm_proj",
    )(h, g, shift, scale, w)


def _rope_tables(n_tok):
    t = jnp.arange(n_tok, dtype=jnp.int32)
    row = (t // GRID_W).astype(F32)
    col = (t % GRID_W).astype(F32)
    half = DA_HEAD_DIM // 4
    inv = ROPE_THETA ** (-jnp.arange(half, dtype=F32) / half)
    ar = row[:, None] * inv
    ac = col[:, None] * inv
    ang = jnp.concatenate([ar, ar, ac, ac], axis=-1)
    sign = np.tile(np.concatenate([-np.ones(half), np.ones(half)]), 2).astype(np.float32)
    cos = jnp.cos(ang)
    sin = jnp.sin(ang) * sign
    return jnp.concatenate([cos, cos], axis=-1), jnp.concatenate([sin, sin], axis=-1)


DA_Q_SCALE = DA_HEAD_DIM ** -0.5 * math.log2(math.e)


def _rope_kernel(q_ref, k_ref, v_ref, cos_ref, sin_ref, qo_ref, ko_ref, vo_ref):
    cos = cos_ref[...]
    sin = sin_ref[...]
    lane = lax.broadcasted_iota(jnp.int32, cos.shape, 1)
    first = (lane % (DA_HEAD_DIM // 2)) < (DA_HEAD_DIM // 4)
    seg = DA_HEAD_DIM // 4

    def rope(x):
        partner = jnp.where(first, pltpu.roll(x, LANES - seg, 1), pltpu.roll(x, seg, 1))
        return x * cos + partner * sin

    for hd in range(DA_HEADS):
        sl = slice(hd * HEAD_PAIR, (hd + 1) * HEAD_PAIR)
        q = q_ref[0, :, sl].astype(F32)
        k = k_ref[0, :, sl].astype(F32)
        qo_ref[0, hd] = (rope(q) * DA_Q_SCALE).astype(BF16)
        ko_ref[0, hd] = rope(k).astype(BF16)
        vo_ref[0, hd] = v_ref[0, :, sl]


def _rope(p, cos, sin, tm):
    b, n, _ = p.shape
    w = DA_HEADS * HEAD_PAIR
    hm = jax.ShapeDtypeStruct((b, DA_HEADS, n, HEAD_PAIR), BF16)
    hm_spec = pl.BlockSpec((1, DA_HEADS, tm, HEAD_PAIR), lambda bi, i: (bi, 0, i, 0))
    return pl.pallas_call(
        _rope_kernel,
        grid=(b, n // tm),
        in_specs=[
            pl.BlockSpec((1, tm, w), lambda bi, i: (bi, i, OFF_QA // w)),
            pl.BlockSpec((1, tm, w), lambda bi, i: (bi, i, OFF_KA // w)),
            pl.BlockSpec((1, tm, w), lambda bi, i: (bi, i, OFF_VA // w)),
            pl.BlockSpec((tm, LANES), lambda bi, i: (i, 0)),
            pl.BlockSpec((tm, LANES), lambda bi, i: (i, 0)),
        ],
        out_specs=[hm_spec, hm_spec, hm_spec],
        out_shape=[hm, hm, hm],
        compiler_params=_params(("parallel", "parallel")),
        name="rope_heads",
    )(p, p, p, cos, sin)


def _heads_major(t):
    b, n, _ = t.shape
    return t.reshape(b, n, DA_HEADS, HEAD_PAIR).transpose(0, 2, 1, 3)


def _split_pair(q):
    lo = lax.broadcasted_iota(jnp.int32, q.shape, 1) < (HEAD_PAIR // 2)
    zero = jnp.zeros_like(q)
    return jnp.concatenate([jnp.where(lo, q, zero), jnp.where(lo, zero, q)], axis=0)


DA_DOT_ROWS = 256


def _diff_attn_kernel(lam_ref, g_ref, q_ref, k_ref, v_ref, o_ref, qs_ref, m_ref, l_ref, acc_ref, *, tq, lam_init):
    ki = pl.program_id(3)

    @pl.when(ki == 0)
    def _():
        qs_ref[...] = _split_pair(q_ref[0, 0])
        m_ref[...] = jnp.full(m_ref.shape, NEG_INF, F32)
        l_ref[...] = jnp.zeros(l_ref.shape, F32)
        acc_ref[...] = jnp.zeros(acc_ref.shape, F32)

    k = k_ref[0, 0]
    v = v_ref[0, 0]
    n_lt = k.shape[0] // LANES
    chunks = [slice(r0, r0 + DA_DOT_ROWS) for r0 in range(0, 2 * tq, DA_DOT_ROWS)]
    s = jnp.concatenate([lax.dot_general(qs_ref[c], k, _CONTRACT_LAST, preferred_element_type=F32)
                         for c in chunks], axis=0)
    tiles = [s[:, j * LANES:(j + 1) * LANES] for j in range(n_lt)]
    m_prev = m_ref[...]
    m_new = jnp.maximum(m_prev, jnp.max(functools.reduce(jnp.maximum, tiles), axis=-1, keepdims=True))
    alpha = jnp.exp2(m_prev - m_new)
    p_tiles = [jnp.exp2(t - m_new) for t in tiles]
    l_ref[...] = alpha * l_ref[...] + functools.reduce(jnp.add, p_tiles)
    p = jnp.concatenate(p_tiles, axis=1).astype(BF16)
    pv = jnp.concatenate([jnp.dot(p[c], v, preferred_element_type=F32) for c in chunks], axis=0)
    acc_ref[...] = alpha * acc_ref[...] + pv
    m_ref[...] = m_new

    @pl.when(ki == pl.num_programs(3) - 1)
    def _():
        o = acc_ref[...] / jnp.sum(l_ref[...], axis=-1, keepdims=True)
        lm = lam_ref[...]
        lam = (jnp.exp(jnp.sum(lm[0:1] * lm[1:2], axis=-1, keepdims=True))
               - jnp.exp(jnp.sum(lm[2:3] * lm[3:4], axis=-1, keepdims=True)) + lam_init)
        d = o[:tq] - lam * o[tq:]
        y = d * lax.rsqrt(jnp.mean(d * d, axis=-1, keepdims=True) + EPS)
        o_ref[0] = ((y * g_ref[...]) * (1.0 - lam_init)).astype(o_ref.dtype)


def _pick(n, options):
    for o in options:
        if n % o == 0:
            return o
    raise ValueError(f"no tile in {options} divides {n}")


def _diff_attn(q, k, v, lam_params, g, lam_init):
    b, nh, nq, _ = q.shape
    nk = k.shape[2]
    tq = _pick(nq, (512, 256))
    tk = _pick(nk, (2816, 1408, 768, 512, 256))
    kern = functools.partial(_diff_attn_kernel, tq=tq, lam_init=lam_init)
    return pl.pallas_call(
        kern,
        grid=(b, nh, nq // tq, nk // tk),
        in_specs=[
            pl.BlockSpec((4, DA_HEAD_DIM), lambda bi, h, i, j: (0, 0)),
            pl.BlockSpec((1, HEAD_PAIR), lambda bi, h, i, j: (0, 0)),
            pl.BlockSpec((1, 1, tq, HEAD_PAIR), lambda bi, h, i, j: (bi, h, i, 0)),
            pl.BlockSpec((1, 1, tk, HEAD_PAIR), lambda bi, h, i, j: (bi, h, j, 0)),
            pl.BlockSpec((1, 1, tk, HEAD_PAIR), lambda bi, h, i, j: (bi, h, j, 0)),
        ],
        out_specs=pl.BlockSpec((1, tq, HEAD_PAIR), lambda bi, h, i, j: (bi, i, h)),
        out_shape=jax.ShapeDtypeStruct((b, nq, nh * HEAD_PAIR), BF16),
        scratch_shapes=[
            pltpu.VMEM((2 * tq, HEAD_PAIR), BF16),
            pltpu.VMEM((2 * tq, LANES), F32),
            pltpu.VMEM((2 * tq, LANES), F32),
            pltpu.VMEM((2 * tq, HEAD_PAIR), F32),
        ],
        compiler_params=_params(("parallel", "parallel", "parallel", "arbitrary")),
        name="diff_attn",
    )(lam_params, g, q, k, v)


NA_QROWS = 8
NA_QTOK = NA_QROWS * GRID_W
NA_KBLK = 4 * GRID_W
NA_PIECES = 4
NA_WIN = NA_PIECES * NA_KBLK


def _na_bias_tables(rpb, rows):
    n_h = rpb.shape[0]
    n_kj = NA_PIECES * 4
    pad_r = n_kj - NA_ROWS
    pad_c = GRID_W - NA_COLS
    rp = jnp.pad(rpb * math.log2(math.e), ((0, 0), (pad_r, pad_r), (pad_c, pad_c)))
    a = jnp.stack([rp[:, :, GRID_W - 1 - qc:2 * GRID_W - 1 - qc] for qc in range(GRID_W)], axis=2)
    bias = jnp.stack([a[:, 3 + pad_r - qi:3 + pad_r - qi + n_kj].transpose(0, 2, 1, 3) for qi in range(NA_QROWS)],
                     axis=1).reshape(n_h, NA_QTOK, NA_WIN)

    n_r = rows // NA_QROWS
    qi = np.arange(NA_QROWS)
    kj = np.arange(n_kj)
    c = np.arange(GRID_W)
    cstart = np.clip(c - NA_COLS // 2, 0, GRID_W - NA_COLS)
    col_ok = (c[None, :] >= cstart[:, None]) & (c[None, :] < cstart[:, None] + NA_COLS)
    row_ok = []
    for r_grp in (0, min(1, n_r - 1), n_r - 1):
        r = NA_QROWS * r_grp + qi
        rs = np.clip(r - NA_ROWS // 2, 0, rows - NA_ROWS)
        krow = NA_QROWS * r_grp - 4 + kj
        row_ok.append((krow[None, :] >= rs[:, None]) & (krow[None, :] < rs[:, None] + NA_ROWS)
                      & (krow[None, :] >= 0) & (krow[None, :] < rows))
    ok = jnp.asarray(np.stack(row_ok))[:, :, None, :, None] & jnp.asarray(col_ok)[None, None, :, None, :]
    mask = jnp.where(ok, 0.0, NEG_INF).astype(F32).reshape(3, NA_QTOK, NA_WIN)
    return bias, mask


def _pair_softmax_pv(qs, k_list, v_list, bias_list):
    n_rows = qs.shape[0]
    chunks = [slice(r0, r0 + DA_DOT_ROWS) for r0 in range(0, n_rows, DA_DOT_ROWS)]
    tiles_list = []
    for k, bias in zip(k_list, bias_list):
        s = jnp.concatenate([lax.dot_general(qs[c], k, _CONTRACT_LAST, preferred_element_type=F32)
                             for c in chunks], axis=0)
        s = s if bias is None else s + bias
        tiles_list.append([s[:, j * LANES:(j + 1) * LANES] for j in range(k.shape[0] // LANES)])
    all_tiles = [t for tiles in tiles_list for t in tiles]
    m = jnp.max(functools.reduce(jnp.maximum, all_tiles), axis=-1, keepdims=True)
    p_list = [[jnp.exp2(t - m) for t in tiles] for tiles in tiles_list]
    l = jnp.sum(functools.reduce(jnp.add, [t for tiles in p_list for t in tiles]), axis=-1, keepdims=True)
    o = None
    for tiles, v in zip(p_list, v_list):
        p = jnp.concatenate(tiles, axis=1).astype(BF16)
        pv = jnp.concatenate([jnp.dot(p[c], v, preferred_element_type=F32) for c in chunks], axis=0)
        o = pv if o is None else o + pv
    o = o / l
    t = qs.shape[0] // 2
    lo = lax.broadcasted_iota(jnp.int32, (t, HEAD_PAIR), 1) < (HEAD_PAIR // 2)
    return jnp.where(lo, o[:t], o[t:])


def _scaled_pair(q, head_dim):
    return _split_pair((q.astype(F32) * (head_dim ** -0.5 * math.log2(math.e))).astype(BF16))


def _na_kernel(q_ref, k0, k1, k2, k3, v0, v1, v2, v3, kc_ref, vc_ref, b_ref, mk_ref, o_ref):
    qs = _scaled_pair(q_ref[0], HEAD_PAIR // 2)
    k_win = jnp.concatenate([k0[0], k1[0], k2[0], k3[0]], axis=0)
    v_win = jnp.concatenate([v0[0], v1[0], v2[0], v3[0]], axis=0)
    bias = (b_ref[...] + mk_ref[...]).reshape(2 * NA_QTOK, NA_WIN)
    o = _pair_softmax_pv(qs, [k_win, kc_ref[0]], [v_win, vc_ref[0]], [bias, None])
    o_ref[0] = o.astype(o_ref.dtype)


def _na_attn(p, pc, bias, mask):
    b, n, _ = p.shape
    n_ctx = pc.shape[1]
    n_r = n // NA_QTOK
    n_kb = n // NA_KBLK
    n_hp = NA_HEADS // 2

    def case(r):
        return jnp.where(r == 0, 0, jnp.where(r == n_r - 1, 2, 1))

    def kv_spec(off, piece):
        return pl.BlockSpec(
            (1, NA_KBLK, HEAD_PAIR),
            lambda bi, hp, r: (bi, jnp.clip(2 * r - 1 + piece, 0, n_kb - 1), off // HEAD_PAIR + hp))

    in_specs = [pl.BlockSpec((1, NA_QTOK, HEAD_PAIR), lambda bi, hp, r: (bi, r, OFF_QC // HEAD_PAIR + hp))]
    in_specs += [kv_spec(OFF_KC, i) for i in range(NA_PIECES)]
    in_specs += [kv_spec(OFF_VC, i) for i in range(NA_PIECES)]
    in_specs += [
        pl.BlockSpec((1, n_ctx, HEAD_PAIR), lambda bi, hp, r: (bi, 0, OFF_KC // HEAD_PAIR + hp)),
        pl.BlockSpec((1, n_ctx, HEAD_PAIR), lambda bi, hp, r: (bi, 0, OFF_VC // HEAD_PAIR + hp)),
        pl.BlockSpec((2, NA_QTOK, NA_WIN), lambda bi, hp, r: (hp, 0, 0)),
        pl.BlockSpec((1, NA_QTOK, NA_WIN), lambda bi, hp, r: (case(r), 0, 0)),
    ]
    return pl.pallas_call(
        _na_kernel,
        grid=(b, n_hp, n_r),
        in_specs=in_specs,
        out_specs=pl.BlockSpec((1, NA_QTOK, HEAD_PAIR), lambda bi, hp, r: (bi, r, hp)),
        out_shape=jax.ShapeDtypeStruct((b, n, NA_HEADS * HEAD_PAIR // 2), BF16),
        compiler_params=_params(("parallel", "parallel", "arbitrary")),
        name="na_attn",
    )(p, *([p] * (2 * NA_PIECES)), pc, pc, bias, mask)


def _ctx_mha_kernel(q_ref, k_ref, v_ref, o_ref):
    qs = _scaled_pair(q_ref[0], HEAD_PAIR // 2)
    o_ref[0] = _pair_softmax_pv(qs, [k_ref[0]], [v_ref[0]], [None]).astype(o_ref.dtype)


def _ctx_mha(pc):
    b, n_ctx, _ = pc.shape
    n_hp = NA_HEADS // 2

    def spec(off):
        return pl.BlockSpec((1, n_ctx, HEAD_PAIR), lambda bi, hp: (bi, 0, off // HEAD_PAIR + hp))

    return pl.pallas_call(
        _ctx_mha_kernel,
        grid=(b, n_hp),
        in_specs=[spec(OFF_QC), spec(OFF_KC), spec(OFF_VC)],
        out_specs=pl.BlockSpec((1, n_ctx, HEAD_PAIR), lambda bi, hp: (bi, 0, hp)),
        out_shape=jax.ShapeDtypeStruct((b, n_ctx, NA_HEADS * HEAD_PAIR // 2), BF16),
        compiler_params=_params(("parallel", "parallel")),
        name="ctx_mha",
    )(pc, pc, pc)


def _gelu_tanh(x):
    return 0.5 * x * (1.0 + jnp.tanh(math.sqrt(2.0 / math.pi) * (x + 0.044715 * (x * x * x))))


SUBLANES = 8


def _store_token_tiles(ref, x):
    t = x.shape[0]
    for j in range(SUBLANES):
        ref[pl.ds(j, t, stride=SUBLANES), :] = x[:, j * LANES:(j + 1) * LANES]


def _load_token_tiles(ref, t):
    return jnp.concatenate([ref[pl.ds(j, t, stride=SUBLANES), :] for j in range(SUBLANES)], axis=1)


def _merge_kernel(h_ref, ya_ref, z_ref, yc_ref, g0_ref, g1_ref, g2_ref, wb_ref, wo_ref, sgw_ref, sgb_ref,
                  lng_ref, lnb_ref, gt_ref, gf_ref, sh_ref, sc_ref, ho_ref, xo_ref, *, tm):
    z = _gelu_tanh(z_ref[0].astype(F32))
    u = z[:, :SG_WIDTH]
    vv = z[:, SG_WIDTH:]
    mu = jnp.mean(vv, axis=-1, keepdims=True)
    var = jnp.mean(jnp.square(vv - mu), axis=-1, keepdims=True)
    vv = ((vv - mu) * lax.rsqrt(var + EPS)) * lng_ref[...] + lnb_ref[...]
    vv = vv.astype(BF16)
    gd = SG_WIDTH // SG_GROUPS
    chunks = []
    for c in range(tm // SG_CHUNK):
        rows = slice(c * SG_CHUNK, (c + 1) * SG_CHUNK)
        groups = []
        for g in range(SG_GROUPS):
            s = jnp.dot(sgw_ref[g], vv[rows, g * gd:(g + 1) * gd], preferred_element_type=F32) + sgb_ref[g]
            groups.append(s)
        chunks.append(jnp.concatenate(groups, axis=1))
    y_b = (u * jnp.concatenate(chunks, axis=0)).astype(BF16)

    merged = None
    for y, gate_ref, i in ((ya_ref[0], g0_ref, 0), (y_b, g1_ref, 1), (yc_ref[0], g2_ref, 2)):
        t = jax.nn.sigmoid(gate_ref[0].astype(F32)) * _dot_rows(y, wb_ref[i])
        merged = t if merged is None else merged + t
    out = _dot_rows(merged.astype(BF16), wo_ref[...])
    h_new = h_ref[0] + gt_ref[0] * out
    ho_ref[0] = h_new
    _store_token_tiles(xo_ref, _rms_mod(h_new, gf_ref[...], sh_ref[0], sc_ref[0]))


def _merge(h, y_a, p, y_c, wb, wo, sgw, sgb, lng, lnb, gt1, g_ffn, sh2, sc2, xn2_buf, xn2_rows, row_off, tm):
    b, n, d = h.shape
    n_i = n // tm
    off_blk = row_off // tm
    vec = pl.BlockSpec((1, 1, d), lambda bi, i: (bi, 0, 0))

    def col(width, off):
        return pl.BlockSpec((1, tm, width), lambda bi, i: (bi, i, off // width))

    kern = functools.partial(_merge_kernel, tm=tm)
    args = [h, y_a, p, y_c, p, p, p, wb, wo, sgw, sgb, lng, lnb, gt1, g_ffn, sh2, sc2]
    in_specs = [
        pl.BlockSpec((1, tm, d), lambda bi, i: (bi, i, 0)),
        col(BRANCH_WIDTH, 0),
        col(2 * SG_WIDTH, OFF_ZB),
        col(BRANCH_WIDTH, 0),
        col(d, OFF_GATE), col(d, OFF_GATE + d), col(d, OFF_GATE + 2 * d),
        pl.BlockSpec((N_BRANCH, BRANCH_WIDTH, d), lambda bi, i: (0, 0, 0)),
        pl.BlockSpec((d, d), lambda bi, i: (0, 0)),
        pl.BlockSpec((SG_GROUPS, SG_CHUNK, SG_CHUNK), lambda bi, i: (0, 0, 0)),
        pl.BlockSpec((SG_GROUPS, SG_CHUNK, SG_CHUNK), lambda bi, i: (0, 0, 0)),
        pl.BlockSpec((1, SG_WIDTH), lambda bi, i: (0, 0)),
        pl.BlockSpec((1, SG_WIDTH), lambda bi, i: (0, 0)),
        vec,
        pl.BlockSpec((1, d), lambda bi, i: (0, 0)),
        vec, vec,
    ]
    aliases = {}
    if xn2_buf is not None:
        args.append(xn2_buf)
        in_specs.append(pl.BlockSpec(memory_space=pl.ANY))
        aliases = {len(args) - 1: 1}
    assert d == SUBLANES * LANES
    xn2_shape = jax.ShapeDtypeStruct((xn2_rows * SUBLANES, LANES), F32)

    def body(*refs):
        n_in = 17
        kern(*refs[:n_in], *refs[len(refs) - 2:])

    return pl.pallas_call(
        body,
        grid=(b, n_i),
        in_specs=in_specs,
        out_specs=[
            pl.BlockSpec((1, tm, d), lambda bi, i: (bi, i, 0)),
            pl.BlockSpec((tm * SUBLANES, LANES), lambda bi, i: (off_blk + bi * n_i + i, 0)),
        ],
        out_shape=[jax.ShapeDtypeStruct((b, n, d), F32), xn2_shape],
        input_output_aliases=aliases,
        compiler_params=_params(("parallel", "parallel")),
        name="merge_branches",
    )(*args)


R_E1, R_E2, R_W1, R_W2, R_RANK1, R_RANK2 = range(6)


def _router_kernel(x_ref, w_ref, b_ref, o_ref, cnt_ref, run_ref, *, tm):
    @pl.when(pl.program_id(0) == 0)
    def _():
        run_ref[...] = jnp.zeros(run_ref.shape, F32)

    logits = jnp.dot(_load_token_tiles(x_ref, tm), w_ref[...], preferred_element_type=F32,
                     precision=lax.Precision.HIGHEST) + b_ref[...]
    lane = lax.broadcasted_iota(jnp.int32, logits.shape, 1)
    lane_f = lane.astype(F32)
    far = jnp.float32(1e9)

    def first_lane(mask):
        return jnp.min(jnp.where(mask, lane_f, far), axis=-1, keepdims=True)

    is_g = lane < N_GROUPS
    gl = jnp.where(is_g, logits, NEG_INF)
    g_max = jnp.max(gl, axis=-1, keepdims=True)
    g_idx = first_lane(is_g & (gl == g_max))
    g_sum = jnp.sum(jnp.where(is_g, jnp.exp(gl - g_max), 0.0), axis=-1, keepdims=True)
    g_w = 1.0 / g_sum

    e_lane = lane - N_GROUPS
    in_grp = (e_lane >= 0) & (e_lane < N_EXPERTS) & \
        (jnp.right_shift(e_lane, 3).astype(F32) == g_idx)
    el = jnp.where(in_grp, logits, NEG_INF)
    l1 = jnp.max(el, axis=-1, keepdims=True)
    i1 = first_lane(in_grp & (el == l1))
    rest = in_grp & (lane_f != i1)
    el2 = jnp.where(rest, logits, NEG_INF)
    l2 = jnp.max(el2, axis=-1, keepdims=True)
    i2 = first_lane(rest & (el2 == l2))
    t = jnp.exp(l2 - l1)
    w1 = g_w / (1.0 + t)
    w2 = g_w * t / (1.0 + t)

    oh1 = lane_f == i1
    oh2 = lane_f == i2
    oh = jnp.where(oh1 | oh2, 1.0, 0.0)
    row = lax.broadcasted_iota(jnp.int32, (tm, tm), 0)
    colm = lax.broadcasted_iota(jnp.int32, (tm, tm), 1)
    before = jnp.where(colm < row, 1.0, 0.0).astype(BF16)
    prior = jnp.dot(before, oh.astype(BF16), preferred_element_type=F32) + run_ref[...]
    rank1 = jnp.sum(jnp.where(oh1, prior, 0.0), axis=-1, keepdims=True)
    rank2 = jnp.sum(jnp.where(oh2, prior, 0.0), axis=-1, keepdims=True)
    run_new = run_ref[...] + jnp.sum(oh, axis=0, keepdims=True)
    run_ref[...] = run_new
    cnt_ref[...] = run_new

    slab = jnp.zeros(logits.shape, F32)
    for ln, val in ((R_E1, i1 - N_GROUPS), (R_E2, i2 - N_GROUPS), (R_W1, w1), (R_W2, w2),
                    (R_RANK1, rank1), (R_RANK2, rank2)):
        slab = jnp.where(lane == ln, val, slab)
    o_ref[...] = slab


def _router(xn2, w_rt, b_rt):
    t_tok = xn2.shape[0] // SUBLANES
    d = w_rt.shape[0]
    tm = 512
    kern = functools.partial(_router_kernel, tm=tm)
    return pl.pallas_call(
        kern,
        grid=(t_tok // tm,),
        in_specs=[
            pl.BlockSpec((tm * SUBLANES, LANES), lambda i: (i, 0)),
            pl.BlockSpec((d, LANES), lambda i: (0, 0)),
            pl.BlockSpec((1, LANES), lambda i: (0, 0)),
        ],
        out_specs=[pl.BlockSpec((tm, LANES), lambda i: (i, 0)), pl.BlockSpec((1, LANES), lambda i: (0, 0))],
        out_shape=[jax.ShapeDtypeStruct((t_tok, LANES), F32), jax.ShapeDtypeStruct((1, LANES), F32)],
        scratch_shapes=[pltpu.VMEM((1, LANES), F32)],
        compiler_params=_params(("arbitrary",)),
        name="moe_router",
    )(xn2, w_rt, b_rt)


def _tile_rows(index):
    return pl.ds(pl.multiple_of(index * SUBLANES, SUBLANES), SUBLANES)


RANK_BITS = 20


def _slot(code_ref, start_ref, a):
    code = code_ref[a]
    return start_ref[jnp.right_shift(code, RANK_BITS)] + jnp.bitwise_and(code, (1 << RANK_BITS) - 1)


def _dispatch_kernel(code_ref, start_ref, end_ref, x_ref, xs_ref, zero_ref, sem, zsem, *, n_blocks):
    step = pl.program_id(0)
    blk_rows = MOE_ROWS * SUBLANES

    @pl.when(step == 0)
    def _():
        zero_ref[...] = jnp.zeros(zero_ref.shape, F32)

        def fill(blk):
            return pltpu.make_async_copy(zero_ref, xs_ref.at[pl.ds(pl.multiple_of(blk * blk_rows, blk_rows),
                                                                  blk_rows), :], zsem.at[0])

        def expert_tail(op):
            def body(e, carry):
                @pl.when(end_ref[e] > start_ref[e])
                def _():
                    op(fill(end_ref[e] // MOE_ROWS - 1))
                return carry
            lax.fori_loop(0, N_EXPERTS, body, 0)

        def unused(op):
            def body(blk, carry):
                @pl.when(blk * MOE_ROWS >= end_ref[N_EXPERTS - 1])
                def _():
                    op(fill(blk))
                return carry
            lax.fori_loop(0, n_blocks, body, 0)

        for phase in (lambda c: c.start(), lambda c: c.wait()):
            expert_tail(phase)
            unused(phase)

    def issue(i, carry):
        tok = step * DMA_CHUNK + i
        for k in range(2):
            pltpu.make_async_copy(x_ref.at[_tile_rows(i), :],
                                  xs_ref.at[_tile_rows(_slot(code_ref, start_ref, 2 * tok + k)), :],
                                  sem.at[0]).start()
        return carry

    lax.fori_loop(0, DMA_CHUNK, issue, 0, unroll=4)
    for _ in range(2):
        pltpu.make_async_copy(x_ref, xs_ref.at[pl.ds(0, DMA_CHUNK * SUBLANES), :], sem.at[0]).wait()


def _dispatch(code, pad_start, pad_end, xn2, n_slots):
    t_tok = xn2.shape[0] // SUBLANES
    kern = functools.partial(_dispatch_kernel, n_blocks=n_slots // MOE_ROWS)
    grid_spec = pltpu.PrefetchScalarGridSpec(
        num_scalar_prefetch=3,
        grid=(t_tok // DMA_CHUNK,),
        in_specs=[pl.BlockSpec((DMA_CHUNK * SUBLANES, LANES), lambda i, *_: (i, 0))],
        out_specs=pl.BlockSpec(memory_space=pl.ANY),
        scratch_shapes=[pltpu.VMEM((MOE_ROWS * SUBLANES, LANES), F32), pltpu.SemaphoreType.DMA((1,)),
                        pltpu.SemaphoreType.DMA((1,))],
    )
    return pl.pallas_call(
        kern,
        grid_spec=grid_spec,
        out_shape=jax.ShapeDtypeStruct((n_slots * SUBLANES, LANES), F32),
        compiler_params=_params(("arbitrary",)),
        name="moe_dispatch",
    )(code, pad_start, pad_end, xn2)


def _expert_kernel(be_ref, nu_ref, x_ref, wg_ref, wu_ref, wd_ref, o_ref, wg_s, wu_s, wd_s):
    i = pl.program_id(0)
    used = i < nu_ref[0]

    @pl.when(used & ((i == 0) | (be_ref[i] != be_ref[jnp.maximum(i - 1, 0)])))
    def _():
        wg_s[...] = wg_ref[0].astype(BF16)
        wu_s[...] = wu_ref[0].astype(BF16)
        wd_s[...] = wd_ref[0].astype(BF16)

    @pl.when(used)
    def _():
        x = _load_token_tiles(x_ref, MOE_ROWS).astype(BF16)
        gate = jnp.dot(x, wg_s[...], preferred_element_type=F32)
        up = jnp.dot(x, wu_s[...], preferred_element_type=F32)
        hdn = (gate * jax.nn.sigmoid(gate)) * up
        _store_token_tiles(o_ref, jnp.dot(hdn.astype(BF16), wd_s[...], preferred_element_type=F32))

    @pl.when(i >= nu_ref[0])
    def _():
        o_ref[...] = jnp.zeros(o_ref.shape, F32)


def _experts(block_expert, n_used, xs, layer, w_gate, w_up, w_down):
    n_blocks = xs.shape[0] // (MOE_ROWS * SUBLANES)
    n_layers, n_e, d, de = w_gate.shape
    w_gate, w_up, w_down = (w.reshape(n_layers * n_e, *w.shape[2:]) for w in (w_gate, w_up, w_down))
    blk = pl.BlockSpec((MOE_ROWS * SUBLANES, LANES), lambda i, be, nu: (i, 0))
    grid_spec = pltpu.PrefetchScalarGridSpec(
        num_scalar_prefetch=2,
        grid=(n_blocks,),
        in_specs=[
            blk,
            pl.BlockSpec((1, d, de), lambda i, be, nu: (layer * n_e + be[i], 0, 0)),
            pl.BlockSpec((1, d, de), lambda i, be, nu: (layer * n_e + be[i], 0, 0)),
            pl.BlockSpec((1, de, d), lambda i, be, nu: (layer * n_e + be[i], 0, 0)),
        ],
        out_specs=blk,
        scratch_shapes=[pltpu.VMEM((d, de), BF16), pltpu.VMEM((d, de), BF16), pltpu.VMEM((de, d), BF16)],
    )
    return pl.pallas_call(
        _expert_kernel,
        grid_spec=grid_spec,
        out_shape=jax.ShapeDtypeStruct(xs.shape, F32),
        compiler_params=_params(("arbitrary",)),
        name="moe_experts",
    )(block_expert, n_used, xs, w_gate, w_up, w_down)


def _residual_kernel(code_ref, start_ref, h_ref, r_ref, gt_ref, gf_ref, ys_ref, o_ref, buf_ref, sem,
                     *, tm, n_i, row_off, final):
    step = pl.program_id(0) * n_i + pl.program_id(1)
    n_steps = pl.num_programs(0) * n_i
    slot = step % 2

    def gather(tile, into):
        base = row_off + tile * tm

        def issue(t, carry):
            for k in range(2):
                pltpu.make_async_copy(ys_ref.at[_tile_rows(_slot(code_ref, start_ref, 2 * (base + t) + k)), :],
                                      buf_ref.at[into, k, _tile_rows(t), :], sem.at[into]).start()
            return carry

        lax.fori_loop(0, tm, issue, 0, unroll=4)

    @pl.when(step == 0)
    def _():
        gather(step, slot)

    @pl.when(step + 1 < n_steps)
    def _():
        gather(step + 1, 1 - slot)

    for k in range(2):
        pltpu.make_async_copy(ys_ref.at[pl.ds(0, tm * SUBLANES), :], buf_ref.at[slot, k], sem.at[slot]).wait()

    r = r_ref[...]
    y = (_load_token_tiles(buf_ref.at[slot, 0], tm) * r[:, R_W1:R_W1 + 1]
         + _load_token_tiles(buf_ref.at[slot, 1], tm) * r[:, R_W2:R_W2 + 1])
    h_new = h_ref[0] + gt_ref[0] * y
    if final:
        h_new = (h_new * lax.rsqrt(jnp.mean(h_new * h_new, axis=-1, keepdims=True) + EPS)) * gf_ref[...]
    o_ref[0] = h_new


def _residual(code, pad_start, h, ys, route, gt2, g_final, row_off, final):
    b, n, d = h.shape
    tm = 256
    n_i = n // tm
    off_blk = row_off // tm
    kern = functools.partial(_residual_kernel, tm=tm, n_i=n_i, row_off=row_off, final=final)
    grid_spec = pltpu.PrefetchScalarGridSpec(
        num_scalar_prefetch=2,
        grid=(b, n_i),
        in_specs=[
            pl.BlockSpec((1, tm, d), lambda bi, i, *_: (bi, i, 0)),
            pl.BlockSpec((tm, LANES), lambda bi, i, *_: (off_blk + bi * n_i + i, 0)),
            pl.BlockSpec((1, 1, d), lambda bi, i, *_: (bi, 0, 0)),
            pl.BlockSpec((1, d), lambda bi, i, *_: (0, 0)),
            pl.BlockSpec(memory_space=pl.ANY),
        ],
        out_specs=pl.BlockSpec((1, tm, d), lambda bi, i, *_: (bi, i, 0)),
        scratch_shapes=[pltpu.VMEM((2, 2, tm * SUBLANES, LANES), F32), pltpu.SemaphoreType.DMA((2,))],
    )
    return pl.pallas_call(
        kern,
        grid_spec=grid_spec,
        out_shape=jax.ShapeDtypeStruct((b, n, d), F32),
        compiler_params=_params(("arbitrary", "arbitrary")),
        name="moe_residual",
    )(code, pad_start, h, route, gt2, g_final, ys)


def _moe(xn2, w_rt, b_rt, layer, w_gate, w_up, w_down):
    t_tok = xn2.shape[0] // SUBLANES
    assert 2 * t_tok < (1 << RANK_BITS)
    route, counts = _router(xn2, w_rt, b_rt)
    cnt = counts[0, N_GROUPS:N_GROUPS + N_EXPERTS].astype(jnp.int32)
    padded = (cnt + MOE_ROWS - 1) // MOE_ROWS * MOE_ROWS
    pad_end = jnp.cumsum(padded)
    pad_start = pad_end - padded
    ids = route[:, R_E1:R_RANK2 + 1].astype(jnp.int32)
    code = (jnp.left_shift(ids[:, R_E1:R_E2 + 1], RANK_BITS) + ids[:, R_RANK1:R_RANK2 + 1]).reshape(2 * t_tok)
    n_blocks = -(-(2 * t_tok + N_EXPERTS * (MOE_ROWS - 1)) // MOE_ROWS)
    n_slots = n_blocks * MOE_ROWS
    starts = jnp.arange(n_blocks, dtype=jnp.int32) * MOE_ROWS
    block_expert = jnp.minimum(jnp.sum(starts[:, None] >= pad_end[None, :], axis=1), N_EXPERTS - 1).astype(jnp.int32)
    n_used = (pad_end[-1:] // MOE_ROWS).astype(jnp.int32)

    pad_start = pad_start.astype(jnp.int32)
    xs = _dispatch(code, pad_start, pad_end.astype(jnp.int32), xn2, n_slots)
    return route, code, pad_start, _experts(block_expert, n_used, xs, layer, w_gate, w_up, w_down)


def kernel(x, c, ctx, c_ctx, w_ada, b_ada, g_norm_mix, g_norm_ffn, w_in, da_lambda, da_subln_g, sg_ln_g, sg_ln_b, sg_w, sg_b, na_rpb, w_branch, w_out, moe_w_group, moe_b_group, moe_w_router, moe_b_router, moe_w_gate, moe_w_up, moe_w_down, g_final):
    b, n_lat, d = x.shape
    n_ctx = ctx.shape[1]
    depth = w_in.shape[0]
    rows = n_lat // GRID_W
    assert d == D_MODEL and n_lat % NA_QTOK == 0 and rows >= 2 * NA_QROWS and n_ctx % 256 == 0 and b <= 7
    tm_lat = 1024 if n_lat % 1024 == 0 else 512
    tm_mrg = 512
    tm_ctx = 256

    cos, sin = _rope_tables(n_lat)
    cond = jnp.zeros((8, d), F32).at[:b].set(c).at[b].set(c_ctx)
    mods = _ada(cond, w_ada, b_ada.reshape(depth, 1, 6 * d))

    h, hc = x, ctx
    for l in range(depth):
        last = l == depth - 1
        lam_init = 0.8 - 0.6 * math.exp(-0.3 * l)
        m_lat = mods[l, :b].reshape(b, 1, 6, d)
        m_ctx = jnp.broadcast_to(mods[l, b].reshape(1, 1, 6, d), (b, 1, 6, d))
        sh1, sc1, gt1, sh2, sc2, gt2 = (m_lat[:, :, i] for i in range(6))
        csh1, csc1, cgt1, csh2, csc2, cgt2 = (m_ctx[:, :, i] for i in range(6))

        w_in_l = w_in[l].astype(BF16)
        g_mix = g_norm_mix[l].reshape(1, d)
        g_ffn = g_norm_ffn[l].reshape(1, d)
        p = _norm_proj(h, g_mix, sh1, sc1, w_in_l, IN_COLS, tm_lat)
        pc = _norm_proj(hc, g_mix, csh1, csc1, w_in_l, KV_COLS if last else IN_COLS, tm_ctx)

        q_hm, k_hm, v_hm = _rope(p, cos, sin, tm_mrg)
        kc_hm = _heads_major(pc[..., OFF_KA:OFF_VA])
        vc_hm = _heads_major(pc[..., OFF_VA:OFF_KC])
        k_all = jnp.concatenate([kc_hm, k_hm], axis=2)
        v_all = jnp.concatenate([vc_hm, v_hm], axis=2)
        g_sub = da_subln_g[l].reshape(1, 2 * DA_HEAD_DIM)
        y_a = _diff_attn(q_hm, k_all, v_all, da_lambda[l], g_sub, lam_init)

        y_c = _na_attn(p, pc, *_na_bias_tables(na_rpb[l], rows))

        wb = w_branch[l].astype(BF16)
        wo = w_out[l].astype(BF16)
        sgw = sg_w[l].astype(BF16)
        sgb = jnp.broadcast_to(sg_b[l][:, :, None], (SG_GROUPS, SG_CHUNK, SG_CHUNK))
        lng = sg_ln_g[l].reshape(1, SG_WIDTH)
        lnb = sg_ln_b[l].reshape(1, SG_WIDTH)
        t_lat = b * n_lat
        t_tok = t_lat if last else t_lat + b * n_ctx
        xn2_buf = None if last else jnp.zeros((t_tok * SUBLANES, LANES), F32)
        h, xn2 = _merge(h, y_a, p, y_c, wb, wo, sgw, sgb, lng, lnb, gt1, g_ffn, sh2, sc2, xn2_buf, t_tok, 0, tm_mrg)
        if not last:
            qc_hm = (_heads_major(pc[..., OFF_QA:OFF_QC]).astype(F32) * DA_Q_SCALE).astype(BF16)
            ya_c = _diff_attn(qc_hm, kc_hm, vc_hm, da_lambda[l], g_sub, lam_init)
            yc_c = _ctx_mha(pc)
            hc, xn2 = _merge(hc, ya_c, pc, yc_c, wb, wo, sgw, sgb, lng, lnb, cgt1, g_ffn, csh2, csc2,
                             xn2, t_tok, t_lat, tm_ctx)

        w_rt = jnp.zeros((d, LANES), F32).at[:, :N_GROUPS].set(moe_w_group[l]) \
            .at[:, N_GROUPS:N_GROUPS + N_EXPERTS].set(moe_w_router[l])
        b_rt = jnp.zeros((1, LANES), F32).at[0, :N_GROUPS].set(moe_b_group[l]) \
            .at[0, N_GROUPS:N_GROUPS + N_EXPERTS].set(moe_b_router[l])
        route, code, pad_start, ys = _moe(xn2, w_rt, b_rt, l, moe_w_gate, moe_w_up, moe_w_down)
        h = _residual(code, pad_start, h, ys, route, gt2, g_final.reshape(1, d), 0, last)
        if not last:
            hc = _residual(code, pad_start, hc, ys, route, cgt2, g_final.reshape(1, d), t_lat, False)
    return h
```

```python
import functools
import math

import numpy as np
import jax
import jax.numpy as jnp
from jax import lax
from jax.experimental import pallas as pl
from jax.experimental.pallas import tpu as pltpu

F32 = jnp.float32
BF16 = jnp.bfloat16

D_MODEL = 1024
GRID_W = 64
EPS = 1e-6
NEG_INF = -1e30
ROPE_THETA = 10000.0

DA_HEADS = 4
DA_HEAD_DIM = 64
NA_HEADS = 8
NA_ROWS = 8
NA_COLS = 16
SG_CHUNK = 128
SG_GROUPS = 4
SG_WIDTH = 512
BRANCH_WIDTH = 512
N_BRANCH = 3

OFF_KA = 0
OFF_VA = 512
OFF_KC = 1024
OFF_VC = 1536
KV_COLS = 2048
OFF_QA = 2048
OFF_QC = 2560
OFF_ZB = 3072
OFF_GATE = 4096
IN_COLS = 7168

N_GROUPS = 4
EXPERTS_PER_GROUP = 8
N_EXPERTS = 32
D_EXPERT = 512

LANES = 128
HEAD_PAIR = LANES
VMEM_LIMIT = 56 * 1024 * 1024

MOE_ROWS = 256
DMA_CHUNK = 512

_CONTRACT_LAST = (((1,), (1,)), ((), ()))


def _params(sem, vmem=VMEM_LIMIT, flags=None):
    return pltpu.CompilerParams(dimension_semantics=sem, vmem_limit_bytes=vmem, flags=flags)


DOT_ROWS = 256


def _dot_rows(a, b, contract_last=False):
    dims = _CONTRACT_LAST if contract_last else (((1,), (0,)), ((), ()))
    n = a.shape[0]
    if n <= DOT_ROWS:
        return lax.dot_general(a, b, dims, preferred_element_type=F32)
    return jnp.concatenate([lax.dot_general(a[r0:r0 + DOT_ROWS], b, dims, preferred_element_type=F32)
                            for r0 in range(0, n, DOT_ROWS)], axis=0)


def _ada_kernel(cond_ref, w_ref, b_ref, o_ref):
    c = cond_ref[...]
    c = c * jax.nn.sigmoid(c)
    o_ref[0] = jnp.dot(c, w_ref[0], preferred_element_type=F32, precision=lax.Precision.HIGHEST) + b_ref[0]


def _ada(cond, w_ada, b_ada):
    n_layers, d, d6 = w_ada.shape
    tn = 1024
    return pl.pallas_call(
        _ada_kernel,
        grid=(n_layers, d6 // tn),
        in_specs=[
            pl.BlockSpec((8, d), lambda l, j: (0, 0)),
            pl.BlockSpec((1, d, tn), lambda l, j: (l, 0, j)),
            pl.BlockSpec((1, 1, tn), lambda l, j: (l, 0, j)),
        ],
        out_specs=pl.BlockSpec((1, 8, tn), lambda l, j: (l, 0, j)),
        out_shape=jax.ShapeDtypeStruct((n_layers, 8, d6), F32),
        compiler_params=_params(("parallel", "parallel")),
        name="ada_mod",
    )(cond, w_ada, b_ada)


def _rms_mod(x, g, shift, scale):
    y = x * lax.rsqrt(jnp.mean(x * x, axis=-1, keepdims=True) + EPS)
    return (y * g) * (1.0 + scale) + shift


def _norm_proj_kernel(h_ref, g_ref, sh_ref, sc_ref, w_ref, o_ref, xn_ref):
    @pl.when(pl.program_id(2) == 0)
    def _():
        xn_ref[...] = _rms_mod(h_ref[0], g_ref[...], sh_ref[0], sc_ref[0]).astype(BF16)

    o_ref[0] = _dot_rows(xn_ref[...], w_ref[...]).astype(o_ref.dtype)


def _norm_proj(h, g, shift, scale, w, n_cols, tm):
    b, n, d = h.shape
    tn = 1024
    return pl.pallas_call(
        _norm_proj_kernel,
        grid=(b, n // tm, n_cols // tn),
        in_specs=[
            pl.BlockSpec((1, tm, d), lambda bi, i, j: (bi, i, 0)),
            pl.BlockSpec((1, d), lambda bi, i, j: (0, 0)),
            pl.BlockSpec((1, 1, d), lambda bi, i, j: (bi, 0, 0)),
            pl.BlockSpec((1, 1, d), lambda bi, i, j: (bi, 0, 0)),
            pl.BlockSpec((d, tn), lambda bi, i, j: (0, j)),
        ],
        out_specs=pl.BlockSpec((1, tm, tn), lambda bi, i, j: (bi, i, j)),
        out_shape=jax.ShapeDtypeStruct((b, n, n_cols), BF16),
        scratch_shapes=[pltpu.VMEM((tm, d), BF16)],
        compiler_params=_params(("parallel", "parallel", "arbitrary")),
        name="norm_proj",
    )(h, g, shift, scale, w)


def _rope_tables(n_tok):
    t = jnp.arange(n_tok, dtype=jnp.int32)
    row = (t // GRID_W).astype(F32)
    col = (t % GRID_W).astype(F32)
    half = DA_HEAD_DIM // 4
    inv = ROPE_THETA ** (-jnp.arange(half, dtype=F32) / half)
    ar = row[:, None] * inv
    ac = col[:, None] * inv
    ang = jnp.concatenate([ar, ar, ac, ac], axis=-1)
    sign = np.tile(np.concatenate([-np.ones(half), np.ones(half)]), 2).astype(np.float32)
    cos = jnp.cos(ang)
    sin = jnp.sin(ang) * sign
    return jnp.concatenate([cos, cos], axis=-1), jnp.concatenate([sin, sin], axis=-1)


DA_Q_SCALE = DA_HEAD_DIM ** -0.5 * math.log2(math.e)


def _rope_kernel(q_ref, k_ref, v_ref, cos_ref, sin_ref, qo_ref, ko_ref, vo_ref):
    cos = cos_ref[...]
    sin = sin_ref[...]
    lane = lax.broadcasted_iota(jnp.int32, cos.shape, 1)
    first = (lane % (DA_HEAD_DIM // 2)) < (DA_HEAD_DIM // 4)
    seg = DA_HEAD_DIM // 4

    def rope(x):
        partner = jnp.where(first, pltpu.roll(x, LANES - seg, 1), pltpu.roll(x, seg, 1))
        return x * cos + partner * sin

    for hd in range(DA_HEADS):
        sl = slice(hd * HEAD_PAIR, (hd + 1) * HEAD_PAIR)
        q = q_ref[0, :, sl].astype(F32)
        k = k_ref[0, :, sl].astype(F32)
        qo_ref[0, hd] = (rope(q) * DA_Q_SCALE).astype(BF16)
        ko_ref[0, hd] = rope(k).astype(BF16)
        vo_ref[0, hd] = v_ref[0, :, sl]


def _rope(p, cos, sin, tm):
    b, n, _ = p.shape
    w = DA_HEADS * HEAD_PAIR
    hm = jax.ShapeDtypeStruct((b, DA_HEADS, n, HEAD_PAIR), BF16)
    hm_spec = pl.BlockSpec((1, DA_HEADS, tm, HEAD_PAIR), lambda bi, i: (bi, 0, i, 0))
    return pl.pallas_call(
        _rope_kernel,
        grid=(b, n // tm),
        in_specs=[
            pl.BlockSpec((1, tm, w), lambda bi, i: (bi, i, OFF_QA // w)),
            pl.BlockSpec((1, tm, w), lambda bi, i: (bi, i, OFF_KA // w)),
            pl.BlockSpec((1, tm, w), lambda bi, i: (bi, i, OFF_VA // w)),
            pl.BlockSpec((tm, LANES), lambda bi, i: (i, 0)),
            pl.BlockSpec((tm, LANES), lambda bi, i: (i, 0)),
        ],
        out_specs=[hm_spec, hm_spec, hm_spec],
        out_shape=[hm, hm, hm],
        compiler_params=_params(("parallel", "parallel")),
        name="rope_heads",
    )(p, p, p, cos, sin)


def _heads_major(t):
    b, n, _ = t.shape
    return t.reshape(b, n, DA_HEADS, HEAD_PAIR).transpose(0, 2, 1, 3)


def _split_pair(q):
    lo = lax.broadcasted_iota(jnp.int32, q.shape, 1) < (HEAD_PAIR // 2)
    zero = jnp.zeros_like(q)
    return jnp.concatenate([jnp.where(lo, q, zero), jnp.where(lo, zero, q)], axis=0)


DA_DOT_ROWS = 256


def _diff_attn_kernel(lam_ref, g_ref, q_ref, k_ref, v_ref, o_ref, qs_ref, m_ref, l_ref, acc_ref,
                      *, tq, tk, lam_init):
    n_blk = k_ref.shape[2] // tk
    n_lt = tk // LANES
    qs_ref[...] = _split_pair(q_ref[0, 0])
    chunks = [slice(r0, r0 + DA_DOT_ROWS) for r0 in range(0, 2 * tq, DA_DOT_ROWS)]

    m_ref[...] = jnp.full(m_ref.shape, NEG_INF, F32)
    l_ref[...] = jnp.zeros(l_ref.shape, F32)
    acc_ref[...] = jnp.zeros(acc_ref.shape, F32)

    def key_block(j):
        keys = slice(0, tk) if n_blk == 1 else pl.ds(pl.multiple_of(j * tk, tk), tk)
        k = k_ref[0, 0, keys, :]
        v = v_ref[0, 0, keys, :]
        s = jnp.concatenate([lax.dot_general(qs_ref[c], k, _CONTRACT_LAST, preferred_element_type=F32)
                             for c in chunks], axis=0)
        tiles = [s[:, t * LANES:(t + 1) * LANES] for t in range(n_lt)]
        m_prev = m_ref[...]
        m_new = jnp.maximum(m_prev, jnp.max(functools.reduce(jnp.maximum, tiles), axis=-1, keepdims=True))
        alpha = jnp.exp2(m_prev - m_new)
        p_tiles = [jnp.exp2(t - m_new) for t in tiles]
        l_ref[...] = alpha * l_ref[...] + functools.reduce(jnp.add, p_tiles)
        p = jnp.concatenate(p_tiles, axis=1).astype(BF16)
        pv = jnp.concatenate([jnp.dot(p[c], v, preferred_element_type=F32) for c in chunks], axis=0)
        acc_ref[...] = alpha * acc_ref[...] + pv
        m_ref[...] = m_new

    if n_blk == 1:
        key_block(0)
    else:
        def body(j, carry):
            key_block(j)
            return carry
        lax.fori_loop(0, n_blk, body, 0)

    o = acc_ref[...] / jnp.sum(l_ref[...], axis=-1, keepdims=True)
    lm = lam_ref[...]
    lam = (jnp.exp(jnp.sum(lm[0:1] * lm[1:2], axis=-1, keepdims=True))
           - jnp.exp(jnp.sum(lm[2:3] * lm[3:4], axis=-1, keepdims=True)) + lam_init)
    d = o[:tq] - lam * o[tq:]
    y = d * lax.rsqrt(jnp.mean(d * d, axis=-1, keepdims=True) + EPS)
    o_ref[0] = ((y * g_ref[...]) * (1.0 - lam_init)).astype(o_ref.dtype)


def _pick(n, options):
    for o in options:
        if n % o == 0:
            return o
    raise ValueError(f"no tile in {options} divides {n}")


def _diff_attn(q, k, v, lam_params, g, lam_init):
    b, nh, nq, _ = q.shape
    nk = k.shape[2]
    tq = _pick(nq, (512, 256))
    tk = _pick(nk, (2816, 1408, 768, 512, 256))
    kern = functools.partial(_diff_attn_kernel, tq=tq, tk=tk, lam_init=lam_init)
    return pl.pallas_call(
        kern,
        grid=(b, nh, nq // tq),
        in_specs=[
            pl.BlockSpec((4, DA_HEAD_DIM), lambda bi, h, i: (0, 0)),
            pl.BlockSpec((1, HEAD_PAIR), lambda bi, h, i: (0, 0)),
            pl.BlockSpec((1, 1, tq, HEAD_PAIR), lambda bi, h, i: (bi, h, i, 0)),
            pl.BlockSpec((1, 1, nk, HEAD_PAIR), lambda bi, h, i: (bi, h, 0, 0)),
            pl.BlockSpec((1, 1, nk, HEAD_PAIR), lambda bi, h, i: (bi, h, 0, 0)),
        ],
        out_specs=pl.BlockSpec((1, tq, HEAD_PAIR), lambda bi, h, i: (bi, i, h)),
        out_shape=jax.ShapeDtypeStruct((b, nq, nh * HEAD_PAIR), BF16),
        scratch_shapes=[
            pltpu.VMEM((2 * tq, HEAD_PAIR), BF16),
            pltpu.VMEM((2 * tq, LANES), F32),
            pltpu.VMEM((2 * tq, LANES), F32),
            pltpu.VMEM((2 * tq, HEAD_PAIR), F32),
        ],
        compiler_params=_params(("parallel", "parallel", "parallel")),
        name="diff_attn",
    )(lam_params, g, q, k, v)


NA_QROWS = 8
NA_QTOK = NA_QROWS * GRID_W
NA_KBLK = 4 * GRID_W
NA_PIECES = 4
NA_WIN = NA_PIECES * NA_KBLK


def _na_bias_tables(rpb, rows):
    n_h = rpb.shape[0]
    n_kj = NA_PIECES * 4
    pad_r = n_kj - NA_ROWS
    pad_c = GRID_W - NA_COLS
    rp = jnp.pad(rpb * math.log2(math.e), ((0, 0), (pad_r, pad_r), (pad_c, pad_c)))
    a = jnp.stack([rp[:, :, GRID_W - 1 - qc:2 * GRID_W - 1 - qc] for qc in range(GRID_W)], axis=2)
    bias = jnp.stack([a[:, 3 + pad_r - qi:3 + pad_r - qi + n_kj].transpose(0, 2, 1, 3) for qi in range(NA_QROWS)],
                     axis=1).reshape(n_h, NA_QTOK, NA_WIN)

    n_r = rows // NA_QROWS
    qi = np.arange(NA_QROWS)
    kj = np.arange(n_kj)
    c = np.arange(GRID_W)
    cstart = np.clip(c - NA_COLS // 2, 0, GRID_W - NA_COLS)
    col_ok = (c[None, :] >= cstart[:, None]) & (c[None, :] < cstart[:, None] + NA_COLS)
    row_ok = []
    for r_grp in (0, min(1, n_r - 1), n_r - 1):
        r = NA_QROWS * r_grp + qi
        rs = np.clip(r - NA_ROWS // 2, 0, rows - NA_ROWS)
        krow = NA_QROWS * r_grp - 4 + kj
        row_ok.append((krow[None, :] >= rs[:, None]) & (krow[None, :] < rs[:, None] + NA_ROWS)
                      & (krow[None, :] >= 0) & (krow[None, :] < rows))
    ok = jnp.asarray(np.stack(row_ok))[:, :, None, :, None] & jnp.asarray(col_ok)[None, None, :, None, :]
    mask = jnp.where(ok, 0.0, NEG_INF).astype(F32).reshape(3, NA_QTOK, NA_WIN)
    return bias, mask


def _pair_softmax_pv(qs, k_list, v_list, bias_list):
    n_rows = qs.shape[0]
    chunks = [slice(r0, r0 + DA_DOT_ROWS) for r0 in range(0, n_rows, DA_DOT_ROWS)]
    tiles_list = []
    for k, bias in zip(k_list, bias_list):
        s = jnp.concatenate([lax.dot_general(qs[c], k, _CONTRACT_LAST, preferred_element_type=F32)
                             for c in chunks], axis=0)
        s = s if bias is None else s + bias
        tiles_list.append([s[:, j * LANES:(j + 1) * LANES] for j in range(k.shape[0] // LANES)])
    all_tiles = [t for tiles in tiles_list for t in tiles]
    m = jnp.max(functools.reduce(jnp.maximum, all_tiles), axis=-1, keepdims=True)
    p_list = [[jnp.exp2(t - m) for t in tiles] for tiles in tiles_list]
    l = jnp.sum(functools.reduce(jnp.add, [t for tiles in p_list for t in tiles]), axis=-1, keepdims=True)
    o = None
    for tiles, v in zip(p_list, v_list):
        p = jnp.concatenate(tiles, axis=1).astype(BF16)
        pv = jnp.concatenate([jnp.dot(p[c], v, preferred_element_type=F32) for c in chunks], axis=0)
        o = pv if o is None else o + pv
    o = o / l
    t = qs.shape[0] // 2
    lo = lax.broadcasted_iota(jnp.int32, (t, HEAD_PAIR), 1) < (HEAD_PAIR // 2)
    return jnp.where(lo, o[:t], o[t:])


def _scaled_pair(q, head_dim):
    return _split_pair((q.astype(F32) * (head_dim ** -0.5 * math.log2(math.e))).astype(BF16))


def _na_kernel(q_ref, k0, k1, k2, k3, v0, v1, v2, v3, kc_ref, vc_ref, b_ref, mk_ref, o_ref):
    qs = _scaled_pair(q_ref[0], HEAD_PAIR // 2)
    k_win = jnp.concatenate([k0[0], k1[0], k2[0], k3[0]], axis=0)
    v_win = jnp.concatenate([v0[0], v1[0], v2[0], v3[0]], axis=0)
    bias = (b_ref[...] + mk_ref[...]).reshape(2 * NA_QTOK, NA_WIN)
    o = _pair_softmax_pv(qs, [k_win, kc_ref[0]], [v_win, vc_ref[0]], [bias, None])
    o_ref[0] = o.astype(o_ref.dtype)


def _na_attn(p, pc, bias, mask):
    b, n, _ = p.shape
    n_ctx = pc.shape[1]
    n_r = n // NA_QTOK
    n_kb = n // NA_KBLK
    n_hp = NA_HEADS // 2

    def case(r):
        return jnp.where(r == 0, 0, jnp.where(r == n_r - 1, 2, 1))

    def kv_spec(off, piece):
        return pl.BlockSpec(
            (1, NA_KBLK, HEAD_PAIR),
            lambda bi, hp, r: (bi, jnp.clip(2 * r - 1 + piece, 0, n_kb - 1), off // HEAD_PAIR + hp))

    in_specs = [pl.BlockSpec((1, NA_QTOK, HEAD_PAIR), lambda bi, hp, r: (bi, r, OFF_QC // HEAD_PAIR + hp))]
    in_specs += [kv_spec(OFF_KC, i) for i in range(NA_PIECES)]
    in_specs += [kv_spec(OFF_VC, i) for i in range(NA_PIECES)]
    in_specs += [
        pl.BlockSpec((1, n_ctx, HEAD_PAIR), lambda bi, hp, r: (bi, 0, OFF_KC // HEAD_PAIR + hp)),
        pl.BlockSpec((1, n_ctx, HEAD_PAIR), lambda bi, hp, r: (bi, 0, OFF_VC // HEAD_PAIR + hp)),
        pl.BlockSpec((2, NA_QTOK, NA_WIN), lambda bi, hp, r: (hp, 0, 0)),
        pl.BlockSpec((1, NA_QTOK, NA_WIN), lambda bi, hp, r: (case(r), 0, 0)),
    ]
    return pl.pallas_call(
        _na_kernel,
        grid=(b, n_hp, n_r),
        in_specs=in_specs,
        out_specs=pl.BlockSpec((1, NA_QTOK, HEAD_PAIR), lambda bi, hp, r: (bi, r, hp)),
        out_shape=jax.ShapeDtypeStruct((b, n, NA_HEADS * HEAD_PAIR // 2), BF16),
        compiler_params=_params(("parallel", "parallel", "arbitrary")),
        name="na_attn",
    )(p, *([p] * (2 * NA_PIECES)), pc, pc, bias, mask)


def _ctx_mha_kernel(q_ref, k_ref, v_ref, o_ref):
    qs = _scaled_pair(q_ref[0], HEAD_PAIR // 2)
    o_ref[0] = _pair_softmax_pv(qs, [k_ref[0]], [v_ref[0]], [None]).astype(o_ref.dtype)


def _ctx_mha(pc):
    b, n_ctx, _ = pc.shape
    n_hp = NA_HEADS // 2

    def spec(off):
        return pl.BlockSpec((1, n_ctx, HEAD_PAIR), lambda bi, hp: (bi, 0, off // HEAD_PAIR + hp))

    return pl.pallas_call(
        _ctx_mha_kernel,
        grid=(b, n_hp),
        in_specs=[spec(OFF_QC), spec(OFF_KC), spec(OFF_VC)],
        out_specs=pl.BlockSpec((1, n_ctx, HEAD_PAIR), lambda bi, hp: (bi, 0, hp)),
        out_shape=jax.ShapeDtypeStruct((b, n_ctx, NA_HEADS * HEAD_PAIR // 2), BF16),
        compiler_params=_params(("parallel", "parallel")),
        name="ctx_mha",
    )(pc, pc, pc)


def _gelu_tanh(x):
    return 0.5 * x * (1.0 + jnp.tanh(math.sqrt(2.0 / math.pi) * (x + 0.044715 * (x * x * x))))


SUBLANES = 8


def _store_token_tiles(ref, x):
    t = x.shape[0]
    for j in range(SUBLANES):
        ref[pl.ds(j, t, stride=SUBLANES), :] = x[:, j * LANES:(j + 1) * LANES]


def _load_token_tiles(ref, t):
    return jnp.concatenate([ref[pl.ds(j, t, stride=SUBLANES), :] for j in range(SUBLANES)], axis=1)


def _merge_kernel(h_ref, ya_ref, z_ref, yc_ref, g0_ref, g1_ref, g2_ref, wb_ref, wo_ref, sgw_ref, sgb_ref,
                  lng_ref, lnb_ref, gt_ref, gf_ref, sh_ref, sc_ref, ho_ref, xo_ref, *, tm):
    z = _gelu_tanh(z_ref[0].astype(F32))
    u = z[:, :SG_WIDTH]
    vv = z[:, SG_WIDTH:]
    mu = jnp.mean(vv, axis=-1, keepdims=True)
    var = jnp.mean(jnp.square(vv - mu), axis=-1, keepdims=True)
    vv = ((vv - mu) * lax.rsqrt(var + EPS)) * lng_ref[...] + lnb_ref[...]
    vv = vv.astype(BF16)
    gd = SG_WIDTH // SG_GROUPS
    chunks = []
    for c in range(tm // SG_CHUNK):
        rows = slice(c * SG_CHUNK, (c + 1) * SG_CHUNK)
        groups = []
        for g in range(SG_GROUPS):
            s = jnp.dot(sgw_ref[g], vv[rows, g * gd:(g + 1) * gd], preferred_element_type=F32) + sgb_ref[g]
            groups.append(s)
        chunks.append(jnp.concatenate(groups, axis=1))
    y_b = (u * jnp.concatenate(chunks, axis=0)).astype(BF16)

    merged = None
    for y, gate_ref, i in ((ya_ref[0], g0_ref, 0), (y_b, g1_ref, 1), (yc_ref[0], g2_ref, 2)):
        t = jax.nn.sigmoid(gate_ref[0].astype(F32)) * _dot_rows(y, wb_ref[i])
        merged = t if merged is None else merged + t
    out = _dot_rows(merged.astype(BF16), wo_ref[...])
    h_new = h_ref[0] + gt_ref[0] * out
    ho_ref[0] = h_new
    _store_token_tiles(xo_ref, _rms_mod(h_new, gf_ref[...], sh_ref[0], sc_ref[0]))


def _merge(h, y_a, p, y_c, wb, wo, sgw, sgb, lng, lnb, gt1, g_ffn, sh2, sc2, xn2_buf, xn2_rows, row_off, tm):
    b, n, d = h.shape
    n_i = n // tm
    off_blk = row_off // tm
    vec = pl.BlockSpec((1, 1, d), lambda bi, i: (bi, 0, 0))

    def col(width, off):
        return pl.BlockSpec((1, tm, width), lambda bi, i: (bi, i, off // width))

    kern = functools.partial(_merge_kernel, tm=tm)
    args = [h, y_a, p, y_c, p, p, p, wb, wo, sgw, sgb, lng, lnb, gt1, g_ffn, sh2, sc2]
    in_specs = [
        pl.BlockSpec((1, tm, d), lambda bi, i: (bi, i, 0)),
        col(BRANCH_WIDTH, 0),
        col(2 * SG_WIDTH, OFF_ZB),
        col(BRANCH_WIDTH, 0),
        col(d, OFF_GATE), col(d, OFF_GATE + d), col(d, OFF_GATE + 2 * d),
        pl.BlockSpec((N_BRANCH, BRANCH_WIDTH, d), lambda bi, i: (0, 0, 0)),
        pl.BlockSpec((d, d), lambda bi, i: (0, 0)),
        pl.BlockSpec((SG_GROUPS, SG_CHUNK, SG_CHUNK), lambda bi, i: (0, 0, 0)),
        pl.BlockSpec((SG_GROUPS, SG_CHUNK, SG_CHUNK), lambda bi, i: (0, 0, 0)),
        pl.BlockSpec((1, SG_WIDTH), lambda bi, i: (0, 0)),
        pl.BlockSpec((1, SG_WIDTH), lambda bi, i: (0, 0)),
        vec,
        pl.BlockSpec((1, d), lambda bi, i: (0, 0)),
        vec, vec,
    ]
    aliases = {}
    if xn2_buf is not None:
        args.append(xn2_buf)
        in_specs.append(pl.BlockSpec(memory_space=pl.ANY))
        aliases = {len(args) - 1: 1}
    assert d == SUBLANES * LANES
    xn2_shape = jax.ShapeDtypeStruct((xn2_rows * SUBLANES, LANES), F32)

    def body(*refs):
        n_in = 17
        kern(*refs[:n_in], *refs[len(refs) - 2:])

    return pl.pallas_call(
        body,
        grid=(b, n_i),
        in_specs=in_specs,
        out_specs=[
            pl.BlockSpec((1, tm, d), lambda bi, i: (bi, i, 0)),
            pl.BlockSpec((tm * SUBLANES, LANES), lambda bi, i: (off_blk + bi * n_i + i, 0)),
        ],
        out_shape=[jax.ShapeDtypeStruct((b, n, d), F32), xn2_shape],
        input_output_aliases=aliases,
        compiler_params=_params(("parallel", "parallel")),
        name="merge_branches",
    )(*args)


R_E1, R_E2, R_W1, R_W2, R_RANK1, R_RANK2 = range(6)


def _router_kernel(x_ref, w_ref, b_ref, o_ref, cnt_ref, run_ref, *, tm):
    @pl.when(pl.program_id(0) == 0)
    def _():
        run_ref[...] = jnp.zeros(run_ref.shape, F32)

    logits = jnp.dot(_load_token_tiles(x_ref, tm), w_ref[...], preferred_element_type=F32,
                     precision=lax.Precision.HIGHEST) + b_ref[...]
    lane = lax.broadcasted_iota(jnp.int32, logits.shape, 1)
    lane_f = lane.astype(F32)
    far = jnp.float32(1e9)

    def first_lane(mask):
        return jnp.min(jnp.where(mask, lane_f, far), axis=-1, keepdims=True)

    is_g = lane < N_GROUPS
    gl = jnp.where(is_g, logits, NEG_INF)
    g_max = jnp.max(gl, axis=-1, keepdims=True)
    g_idx = first_lane(is_g & (gl == g_max))
    g_sum = jnp.sum(jnp.where(is_g, jnp.exp(gl - g_max), 0.0), axis=-1, keepdims=True)
    g_w = 1.0 / g_sum

    e_lane = lane - N_GROUPS
    in_grp = (e_lane >= 0) & (e_lane < N_EXPERTS) & \
        (jnp.right_shift(e_lane, 3).astype(F32) == g_idx)
    el = jnp.where(in_grp, logits, NEG_INF)
    l1 = jnp.max(el, axis=-1, keepdims=True)
    i1 = first_lane(in_grp & (el == l1))
    rest = in_grp & (lane_f != i1)
    el2 = jnp.where(rest, logits, NEG_INF)
    l2 = jnp.max(el2, axis=-1, keepdims=True)
    i2 = first_lane(rest & (el2 == l2))
    t = jnp.exp(l2 - l1)
    w1 = g_w / (1.0 + t)
    w2 = g_w * t / (1.0 + t)

    oh1 = lane_f == i1
    oh2 = lane_f == i2
    oh = jnp.where(oh1 | oh2, 1.0, 0.0)
    row = lax.broadcasted_iota(jnp.int32, (tm, tm), 0)
    colm = lax.broadcasted_iota(jnp.int32, (tm, tm), 1)
    before = jnp.where(colm < row, 1.0, 0.0).astype(BF16)
    prior = jnp.dot(before, oh.astype(BF16), preferred_element_type=F32) + run_ref[...]
    rank1 = jnp.sum(jnp.where(oh1, prior, 0.0), axis=-1, keepdims=True)
    rank2 = jnp.sum(jnp.where(oh2, prior, 0.0), axis=-1, keepdims=True)
    run_new = run_ref[...] + jnp.sum(oh, axis=0, keepdims=True)
    run_ref[...] = run_new
    cnt_ref[...] = run_new

    slab = jnp.zeros(logits.shape, F32)
    for ln, val in ((R_E1, i1 - N_GROUPS), (R_E2, i2 - N_GROUPS), (R_W1, w1), (R_W2, w2),
                    (R_RANK1, rank1), (R_RANK2, rank2)):
        slab = jnp.where(lane == ln, val, slab)
    o_ref[...] = slab


def _router(xn2, w_rt, b_rt):
    t_tok = xn2.shape[0] // SUBLANES
    d = w_rt.shape[0]
    tm = 512
    kern = functools.partial(_router_kernel, tm=tm)
    return pl.pallas_call(
        kern,
        grid=(t_tok // tm,),
        in_specs=[
            pl.BlockSpec((tm * SUBLANES, LANES), lambda i: (i, 0)),
            pl.BlockSpec((d, LANES), lambda i: (0, 0)),
            pl.BlockSpec((1, LANES), lambda i: (0, 0)),
        ],
        out_specs=[pl.BlockSpec((tm, LANES), lambda i: (i, 0)), pl.BlockSpec((1, LANES), lambda i: (0, 0))],
        out_shape=[jax.ShapeDtypeStruct((t_tok, LANES), F32), jax.ShapeDtypeStruct((1, LANES), F32)],
        scratch_shapes=[pltpu.VMEM((1, LANES), F32)],
        compiler_params=_params(("arbitrary",)),
        name="moe_router",
    )(xn2, w_rt, b_rt)


def _tile_rows(index):
    return pl.ds(pl.multiple_of(index * SUBLANES, SUBLANES), SUBLANES)


RANK_BITS = 20


def _slot(code_ref, start_ref, a):
    code = code_ref[a]
    return start_ref[jnp.right_shift(code, RANK_BITS)] + jnp.bitwise_and(code, (1 << RANK_BITS) - 1)


def _dispatch_kernel(code_ref, start_ref, end_ref, x_ref, xs_ref, zero_ref, sem, zsem, *, n_blocks):
    step = pl.program_id(0)
    blk_rows = MOE_ROWS * SUBLANES

    @pl.when(step == 0)
    def _():
        zero_ref[...] = jnp.zeros(zero_ref.shape, F32)

        def fill(blk):
            return pltpu.make_async_copy(zero_ref, xs_ref.at[pl.ds(pl.multiple_of(blk * blk_rows, blk_rows),
                                                                  blk_rows), :], zsem.at[0])

        def expert_tail(op):
            def body(e, carry):
                @pl.when(end_ref[e] > start_ref[e])
                def _():
                    op(fill(end_ref[e] // MOE_ROWS - 1))
                return carry
            lax.fori_loop(0, N_EXPERTS, body, 0)

        def unused(op):
            def body(blk, carry):
                @pl.when(blk * MOE_ROWS >= end_ref[N_EXPERTS - 1])
                def _():
                    op(fill(blk))
                return carry
            lax.fori_loop(0, n_blocks, body, 0)

        for phase in (lambda c: c.start(), lambda c: c.wait()):
            expert_tail(phase)
            unused(phase)

    def issue(i, carry):
        tok = step * DMA_CHUNK + i
        for k in range(2):
            pltpu.make_async_copy(x_ref.at[_tile_rows(i), :],
                                  xs_ref.at[_tile_rows(_slot(code_ref, start_ref, 2 * tok + k)), :],
                                  sem.at[0]).start()
        return carry

    lax.fori_loop(0, DMA_CHUNK, issue, 0, unroll=4)
    for _ in range(2):
        pltpu.make_async_copy(x_ref, xs_ref.at[pl.ds(0, DMA_CHUNK * SUBLANES), :], sem.at[0]).wait()


def _dispatch(code, pad_start, pad_end, xn2, n_slots):
    t_tok = xn2.shape[0] // SUBLANES
    kern = functools.partial(_dispatch_kernel, n_blocks=n_slots // MOE_ROWS)
    grid_spec = pltpu.PrefetchScalarGridSpec(
        num_scalar_prefetch=3,
        grid=(t_tok // DMA_CHUNK,),
        in_specs=[pl.BlockSpec((DMA_CHUNK * SUBLANES, LANES), lambda i, *_: (i, 0))],
        out_specs=pl.BlockSpec(memory_space=pl.ANY),
        scratch_shapes=[pltpu.VMEM((MOE_ROWS * SUBLANES, LANES), F32), pltpu.SemaphoreType.DMA((1,)),
                        pltpu.SemaphoreType.DMA((1,))],
    )
    return pl.pallas_call(
        kern,
        grid_spec=grid_spec,
        out_shape=jax.ShapeDtypeStruct((n_slots * SUBLANES, LANES), F32),
        compiler_params=_params(("arbitrary",)),
        name="moe_dispatch",
    )(code, pad_start, pad_end, xn2)


def _expert_kernel(be_ref, nxt_ref, nu_ref, x_ref, wg_hbm, wu_hbm, wd_hbm, o_ref,
                   wg_f, wu_f, wd_f, wg_s, wu_s, wd_s, sem, turn_ref, *, w_off):
    i = pl.program_id(0)
    used = i < nu_ref[0]

    def fetch(expert, slot):
        return [pltpu.make_async_copy(src.at[w_off + expert], dst.at[slot], sem.at[slot])
                for src, dst in ((wg_hbm, wg_f), (wu_hbm, wu_f), (wd_hbm, wd_f))]

    @pl.when(i == 0)
    def _():
        turn_ref[0] = 0
        for c in fetch(be_ref[0], 0):
            c.start()

    @pl.when(used & ((i == 0) | (be_ref[i] != be_ref[jnp.maximum(i - 1, 0)])))
    def _():
        slot = turn_ref[0] % 2
        for c in fetch(be_ref[i], slot):
            c.wait()
        nxt = nxt_ref[be_ref[i]]

        @pl.when(nxt >= 0)
        def _():
            for c in fetch(nxt, 1 - slot):
                c.start()

        wg_s[...] = wg_f[slot].astype(BF16)
        wu_s[...] = wu_f[slot].astype(BF16)
        wd_s[...] = wd_f[slot].astype(BF16)
        turn_ref[0] = turn_ref[0] + 1

    @pl.when(used)
    def _():
        x = _load_token_tiles(x_ref, MOE_ROWS).astype(BF16)
        gate = jnp.dot(x, wg_s[...], preferred_element_type=F32)
        up = jnp.dot(x, wu_s[...], preferred_element_type=F32)
        hdn = (gate * jax.nn.sigmoid(gate)) * up
        _store_token_tiles(o_ref, jnp.dot(hdn.astype(BF16), wd_s[...], preferred_element_type=F32))

    @pl.when(i >= nu_ref[0])
    def _():
        o_ref[...] = jnp.zeros(o_ref.shape, F32)


def _experts(block_expert, next_expert, n_used, xs, layer, w_gate, w_up, w_down):
    n_blocks = xs.shape[0] // (MOE_ROWS * SUBLANES)
    n_layers, n_e, d, de = w_gate.shape
    w_gate, w_up, w_down = (w.reshape(n_layers * n_e, *w.shape[2:]) for w in (w_gate, w_up, w_down))
    blk = pl.BlockSpec((MOE_ROWS * SUBLANES, LANES), lambda i, *_: (i, 0))
    hbm = pl.BlockSpec(memory_space=pl.ANY)
    grid_spec = pltpu.PrefetchScalarGridSpec(
        num_scalar_prefetch=3,
        grid=(n_blocks,),
        in_specs=[blk, hbm, hbm, hbm],
        out_specs=blk,
        scratch_shapes=[
            pltpu.VMEM((2, d, de), F32), pltpu.VMEM((2, d, de), F32), pltpu.VMEM((2, de, d), F32),
            pltpu.VMEM((d, de), BF16), pltpu.VMEM((d, de), BF16), pltpu.VMEM((de, d), BF16),
            pltpu.SemaphoreType.DMA((2,)), pltpu.SMEM((1,), jnp.int32),
        ],
    )
    return pl.pallas_call(
        functools.partial(_expert_kernel, w_off=layer * n_e),
        grid_spec=grid_spec,
        out_shape=jax.ShapeDtypeStruct(xs.shape, F32),
        compiler_params=_params(("arbitrary",)),
        name="moe_experts",
    )(block_expert, next_expert, n_used, xs, w_gate, w_up, w_down)


def _residual_kernel(code_ref, start_ref, h_ref, r_ref, gt_ref, gf_ref, ys_ref, o_ref, buf_ref, sem,
                     *, tm, n_i, row_off, final):
    step = pl.program_id(0) * n_i + pl.program_id(1)
    n_steps = pl.num_programs(0) * n_i
    slot = step % 2

    def gather(tile, into):
        base = row_off + tile * tm

        def issue(t, carry):
            for k in range(2):
                pltpu.make_async_copy(ys_ref.at[_tile_rows(_slot(code_ref, start_ref, 2 * (base + t) + k)), :],
                                      buf_ref.at[into, k, _tile_rows(t), :], sem.at[into]).start()
            return carry

        lax.fori_loop(0, tm, issue, 0, unroll=4)

    @pl.when(step == 0)
    def _():
        gather(step, slot)

    @pl.when(step + 1 < n_steps)
    def _():
        gather(step + 1, 1 - slot)

    for k in range(2):
        pltpu.make_async_copy(ys_ref.at[pl.ds(0, tm * SUBLANES), :], buf_ref.at[slot, k], sem.at[slot]).wait()

    r = r_ref[...]
    y = (_load_token_tiles(buf_ref.at[slot, 0], tm) * r[:, R_W1:R_W1 + 1]
         + _load_token_tiles(buf_ref.at[slot, 1], tm) * r[:, R_W2:R_W2 + 1])
    h_new = h_ref[0] + gt_ref[0] * y
    if final:
        h_new = (h_new * lax.rsqrt(jnp.mean(h_new * h_new, axis=-1, keepdims=True) + EPS)) * gf_ref[...]
    o_ref[0] = h_new


def _residual(code, pad_start, h, ys, route, gt2, g_final, row_off, final):
    b, n, d = h.shape
    tm = 256
    n_i = n // tm
    off_blk = row_off // tm
    kern = functools.partial(_residual_kernel, tm=tm, n_i=n_i, row_off=row_off, final=final)
    grid_spec = pltpu.PrefetchScalarGridSpec(
        num_scalar_prefetch=2,
        grid=(b, n_i),
        in_specs=[
            pl.BlockSpec((1, tm, d), lambda bi, i, *_: (bi, i, 0)),
            pl.BlockSpec((tm, LANES), lambda bi, i, *_: (off_blk + bi * n_i + i, 0)),
            pl.BlockSpec((1, 1, d), lambda bi, i, *_: (bi, 0, 0)),
            pl.BlockSpec((1, d), lambda bi, i, *_: (0, 0)),
            pl.BlockSpec(memory_space=pl.ANY),
        ],
        out_specs=pl.BlockSpec((1, tm, d), lambda bi, i, *_: (bi, i, 0)),
        scratch_shapes=[pltpu.VMEM((2, 2, tm * SUBLANES, LANES), F32), pltpu.SemaphoreType.DMA((2,))],
    )
    return pl.pallas_call(
        kern,
        grid_spec=grid_spec,
        out_shape=jax.ShapeDtypeStruct((b, n, d), F32),
        compiler_params=_params(("arbitrary", "arbitrary")),
        name="moe_residual",
    )(code, pad_start, h, route, gt2, g_final, ys)


def _moe(xn2, w_rt, b_rt, layer, w_gate, w_up, w_down):
    t_tok = xn2.shape[0] // SUBLANES
    assert 2 * t_tok < (1 << RANK_BITS)
    route, counts = _router(xn2, w_rt, b_rt)
    cnt = counts[0, N_GROUPS:N_GROUPS + N_EXPERTS].astype(jnp.int32)
    padded = (cnt + MOE_ROWS - 1) // MOE_ROWS * MOE_ROWS
    pad_end = jnp.cumsum(padded)
    pad_start = pad_end - padded
    ids = route[:, R_E1:R_RANK2 + 1].astype(jnp.int32)
    code = (jnp.left_shift(ids[:, R_E1:R_E2 + 1], RANK_BITS) + ids[:, R_RANK1:R_RANK2 + 1]).reshape(2 * t_tok)
    n_blocks = -(-(2 * t_tok + N_EXPERTS * (MOE_ROWS - 1)) // MOE_ROWS)
    n_slots = n_blocks * MOE_ROWS
    starts = jnp.arange(n_blocks, dtype=jnp.int32) * MOE_ROWS
    block_expert = jnp.minimum(jnp.sum(starts[:, None] >= pad_end[None, :], axis=1), N_EXPERTS - 1).astype(jnp.int32)
    n_used = (pad_end[-1:] // MOE_ROWS).astype(jnp.int32)

    e_ids = jnp.arange(N_EXPERTS, dtype=jnp.int32)
    later = (padded > 0)[None, :] & (e_ids[None, :] > e_ids[:, None])
    next_expert = jnp.min(jnp.where(later, e_ids[None, :], N_EXPERTS), axis=1)
    next_expert = jnp.where(next_expert == N_EXPERTS, -1, next_expert).astype(jnp.int32)
    pad_start = pad_start.astype(jnp.int32)
    xs = _dispatch(code, pad_start, pad_end.astype(jnp.int32), xn2, n_slots)
    return route, code, pad_start, _experts(block_expert, next_expert, n_used, xs, layer, w_gate, w_up, w_down)


def kernel(x, c, ctx, c_ctx, w_ada, b_ada, g_norm_mix, g_norm_ffn, w_in, da_lambda, da_subln_g, sg_ln_g, sg_ln_b, sg_w, sg_b, na_rpb, w_branch, w_out, moe_w_group, moe_b_group, moe_w_router, moe_b_router, moe_w_gate, moe_w_up, moe_w_down, g_final):
    b, n_lat, d = x.shape
    n_ctx = ctx.shape[1]
    depth = w_in.shape[0]
    rows = n_lat // GRID_W
    assert d == D_MODEL and n_lat % NA_QTOK == 0 and rows >= 2 * NA_QROWS and n_ctx % 256 == 0 and b <= 7
    tm_lat = 1024 if n_lat % 1024 == 0 else 512
    tm_mrg = 512
    tm_ctx = 256

    cos, sin = _rope_tables(n_lat)
    cond = jnp.zeros((8, d), F32).at[:b].set(c).at[b].set(c_ctx)
    mods = _ada(cond, w_ada, b_ada.reshape(depth, 1, 6 * d))

    h, hc = x, ctx
    for l in range(depth):
        last = l == depth - 1
        lam_init = 0.8 - 0.6 * math.exp(-0.3 * l)
        m_lat = mods[l, :b].reshape(b, 1, 6, d)
        m_ctx = jnp.broadcast_to(mods[l, b].reshape(1, 1, 6, d), (b, 1, 6, d))
        sh1, sc1, gt1, sh2, sc2, gt2 = (m_lat[:, :, i] for i in range(6))
        csh1, csc1, cgt1, csh2, csc2, cgt2 = (m_ctx[:, :, i] for i in range(6))

        w_in_l = w_in[l].astype(BF16)
        g_mix = g_norm_mix[l].reshape(1, d)
        g_ffn = g_norm_ffn[l].reshape(1, d)
        p = _norm_proj(h, g_mix, sh1, sc1, w_in_l, IN_COLS, tm_lat)
        pc = _norm_proj(hc, g_mix, csh1, csc1, w_in_l, KV_COLS if last else IN_COLS, tm_ctx)

        q_hm, k_hm, v_hm = _rope(p, cos, sin, tm_mrg)
        kc_hm = _heads_major(pc[..., OFF_KA:OFF_VA])
        vc_hm = _heads_major(pc[..., OFF_VA:OFF_KC])
        k_all = jnp.concatenate([kc_hm, k_hm], axis=2)
        v_all = jnp.concatenate([vc_hm, v_hm], axis=2)
        g_sub = da_subln_g[l].reshape(1, 2 * DA_HEAD_DIM)
        y_a = _diff_attn(q_hm, k_all, v_all, da_lambda[l], g_sub, lam_init)

        y_c = _na_attn(p, pc, *_na_bias_tables(na_rpb[l], rows))

        wb = w_branch[l].astype(BF16)
        wo = w_out[l].astype(BF16)
        sgw = sg_w[l].astype(BF16)
        sgb = jnp.broadcast_to(sg_b[l][:, :, None], (SG_GROUPS, SG_CHUNK, SG_CHUNK))
        lng = sg_ln_g[l].reshape(1, SG_WIDTH)
        lnb = sg_ln_b[l].reshape(1, SG_WIDTH)
        t_lat = b * n_lat
        t_tok = t_lat if last else t_lat + b * n_ctx
        xn2_buf = None if last else jnp.zeros((t_tok * SUBLANES, LANES), F32)
        h, xn2 = _merge(h, y_a, p, y_c, wb, wo, sgw, sgb, lng, lnb, gt1, g_ffn, sh2, sc2, xn2_buf, t_tok, 0, tm_mrg)
        if not last:
            qc_hm = (_heads_major(pc[..., OFF_QA:OFF_QC]).astype(F32) * DA_Q_SCALE).astype(BF16)
            ya_c = _diff_attn(qc_hm, kc_hm, vc_hm, da_lambda[l], g_sub, lam_init)
            yc_c = _ctx_mha(pc)
            hc, xn2 = _merge(hc, ya_c, pc, yc_c, wb, wo, sgw, sgb, lng, lnb, cgt1, g_ffn, csh2, csc2,
                             xn2, t_tok, t_lat, tm_ctx)

        w_rt = jnp.zeros((d, LANES), F32).at[:, :N_GROUPS].set(moe_w_group[l]) \
            .at[:, N_GROUPS:N_GROUPS + N_EXPERTS].set(moe_w_router[l])
        b_rt = jnp.zeros((1, LANES), F32).at[0, :N_GROUPS].set(moe_b_group[l]) \
            .at[0, N_GROUPS:N_GROUPS + N_EXPERTS].set(moe_b_router[l])
        route, code, pad_start, ys = _moe(xn2, w_rt, b_rt, l, moe_w_gate, moe_w_up, moe_w_down)
        h = _residual(code, pad_start, h, ys, route, gt2, g_final.reshape(1, d), 0, last)
        if not last:
            hc = _residual(code, pad_start, hc, ys, route, cgt2, g_final.reshape(1, d), t_lat, False)
    return h
```

```python
import functools
import math

import numpy as np
import jax
import jax.numpy as jnp
from jax import lax
from jax.experimental import pallas as pl
from jax.experimental.pallas import tpu as pltpu

F32 = jnp.float32
BF16 = jnp.bfloat16

D_MODEL = 1024
GRID_W = 64
EPS = 1e-6
NEG_INF = -1e30
ROPE_THETA = 10000.0

DA_HEADS = 4
DA_HEAD_DIM = 64
NA_HEADS = 8
NA_ROWS = 8
NA_COLS = 16
SG_CHUNK = 128
SG_GROUPS = 4
SG_WIDTH = 512
BRANCH_WIDTH = 512
N_BRANCH = 3

OFF_KA = 0
OFF_VA = 512
OFF_KC = 1024
OFF_VC = 1536
KV_COLS = 2048
OFF_QA = 2048
OFF_QC = 2560
OFF_ZB = 3072
OFF_GATE = 4096
IN_COLS = 7168

N_GROUPS = 4
EXPERTS_PER_GROUP = 8
N_EXPERTS = 32
D_EXPERT = 512

LANES = 128
HEAD_PAIR = LANES
VMEM_LIMIT = 56 * 1024 * 1024

MOE_ROWS = 256
DMA_CHUNK = 512

_CONTRACT_LAST = (((1,), (1,)), ((), ()))


def _params(sem, vmem=VMEM_LIMIT, flags=None):
    return pltpu.CompilerParams(dimension_semantics=sem, vmem_limit_bytes=vmem, flags=flags)


DOT_ROWS = 256


def _dot_rows(a, b, contract_last=False):
    dims = _CONTRACT_LAST if contract_last else (((1,), (0,)), ((), ()))
    n = a.shape[0]
    if n <= DOT_ROWS:
        return lax.dot_general(a, b, dims, preferred_element_type=F32)
    return jnp.concatenate([lax.dot_general(a[r0:r0 + DOT_ROWS], b, dims, preferred_element_type=F32)
                            for r0 in range(0, n, DOT_ROWS)], axis=0)


def _ada_kernel(cond_ref, w_ref, b_ref, o_ref):
    c = cond_ref[...]
    c = c * jax.nn.sigmoid(c)
    o_ref[0] = jnp.dot(c, w_ref[0], preferred_element_type=F32, precision=lax.Precision.HIGHEST) + b_ref[0]


def _ada(cond, w_ada, b_ada):
    n_layers, d, d6 = w_ada.shape
    tn = 1024
    return pl.pallas_call(
        _ada_kernel,
        grid=(n_layers, d6 // tn),
        in_specs=[
            pl.BlockSpec((8, d), lambda l, j: (0, 0)),
            pl.BlockSpec((1, d, tn), lambda l, j: (l, 0, j)),
            pl.BlockSpec((1, 1, tn), lambda l, j: (l, 0, j)),
        ],
        out_specs=pl.BlockSpec((1, 8, tn), lambda l, j: (l, 0, j)),
        out_shape=jax.ShapeDtypeStruct((n_layers, 8, d6), F32),
        compiler_params=_params(("parallel", "parallel")),
        name="ada_mod",
    )(cond, w_ada, b_ada)


def _rms_mod(x, g, shift, scale):
    y = x * lax.rsqrt(jnp.mean(x * x, axis=-1, keepdims=True) + EPS)
    return (y * g) * (1.0 + scale) + shift


def _norm_proj_kernel(h_ref, g_ref, sh_ref, sc_ref, w_ref, o_ref, xn_ref):
    @pl.when(pl.program_id(2) == 0)
    def _():
        xn_ref[...] = _rms_mod(h_ref[0], g_ref[...], sh_ref[0], sc_ref[0]).astype(BF16)

    o_ref[0] = _dot_rows(xn_ref[...], w_ref[...]).astype(o_ref.dtype)


def _norm_proj(h, g, shift, scale, w, n_cols, tm):
    b, n, d = h.shape
    tn = 1024
    return pl.pallas_call(
        _norm_proj_kernel,
        grid=(b, n // tm, n_cols // tn),
        in_specs=[
            pl.BlockSpec((1, tm, d), lambda bi, i, j: (bi, i, 0)),
            pl.BlockSpec((1, d), lambda bi, i, j: (0, 0)),
            pl.BlockSpec((1, 1, d), lambda bi, i, j: (bi, 0, 0)),
            pl.BlockSpec((1, 1, d), lambda bi, i, j: (bi, 0, 0)),
            pl.BlockSpec((d, tn), lambda bi, i, j: (0, j)),
        ],
        out_specs=pl.BlockSpec((1, tm, tn), lambda bi, i, j: (bi, i, j)),
        out_shape=jax.ShapeDtypeStruct((b, n, n_cols), BF16),
        scratch_shapes=[pltpu.VMEM((tm, d), BF16)],
        compiler_params=_params(("parallel", "parallel", "arbitrary")),
        name="norm_proj",
    )(h, g, shift, scale, w)


def _rope_tables(n_tok):
    t = jnp.arange(n_tok, dtype=jnp.int32)
    row = (t // GRID_W).astype(F32)
    col = (t % GRID_W).astype(F32)
    half = DA_HEAD_DIM // 4
    inv = ROPE_THETA ** (-jnp.arange(half, dtype=F32) / half)
    ar = row[:, None] * inv
    ac = col[:, None] * inv
    ang = jnp.concatenate([ar, ar, ac, ac], axis=-1)
    sign = np.tile(np.concatenate([-np.ones(half), np.ones(half)]), 2).astype(np.float32)
    cos = jnp.cos(ang)
    sin = jnp.sin(ang) * sign
    return jnp.concatenate([cos, cos], axis=-1), jnp.concatenate([sin, sin], axis=-1)


DA_Q_SCALE = DA_HEAD_DIM ** -0.5 * math.log2(math.e)


def _rope_kernel(q_ref, k_ref, v_ref, cos_ref, sin_ref, qo_ref, ko_ref, vo_ref):
    cos = cos_ref[...]
    sin = sin_ref[...]
    lane = lax.broadcasted_iota(jnp.int32, cos.shape, 1)
    first = (lane % (DA_HEAD_DIM // 2)) < (DA_HEAD_DIM // 4)
    seg = DA_HEAD_DIM // 4

    def rope(x):
        partner = jnp.where(first, pltpu.roll(x, LANES - seg, 1), pltpu.roll(x, seg, 1))
        return x * cos + partner * sin

    for hd in range(DA_HEADS):
        sl = slice(hd * HEAD_PAIR, (hd + 1) * HEAD_PAIR)
        q = q_ref[0, :, sl].astype(F32)
        k = k_ref[0, :, sl].astype(F32)
        qo_ref[0, hd] = (rope(q) * DA_Q_SCALE).astype(BF16)
        ko_ref[0, hd] = rope(k).astype(BF16)
        vo_ref[0, hd] = v_ref[0, :, sl]


def _rope(p, cos, sin, tm):
    b, n, _ = p.shape
    w = DA_HEADS * HEAD_PAIR
    hm = jax.ShapeDtypeStruct((b, DA_HEADS, n, HEAD_PAIR), BF16)
    hm_spec = pl.BlockSpec((1, DA_HEADS, tm, HEAD_PAIR), lambda bi, i: (bi, 0, i, 0))
    return pl.pallas_call(
        _rope_kernel,
        grid=(b, n // tm),
        in_specs=[
            pl.BlockSpec((1, tm, w), lambda bi, i: (bi, i, OFF_QA // w)),
            pl.BlockSpec((1, tm, w), lambda bi, i: (bi, i, OFF_KA // w)),
            pl.BlockSpec((1, tm, w), lambda bi, i: (bi, i, OFF_VA // w)),
            pl.BlockSpec((tm, LANES), lambda bi, i: (i, 0)),
            pl.BlockSpec((tm, LANES), lambda bi, i: (i, 0)),
        ],
        out_specs=[hm_spec, hm_spec, hm_spec],
        out_shape=[hm, hm, hm],
        compiler_params=_params(("parallel", "parallel")),
        name="rope_heads",
    )(p, p, p, cos, sin)


def _heads_major(t):
    b, n, _ = t.shape
    return t.reshape(b, n, DA_HEADS, HEAD_PAIR).transpose(0, 2, 1, 3)


def _split_pair(q):
    lo = lax.broadcasted_iota(jnp.int32, q.shape, 1) < (HEAD_PAIR // 2)
    zero = jnp.zeros_like(q)
    return jnp.concatenate([jnp.where(lo, q, zero), jnp.where(lo, zero, q)], axis=0)


DA_DOT_ROWS = 256


def _diff_attn_kernel(lam_ref, g_ref, q_ref, k_ref, v_ref, o_ref, qs_ref, m_ref, l_ref, acc_ref,
                      *, tq, tk, lam_init):
    n_blk = k_ref.shape[2] // tk
    n_lt = tk // LANES
    qs_ref[...] = _split_pair(q_ref[0, 0])
    chunks = [slice(r0, r0 + DA_DOT_ROWS) for r0 in range(0, 2 * tq, DA_DOT_ROWS)]

    m_ref[...] = jnp.full(m_ref.shape, NEG_INF, F32)
    l_ref[...] = jnp.zeros(l_ref.shape, F32)
    acc_ref[...] = jnp.zeros(acc_ref.shape, F32)

    def key_block(j):
        keys = slice(0, tk) if n_blk == 1 else pl.ds(pl.multiple_of(j * tk, tk), tk)
        k = k_ref[0, 0, keys, :]
        v = v_ref[0, 0, keys, :]
        s = jnp.concatenate([lax.dot_general(qs_ref[c], k, _CONTRACT_LAST, preferred_element_type=F32)
                             for c in chunks], axis=0)
        tiles = [s[:, t * LANES:(t + 1) * LANES] for t in range(n_lt)]
        m_prev = m_ref[...]
        m_new = jnp.maximum(m_prev, jnp.max(functools.reduce(jnp.maximum, tiles), axis=-1, keepdims=True))
        alpha = jnp.exp2(m_prev - m_new)
        p_tiles = [jnp.exp2(t - m_new) for t in tiles]
        l_ref[...] = alpha * l_ref[...] + functools.reduce(jnp.add, p_tiles)
        p = jnp.concatenate(p_tiles, axis=1).astype(BF16)
        pv = jnp.concatenate([jnp.dot(p[c], v, preferred_element_type=F32) for c in chunks], axis=0)
        acc_ref[...] = alpha * acc_ref[...] + pv
        m_ref[...] = m_new

    if n_blk == 1:
        key_block(0)
    else:
        def body(j, carry):
            key_block(j)
            return carry
        lax.fori_loop(0, n_blk, body, 0)

    o = acc_ref[...] / jnp.sum(l_ref[...], axis=-1, keepdims=True)
    lm = lam_ref[...]
    lam = (jnp.exp(jnp.sum(lm[0:1] * lm[1:2], axis=-1, keepdims=True))
           - jnp.exp(jnp.sum(lm[2:3] * lm[3:4], axis=-1, keepdims=True)) + lam_init)
    d = o[:tq] - lam * o[tq:]
    y = d * lax.rsqrt(jnp.mean(d * d, axis=-1, keepdims=True) + EPS)
    o_ref[0] = ((y * g_ref[...]) * (1.0 - lam_init)).astype(o_ref.dtype)


def _pick(n, options):
    for o in options:
        if n % o == 0:
            return o
    raise ValueError(f"no tile in {options} divides {n}")


def _diff_attn(q, k, v, lam_params, g, lam_init):
    b, nh, nq, _ = q.shape
    nk = k.shape[2]
    tq = _pick(nq, (512, 256))
    tk = _pick(nk, (2816, 1408, 768, 512, 256))
    kern = functools.partial(_diff_attn_kernel, tq=tq, tk=tk, lam_init=lam_init)
    return pl.pallas_call(
        kern,
        grid=(b, nh, nq // tq),
        in_specs=[
            pl.BlockSpec((4, DA_HEAD_DIM), lambda bi, h, i: (0, 0)),
            pl.BlockSpec((1, HEAD_PAIR), lambda bi, h, i: (0, 0)),
            pl.BlockSpec((1, 1, tq, HEAD_PAIR), lambda bi, h, i: (bi, h, i, 0)),
            pl.BlockSpec((1, 1, nk, HEAD_PAIR), lambda bi, h, i: (bi, h, 0, 0)),
            pl.BlockSpec((1, 1, nk, HEAD_PAIR), lambda bi, h, i: (bi, h, 0, 0)),
        ],
        out_specs=pl.BlockSpec((1, tq, HEAD_PAIR), lambda bi, h, i: (bi, i, h)),
        out_shape=jax.ShapeDtypeStruct((b, nq, nh * HEAD_PAIR), BF16),
        scratch_shapes=[
            pltpu.VMEM((2 * tq, HEAD_PAIR), BF16),
            pltpu.VMEM((2 * tq, LANES), F32),
            pltpu.VMEM((2 * tq, LANES), F32),
            pltpu.VMEM((2 * tq, HEAD_PAIR), F32),
        ],
        compiler_params=_params(("parallel", "parallel", "parallel")),
        name="diff_attn",
    )(lam_params, g, q, k, v)


NA_QROWS = 8
NA_QTOK = NA_QROWS * GRID_W
NA_KBLK = 4 * GRID_W
NA_PIECES = 4
NA_WIN = NA_PIECES * NA_KBLK


def _na_bias_tables(rpb, rows):
    n_h = rpb.shape[0]
    n_kj = NA_PIECES * 4
    pad_r = n_kj - NA_ROWS
    pad_c = GRID_W - NA_COLS
    rp = jnp.pad(rpb * math.log2(math.e), ((0, 0), (pad_r, pad_r), (pad_c, pad_c)))
    a = jnp.stack([rp[:, :, GRID_W - 1 - qc:2 * GRID_W - 1 - qc] for qc in range(GRID_W)], axis=2)
    bias = jnp.stack([a[:, 3 + pad_r - qi:3 + pad_r - qi + n_kj].transpose(0, 2, 1, 3) for qi in range(NA_QROWS)],
                     axis=1).reshape(n_h, NA_QTOK, NA_WIN)

    n_r = rows // NA_QROWS
    qi = np.arange(NA_QROWS)
    kj = np.arange(n_kj)
    c = np.arange(GRID_W)
    cstart = np.clip(c - NA_COLS // 2, 0, GRID_W - NA_COLS)
    col_ok = (c[None, :] >= cstart[:, None]) & (c[None, :] < cstart[:, None] + NA_COLS)
    row_ok = []
    for r_grp in (0, min(1, n_r - 1), n_r - 1):
        r = NA_QROWS * r_grp + qi
        rs = np.clip(r - NA_ROWS // 2, 0, rows - NA_ROWS)
        krow = NA_QROWS * r_grp - 4 + kj
        row_ok.append((krow[None, :] >= rs[:, None]) & (krow[None, :] < rs[:, None] + NA_ROWS)
                      & (krow[None, :] >= 0) & (krow[None, :] < rows))
    ok = jnp.asarray(np.stack(row_ok))[:, :, None, :, None] & jnp.asarray(col_ok)[None, None, :, None, :]
    mask = jnp.where(ok, 0.0, NEG_INF).astype(F32).reshape(3, NA_QTOK, NA_WIN)
    return bias, mask


def _pair_softmax_pv(qs, k_list, v_list, bias_list):
    n_rows = qs.shape[0]
    chunks = [slice(r0, r0 + DA_DOT_ROWS) for r0 in range(0, n_rows, DA_DOT_ROWS)]
    tiles_list = []
    for k, bias in zip(k_list, bias_list):
        s = jnp.concatenate([lax.dot_general(qs[c], k, _CONTRACT_LAST, preferred_element_type=F32)
                             for c in chunks], axis=0)
        tiles = [s[:, j * LANES:(j + 1) * LANES] for j in range(k.shape[0] // LANES)]
        tiles_list.append(tiles if bias is None else [t + bias(j) for j, t in enumerate(tiles)])
    all_tiles = [t for tiles in tiles_list for t in tiles]
    m = jnp.max(functools.reduce(jnp.maximum, all_tiles), axis=-1, keepdims=True)
    p_list = [[jnp.exp2(t - m) for t in tiles] for tiles in tiles_list]
    l = jnp.sum(functools.reduce(jnp.add, [t for tiles in p_list for t in tiles]), axis=-1, keepdims=True)
    o = None
    for tiles, v in zip(p_list, v_list):
        p = jnp.concatenate(tiles, axis=1).astype(BF16)
        pv = jnp.concatenate([jnp.dot(p[c], v, preferred_element_type=F32) for c in chunks], axis=0)
        o = pv if o is None else o + pv
    o = o / l
    t = qs.shape[0] // 2
    lo = lax.broadcasted_iota(jnp.int32, (t, HEAD_PAIR), 1) < (HEAD_PAIR // 2)
    return jnp.where(lo, o[:t], o[t:])


def _scaled_pair(q, head_dim):
    return _split_pair((q.astype(F32) * (head_dim ** -0.5 * math.log2(math.e))).astype(BF16))


def _na_kernel(q_ref, k0, k1, k2, k3, v0, v1, v2, v3, kc_ref, vc_ref, b_ref, mk_ref, o_ref):
    qs = _scaled_pair(q_ref[0], HEAD_PAIR // 2)
    k_win = jnp.concatenate([k0[0], k1[0], k2[0], k3[0]], axis=0)
    v_win = jnp.concatenate([v0[0], v1[0], v2[0], v3[0]], axis=0)
    def bias(j):
        lanes = slice(j * LANES, (j + 1) * LANES)
        window = mk_ref[0, :, lanes]
        return jnp.concatenate([b_ref[0, :, lanes] + window, b_ref[1, :, lanes] + window], axis=0)

    o = _pair_softmax_pv(qs, [k_win, kc_ref[0]], [v_win, vc_ref[0]], [bias, None])
    o_ref[0] = o.astype(o_ref.dtype)


def _na_attn(p, pc, bias, mask):
    b, n, _ = p.shape
    n_ctx = pc.shape[1]
    n_r = n // NA_QTOK
    n_kb = n // NA_KBLK
    n_hp = NA_HEADS // 2

    def case(r):
        return jnp.where(r == 0, 0, jnp.where(r == n_r - 1, 2, 1))

    def kv_spec(off, piece):
        return pl.BlockSpec(
            (1, NA_KBLK, HEAD_PAIR),
            lambda bi, hp, r: (bi, jnp.clip(2 * r - 1 + piece, 0, n_kb - 1), off // HEAD_PAIR + hp))

    in_specs = [pl.BlockSpec((1, NA_QTOK, HEAD_PAIR), lambda bi, hp, r: (bi, r, OFF_QC // HEAD_PAIR + hp))]
    in_specs += [kv_spec(OFF_KC, i) for i in range(NA_PIECES)]
    in_specs += [kv_spec(OFF_VC, i) for i in range(NA_PIECES)]
    in_specs += [
        pl.BlockSpec((1, n_ctx, HEAD_PAIR), lambda bi, hp, r: (bi, 0, OFF_KC // HEAD_PAIR + hp)),
        pl.BlockSpec((1, n_ctx, HEAD_PAIR), lambda bi, hp, r: (bi, 0, OFF_VC // HEAD_PAIR + hp)),
        pl.BlockSpec((2, NA_QTOK, NA_WIN), lambda bi, hp, r: (hp, 0, 0)),
        pl.BlockSpec((1, NA_QTOK, NA_WIN), lambda bi, hp, r: (case(r), 0, 0)),
    ]
    return pl.pallas_call(
        _na_kernel,
        grid=(b, n_hp, n_r),
        in_specs=in_specs,
        out_specs=pl.BlockSpec((1, NA_QTOK, HEAD_PAIR), lambda bi, hp, r: (bi, r, hp)),
        out_shape=jax.ShapeDtypeStruct((b, n, NA_HEADS * HEAD_PAIR // 2), BF16),
        compiler_params=_params(("parallel", "parallel", "arbitrary")),
        name="na_attn",
    )(p, *([p] * (2 * NA_PIECES)), pc, pc, bias, mask)


def _ctx_mha_kernel(q_ref, k_ref, v_ref, o_ref):
    qs = _scaled_pair(q_ref[0], HEAD_PAIR // 2)
    o_ref[0] = _pair_softmax_pv(qs, [k_ref[0]], [v_ref[0]], [None]).astype(o_ref.dtype)


def _ctx_mha(pc):
    b, n_ctx, _ = pc.shape
    n_hp = NA_HEADS // 2

    def spec(off):
        return pl.BlockSpec((1, n_ctx, HEAD_PAIR), lambda bi, hp: (bi, 0, off // HEAD_PAIR + hp))

    return pl.pallas_call(
        _ctx_mha_kernel,
        grid=(b, n_hp),
        in_specs=[spec(OFF_QC), spec(OFF_KC), spec(OFF_VC)],
        out_specs=pl.BlockSpec((1, n_ctx, HEAD_PAIR), lambda bi, hp: (bi, 0, hp)),
        out_shape=jax.ShapeDtypeStruct((b, n_ctx, NA_HEADS * HEAD_PAIR // 2), BF16),
        compiler_params=_params(("parallel", "parallel")),
        name="ctx_mha",
    )(pc, pc, pc)


def _gelu_tanh(x):
    return 0.5 * x * (1.0 + jnp.tanh(math.sqrt(2.0 / math.pi) * (x + 0.044715 * (x * x * x))))


SUBLANES = 8


def _store_token_tiles(ref, x):
    t = x.shape[0]
    for j in range(SUBLANES):
        ref[pl.ds(j, t, stride=SUBLANES), :] = x[:, j * LANES:(j + 1) * LANES]


def _load_token_tiles(ref, t):
    return jnp.concatenate([ref[pl.ds(j, t, stride=SUBLANES), :] for j in range(SUBLANES)], axis=1)


def _merge_kernel(h_ref, ya_ref, z_ref, yc_ref, g0_ref, g1_ref, g2_ref, wb_ref, wo_ref, sgw_ref, sgb_ref,
                  lng_ref, lnb_ref, gt_ref, gf_ref, sh_ref, sc_ref, ho_ref, xo_ref, *, tm):
    z = _gelu_tanh(z_ref[0].astype(F32))
    u = z[:, :SG_WIDTH]
    vv = z[:, SG_WIDTH:]
    mu = jnp.mean(vv, axis=-1, keepdims=True)
    var = jnp.mean(jnp.square(vv - mu), axis=-1, keepdims=True)
    vv = ((vv - mu) * lax.rsqrt(var + EPS)) * lng_ref[...] + lnb_ref[...]
    vv = vv.astype(BF16)
    gd = SG_WIDTH // SG_GROUPS
    chunks = []
    for c in range(tm // SG_CHUNK):
        rows = slice(c * SG_CHUNK, (c + 1) * SG_CHUNK)
        groups = []
        for g in range(SG_GROUPS):
            s = jnp.dot(sgw_ref[g], vv[rows, g * gd:(g + 1) * gd], preferred_element_type=F32) + sgb_ref[g]
            groups.append(s)
        chunks.append(jnp.concatenate(groups, axis=1))
    y_b = (u * jnp.concatenate(chunks, axis=0)).astype(BF16)

    merged = None
    for y, gate_ref, i in ((ya_ref[0], g0_ref, 0), (y_b, g1_ref, 1), (yc_ref[0], g2_ref, 2)):
        t = jax.nn.sigmoid(gate_ref[0].astype(F32)) * _dot_rows(y, wb_ref[i])
        merged = t if merged is None else merged + t
    out = _dot_rows(merged.astype(BF16), wo_ref[...])
    h_new = h_ref[0] + gt_ref[0] * out
    ho_ref[0] = h_new
    _store_token_tiles(xo_ref, _rms_mod(h_new, gf_ref[...], sh_ref[0], sc_ref[0]))


def _merge(h, y_a, p, y_c, wb, wo, sgw, sgb, lng, lnb, gt1, g_ffn, sh2, sc2, xn2_buf, xn2_rows, row_off, tm):
    b, n, d = h.shape
    n_i = n // tm
    off_blk = row_off // tm
    vec = pl.BlockSpec((1, 1, d), lambda bi, i: (bi, 0, 0))

    def col(width, off):
        return pl.BlockSpec((1, tm, width), lambda bi, i: (bi, i, off // width))

    kern = functools.partial(_merge_kernel, tm=tm)
    args = [h, y_a, p, y_c, p, p, p, wb, wo, sgw, sgb, lng, lnb, gt1, g_ffn, sh2, sc2]
    in_specs = [
        pl.BlockSpec((1, tm, d), lambda bi, i: (bi, i, 0)),
        col(BRANCH_WIDTH, 0),
        col(2 * SG_WIDTH, OFF_ZB),
        col(BRANCH_WIDTH, 0),
        col(d, OFF_GATE), col(d, OFF_GATE + d), col(d, OFF_GATE + 2 * d),
        pl.BlockSpec((N_BRANCH, BRANCH_WIDTH, d), lambda bi, i: (0, 0, 0)),
        pl.BlockSpec((d, d), lambda bi, i: (0, 0)),
        pl.BlockSpec((SG_GROUPS, SG_CHUNK, SG_CHUNK), lambda bi, i: (0, 0, 0)),
        pl.BlockSpec((SG_GROUPS, SG_CHUNK, SG_CHUNK), lambda bi, i: (0, 0, 0)),
        pl.BlockSpec((1, SG_WIDTH), lambda bi, i: (0, 0)),
        pl.BlockSpec((1, SG_WIDTH), lambda bi, i: (0, 0)),
        vec,
        pl.BlockSpec((1, d), lambda bi, i: (0, 0)),
        vec, vec,
    ]
    aliases = {}
    if xn2_buf is not None:
        args.append(xn2_buf)
        in_specs.append(pl.BlockSpec(memory_space=pl.ANY))
        aliases = {len(args) - 1: 1}
    assert d == SUBLANES * LANES
    xn2_shape = jax.ShapeDtypeStruct((xn2_rows * SUBLANES, LANES), F32)

    def body(*refs):
        n_in = 17
        kern(*refs[:n_in], *refs[len(refs) - 2:])

    return pl.pallas_call(
        body,
        grid=(b, n_i),
        in_specs=in_specs,
        out_specs=[
            pl.BlockSpec((1, tm, d), lambda bi, i: (bi, i, 0)),
            pl.BlockSpec((tm * SUBLANES, LANES), lambda bi, i: (off_blk + bi * n_i + i, 0)),
        ],
        out_shape=[jax.ShapeDtypeStruct((b, n, d), F32), xn2_shape],
        input_output_aliases=aliases,
        compiler_params=_params(("parallel", "parallel")),
        name="merge_branches",
    )(*args)


R_E1, R_E2, R_W1, R_W2, R_RANK1, R_RANK2 = range(6)


def _router_kernel(x_ref, w_ref, b_ref, o_ref, cnt_ref, run_ref, *, tm):
    @pl.when(pl.program_id(0) == 0)
    def _():
        run_ref[...] = jnp.zeros(run_ref.shape, F32)

    x = _load_token_tiles(x_ref, tm)
    x_hi = x.astype(BF16)
    x_lo = (x - x_hi.astype(F32)).astype(BF16)
    w_hi, w_lo = w_ref[0], w_ref[1]
    logits = (jnp.dot(x_hi, w_hi, preferred_element_type=F32) + jnp.dot(x_lo, w_hi, preferred_element_type=F32)
              + jnp.dot(x_hi, w_lo, preferred_element_type=F32)) + b_ref[...]
    lane = lax.broadcasted_iota(jnp.int32, logits.shape, 1)
    lane_f = lane.astype(F32)
    far = jnp.float32(1e9)

    def first_lane(mask):
        return jnp.min(jnp.where(mask, lane_f, far), axis=-1, keepdims=True)

    is_g = lane < N_GROUPS
    gl = jnp.where(is_g, logits, NEG_INF)
    g_max = jnp.max(gl, axis=-1, keepdims=True)
    g_idx = first_lane(is_g & (gl == g_max))
    g_sum = jnp.sum(jnp.where(is_g, jnp.exp(gl - g_max), 0.0), axis=-1, keepdims=True)
    g_w = 1.0 / g_sum

    e_lane = lane - N_GROUPS
    in_grp = (e_lane >= 0) & (e_lane < N_EXPERTS) & \
        (jnp.right_shift(e_lane, 3).astype(F32) == g_idx)
    el = jnp.where(in_grp, logits, NEG_INF)
    l1 = jnp.max(el, axis=-1, keepdims=True)
    i1 = first_lane(in_grp & (el == l1))
    rest = in_grp & (lane_f != i1)
    el2 = jnp.where(rest, logits, NEG_INF)
    l2 = jnp.max(el2, axis=-1, keepdims=True)
    i2 = first_lane(rest & (el2 == l2))
    t = jnp.exp(l2 - l1)
    w1 = g_w / (1.0 + t)
    w2 = g_w * t / (1.0 + t)

    oh1 = lane_f == i1
    oh2 = lane_f == i2
    oh = jnp.where(oh1 | oh2, 1.0, 0.0)
    row = lax.broadcasted_iota(jnp.int32, (tm, tm), 0)
    colm = lax.broadcasted_iota(jnp.int32, (tm, tm), 1)
    before = jnp.where(colm < row, 1.0, 0.0).astype(BF16)
    prior = jnp.dot(before, oh.astype(BF16), preferred_element_type=F32) + run_ref[...]
    rank1 = jnp.sum(jnp.where(oh1, prior, 0.0), axis=-1, keepdims=True)
    rank2 = jnp.sum(jnp.where(oh2, prior, 0.0), axis=-1, keepdims=True)
    run_new = run_ref[...] + jnp.sum(oh, axis=0, keepdims=True)
    run_ref[...] = run_new
    cnt_ref[...] = run_new

    slab = jnp.zeros(logits.shape, F32)
    for ln, val in ((R_E1, i1 - N_GROUPS), (R_E2, i2 - N_GROUPS), (R_W1, w1), (R_W2, w2),
                    (R_RANK1, rank1), (R_RANK2, rank2)):
        slab = jnp.where(lane == ln, val, slab)
    o_ref[...] = slab


def _router(xn2, w_rt, b_rt):
    t_tok = xn2.shape[0] // SUBLANES
    d = w_rt.shape[0]
    tm = 512
    w_hi = w_rt.astype(BF16)
    w_split = jnp.stack([w_hi, (w_rt - w_hi.astype(F32)).astype(BF16)])
    kern = functools.partial(_router_kernel, tm=tm)
    return pl.pallas_call(
        kern,
        grid=(t_tok // tm,),
        in_specs=[
            pl.BlockSpec((tm * SUBLANES, LANES), lambda i: (i, 0)),
            pl.BlockSpec((2, d, LANES), lambda i: (0, 0, 0)),
            pl.BlockSpec((1, LANES), lambda i: (0, 0)),
        ],
        out_specs=[pl.BlockSpec((tm, LANES), lambda i: (i, 0)), pl.BlockSpec((1, LANES), lambda i: (0, 0))],
        out_shape=[jax.ShapeDtypeStruct((t_tok, LANES), F32), jax.ShapeDtypeStruct((1, LANES), F32)],
        scratch_shapes=[pltpu.VMEM((1, LANES), F32)],
        compiler_params=_params(("arbitrary",)),
        name="moe_router",
    )(xn2, w_split, b_rt)


def _tile_rows(index):
    return pl.ds(pl.multiple_of(index * SUBLANES, SUBLANES), SUBLANES)


RANK_BITS = 20


def _slot(code_ref, start_ref, a):
    code = code_ref[a]
    return start_ref[jnp.right_shift(code, RANK_BITS)] + jnp.bitwise_and(code, (1 << RANK_BITS) - 1)


def _dispatch_kernel(code_ref, start_ref, end_ref, x_ref, xs_ref, zero_ref, sem, zsem, *, n_blocks):
    step = pl.program_id(0)
    blk_rows = MOE_ROWS * SUBLANES

    @pl.when(step == 0)
    def _():
        zero_ref[...] = jnp.zeros(zero_ref.shape, F32)

        def fill(blk):
            return pltpu.make_async_copy(zero_ref, xs_ref.at[pl.ds(pl.multiple_of(blk * blk_rows, blk_rows),
                                                                  blk_rows), :], zsem.at[0])

        def expert_tail(op):
            def body(e, carry):
                @pl.when(end_ref[e] > start_ref[e])
                def _():
                    op(fill(end_ref[e] // MOE_ROWS - 1))
                return carry
            lax.fori_loop(0, N_EXPERTS, body, 0)

        def unused(op):
            def body(blk, carry):
                @pl.when(blk * MOE_ROWS >= end_ref[N_EXPERTS - 1])
                def _():
                    op(fill(blk))
                return carry
            lax.fori_loop(0, n_blocks, body, 0)

        for phase in (lambda c: c.start(), lambda c: c.wait()):
            expert_tail(phase)
            unused(phase)

    def issue(i, carry):
        tok = step * DMA_CHUNK + i
        for k in range(2):
            pltpu.make_async_copy(x_ref.at[_tile_rows(i), :],
                                  xs_ref.at[_tile_rows(_slot(code_ref, start_ref, 2 * tok + k)), :],
                                  sem.at[0]).start()
        return carry

    lax.fori_loop(0, DMA_CHUNK, issue, 0, unroll=4)
    for _ in range(2):
        pltpu.make_async_copy(x_ref, xs_ref.at[pl.ds(0, DMA_CHUNK * SUBLANES), :], sem.at[0]).wait()


def _dispatch(code, pad_start, pad_end, xn2, n_slots):
    t_tok = xn2.shape[0] // SUBLANES
    kern = functools.partial(_dispatch_kernel, n_blocks=n_slots // MOE_ROWS)
    grid_spec = pltpu.PrefetchScalarGridSpec(
        num_scalar_prefetch=3,
        grid=(t_tok // DMA_CHUNK,),
        in_specs=[pl.BlockSpec((DMA_CHUNK * SUBLANES, LANES), lambda i, *_: (i, 0))],
        out_specs=pl.BlockSpec(memory_space=pl.ANY),
        scratch_shapes=[pltpu.VMEM((MOE_ROWS * SUBLANES, LANES), F32), pltpu.SemaphoreType.DMA((1,)),
                        pltpu.SemaphoreType.DMA((1,))],
    )
    return pl.pallas_call(
        kern,
        grid_spec=grid_spec,
        out_shape=jax.ShapeDtypeStruct((n_slots * SUBLANES, LANES), F32),
        compiler_params=_params(("arbitrary",)),
        name="moe_dispatch",
    )(code, pad_start, pad_end, xn2)


def _expert_kernel(be_ref, nxt_ref, nu_ref, x_ref, wg_hbm, wu_hbm, wd_hbm, o_ref,
                   wg_f, wu_f, wd_f, wg_s, wu_s, wd_s, sem, turn_ref, *, w_off):
    i = pl.program_id(0)
    used = i < nu_ref[0]

    def fetch(expert, slot):
        return [pltpu.make_async_copy(src.at[w_off + expert], dst.at[slot], sem.at[slot])
                for src, dst in ((wg_hbm, wg_f), (wu_hbm, wu_f), (wd_hbm, wd_f))]

    @pl.when(i == 0)
    def _():
        turn_ref[0] = 0
        for c in fetch(be_ref[0], 0):
            c.start()

    @pl.when(used & ((i == 0) | (be_ref[i] != be_ref[jnp.maximum(i - 1, 0)])))
    def _():
        slot = turn_ref[0] % 2
        for c in fetch(be_ref[i], slot):
            c.wait()
        nxt = nxt_ref[be_ref[i]]

        @pl.when(nxt >= 0)
        def _():
            for c in fetch(nxt, 1 - slot):
                c.start()

        wg_s[...] = wg_f[slot].astype(BF16)
        wu_s[...] = wu_f[slot].astype(BF16)
        wd_s[...] = wd_f[slot].astype(BF16)
        turn_ref[0] = turn_ref[0] + 1

    @pl.when(used)
    def _():
        x = _load_token_tiles(x_ref, MOE_ROWS).astype(BF16)
        gate = jnp.dot(x, wg_s[...], preferred_element_type=F32)
        up = jnp.dot(x, wu_s[...], preferred_element_type=F32)
        hdn = (gate * jax.nn.sigmoid(gate)) * up
        _store_token_tiles(o_ref, jnp.dot(hdn.astype(BF16), wd_s[...], preferred_element_type=F32))

    @pl.when(i >= nu_ref[0])
    def _():
        o_ref[...] = jnp.zeros(o_ref.shape, F32)


def _experts(block_expert, next_expert, n_used, xs, layer, w_gate, w_up, w_down):
    n_blocks = xs.shape[0] // (MOE_ROWS * SUBLANES)
    n_layers, n_e, d, de = w_gate.shape
    w_gate, w_up, w_down = (w.reshape(n_layers * n_e, *w.shape[2:]) for w in (w_gate, w_up, w_down))
    blk = pl.BlockSpec((MOE_ROWS * SUBLANES, LANES), lambda i, *_: (i, 0))
    hbm = pl.BlockSpec(memory_space=pl.ANY)
    grid_spec = pltpu.PrefetchScalarGridSpec(
        num_scalar_prefetch=3,
        grid=(n_blocks,),
        in_specs=[blk, hbm, hbm, hbm],
        out_specs=blk,
        scratch_shapes=[
            pltpu.VMEM((2, d, de), F32), pltpu.VMEM((2, d, de), F32), pltpu.VMEM((2, de, d), F32),
            pltpu.VMEM((d, de), BF16), pltpu.VMEM((d, de), BF16), pltpu.VMEM((de, d), BF16),
            pltpu.SemaphoreType.DMA((2,)), pltpu.SMEM((1,), jnp.int32),
        ],
    )
    return pl.pallas_call(
        functools.partial(_expert_kernel, w_off=layer * n_e),
        grid_spec=grid_spec,
        out_shape=jax.ShapeDtypeStruct(xs.shape, F32),
        compiler_params=_params(("arbitrary",)),
        name="moe_experts",
    )(block_expert, next_expert, n_used, xs, w_gate, w_up, w_down)


def _residual_kernel(code_ref, start_ref, h_ref, r_ref, gt_ref, gf_ref, ys_ref, o_ref, buf_ref, sem,
                     *, tm, n_i, row_off, final):
    step = pl.program_id(0) * n_i + pl.program_id(1)
    n_steps = pl.num_programs(0) * n_i
    slot = step % 2

    def gather(tile, into):
        base = row_off + tile * tm

        def issue(t, carry):
            for k in range(2):
                pltpu.make_async_copy(ys_ref.at[_tile_rows(_slot(code_ref, start_ref, 2 * (base + t) + k)), :],
                                      buf_ref.at[into, k, _tile_rows(t), :], sem.at[into]).start()
            return carry

        lax.fori_loop(0, tm, issue, 0, unroll=4)

    @pl.when(step == 0)
    def _():
        gather(step, slot)

    @pl.when(step + 1 < n_steps)
    def _():
        gather(step + 1, 1 - slot)

    for k in range(2):
        pltpu.make_async_copy(ys_ref.at[pl.ds(0, tm * SUBLANES), :], buf_ref.at[slot, k], sem.at[slot]).wait()

    r = r_ref[...]
    y = (_load_token_tiles(buf_ref.at[slot, 0], tm) * r[:, R_W1:R_W1 + 1]
         + _load_token_tiles(buf_ref.at[slot, 1], tm) * r[:, R_W2:R_W2 + 1])
    h_new = h_ref[0] + gt_ref[0] * y
    if final:
        h_new = (h_new * lax.rsqrt(jnp.mean(h_new * h_new, axis=-1, keepdims=True) + EPS)) * gf_ref[...]
    o_ref[0] = h_new


def _residual(code, pad_start, h, ys, route, gt2, g_final, row_off, final):
    b, n, d = h.shape
    tm = 256
    n_i = n // tm
    off_blk = row_off // tm
    kern = functools.partial(_residual_kernel, tm=tm, n_i=n_i, row_off=row_off, final=final)
    grid_spec = pltpu.PrefetchScalarGridSpec(
        num_scalar_prefetch=2,
        grid=(b, n_i),
        in_specs=[
            pl.BlockSpec((1, tm, d), lambda bi, i, *_: (bi, i, 0)),
            pl.BlockSpec((tm, LANES), lambda bi, i, *_: (off_blk + bi * n_i + i, 0)),
            pl.BlockSpec((1, 1, d), lambda bi, i, *_: (bi, 0, 0)),
            pl.BlockSpec((1, d), lambda bi, i, *_: (0, 0)),
            pl.BlockSpec(memory_space=pl.ANY),
        ],
        out_specs=pl.BlockSpec((1, tm, d), lambda bi, i, *_: (bi, i, 0)),
        scratch_shapes=[pltpu.VMEM((2, 2, tm * SUBLANES, LANES), F32), pltpu.SemaphoreType.DMA((2,))],
    )
    return pl.pallas_call(
        kern,
        grid_spec=grid_spec,
        out_shape=jax.ShapeDtypeStruct((b, n, d), F32),
        compiler_params=_params(("arbitrary", "arbitrary")),
        name="moe_residual",
    )(code, pad_start, h, route, gt2, g_final, ys)


def _moe(xn2, w_rt, b_rt, layer, w_gate, w_up, w_down):
    t_tok = xn2.shape[0] // SUBLANES
    assert 2 * t_tok < (1 << RANK_BITS)
    route, counts = _router(xn2, w_rt, b_rt)
    cnt = counts[0, N_GROUPS:N_GROUPS + N_EXPERTS].astype(jnp.int32)
    padded = (cnt + MOE_ROWS - 1) // MOE_ROWS * MOE_ROWS
    pad_end = jnp.cumsum(padded)
    pad_start = pad_end - padded
    ids = route[:, R_E1:R_RANK2 + 1].astype(jnp.int32)
    code = (jnp.left_shift(ids[:, R_E1:R_E2 + 1], RANK_BITS) + ids[:, R_RANK1:R_RANK2 + 1]).reshape(2 * t_tok)
    n_blocks = -(-(2 * t_tok + N_EXPERTS * (MOE_ROWS - 1)) // MOE_ROWS)
    n_slots = n_blocks * MOE_ROWS
    starts = jnp.arange(n_blocks, dtype=jnp.int32) * MOE_ROWS
    block_expert = jnp.minimum(jnp.sum(starts[:, None] >= pad_end[None, :], axis=1), N_EXPERTS - 1).astype(jnp.int32)
    n_used = (pad_end[-1:] // MOE_ROWS).astype(jnp.int32)

    e_ids = jnp.arange(N_EXPERTS, dtype=jnp.int32)
    later = (padded > 0)[None, :] & (e_ids[None, :] > e_ids[:, None])
    next_expert = jnp.min(jnp.where(later, e_ids[None, :], N_EXPERTS), axis=1)
    next_expert = jnp.where(next_expert == N_EXPERTS, -1, next_expert).astype(jnp.int32)
    pad_start = pad_start.astype(jnp.int32)
    xs = _dispatch(code, pad_start, pad_end.astype(jnp.int32), xn2, n_slots)
    return route, code, pad_start, _experts(block_expert, next_expert, n_used, xs, layer, w_gate, w_up, w_down)


def kernel(x, c, ctx, c_ctx, w_ada, b_ada, g_norm_mix, g_norm_ffn, w_in, da_lambda, da_subln_g, sg_ln_g, sg_ln_b, sg_w, sg_b, na_rpb, w_branch, w_out, moe_w_group, moe_b_group, moe_w_router, moe_b_router, moe_w_gate, moe_w_up, moe_w_down, g_final):
    b, n_lat, d = x.shape
    n_ctx = ctx.shape[1]
    depth = w_in.shape[0]
    rows = n_lat // GRID_W
    assert d == D_MODEL and n_lat % NA_QTOK == 0 and rows >= 2 * NA_QROWS and n_ctx % 256 == 0 and b <= 7
    tm_lat = 1024 if n_lat % 1024 == 0 else 512
    tm_mrg = 512
    tm_ctx = 256

    cos, sin = _rope_tables(n_lat)
    cond = jnp.zeros((8, d), F32).at[:b].set(c).at[b].set(c_ctx)
    mods = _ada(cond, w_ada, b_ada.reshape(depth, 1, 6 * d))

    h, hc = x, ctx
    for l in range(depth):
        last = l == depth - 1
        lam_init = 0.8 - 0.6 * math.exp(-0.3 * l)
        m_lat = mods[l, :b].reshape(b, 1, 6, d)
        m_ctx = jnp.broadcast_to(mods[l, b].reshape(1, 1, 6, d), (b, 1, 6, d))
        sh1, sc1, gt1, sh2, sc2, gt2 = (m_lat[:, :, i] for i in range(6))
        csh1, csc1, cgt1, csh2, csc2, cgt2 = (m_ctx[:, :, i] for i in range(6))

        w_in_l = w_in[l].astype(BF16)
        g_mix = g_norm_mix[l].reshape(1, d)
        g_ffn = g_norm_ffn[l].reshape(1, d)
        p = _norm_proj(h, g_mix, sh1, sc1, w_in_l, IN_COLS, tm_lat)
        pc = _norm_proj(hc, g_mix, csh1, csc1, w_in_l, KV_COLS if last else IN_COLS, tm_ctx)

        q_hm, k_hm, v_hm = _rope(p, cos, sin, tm_mrg)
        kc_hm = _heads_major(pc[..., OFF_KA:OFF_VA])
        vc_hm = _heads_major(pc[..., OFF_VA:OFF_KC])
        k_all = jnp.concatenate([kc_hm, k_hm], axis=2)
        v_all = jnp.concatenate([vc_hm, v_hm], axis=2)
        g_sub = da_subln_g[l].reshape(1, 2 * DA_HEAD_DIM)
        y_a = _diff_attn(q_hm, k_all, v_all, da_lambda[l], g_sub, lam_init)

        y_c = _na_attn(p, pc, *_na_bias_tables(na_rpb[l], rows))

        wb = w_branch[l].astype(BF16)
        wo = w_out[l].astype(BF16)
        sgw = sg_w[l].astype(BF16)
        sgb = jnp.broadcast_to(sg_b[l][:, :, None], (SG_GROUPS, SG_CHUNK, SG_CHUNK))
        lng = sg_ln_g[l].reshape(1, SG_WIDTH)
        lnb = sg_ln_b[l].reshape(1, SG_WIDTH)
        t_lat = b * n_lat
        t_tok = t_lat if last else t_lat + b * n_ctx
        xn2_buf = None if last else jnp.zeros((t_tok * SUBLANES, LANES), F32)
        h, xn2 = _merge(h, y_a, p, y_c, wb, wo, sgw, sgb, lng, lnb, gt1, g_ffn, sh2, sc2, xn2_buf, t_tok, 0, tm_mrg)
        if not last:
            qc_hm = (_heads_major(pc[..., OFF_QA:OFF_QC]).astype(F32) * DA_Q_SCALE).astype(BF16)
            ya_c = _diff_attn(qc_hm, kc_hm, vc_hm, da_lambda[l], g_sub, lam_init)
            yc_c = _ctx_mha(pc)
            hc, xn2 = _merge(hc, ya_c, pc, yc_c, wb, wo, sgw, sgb, lng, lnb, cgt1, g_ffn, csh2, csc2,
                             xn2, t_tok, t_lat, tm_ctx)

        w_rt = jnp.zeros((d, LANES), F32).at[:, :N_GROUPS].set(moe_w_group[l]) \
            .at[:, N_GROUPS:N_GROUPS + N_EXPERTS].set(moe_w_router[l])
        b_rt = jnp.zeros((1, LANES), F32).at[0, :N_GROUPS].set(moe_b_group[l]) \
            .at[0, N_GROUPS:N_GROUPS + N_EXPERTS].set(moe_b_router[l])
        route, code, pad_start, ys = _moe(xn2, w_rt, b_rt, l, moe_w_gate, moe_w_up, moe_w_down)
        h = _residual(code, pad_start, h, ys, route, gt2, g_final.reshape(1, d), 0, last)
        if not last:
            hc = _residual(code, pad_start, hc, ys, route, cgt2, g_final.reshape(1, d), t_lat, False)
    return h
```

```python
import functools
import math

import numpy as np
import jax
import jax.numpy as jnp
from jax import lax
from jax.experimental import pallas as pl
from jax.experimental.pallas import tpu as pltpu

F32 = jnp.float32
BF16 = jnp.bfloat16

D_MODEL = 1024
GRID_W = 64
EPS = 1e-6
NEG_INF = -1e30
ROPE_THETA = 10000.0

DA_HEADS = 4
DA_HEAD_DIM = 64
NA_HEADS = 8
NA_ROWS = 8
NA_COLS = 16
SG_CHUNK = 128
SG_GROUPS = 4
SG_WIDTH = 512
BRANCH_WIDTH = 512
N_BRANCH = 3

OFF_KA = 0
OFF_VA = 512
OFF_KC = 1024
OFF_VC = 1536
KV_COLS = 2048
OFF_QA = 2048
OFF_QC = 2560
OFF_ZB = 3072
OFF_GATE = 4096
IN_COLS = 7168

N_GROUPS = 4
EXPERTS_PER_GROUP = 8
N_EXPERTS = 32
D_EXPERT = 512

LANES = 128
HEAD_PAIR = LANES
VMEM_LIMIT = 56 * 1024 * 1024

MOE_ROWS = 256
DMA_CHUNK = 512

_CONTRACT_LAST = (((1,), (1,)), ((), ()))


def _params(sem, vmem=VMEM_LIMIT, flags=None):
    return pltpu.CompilerParams(dimension_semantics=sem, vmem_limit_bytes=vmem, flags=flags)


DOT_ROWS = 256


def _dot_rows(a, b, contract_last=False):
    dims = _CONTRACT_LAST if contract_last else (((1,), (0,)), ((), ()))
    n = a.shape[0]
    if n <= DOT_ROWS:
        return lax.dot_general(a, b, dims, preferred_element_type=F32)
    return jnp.concatenate([lax.dot_general(a[r0:r0 + DOT_ROWS], b, dims, preferred_element_type=F32)
                            for r0 in range(0, n, DOT_ROWS)], axis=0)


def _ada_kernel(cond_ref, w_ref, b_ref, o_ref):
    c = cond_ref[...]
    c = c * jax.nn.sigmoid(c)
    o_ref[0] = jnp.dot(c, w_ref[0], preferred_element_type=F32, precision=lax.Precision.HIGHEST) + b_ref[0]


def _ada(cond, w_ada, b_ada):
    n_layers, d, d6 = w_ada.shape
    tn = 1024
    return pl.pallas_call(
        _ada_kernel,
        grid=(n_layers, d6 // tn),
        in_specs=[
            pl.BlockSpec((8, d), lambda l, j: (0, 0)),
            pl.BlockSpec((1, d, tn), lambda l, j: (l, 0, j)),
            pl.BlockSpec((1, 1, tn), lambda l, j: (l, 0, j)),
        ],
        out_specs=pl.BlockSpec((1, 8, tn), lambda l, j: (l, 0, j)),
        out_shape=jax.ShapeDtypeStruct((n_layers, 8, d6), F32),
        compiler_params=_params(("parallel", "parallel")),
        name="ada_mod",
    )(cond, w_ada, b_ada)


def _rms_mod(x, g, shift, scale):
    y = x * lax.rsqrt(jnp.mean(x * x, axis=-1, keepdims=True) + EPS)
    return (y * g) * (1.0 + scale) + shift


PROJ_TN = 1024
KV_TILE = OFF_KA // PROJ_TN
Q_TILE = OFF_QA // PROJ_TN


def _norm_proj_kernel(*refs, rope, with_q, aliased):
    n_in = 5 + (2 if rope else 0) + (2 if aliased else 0)
    h_ref, g_ref, sh_ref, sc_ref, w_ref = refs[:5]
    outs = refs[n_in:-1]
    o_ref, k_ref, v_ref = outs[0], outs[-2], outs[-1]
    xn_ref = refs[-1]
    j = pl.program_id(2)

    @pl.when(j == 0)
    def _():
        xn_ref[...] = _rms_mod(h_ref[0], g_ref[...], sh_ref[0], sc_ref[0]).astype(BF16)

    res = _dot_rows(xn_ref[...], w_ref[...])
    o_ref[0] = res.astype(o_ref.dtype)

    if rope:
        cos, sin = refs[5][...], refs[6][...]
        seg = DA_HEAD_DIM // 4
        first = (lax.broadcasted_iota(jnp.int32, cos.shape, 1) % (2 * seg)) < seg

        def rotate(x):
            partner = jnp.where(first, pltpu.roll(x, LANES - seg, 1), pltpu.roll(x, seg, 1))
            return x * cos + partner * sin
    else:
        def rotate(x):
            return x

    @pl.when(j == KV_TILE)
    def _():
        for hd in range(DA_HEADS):
            k_ref[0, hd] = rotate(res[:, hd * HEAD_PAIR:(hd + 1) * HEAD_PAIR]).astype(BF16)
            v_ref[0, hd] = res[:, OFF_VA + hd * HEAD_PAIR:OFF_VA + (hd + 1) * HEAD_PAIR].astype(BF16)

    if with_q:
        @pl.when(j == Q_TILE)
        def _():
            for hd in range(DA_HEADS):
                outs[1][0, hd] = (rotate(res[:, hd * HEAD_PAIR:(hd + 1) * HEAD_PAIR]) * DA_Q_SCALE).astype(BF16)


def _norm_proj(h, g, shift, scale, w, n_cols, tm, rope_tables=None, kv_all=None, n_keys=None, key_off=0):
    b, n, d = h.shape
    rope = rope_tables is not None
    aliased = kv_all is not None
    with_q = n_cols > OFF_QA
    off_blk = key_off // tm
    args = [h, g, shift, scale, w]
    in_specs = [
        pl.BlockSpec((1, tm, d), lambda bi, i, j: (bi, i, 0)),
        pl.BlockSpec((1, d), lambda bi, i, j: (0, 0)),
        pl.BlockSpec((1, 1, d), lambda bi, i, j: (bi, 0, 0)),
        pl.BlockSpec((1, 1, d), lambda bi, i, j: (bi, 0, 0)),
        pl.BlockSpec((d, PROJ_TN), lambda bi, i, j: (0, j)),
    ]
    if rope:
        args += list(rope_tables)
        in_specs += [pl.BlockSpec((tm, LANES), lambda bi, i, j: (i, 0))] * 2
    aliases = {}
    if aliased:
        n_keys = kv_all[0].shape[2]
        aliases = {len(args): 1 + with_q, len(args) + 1: 2 + with_q}
        args += list(kv_all)
        in_specs += [pl.BlockSpec(memory_space=pl.ANY)] * 2
    hm = lambda rows: jax.ShapeDtypeStruct((b, DA_HEADS, rows, HEAD_PAIR), BF16)
    q_spec = pl.BlockSpec((1, DA_HEADS, tm, HEAD_PAIR), lambda bi, i, j: (bi, 0, i, 0))
    kv_spec = pl.BlockSpec((1, DA_HEADS, tm, HEAD_PAIR), lambda bi, i, j: (bi, 0, off_blk + i, 0))
    out_specs = [pl.BlockSpec((1, tm, PROJ_TN), lambda bi, i, j: (bi, i, j))] + [q_spec] * with_q + [kv_spec] * 2
    out_shape = [jax.ShapeDtypeStruct((b, n, n_cols), BF16)] + [hm(n)] * with_q + [hm(n_keys)] * 2
    return pl.pallas_call(
        functools.partial(_norm_proj_kernel, rope=rope, with_q=with_q, aliased=aliased),
        grid=(b, n // tm, n_cols // PROJ_TN),
        in_specs=in_specs,
        out_specs=out_specs,
        out_shape=out_shape,
        input_output_aliases=aliases,
        scratch_shapes=[pltpu.VMEM((tm, d), BF16)],
        compiler_params=_params(("parallel", "parallel", "arbitrary")),
        name="norm_proj",
    )(*args)


def _rope_tables(n_tok):
    t = jnp.arange(n_tok, dtype=jnp.int32)
    row = (t // GRID_W).astype(F32)
    col = (t % GRID_W).astype(F32)
    half = DA_HEAD_DIM // 4
    inv = ROPE_THETA ** (-jnp.arange(half, dtype=F32) / half)
    ar = row[:, None] * inv
    ac = col[:, None] * inv
    ang = jnp.concatenate([ar, ar, ac, ac], axis=-1)
    sign = np.tile(np.concatenate([-np.ones(half), np.ones(half)]), 2).astype(np.float32)
    cos = jnp.cos(ang)
    sin = jnp.sin(ang) * sign
    return jnp.concatenate([cos, cos], axis=-1), jnp.concatenate([sin, sin], axis=-1)


DA_Q_SCALE = DA_HEAD_DIM ** -0.5 * math.log2(math.e)


def _split_pair(q):
    lo = lax.broadcasted_iota(jnp.int32, q.shape, 1) < (HEAD_PAIR // 2)
    zero = jnp.zeros_like(q)
    return jnp.concatenate([jnp.where(lo, q, zero), jnp.where(lo, zero, q)], axis=0)


DA_DOT_ROWS = 256


def _diff_attn_kernel(lam_ref, g_ref, q_ref, k_ref, v_ref, o_ref, qs_ref, m_ref, l_ref, acc_ref,
                      *, tq, tk, lam_init):
    n_blk = k_ref.shape[2] // tk
    n_lt = tk // LANES
    qs_ref[...] = _split_pair(q_ref[0, 0])
    chunks = [slice(r0, r0 + DA_DOT_ROWS) for r0 in range(0, 2 * tq, DA_DOT_ROWS)]

    m_ref[...] = jnp.full(m_ref.shape, NEG_INF, F32)
    l_ref[...] = jnp.zeros(l_ref.shape, F32)
    acc_ref[...] = jnp.zeros(acc_ref.shape, F32)

    def key_block(j):
        keys = slice(0, tk) if n_blk == 1 else pl.ds(pl.multiple_of(j * tk, tk), tk)
        k = k_ref[0, 0, keys, :]
        v = v_ref[0, 0, keys, :]
        s = jnp.concatenate([lax.dot_general(qs_ref[c], k, _CONTRACT_LAST, preferred_element_type=F32)
                             for c in chunks], axis=0)
        tiles = [s[:, t * LANES:(t + 1) * LANES] for t in range(n_lt)]
        m_prev = m_ref[...]
        m_new = jnp.maximum(m_prev, jnp.max(functools.reduce(jnp.maximum, tiles), axis=-1, keepdims=True))
        alpha = jnp.exp2(m_prev - m_new)
        p_tiles = [jnp.exp2(t - m_new) for t in tiles]
        l_ref[...] = alpha * l_ref[...] + functools.reduce(jnp.add, p_tiles)
        p = jnp.concatenate(p_tiles, axis=1).astype(BF16)
        pv = jnp.concatenate([jnp.dot(p[c], v, preferred_element_type=F32) for c in chunks], axis=0)
        acc_ref[...] = alpha * acc_ref[...] + pv
        m_ref[...] = m_new

    if n_blk == 1:
        key_block(0)
    else:
        def body(j, carry):
            key_block(j)
            return carry
        lax.fori_loop(0, n_blk, body, 0)

    o = acc_ref[...] / jnp.sum(l_ref[...], axis=-1, keepdims=True)
    lm = lam_ref[...]
    lam = (jnp.exp(jnp.sum(lm[0:1] * lm[1:2], axis=-1, keepdims=True))
           - jnp.exp(jnp.sum(lm[2:3] * lm[3:4], axis=-1, keepdims=True)) + lam_init)
    d = o[:tq] - lam * o[tq:]
    y = d * lax.rsqrt(jnp.mean(d * d, axis=-1, keepdims=True) + EPS)
    o_ref[0] = ((y * g_ref[...]) * (1.0 - lam_init)).astype(o_ref.dtype)


def _pick(n, options):
    for o in options:
        if n % o == 0:
            return o
    raise ValueError(f"no tile in {options} divides {n}")


def _diff_attn(q, k, v, lam_params, g, lam_init):
    b, nh, nq, _ = q.shape
    nk = k.shape[2]
    tq = _pick(nq, (512, 256))
    tk = _pick(nk, (2816, 1408, 768, 512, 256))
    kern = functools.partial(_diff_attn_kernel, tq=tq, tk=tk, lam_init=lam_init)
    return pl.pallas_call(
        kern,
        grid=(b, nh, nq // tq),
        in_specs=[
            pl.BlockSpec((4, DA_HEAD_DIM), lambda bi, h, i: (0, 0)),
            pl.BlockSpec((1, HEAD_PAIR), lambda bi, h, i: (0, 0)),
            pl.BlockSpec((1, 1, tq, HEAD_PAIR), lambda bi, h, i: (bi, h, i, 0)),
            pl.BlockSpec((1, 1, nk, HEAD_PAIR), lambda bi, h, i: (bi, h, 0, 0)),
            pl.BlockSpec((1, 1, nk, HEAD_PAIR), lambda bi, h, i: (bi, h, 0, 0)),
        ],
        out_specs=pl.BlockSpec((1, tq, HEAD_PAIR), lambda bi, h, i: (bi, i, h)),
        out_shape=jax.ShapeDtypeStruct((b, nq, nh * HEAD_PAIR), BF16),
        scratch_shapes=[
            pltpu.VMEM((2 * tq, HEAD_PAIR), BF16),
            pltpu.VMEM((2 * tq, LANES), F32),
            pltpu.VMEM((2 * tq, LANES), F32),
            pltpu.VMEM((2 * tq, HEAD_PAIR), F32),
        ],
        compiler_params=_params(("parallel", "parallel", "parallel")),
        name="diff_attn",
    )(lam_params, g, q, k, v)


NA_QROWS = 8
NA_QTOK = NA_QROWS * GRID_W
NA_KBLK = 4 * GRID_W
NA_PIECES = 4
NA_WIN = NA_PIECES * NA_KBLK


def _na_bias_tables(rpb, rows):
    n_h = rpb.shape[0]
    n_kj = NA_PIECES * 4
    pad_r = n_kj - NA_ROWS
    pad_c = GRID_W - NA_COLS
    rp = jnp.pad(rpb * math.log2(math.e), ((0, 0), (pad_r, pad_r), (pad_c, pad_c)))
    a = jnp.stack([rp[:, :, GRID_W - 1 - qc:2 * GRID_W - 1 - qc] for qc in range(GRID_W)], axis=2)
    bias = jnp.stack([a[:, 3 + pad_r - qi:3 + pad_r - qi + n_kj].transpose(0, 2, 1, 3) for qi in range(NA_QROWS)],
                     axis=1).reshape(n_h, NA_QTOK, NA_WIN)

    n_r = rows // NA_QROWS
    qi = np.arange(NA_QROWS)
    kj = np.arange(n_kj)
    c = np.arange(GRID_W)
    cstart = np.clip(c - NA_COLS // 2, 0, GRID_W - NA_COLS)
    col_ok = (c[None, :] >= cstart[:, None]) & (c[None, :] < cstart[:, None] + NA_COLS)
    row_ok = []
    for r_grp in (0, min(1, n_r - 1), n_r - 1):
        r = NA_QROWS * r_grp + qi
        rs = np.clip(r - NA_ROWS // 2, 0, rows - NA_ROWS)
        krow = NA_QROWS * r_grp - 4 + kj
        row_ok.append((krow[None, :] >= rs[:, None]) & (krow[None, :] < rs[:, None] + NA_ROWS)
                      & (krow[None, :] >= 0) & (krow[None, :] < rows))
    ok = jnp.asarray(np.stack(row_ok))[:, :, None, :, None] & jnp.asarray(col_ok)[None, None, :, None, :]
    mask = jnp.where(ok, 0.0, NEG_INF).astype(F32).reshape(3, NA_QTOK, NA_WIN)
    return bias, mask


def _pair_softmax_pv(qs, k_list, v_list, bias_list):
    n_rows = qs.shape[0]
    chunks = [slice(r0, r0 + DA_DOT_ROWS) for r0 in range(0, n_rows, DA_DOT_ROWS)]
    tiles_list = []
    for k, bias in zip(k_list, bias_list):
        s = jnp.concatenate([lax.dot_general(qs[c], k, _CONTRACT_LAST, preferred_element_type=F32)
                             for c in chunks], axis=0)
        tiles = [s[:, j * LANES:(j + 1) * LANES] for j in range(k.shape[0] // LANES)]
        tiles_list.append(tiles if bias is None else [t + bias(j) for j, t in enumerate(tiles)])
    all_tiles = [t for tiles in tiles_list for t in tiles]
    m = jnp.max(functools.reduce(jnp.maximum, all_tiles), axis=-1, keepdims=True)
    p_list = [[jnp.exp2(t - m) for t in tiles] for tiles in tiles_list]
    l = jnp.sum(functools.reduce(jnp.add, [t for tiles in p_list for t in tiles]), axis=-1, keepdims=True)
    o = None
    for tiles, v in zip(p_list, v_list):
        p = jnp.concatenate(tiles, axis=1).astype(BF16)
        pv = jnp.concatenate([jnp.dot(p[c], v, preferred_element_type=F32) for c in chunks], axis=0)
        o = pv if o is None else o + pv
    o = o / l
    t = qs.shape[0] // 2
    lo = lax.broadcasted_iota(jnp.int32, (t, HEAD_PAIR), 1) < (HEAD_PAIR // 2)
    return jnp.where(lo, o[:t], o[t:])


def _scaled_pair(q, head_dim):
    return _split_pair((q.astype(F32) * (head_dim ** -0.5 * math.log2(math.e))).astype(BF16))


def _na_kernel(q_ref, k0, k1, k2, k3, v0, v1, v2, v3, kc_ref, vc_ref, b_ref, mk_ref, o_ref):
    qs = _scaled_pair(q_ref[0], HEAD_PAIR // 2)
    k_win = jnp.concatenate([k0[0], k1[0], k2[0], k3[0]], axis=0)
    v_win = jnp.concatenate([v0[0], v1[0], v2[0], v3[0]], axis=0)
    def bias(j):
        lanes = slice(j * LANES, (j + 1) * LANES)
        window = mk_ref[0, :, lanes]
        return jnp.concatenate([b_ref[0, :, lanes] + window, b_ref[1, :, lanes] + window], axis=0)

    o = _pair_softmax_pv(qs, [k_win, kc_ref[0]], [v_win, vc_ref[0]], [bias, None])
    o_ref[0] = o.astype(o_ref.dtype)


def _na_attn(p, pc, bias, mask):
    b, n, _ = p.shape
    n_ctx = pc.shape[1]
    n_r = n // NA_QTOK
    n_kb = n // NA_KBLK
    n_hp = NA_HEADS // 2

    def case(r):
        return jnp.where(r == 0, 0, jnp.where(r == n_r - 1, 2, 1))

    def kv_spec(off, piece):
        return pl.BlockSpec(
            (1, NA_KBLK, HEAD_PAIR),
            lambda bi, hp, r: (bi, jnp.clip(2 * r - 1 + piece, 0, n_kb - 1), off // HEAD_PAIR + hp))

    in_specs = [pl.BlockSpec((1, NA_QTOK, HEAD_PAIR), lambda bi, hp, r: (bi, r, OFF_QC // HEAD_PAIR + hp))]
    in_specs += [kv_spec(OFF_KC, i) for i in range(NA_PIECES)]
    in_specs += [kv_spec(OFF_VC, i) for i in range(NA_PIECES)]
    in_specs += [
        pl.BlockSpec((1, n_ctx, HEAD_PAIR), lambda bi, hp, r: (bi, 0, OFF_KC // HEAD_PAIR + hp)),
        pl.BlockSpec((1, n_ctx, HEAD_PAIR), lambda bi, hp, r: (bi, 0, OFF_VC // HEAD_PAIR + hp)),
        pl.BlockSpec((2, NA_QTOK, NA_WIN), lambda bi, hp, r: (hp, 0, 0)),
        pl.BlockSpec((1, NA_QTOK, NA_WIN), lambda bi, hp, r: (case(r), 0, 0)),
    ]
    return pl.pallas_call(
        _na_kernel,
        grid=(b, n_hp, n_r),
        in_specs=in_specs,
        out_specs=pl.BlockSpec((1, NA_QTOK, HEAD_PAIR), lambda bi, hp, r: (bi, r, hp)),
        out_shape=jax.ShapeDtypeStruct((b, n, NA_HEADS * HEAD_PAIR // 2), BF16),
        compiler_params=_params(("parallel", "parallel", "arbitrary")),
        name="na_attn",
    )(p, *([p] * (2 * NA_PIECES)), pc, pc, bias, mask)


def _ctx_mha_kernel(q_ref, k_ref, v_ref, o_ref):
    qs = _scaled_pair(q_ref[0], HEAD_PAIR // 2)
    o_ref[0] = _pair_softmax_pv(qs, [k_ref[0]], [v_ref[0]], [None]).astype(o_ref.dtype)


def _ctx_mha(pc):
    b, n_ctx, _ = pc.shape
    n_hp = NA_HEADS // 2

    def spec(off):
        return pl.BlockSpec((1, n_ctx, HEAD_PAIR), lambda bi, hp: (bi, 0, off // HEAD_PAIR + hp))

    return pl.pallas_call(
        _ctx_mha_kernel,
        grid=(b, n_hp),
        in_specs=[spec(OFF_QC), spec(OFF_KC), spec(OFF_VC)],
        out_specs=pl.BlockSpec((1, n_ctx, HEAD_PAIR), lambda bi, hp: (bi, 0, hp)),
        out_shape=jax.ShapeDtypeStruct((b, n_ctx, NA_HEADS * HEAD_PAIR // 2), BF16),
        compiler_params=_params(("parallel", "parallel")),
        name="ctx_mha",
    )(pc, pc, pc)


def _gelu_tanh(x):
    return 0.5 * x * (1.0 + jnp.tanh(math.sqrt(2.0 / math.pi) * (x + 0.044715 * (x * x * x))))


SUBLANES = 8


def _store_token_tiles(ref, x):
    t = x.shape[0]
    for j in range(SUBLANES):
        ref[pl.ds(j, t, stride=SUBLANES), :] = x[:, j * LANES:(j + 1) * LANES]


def _load_token_tiles(ref, t):
    return jnp.concatenate([ref[pl.ds(j, t, stride=SUBLANES), :] for j in range(SUBLANES)], axis=1)


def _merge_kernel(h_ref, ya_ref, z_ref, yc_ref, g0_ref, g1_ref, g2_ref, wb_ref, wo_ref, sgw_ref, sgb_ref,
                  lng_ref, lnb_ref, gt_ref, gf_ref, sh_ref, sc_ref, ho_ref, xo_ref, *, tm):
    z = _gelu_tanh(z_ref[0].astype(F32))
    u = z[:, :SG_WIDTH]
    vv = z[:, SG_WIDTH:]
    mu = jnp.mean(vv, axis=-1, keepdims=True)
    var = jnp.mean(jnp.square(vv - mu), axis=-1, keepdims=True)
    vv = ((vv - mu) * lax.rsqrt(var + EPS)) * lng_ref[...] + lnb_ref[...]
    vv = vv.astype(BF16)
    gd = SG_WIDTH // SG_GROUPS
    chunks = []
    for c in range(tm // SG_CHUNK):
        rows = slice(c * SG_CHUNK, (c + 1) * SG_CHUNK)
        groups = []
        for g in range(SG_GROUPS):
            s = jnp.dot(sgw_ref[g], vv[rows, g * gd:(g + 1) * gd], preferred_element_type=F32) + sgb_ref[g]
            groups.append(s)
        chunks.append(jnp.concatenate(groups, axis=1))
    y_b = (u * jnp.concatenate(chunks, axis=0)).astype(BF16)

    merged = None
    for y, gate_ref, i in ((ya_ref[0], g0_ref, 0), (y_b, g1_ref, 1), (yc_ref[0], g2_ref, 2)):
        t = jax.nn.sigmoid(gate_ref[0].astype(F32)) * _dot_rows(y, wb_ref[i])
        merged = t if merged is None else merged + t
    out = _dot_rows(merged.astype(BF16), wo_ref[...])
    h_new = h_ref[0] + gt_ref[0] * out
    ho_ref[0] = h_new
    _store_token_tiles(xo_ref, _rms_mod(h_new, gf_ref[...], sh_ref[0], sc_ref[0]))


def _merge(h, y_a, p, y_c, wb, wo, sgw, sgb, lng, lnb, gt1, g_ffn, sh2, sc2, xn2_buf, xn2_rows, row_off, tm):
    b, n, d = h.shape
    n_i = n // tm
    off_blk = row_off // tm
    vec = pl.BlockSpec((1, 1, d), lambda bi, i: (bi, 0, 0))

    def col(width, off):
        return pl.BlockSpec((1, tm, width), lambda bi, i: (bi, i, off // width))

    kern = functools.partial(_merge_kernel, tm=tm)
    args = [h, y_a, p, y_c, p, p, p, wb, wo, sgw, sgb, lng, lnb, gt1, g_ffn, sh2, sc2]
    in_specs = [
        pl.BlockSpec((1, tm, d), lambda bi, i: (bi, i, 0)),
        col(BRANCH_WIDTH, 0),
        col(2 * SG_WIDTH, OFF_ZB),
        col(BRANCH_WIDTH, 0),
        col(d, OFF_GATE), col(d, OFF_GATE + d), col(d, OFF_GATE + 2 * d),
        pl.BlockSpec((N_BRANCH, BRANCH_WIDTH, d), lambda bi, i: (0, 0, 0)),
        pl.BlockSpec((d, d), lambda bi, i: (0, 0)),
        pl.BlockSpec((SG_GROUPS, SG_CHUNK, SG_CHUNK), lambda bi, i: (0, 0, 0)),
        pl.BlockSpec((SG_GROUPS, SG_CHUNK, SG_CHUNK), lambda bi, i: (0, 0, 0)),
        pl.BlockSpec((1, SG_WIDTH), lambda bi, i: (0, 0)),
        pl.BlockSpec((1, SG_WIDTH), lambda bi, i: (0, 0)),
        vec,
        pl.BlockSpec((1, d), lambda bi, i: (0, 0)),
        vec, vec,
    ]
    aliases = {}
    if xn2_buf is not None:
        args.append(xn2_buf)
        in_specs.append(pl.BlockSpec(memory_space=pl.ANY))
        aliases = {len(args) - 1: 1}
    assert d == SUBLANES * LANES
    xn2_shape = jax.ShapeDtypeStruct((xn2_rows * SUBLANES, LANES), F32)

    def body(*refs):
        n_in = 17
        kern(*refs[:n_in], *refs[len(refs) - 2:])

    return pl.pallas_call(
        body,
        grid=(b, n_i),
        in_specs=in_specs,
        out_specs=[
            pl.BlockSpec((1, tm, d), lambda bi, i: (bi, i, 0)),
            pl.BlockSpec((tm * SUBLANES, LANES), lambda bi, i: (off_blk + bi * n_i + i, 0)),
        ],
        out_shape=[jax.ShapeDtypeStruct((b, n, d), F32), xn2_shape],
        input_output_aliases=aliases,
        compiler_params=_params(("parallel", "parallel")),
        name="merge_branches",
    )(*args)


R_E1, R_E2, R_W1, R_W2, R_RANK1, R_RANK2 = range(6)


def _router_kernel(x_ref, w_ref, b_ref, o_ref, cnt_ref, run_ref, *, tm):
    @pl.when(pl.program_id(0) == 0)
    def _():
        run_ref[...] = jnp.zeros(run_ref.shape, F32)

    x = _load_token_tiles(x_ref, tm)
    x_hi = x.astype(BF16)
    x_lo = (x - x_hi.astype(F32)).astype(BF16)
    w_hi, w_lo = w_ref[0], w_ref[1]
    logits = (jnp.dot(x_hi, w_hi, preferred_element_type=F32) + jnp.dot(x_lo, w_hi, preferred_element_type=F32)
              + jnp.dot(x_hi, w_lo, preferred_element_type=F32)) + b_ref[...]
    lane = lax.broadcasted_iota(jnp.int32, logits.shape, 1)
    lane_f = lane.astype(F32)
    far = jnp.float32(1e9)

    def first_lane(mask):
        return jnp.min(jnp.where(mask, lane_f, far), axis=-1, keepdims=True)

    is_g = lane < N_GROUPS
    gl = jnp.where(is_g, logits, NEG_INF)
    g_max = jnp.max(gl, axis=-1, keepdims=True)
    g_idx = first_lane(is_g & (gl == g_max))
    g_sum = jnp.sum(jnp.where(is_g, jnp.exp(gl - g_max), 0.0), axis=-1, keepdims=True)
    g_w = 1.0 / g_sum

    e_lane = lane - N_GROUPS
    in_grp = (e_lane >= 0) & (e_lane < N_EXPERTS) & \
        (jnp.right_shift(e_lane, 3).astype(F32) == g_idx)
    el = jnp.where(in_grp, logits, NEG_INF)
    l1 = jnp.max(el, axis=-1, keepdims=True)
    i1 = first_lane(in_grp & (el == l1))
    rest = in_grp & (lane_f != i1)
    el2 = jnp.where(rest, logits, NEG_INF)
    l2 = jnp.max(el2, axis=-1, keepdims=True)
    i2 = first_lane(rest & (el2 == l2))
    t = jnp.exp(l2 - l1)
    w1 = g_w / (1.0 + t)
    w2 = g_w * t / (1.0 + t)

    oh1 = lane_f == i1
    oh2 = lane_f == i2
    oh = jnp.where(oh1 | oh2, 1.0, 0.0)
    row = lax.broadcasted_iota(jnp.int32, (tm, tm), 0)
    colm = lax.broadcasted_iota(jnp.int32, (tm, tm), 1)
    before = jnp.where(colm < row, 1.0, 0.0).astype(BF16)
    prior = jnp.dot(before, oh.astype(BF16), preferred_element_type=F32) + run_ref[...]
    rank1 = jnp.sum(jnp.where(oh1, prior, 0.0), axis=-1, keepdims=True)
    rank2 = jnp.sum(jnp.where(oh2, prior, 0.0), axis=-1, keepdims=True)
    run_new = run_ref[...] + jnp.sum(oh, axis=0, keepdims=True)
    run_ref[...] = run_new
    cnt_ref[...] = run_new

    slab = jnp.zeros(logits.shape, F32)
    for ln, val in ((R_E1, i1 - N_GROUPS), (R_E2, i2 - N_GROUPS), (R_W1, w1), (R_W2, w2),
                    (R_RANK1, rank1), (R_RANK2, rank2)):
        slab = jnp.where(lane == ln, val, slab)
    o_ref[...] = slab


def _router(xn2, w_rt, b_rt):
    t_tok = xn2.shape[0] // SUBLANES
    d = w_rt.shape[0]
    tm = 512
    w_hi = w_rt.astype(BF16)
    w_split = jnp.stack([w_hi, (w_rt - w_hi.astype(F32)).astype(BF16)])
    kern = functools.partial(_router_kernel, tm=tm)
    return pl.pallas_call(
        kern,
        grid=(t_tok // tm,),
        in_specs=[
            pl.BlockSpec((tm * SUBLANES, LANES), lambda i: (i, 0)),
            pl.BlockSpec((2, d, LANES), lambda i: (0, 0, 0)),
            pl.BlockSpec((1, LANES), lambda i: (0, 0)),
        ],
        out_specs=[pl.BlockSpec((tm, LANES), lambda i: (i, 0)), pl.BlockSpec((1, LANES), lambda i: (0, 0))],
        out_shape=[jax.ShapeDtypeStruct((t_tok, LANES), F32), jax.ShapeDtypeStruct((1, LANES), F32)],
        scratch_shapes=[pltpu.VMEM((1, LANES), F32)],
        compiler_params=_params(("arbitrary",)),
        name="moe_router",
    )(xn2, w_split, b_rt)


def _tile_rows(index):
    return pl.ds(pl.multiple_of(index * SUBLANES, SUBLANES), SUBLANES)


RANK_BITS = 20


def _slot(code_ref, start_ref, a):
    code = code_ref[a]
    return start_ref[jnp.right_shift(code, RANK_BITS)] + jnp.bitwise_and(code, (1 << RANK_BITS) - 1)


def _dispatch_kernel(code_ref, start_ref, end_ref, x_ref, xs_ref, zero_ref, sem, zsem, *, n_blocks):
    step = pl.program_id(0)
    blk_rows = MOE_ROWS * SUBLANES

    @pl.when(step == 0)
    def _():
        zero_ref[...] = jnp.zeros(zero_ref.shape, F32)

        def fill(blk):
            return pltpu.make_async_copy(zero_ref, xs_ref.at[pl.ds(pl.multiple_of(blk * blk_rows, blk_rows),
                                                                  blk_rows), :], zsem.at[0])

        def expert_tail(op):
            def body(e, carry):
                @pl.when(end_ref[e] > start_ref[e])
                def _():
                    op(fill(end_ref[e] // MOE_ROWS - 1))
                return carry
            lax.fori_loop(0, N_EXPERTS, body, 0)

        def unused(op):
            def body(blk, carry):
                @pl.when(blk * MOE_ROWS >= end_ref[N_EXPERTS - 1])
                def _():
                    op(fill(blk))
                return carry
            lax.fori_loop(0, n_blocks, body, 0)

        for phase in (lambda c: c.start(), lambda c: c.wait()):
            expert_tail(phase)
            unused(phase)

    def issue(i, carry):
        tok = step * DMA_CHUNK + i
        for k in range(2):
            pltpu.make_async_copy(x_ref.at[_tile_rows(i), :],
                                  xs_ref.at[_tile_rows(_slot(code_ref, start_ref, 2 * tok + k)), :],
                                  sem.at[0]).start()
        return carry

    lax.fori_loop(0, DMA_CHUNK, issue, 0, unroll=4)
    for _ in range(2):
        pltpu.make_async_copy(x_ref, xs_ref.at[pl.ds(0, DMA_CHUNK * SUBLANES), :], sem.at[0]).wait()


def _dispatch(code, pad_start, pad_end, xn2, n_slots):
    t_tok = xn2.shape[0] // SUBLANES
    kern = functools.partial(_dispatch_kernel, n_blocks=n_slots // MOE_ROWS)
    grid_spec = pltpu.PrefetchScalarGridSpec(
        num_scalar_prefetch=3,
        grid=(t_tok // DMA_CHUNK,),
        in_specs=[pl.BlockSpec((DMA_CHUNK * SUBLANES, LANES), lambda i, *_: (i, 0))],
        out_specs=pl.BlockSpec(memory_space=pl.ANY),
        scratch_shapes=[pltpu.VMEM((MOE_ROWS * SUBLANES, LANES), F32), pltpu.SemaphoreType.DMA((1,)),
                        pltpu.SemaphoreType.DMA((1,))],
    )
    return pl.pallas_call(
        kern,
        grid_spec=grid_spec,
        out_shape=jax.ShapeDtypeStruct((n_slots * SUBLANES, LANES), F32),
        compiler_params=_params(("arbitrary",)),
        name="moe_dispatch",
    )(code, pad_start, pad_end, xn2)


def _expert_kernel(be_ref, nxt_ref, nu_ref, x_ref, wg_hbm, wu_hbm, wd_hbm, o_ref,
                   wg_f, wu_f, wd_f, wg_s, wu_s, wd_s, sem, turn_ref, *, w_off):
    i = pl.program_id(0)
    used = i < nu_ref[0]

    def fetch(expert, slot):
        return [pltpu.make_async_copy(src.at[w_off + expert], dst.at[slot], sem.at[slot])
                for src, dst in ((wg_hbm, wg_f), (wu_hbm, wu_f), (wd_hbm, wd_f))]

    @pl.when(i == 0)
    def _():
        turn_ref[0] = 0
        for c in fetch(be_ref[0], 0):
            c.start()

    @pl.when(used & ((i == 0) | (be_ref[i] != be_ref[jnp.maximum(i - 1, 0)])))
    def _():
        slot = turn_ref[0] % 2
        for c in fetch(be_ref[i], slot):
            c.wait()
        nxt = nxt_ref[be_ref[i]]

        @pl.when(nxt >= 0)
        def _():
            for c in fetch(nxt, 1 - slot):
                c.start()

        wg_s[...] = wg_f[slot].astype(BF16)
        wu_s[...] = wu_f[slot].astype(BF16)
        wd_s[...] = wd_f[slot].astype(BF16)
        turn_ref[0] = turn_ref[0] + 1

    @pl.when(used)
    def _():
        x = _load_token_tiles(x_ref, MOE_ROWS).astype(BF16)
        gate = jnp.dot(x, wg_s[...], preferred_element_type=F32)
        up = jnp.dot(x, wu_s[...], preferred_element_type=F32)
        hdn = (gate * jax.nn.sigmoid(gate)) * up
        _store_token_tiles(o_ref, jnp.dot(hdn.astype(BF16), wd_s[...], preferred_element_type=F32))

    @pl.when(i >= nu_ref[0])
    def _():
        o_ref[...] = jnp.zeros(o_ref.shape, F32)


def _experts(block_expert, next_expert, n_used, xs, layer, w_gate, w_up, w_down):
    n_blocks = xs.shape[0] // (MOE_ROWS * SUBLANES)
    n_layers, n_e, d, de = w_gate.shape
    w_gate, w_up, w_down = (w.reshape(n_layers * n_e, *w.shape[2:]) for w in (w_gate, w_up, w_down))
    blk = pl.BlockSpec((MOE_ROWS * SUBLANES, LANES), lambda i, *_: (i, 0))
    hbm = pl.BlockSpec(memory_space=pl.ANY)
    grid_spec = pltpu.PrefetchScalarGridSpec(
        num_scalar_prefetch=3,
        grid=(n_blocks,),
        in_specs=[blk, hbm, hbm, hbm],
        out_specs=blk,
        scratch_shapes=[
            pltpu.VMEM((2, d, de), F32), pltpu.VMEM((2, d, de), F32), pltpu.VMEM((2, de, d), F32),
            pltpu.VMEM((d, de), BF16), pltpu.VMEM((d, de), BF16), pltpu.VMEM((de, d), BF16),
            pltpu.SemaphoreType.DMA((2,)), pltpu.SMEM((1,), jnp.int32),
        ],
    )
    return pl.pallas_call(
        functools.partial(_expert_kernel, w_off=layer * n_e),
        grid_spec=grid_spec,
        out_shape=jax.ShapeDtypeStruct(xs.shape, F32),
        compiler_params=_params(("arbitrary",)),
        name="moe_experts",
    )(block_expert, next_expert, n_used, xs, w_gate, w_up, w_down)


def _residual_kernel(code_ref, start_ref, h_ref, r_ref, gt_ref, gf_ref, ys_ref, o_ref, buf_ref, sem,
                     *, tm, n_i, row_off, final):
    step = pl.program_id(0) * n_i + pl.program_id(1)
    n_steps = pl.num_programs(0) * n_i
    slot = step % 2

    def gather(tile, into):
        base = row_off + tile * tm

        def issue(t, carry):
            for k in range(2):
                pltpu.make_async_copy(ys_ref.at[_tile_rows(_slot(code_ref, start_ref, 2 * (base + t) + k)), :],
                                      buf_ref.at[into, k, _tile_rows(t), :], sem.at[into]).start()
            return carry

        lax.fori_loop(0, tm, issue, 0, unroll=4)

    @pl.when(step == 0)
    def _():
        gather(step, slot)

    @pl.when(step + 1 < n_steps)
    def _():
        gather(step + 1, 1 - slot)

    for k in range(2):
        pltpu.make_async_copy(ys_ref.at[pl.ds(0, tm * SUBLANES), :], buf_ref.at[slot, k], sem.at[slot]).wait()

    r = r_ref[...]
    y = (_load_token_tiles(buf_ref.at[slot, 0], tm) * r[:, R_W1:R_W1 + 1]
         + _load_token_tiles(buf_ref.at[slot, 1], tm) * r[:, R_W2:R_W2 + 1])
    h_new = h_ref[0] + gt_ref[0] * y
    if final:
        h_new = (h_new * lax.rsqrt(jnp.mean(h_new * h_new, axis=-1, keepdims=True) + EPS)) * gf_ref[...]
    o_ref[0] = h_new


def _residual(code, pad_start, h, ys, route, gt2, g_final, row_off, final):
    b, n, d = h.shape
    tm = 256
    n_i = n // tm
    off_blk = row_off // tm
    kern = functools.partial(_residual_kernel, tm=tm, n_i=n_i, row_off=row_off, final=final)
    grid_spec = pltpu.PrefetchScalarGridSpec(
        num_scalar_prefetch=2,
        grid=(b, n_i),
        in_specs=[
            pl.BlockSpec((1, tm, d), lambda bi, i, *_: (bi, i, 0)),
            pl.BlockSpec((tm, LANES), lambda bi, i, *_: (off_blk + bi * n_i + i, 0)),
            pl.BlockSpec((1, 1, d), lambda bi, i, *_: (bi, 0, 0)),
            pl.BlockSpec((1, d), lambda bi, i, *_: (0, 0)),
            pl.BlockSpec(memory_space=pl.ANY),
        ],
        out_specs=pl.BlockSpec((1, tm, d), lambda bi, i, *_: (bi, i, 0)),
        scratch_shapes=[pltpu.VMEM((2, 2, tm * SUBLANES, LANES), F32), pltpu.SemaphoreType.DMA((2,))],
    )
    return pl.pallas_call(
        kern,
        grid_spec=grid_spec,
        out_shape=jax.ShapeDtypeStruct((b, n, d), F32),
        compiler_params=_params(("arbitrary", "arbitrary")),
        name="moe_residual",
    )(code, pad_start, h, route, gt2, g_final, ys)


def _moe(xn2, w_rt, b_rt, layer, w_gate, w_up, w_down):
    t_tok = xn2.shape[0] // SUBLANES
    assert 2 * t_tok < (1 << RANK_BITS)
    route, counts = _router(xn2, w_rt, b_rt)
    cnt = counts[0, N_GROUPS:N_GROUPS + N_EXPERTS].astype(jnp.int32)
    padded = (cnt + MOE_ROWS - 1) // MOE_ROWS * MOE_ROWS
    pad_end = jnp.cumsum(padded)
    pad_start = pad_end - padded
    ids = route[:, R_E1:R_RANK2 + 1].astype(jnp.int32)
    code = (jnp.left_shift(ids[:, R_E1:R_E2 + 1], RANK_BITS) + ids[:, R_RANK1:R_RANK2 + 1]).reshape(2 * t_tok)
    n_blocks = -(-(2 * t_tok + N_EXPERTS * (MOE_ROWS - 1)) // MOE_ROWS)
    n_slots = n_blocks * MOE_ROWS
    starts = jnp.arange(n_blocks, dtype=jnp.int32) * MOE_ROWS
    block_expert = jnp.minimum(jnp.sum(starts[:, None] >= pad_end[None, :], axis=1), N_EXPERTS - 1).astype(jnp.int32)
    n_used = (pad_end[-1:] // MOE_ROWS).astype(jnp.int32)

    e_ids = jnp.arange(N_EXPERTS, dtype=jnp.int32)
    later = (padded > 0)[None, :] & (e_ids[None, :] > e_ids[:, None])
    next_expert = jnp.min(jnp.where(later, e_ids[None, :], N_EXPERTS), axis=1)
    next_expert = jnp.where(next_expert == N_EXPERTS, -1, next_expert).astype(jnp.int32)
    pad_start = pad_start.astype(jnp.int32)
    xs = _dispatch(code, pad_start, pad_end.astype(jnp.int32), xn2, n_slots)
    return route, code, pad_start, _experts(block_expert, next_expert, n_used, xs, layer, w_gate, w_up, w_down)


def kernel(x, c, ctx, c_ctx, w_ada, b_ada, g_norm_mix, g_norm_ffn, w_in, da_lambda, da_subln_g, sg_ln_g, sg_ln_b, sg_w, sg_b, na_rpb, w_branch, w_out, moe_w_group, moe_b_group, moe_w_router, moe_b_router, moe_w_gate, moe_w_up, moe_w_down, g_final):
    b, n_lat, d = x.shape
    n_ctx = ctx.shape[1]
    depth = w_in.shape[0]
    rows = n_lat // GRID_W
    assert d == D_MODEL and n_lat % NA_QTOK == 0 and rows >= 2 * NA_QROWS and n_ctx % 256 == 0 and b <= 7
    tm_lat = 1024 if n_lat % 1024 == 0 else 512
    tm_mrg = 512
    tm_ctx = 256

    cos, sin = _rope_tables(n_lat)
    cond = jnp.zeros((8, d), F32).at[:b].set(c).at[b].set(c_ctx)
    mods = _ada(cond, w_ada, b_ada.reshape(depth, 1, 6 * d))

    h, hc = x, ctx
    for l in range(depth):
        last = l == depth - 1
        lam_init = 0.8 - 0.6 * math.exp(-0.3 * l)
        m_lat = mods[l, :b].reshape(b, 1, 6, d)
        m_ctx = jnp.broadcast_to(mods[l, b].reshape(1, 1, 6, d), (b, 1, 6, d))
        sh1, sc1, gt1, sh2, sc2, gt2 = (m_lat[:, :, i] for i in range(6))
        csh1, csc1, cgt1, csh2, csc2, cgt2 = (m_ctx[:, :, i] for i in range(6))

        w_in_l = w_in[l].astype(BF16)
        g_mix = g_norm_mix[l].reshape(1, d)
        g_ffn = g_norm_ffn[l].reshape(1, d)
        kv_zero = jnp.zeros((b, DA_HEADS, n_lat + n_ctx, HEAD_PAIR), BF16)
        p, q_hm, k_all, v_all = _norm_proj(h, g_mix, sh1, sc1, w_in_l, IN_COLS, tm_lat, rope_tables=(cos, sin),
                                           kv_all=(kv_zero, kv_zero))
        pc, *qc_hm, k_all, v_all = _norm_proj(hc, g_mix, csh1, csc1, w_in_l, KV_COLS if last else IN_COLS, tm_ctx,
                                              kv_all=(k_all, v_all), key_off=n_lat)

        g_sub = da_subln_g[l].reshape(1, 2 * DA_HEAD_DIM)
        y_a = _diff_attn(q_hm, k_all, v_all, da_lambda[l], g_sub, lam_init)

        y_c = _na_attn(p, pc, *_na_bias_tables(na_rpb[l], rows))

        wb = w_branch[l].astype(BF16)
        wo = w_out[l].astype(BF16)
        sgw = sg_w[l].astype(BF16)
        sgb = jnp.broadcast_to(sg_b[l][:, :, None], (SG_GROUPS, SG_CHUNK, SG_CHUNK))
        lng = sg_ln_g[l].reshape(1, SG_WIDTH)
        lnb = sg_ln_b[l].reshape(1, SG_WIDTH)
        t_lat = b * n_lat
        t_tok = t_lat if last else t_lat + b * n_ctx
        xn2_buf = None if last else jnp.zeros((t_tok * SUBLANES, LANES), F32)
        h, xn2 = _merge(h, y_a, p, y_c, wb, wo, sgw, sgb, lng, lnb, gt1, g_ffn, sh2, sc2, xn2_buf, t_tok, 0, tm_mrg)
        if not last:
            ya_c = _diff_attn(qc_hm[0], k_all[:, :, n_lat:], v_all[:, :, n_lat:], da_lambda[l], g_sub, lam_init)
            yc_c = _ctx_mha(pc)
            hc, xn2 = _merge(hc, ya_c, pc, yc_c, wb, wo, sgw, sgb, lng, lnb, cgt1, g_ffn, csh2, csc2,
                             xn2, t_tok, t_lat, tm_ctx)

        w_rt = jnp.zeros((d, LANES), F32).at[:, :N_GROUPS].set(moe_w_group[l]) \
            .at[:, N_GROUPS:N_GROUPS + N_EXPERTS].set(moe_w_router[l])
        b_rt = jnp.zeros((1, LANES), F32).at[0, :N_GROUPS].set(moe_b_group[l]) \
            .at[0, N_GROUPS:N_GROUPS + N_EXPERTS].set(moe_b_router[l])
        route, code, pad_start, ys = _moe(xn2, w_rt, b_rt, l, moe_w_gate, moe_w_up, moe_w_down)
        h = _residual(code, pad_start, h, ys, route, gt2, g_final.reshape(1, d), 0, last)
        if not last:
            hc = _residual(code, pad_start, hc, ys, route, cgt2, g_final.reshape(1, d), t_lat, False)
    return h
```

```python
import functools
import math

import numpy as np
import jax
import jax.numpy as jnp
from jax import lax
from jax.experimental import pallas as pl
from jax.experimental.pallas import tpu as pltpu

F32 = jnp.float32
BF16 = jnp.bfloat16

D_MODEL = 1024
GRID_W = 64
EPS = 1e-6
NEG_INF = -1e30
ROPE_THETA = 10000.0

DA_HEADS = 4
DA_HEAD_DIM = 64
NA_HEADS = 8
NA_ROWS = 8
NA_COLS = 16
SG_CHUNK = 128
SG_GROUPS = 4
SG_WIDTH = 512
BRANCH_WIDTH = 512
N_BRANCH = 3

OFF_KA = 0
OFF_VA = 512
OFF_KC = 1024
OFF_VC = 1536
KV_COLS = 2048
OFF_QA = 2048
OFF_QC = 2560
OFF_ZB = 3072
OFF_GATE = 4096
IN_COLS = 7168

N_GROUPS = 4
EXPERTS_PER_GROUP = 8
N_EXPERTS = 32

LANES = 128
HEAD_PAIR = LANES
VMEM_LIMIT = 56 * 1024 * 1024

MOE_ROWS = 256
DMA_CHUNK = 512

_CONTRACT_LAST = (((1,), (1,)), ((), ()))


def _params(sem):
    return pltpu.CompilerParams(dimension_semantics=sem, vmem_limit_bytes=VMEM_LIMIT)


DOT_ROWS = 256


def _dot_rows(a, b, contract_last=False):
    dims = _CONTRACT_LAST if contract_last else (((1,), (0,)), ((), ()))
    n = a.shape[0]
    if n <= DOT_ROWS:
        return lax.dot_general(a, b, dims, preferred_element_type=F32)
    return jnp.concatenate([lax.dot_general(a[r0:r0 + DOT_ROWS], b, dims, preferred_element_type=F32)
                            for r0 in range(0, n, DOT_ROWS)], axis=0)


def _ada_kernel(cond_ref, w_ref, b_ref, o_ref):
    c = cond_ref[...]
    c = c * jax.nn.sigmoid(c)
    o_ref[0] = jnp.dot(c, w_ref[0], preferred_element_type=F32, precision=lax.Precision.HIGHEST) + b_ref[0]


def _ada(cond, w_ada, b_ada):
    n_layers, d, d6 = w_ada.shape
    tn = 1024
    return pl.pallas_call(
        _ada_kernel,
        grid=(n_layers, d6 // tn),
        in_specs=[
            pl.BlockSpec((8, d), lambda l, j: (0, 0)),
            pl.BlockSpec((1, d, tn), lambda l, j: (l, 0, j)),
            pl.BlockSpec((1, 1, tn), lambda l, j: (l, 0, j)),
        ],
        out_specs=pl.BlockSpec((1, 8, tn), lambda l, j: (l, 0, j)),
        out_shape=jax.ShapeDtypeStruct((n_layers, 8, d6), F32),
        compiler_params=_params(("parallel", "parallel")),
        name="ada_mod",
    )(cond, w_ada, b_ada)


def _rms_mod(x, g, shift, scale):
    y = x * lax.rsqrt(jnp.mean(x * x, axis=-1, keepdims=True) + EPS)
    return (y * g) * (1.0 + scale) + shift


PROJ_TN = 1024
KV_TILE = OFF_KA // PROJ_TN
Q_TILE = OFF_QA // PROJ_TN


def _norm_proj_kernel(*refs, rope, with_q, aliased):
    n_in = 5 + (2 if rope else 0) + (2 if aliased else 0)
    h_ref, g_ref, sh_ref, sc_ref, w_ref = refs[:5]
    outs = refs[n_in:-1]
    o_ref, k_ref, v_ref = outs[0], outs[-2], outs[-1]
    xn_ref = refs[-1]
    j = pl.program_id(2)

    @pl.when(j == 0)
    def _():
        xn_ref[...] = _rms_mod(h_ref[0], g_ref[...], sh_ref[0], sc_ref[0]).astype(BF16)

    res = _dot_rows(xn_ref[...], w_ref[...])
    o_ref[0] = res.astype(o_ref.dtype)

    if rope:
        cos, sin = refs[5][...], refs[6][...]
        seg = DA_HEAD_DIM // 4
        first = (lax.broadcasted_iota(jnp.int32, cos.shape, 1) % (2 * seg)) < seg

        def rotate(x):
            partner = jnp.where(first, pltpu.roll(x, LANES - seg, 1), pltpu.roll(x, seg, 1))
            return x * cos + partner * sin
    else:
        def rotate(x):
            return x

    @pl.when(j == KV_TILE)
    def _():
        for hd in range(DA_HEADS):
            k_ref[0, hd] = rotate(res[:, hd * HEAD_PAIR:(hd + 1) * HEAD_PAIR]).astype(BF16)
            v_ref[0, hd] = res[:, OFF_VA + hd * HEAD_PAIR:OFF_VA + (hd + 1) * HEAD_PAIR].astype(BF16)

    if with_q:
        @pl.when(j == Q_TILE)
        def _():
            for hd in range(DA_HEADS):
                outs[1][0, hd] = (rotate(res[:, hd * HEAD_PAIR:(hd + 1) * HEAD_PAIR]) * DA_Q_SCALE).astype(BF16)


def _norm_proj(h, g, shift, scale, w, n_cols, tm, rope_tables=None, kv_all=None, n_keys=None, key_off=0):
    b, n, d = h.shape
    rope = rope_tables is not None
    aliased = kv_all is not None
    with_q = n_cols > OFF_QA
    off_blk = key_off // tm
    args = [h, g, shift, scale, w]
    in_specs = [
        pl.BlockSpec((1, tm, d), lambda bi, i, j: (bi, i, 0)),
        pl.BlockSpec((1, d), lambda bi, i, j: (0, 0)),
        pl.BlockSpec((1, 1, d), lambda bi, i, j: (bi, 0, 0)),
        pl.BlockSpec((1, 1, d), lambda bi, i, j: (bi, 0, 0)),
        pl.BlockSpec((d, PROJ_TN), lambda bi, i, j: (0, j)),
    ]
    if rope:
        args += list(rope_tables)
        in_specs += [pl.BlockSpec((tm, LANES), lambda bi, i, j: (i, 0))] * 2
    aliases = {}
    if aliased:
        n_keys = kv_all[0].shape[2]
        aliases = {len(args): 1 + with_q, len(args) + 1: 2 + with_q}
        args += list(kv_all)
        in_specs += [pl.BlockSpec(memory_space=pl.ANY)] * 2
    hm = lambda rows: jax.ShapeDtypeStruct((b, DA_HEADS, rows, HEAD_PAIR), BF16)
    q_spec = pl.BlockSpec((1, DA_HEADS, tm, HEAD_PAIR), lambda bi, i, j: (bi, 0, i, 0))
    kv_spec = pl.BlockSpec((1, DA_HEADS, tm, HEAD_PAIR), lambda bi, i, j: (bi, 0, off_blk + i, 0))
    out_specs = [pl.BlockSpec((1, tm, PROJ_TN), lambda bi, i, j: (bi, i, j))] + [q_spec] * with_q + [kv_spec] * 2
    out_shape = [jax.ShapeDtypeStruct((b, n, n_cols), BF16)] + [hm(n)] * with_q + [hm(n_keys)] * 2
    return pl.pallas_call(
        functools.partial(_norm_proj_kernel, rope=rope, with_q=with_q, aliased=aliased),
        grid=(b, n // tm, n_cols // PROJ_TN),
        in_specs=in_specs,
        out_specs=out_specs,
        out_shape=out_shape,
        input_output_aliases=aliases,
        scratch_shapes=[pltpu.VMEM((tm, d), BF16)],
        compiler_params=_params(("parallel", "parallel", "arbitrary")),
        name="norm_proj",
    )(*args)


def _rope_tables(n_tok):
    t = jnp.arange(n_tok, dtype=jnp.int32)
    row = (t // GRID_W).astype(F32)
    col = (t % GRID_W).astype(F32)
    half = DA_HEAD_DIM // 4
    inv = ROPE_THETA ** (-jnp.arange(half, dtype=F32) / half)
    ar = row[:, None] * inv
    ac = col[:, None] * inv
    ang = jnp.concatenate([ar, ar, ac, ac], axis=-1)
    sign = np.tile(np.concatenate([-np.ones(half), np.ones(half)]), 2).astype(np.float32)
    cos = jnp.cos(ang)
    sin = jnp.sin(ang) * sign
    return jnp.concatenate([cos, cos], axis=-1), jnp.concatenate([sin, sin], axis=-1)


DA_Q_SCALE = DA_HEAD_DIM ** -0.5 * math.log2(math.e)


def _split_pair(q):
    lo = lax.broadcasted_iota(jnp.int32, q.shape, 1) < (HEAD_PAIR // 2)
    zero = jnp.zeros_like(q)
    return jnp.concatenate([jnp.where(lo, q, zero), jnp.where(lo, zero, q)], axis=0)


def _diff_attn_kernel(lam_ref, g_ref, q_ref, k_ref, v_ref, o_ref, qs_ref, m_ref, l_ref, acc_ref,
                      *, tq, tk, lam_init):
    n_blk = k_ref.shape[2] // tk
    n_lt = tk // LANES
    qs_ref[...] = _split_pair(q_ref[0, 0])
    chunks = [slice(r0, r0 + DOT_ROWS) for r0 in range(0, 2 * tq, DOT_ROWS)]

    m_ref[...] = jnp.full(m_ref.shape, NEG_INF, F32)
    l_ref[...] = jnp.zeros(l_ref.shape, F32)
    acc_ref[...] = jnp.zeros(acc_ref.shape, F32)

    def key_block(j):
        keys = slice(0, tk) if n_blk == 1 else pl.ds(pl.multiple_of(j * tk, tk), tk)
        k = k_ref[0, 0, keys, :]
        v = v_ref[0, 0, keys, :]
        s = jnp.concatenate([lax.dot_general(qs_ref[c], k, _CONTRACT_LAST, preferred_element_type=F32)
                             for c in chunks], axis=0)
        tiles = [s[:, t * LANES:(t + 1) * LANES] for t in range(n_lt)]
        m_prev = m_ref[...]
        m_new = jnp.maximum(m_prev, jnp.max(functools.reduce(jnp.maximum, tiles), axis=-1, keepdims=True))
        alpha = jnp.exp2(m_prev - m_new)
        p_tiles = [jnp.exp2(t - m_new) for t in tiles]
        l_ref[...] = alpha * l_ref[...] + functools.reduce(jnp.add, p_tiles)
        p = jnp.concatenate(p_tiles, axis=1).astype(BF16)
        pv = jnp.concatenate([jnp.dot(p[c], v, preferred_element_type=F32) for c in chunks], axis=0)
        acc_ref[...] = alpha * acc_ref[...] + pv
        m_ref[...] = m_new

    if n_blk == 1:
        key_block(0)
    else:
        def body(j, carry):
            key_block(j)
            return carry
        lax.fori_loop(0, n_blk, body, 0)

    o = acc_ref[...] / jnp.sum(l_ref[...], axis=-1, keepdims=True)
    lm = lam_ref[...]
    lam = (jnp.exp(jnp.sum(lm[0:1] * lm[1:2], axis=-1, keepdims=True))
           - jnp.exp(jnp.sum(lm[2:3] * lm[3:4], axis=-1, keepdims=True)) + lam_init)
    d = o[:tq] - lam * o[tq:]
    y = d * lax.rsqrt(jnp.mean(d * d, axis=-1, keepdims=True) + EPS)
    o_ref[0] = ((y * g_ref[...]) * (1.0 - lam_init)).astype(o_ref.dtype)


def _pick(n, options):
    for o in options:
        if n % o == 0:
            return o
    raise ValueError(f"no tile in {options} divides {n}")


def _diff_attn(q, k, v, lam_params, g, lam_init):
    b, nh, nq, _ = q.shape
    nk = k.shape[2]
    tq = _pick(nq, (512, 256))
    tk = _pick(nk, (2816, 1408, 768, 512, 256))
    kern = functools.partial(_diff_attn_kernel, tq=tq, tk=tk, lam_init=lam_init)
    return pl.pallas_call(
        kern,
        grid=(b, nh, nq // tq),
        in_specs=[
            pl.BlockSpec((4, DA_HEAD_DIM), lambda bi, h, i: (0, 0)),
            pl.BlockSpec((1, HEAD_PAIR), lambda bi, h, i: (0, 0)),
            pl.BlockSpec((1, 1, tq, HEAD_PAIR), lambda bi, h, i: (bi, h, i, 0)),
            pl.BlockSpec((1, 1, nk, HEAD_PAIR), lambda bi, h, i: (bi, h, 0, 0)),
            pl.BlockSpec((1, 1, nk, HEAD_PAIR), lambda bi, h, i: (bi, h, 0, 0)),
        ],
        out_specs=pl.BlockSpec((1, tq, HEAD_PAIR), lambda bi, h, i: (bi, i, h)),
        out_shape=jax.ShapeDtypeStruct((b, nq, nh * HEAD_PAIR), BF16),
        scratch_shapes=[
            pltpu.VMEM((2 * tq, HEAD_PAIR), BF16),
            pltpu.VMEM((2 * tq, LANES), F32),
            pltpu.VMEM((2 * tq, LANES), F32),
            pltpu.VMEM((2 * tq, HEAD_PAIR), F32),
        ],
        compiler_params=_params(("parallel", "parallel", "parallel")),
        name="diff_attn",
    )(lam_params, g, q, k, v)


NA_QROWS = 8
NA_QTOK = NA_QROWS * GRID_W
NA_KBLK = 4 * GRID_W
NA_PIECES = 4
NA_WIN = NA_PIECES * NA_KBLK


def _na_bias_tables(rpb, rows):
    n_h = rpb.shape[0]
    n_kj = NA_PIECES * 4
    pad_r = n_kj - NA_ROWS
    pad_c = GRID_W - NA_COLS
    rp = jnp.pad(rpb * math.log2(math.e), ((0, 0), (pad_r, pad_r), (pad_c, pad_c)))
    a = jnp.stack([rp[:, :, GRID_W - 1 - qc:2 * GRID_W - 1 - qc] for qc in range(GRID_W)], axis=2)
    bias = jnp.stack([a[:, 3 + pad_r - qi:3 + pad_r - qi + n_kj].transpose(0, 2, 1, 3) for qi in range(NA_QROWS)],
                     axis=1).reshape(n_h, NA_QTOK, NA_WIN)

    n_r = rows // NA_QROWS
    qi = np.arange(NA_QROWS)
    kj = np.arange(n_kj)
    c = np.arange(GRID_W)
    cstart = np.clip(c - NA_COLS // 2, 0, GRID_W - NA_COLS)
    col_ok = (c[None, :] >= cstart[:, None]) & (c[None, :] < cstart[:, None] + NA_COLS)
    row_ok = []
    for r_grp in (0, min(1, n_r - 1), n_r - 1):
        r = NA_QROWS * r_grp + qi
        rs = np.clip(r - NA_ROWS // 2, 0, rows - NA_ROWS)
        krow = NA_QROWS * r_grp - 4 + kj
        row_ok.append((krow[None, :] >= rs[:, None]) & (krow[None, :] < rs[:, None] + NA_ROWS)
                      & (krow[None, :] >= 0) & (krow[None, :] < rows))
    ok = jnp.asarray(np.stack(row_ok))[:, :, None, :, None] & jnp.asarray(col_ok)[None, None, :, None, :]
    mask = jnp.where(ok, 0.0, NEG_INF).astype(F32).reshape(3, NA_QTOK, NA_WIN)
    return bias, mask


def _pair_softmax_pv(qs, k_list, v_list, bias_list):
    n_rows = qs.shape[0]
    chunks = [slice(r0, r0 + DOT_ROWS) for r0 in range(0, n_rows, DOT_ROWS)]
    tiles_list = []
    for k, bias in zip(k_list, bias_list):
        s = jnp.concatenate([lax.dot_general(qs[c], k, _CONTRACT_LAST, preferred_element_type=F32)
                             for c in chunks], axis=0)
        tiles = [s[:, j * LANES:(j + 1) * LANES] for j in range(k.shape[0] // LANES)]
        tiles_list.append(tiles if bias is None else [t + bias(j) for j, t in enumerate(tiles)])
    all_tiles = [t for tiles in tiles_list for t in tiles]
    m = jnp.max(functools.reduce(jnp.maximum, all_tiles), axis=-1, keepdims=True)
    p_list = [[jnp.exp2(t - m) for t in tiles] for tiles in tiles_list]
    l = jnp.sum(functools.reduce(jnp.add, [t for tiles in p_list for t in tiles]), axis=-1, keepdims=True)
    o = None
    for tiles, v in zip(p_list, v_list):
        p = jnp.concatenate(tiles, axis=1).astype(BF16)
        pv = jnp.concatenate([jnp.dot(p[c], v, preferred_element_type=F32) for c in chunks], axis=0)
        o = pv if o is None else o + pv
    o = o / l
    t = qs.shape[0] // 2
    lo = lax.broadcasted_iota(jnp.int32, (t, HEAD_PAIR), 1) < (HEAD_PAIR // 2)
    return jnp.where(lo, o[:t], o[t:])


def _scaled_pair(q, head_dim):
    return _split_pair((q.astype(F32) * (head_dim ** -0.5 * math.log2(math.e))).astype(BF16))


def _na_kernel(q_ref, k0, k1, k2, k3, v0, v1, v2, v3, kc_ref, vc_ref, b_ref, mk_ref, o_ref):
    qs = _scaled_pair(q_ref[0], HEAD_PAIR // 2)
    k_win = jnp.concatenate([k0[0], k1[0], k2[0], k3[0]], axis=0)
    v_win = jnp.concatenate([v0[0], v1[0], v2[0], v3[0]], axis=0)
    def bias(j):
        lanes = slice(j * LANES, (j + 1) * LANES)
        window = mk_ref[0, :, lanes]
        return jnp.concatenate([b_ref[0, :, lanes] + window, b_ref[1, :, lanes] + window], axis=0)

    o = _pair_softmax_pv(qs, [k_win, kc_ref[0]], [v_win, vc_ref[0]], [bias, None])
    o_ref[0] = o.astype(o_ref.dtype)


def _na_attn(p, pc, bias, mask):
    b, n, _ = p.shape
    n_ctx = pc.shape[1]
    n_r = n // NA_QTOK
    n_kb = n // NA_KBLK
    n_hp = NA_HEADS // 2

    def case(r):
        return jnp.where(r == 0, 0, jnp.where(r == n_r - 1, 2, 1))

    def kv_spec(off, piece):
        return pl.BlockSpec(
            (1, NA_KBLK, HEAD_PAIR),
            lambda bi, hp, r: (bi, jnp.clip(2 * r - 1 + piece, 0, n_kb - 1), off // HEAD_PAIR + hp))

    in_specs = [pl.BlockSpec((1, NA_QTOK, HEAD_PAIR), lambda bi, hp, r: (bi, r, OFF_QC // HEAD_PAIR + hp))]
    in_specs += [kv_spec(OFF_KC, i) for i in range(NA_PIECES)]
    in_specs += [kv_spec(OFF_VC, i) for i in range(NA_PIECES)]
    in_specs += [
        pl.BlockSpec((1, n_ctx, HEAD_PAIR), lambda bi, hp, r: (bi, 0, OFF_KC // HEAD_PAIR + hp)),
        pl.BlockSpec((1, n_ctx, HEAD_PAIR), lambda bi, hp, r: (bi, 0, OFF_VC // HEAD_PAIR + hp)),
        pl.BlockSpec((2, NA_QTOK, NA_WIN), lambda bi, hp, r: (hp, 0, 0)),
        pl.BlockSpec((1, NA_QTOK, NA_WIN), lambda bi, hp, r: (case(r), 0, 0)),
    ]
    return pl.pallas_call(
        _na_kernel,
        grid=(b, n_hp, n_r),
        in_specs=in_specs,
        out_specs=pl.BlockSpec((1, NA_QTOK, HEAD_PAIR), lambda bi, hp, r: (bi, r, hp)),
        out_shape=jax.ShapeDtypeStruct((b, n, NA_HEADS * HEAD_PAIR // 2), BF16),
        compiler_params=_params(("parallel", "parallel", "arbitrary")),
        name="na_attn",
    )(p, *([p] * (2 * NA_PIECES)), pc, pc, bias, mask)


def _ctx_mha_kernel(q_ref, k_ref, v_ref, o_ref):
    qs = _scaled_pair(q_ref[0], HEAD_PAIR // 2)
    o_ref[0] = _pair_softmax_pv(qs, [k_ref[0]], [v_ref[0]], [None]).astype(o_ref.dtype)


def _ctx_mha(pc):
    b, n_ctx, _ = pc.shape
    n_hp = NA_HEADS // 2

    def spec(off):
        return pl.BlockSpec((1, n_ctx, HEAD_PAIR), lambda bi, hp: (bi, 0, off // HEAD_PAIR + hp))

    return pl.pallas_call(
        _ctx_mha_kernel,
        grid=(b, n_hp),
        in_specs=[spec(OFF_QC), spec(OFF_KC), spec(OFF_VC)],
        out_specs=pl.BlockSpec((1, n_ctx, HEAD_PAIR), lambda bi, hp: (bi, 0, hp)),
        out_shape=jax.ShapeDtypeStruct((b, n_ctx, NA_HEADS * HEAD_PAIR // 2), BF16),
        compiler_params=_params(("parallel", "parallel")),
        name="ctx_mha",
    )(pc, pc, pc)


def _gelu_tanh(x):
    return 0.5 * x * (1.0 + jnp.tanh(math.sqrt(2.0 / math.pi) * (x + 0.044715 * (x * x * x))))


SUBLANES = 8


def _store_token_tiles(ref, x):
    t = x.shape[0]
    for j in range(SUBLANES):
        ref[pl.ds(j, t, stride=SUBLANES), :] = x[:, j * LANES:(j + 1) * LANES]


def _load_token_tiles(ref, t):
    return jnp.concatenate([ref[pl.ds(j, t, stride=SUBLANES), :] for j in range(SUBLANES)], axis=1)


def _merge_kernel(h_ref, ya_ref, z_ref, yc_ref, g0_ref, g1_ref, g2_ref, wb_ref, wo_ref, sgw_ref, sgb_ref,
                  lng_ref, lnb_ref, gt_ref, gf_ref, sh_ref, sc_ref, ho_ref, xo_ref, *, tm):
    z = _gelu_tanh(z_ref[0].astype(F32))
    u = z[:, :SG_WIDTH]
    vv = z[:, SG_WIDTH:]
    mu = jnp.mean(vv, axis=-1, keepdims=True)
    var = jnp.mean(jnp.square(vv - mu), axis=-1, keepdims=True)
    vv = ((vv - mu) * lax.rsqrt(var + EPS)) * lng_ref[...] + lnb_ref[...]
    vv = vv.astype(BF16)
    gd = SG_WIDTH // SG_GROUPS
    chunks = []
    for c in range(tm // SG_CHUNK):
        rows = slice(c * SG_CHUNK, (c + 1) * SG_CHUNK)
        groups = []
        for g in range(SG_GROUPS):
            s = jnp.dot(sgw_ref[g], vv[rows, g * gd:(g + 1) * gd], preferred_element_type=F32) + sgb_ref[g]
            groups.append(s)
        chunks.append(jnp.concatenate(groups, axis=1))
    y_b = (u * jnp.concatenate(chunks, axis=0)).astype(BF16)

    merged = None
    for y, gate_ref, i in ((ya_ref[0], g0_ref, 0), (y_b, g1_ref, 1), (yc_ref[0], g2_ref, 2)):
        t = jax.nn.sigmoid(gate_ref[0].astype(F32)) * _dot_rows(y, wb_ref[i])
        merged = t if merged is None else merged + t
    out = _dot_rows(merged.astype(BF16), wo_ref[...])
    h_new = h_ref[0] + gt_ref[0] * out
    ho_ref[0] = h_new
    _store_token_tiles(xo_ref, _rms_mod(h_new, gf_ref[...], sh_ref[0], sc_ref[0]))


def _merge(h, y_a, p, y_c, wb, wo, sgw, sgb, lng, lnb, gt1, g_ffn, sh2, sc2, xn2_buf, xn2_rows, row_off, tm):
    b, n, d = h.shape
    n_i = n // tm
    off_blk = row_off // tm
    vec = pl.BlockSpec((1, 1, d), lambda bi, i: (bi, 0, 0))

    def col(width, off):
        return pl.BlockSpec((1, tm, width), lambda bi, i: (bi, i, off // width))

    kern = functools.partial(_merge_kernel, tm=tm)
    args = [h, y_a, p, y_c, p, p, p, wb, wo, sgw, sgb, lng, lnb, gt1, g_ffn, sh2, sc2]
    in_specs = [
        pl.BlockSpec((1, tm, d), lambda bi, i: (bi, i, 0)),
        col(BRANCH_WIDTH, 0),
        col(2 * SG_WIDTH, OFF_ZB),
        col(BRANCH_WIDTH, 0),
        col(d, OFF_GATE), col(d, OFF_GATE + d), col(d, OFF_GATE + 2 * d),
        pl.BlockSpec((N_BRANCH, BRANCH_WIDTH, d), lambda bi, i: (0, 0, 0)),
        pl.BlockSpec((d, d), lambda bi, i: (0, 0)),
        pl.BlockSpec((SG_GROUPS, SG_CHUNK, SG_CHUNK), lambda bi, i: (0, 0, 0)),
        pl.BlockSpec((SG_GROUPS, SG_CHUNK, SG_CHUNK), lambda bi, i: (0, 0, 0)),
        pl.BlockSpec((1, SG_WIDTH), lambda bi, i: (0, 0)),
        pl.BlockSpec((1, SG_WIDTH), lambda bi, i: (0, 0)),
        vec,
        pl.BlockSpec((1, d), lambda bi, i: (0, 0)),
        vec, vec,
    ]
    aliases = {}
    if xn2_buf is not None:
        args.append(xn2_buf)
        in_specs.append(pl.BlockSpec(memory_space=pl.ANY))
        aliases = {len(args) - 1: 1}
    assert d == SUBLANES * LANES
    xn2_shape = jax.ShapeDtypeStruct((xn2_rows * SUBLANES, LANES), F32)

    def body(*refs):
        n_in = 17
        kern(*refs[:n_in], *refs[len(refs) - 2:])

    return pl.pallas_call(
        body,
        grid=(b, n_i),
        in_specs=in_specs,
        out_specs=[
            pl.BlockSpec((1, tm, d), lambda bi, i: (bi, i, 0)),
            pl.BlockSpec((tm * SUBLANES, LANES), lambda bi, i: (off_blk + bi * n_i + i, 0)),
        ],
        out_shape=[jax.ShapeDtypeStruct((b, n, d), F32), xn2_shape],
        input_output_aliases=aliases,
        compiler_params=_params(("parallel", "parallel")),
        name="merge_branches",
    )(*args)


R_E1, R_E2, R_W1, R_W2, R_RANK1, R_RANK2 = range(6)


def _router_kernel(x_ref, w_ref, b_ref, o_ref, cnt_ref, run_ref, *, tm):
    @pl.when(pl.program_id(0) == 0)
    def _():
        run_ref[...] = jnp.zeros(run_ref.shape, F32)

    x = _load_token_tiles(x_ref, tm)
    x_hi = x.astype(BF16)
    x_lo = (x - x_hi.astype(F32)).astype(BF16)
    w_hi, w_lo = w_ref[0], w_ref[1]
    logits = (jnp.dot(x_hi, w_hi, preferred_element_type=F32) + jnp.dot(x_lo, w_hi, preferred_element_type=F32)
              + jnp.dot(x_hi, w_lo, preferred_element_type=F32)) + b_ref[...]
    lane = lax.broadcasted_iota(jnp.int32, logits.shape, 1)
    lane_f = lane.astype(F32)
    far = jnp.float32(1e9)

    def first_lane(mask):
        return jnp.min(jnp.where(mask, lane_f, far), axis=-1, keepdims=True)

    is_g = lane < N_GROUPS
    gl = jnp.where(is_g, logits, NEG_INF)
    g_max = jnp.max(gl, axis=-1, keepdims=True)
    g_idx = first_lane(is_g & (gl == g_max))
    g_sum = jnp.sum(jnp.where(is_g, jnp.exp(gl - g_max), 0.0), axis=-1, keepdims=True)
    g_w = 1.0 / g_sum

    e_lane = lane - N_GROUPS
    in_grp = (e_lane >= 0) & (e_lane < N_EXPERTS) & \
        ((e_lane // EXPERTS_PER_GROUP).astype(F32) == g_idx)
    el = jnp.where(in_grp, logits, NEG_INF)
    l1 = jnp.max(el, axis=-1, keepdims=True)
    i1 = first_lane(in_grp & (el == l1))
    rest = in_grp & (lane_f != i1)
    el2 = jnp.where(rest, logits, NEG_INF)
    l2 = jnp.max(el2, axis=-1, keepdims=True)
    i2 = first_lane(rest & (el2 == l2))
    t = jnp.exp(l2 - l1)
    w1 = g_w / (1.0 + t)
    w2 = g_w * t / (1.0 + t)

    oh1 = lane_f == i1
    oh2 = lane_f == i2
    oh = jnp.where(oh1 | oh2, 1.0, 0.0)
    row = lax.broadcasted_iota(jnp.int32, (tm, tm), 0)
    colm = lax.broadcasted_iota(jnp.int32, (tm, tm), 1)
    before = jnp.where(colm < row, 1.0, 0.0).astype(BF16)
    prior = jnp.dot(before, oh.astype(BF16), preferred_element_type=F32) + run_ref[...]
    rank1 = jnp.sum(jnp.where(oh1, prior, 0.0), axis=-1, keepdims=True)
    rank2 = jnp.sum(jnp.where(oh2, prior, 0.0), axis=-1, keepdims=True)
    run_new = run_ref[...] + jnp.sum(oh, axis=0, keepdims=True)
    run_ref[...] = run_new
    cnt_ref[...] = run_new

    slab = jnp.zeros(logits.shape, F32)
    for ln, val in ((R_E1, i1 - N_GROUPS), (R_E2, i2 - N_GROUPS), (R_W1, w1), (R_W2, w2),
                    (R_RANK1, rank1), (R_RANK2, rank2)):
        slab = jnp.where(lane == ln, val, slab)
    o_ref[...] = slab


def _router(xn2, w_rt, b_rt):
    t_tok = xn2.shape[0] // SUBLANES
    d = w_rt.shape[0]
    tm = 512
    w_hi = w_rt.astype(BF16)
    w_split = jnp.stack([w_hi, (w_rt - w_hi.astype(F32)).astype(BF16)])
    kern = functools.partial(_router_kernel, tm=tm)
    return pl.pallas_call(
        kern,
        grid=(t_tok // tm,),
        in_specs=[
            pl.BlockSpec((tm * SUBLANES, LANES), lambda i: (i, 0)),
            pl.BlockSpec((2, d, LANES), lambda i: (0, 0, 0)),
            pl.BlockSpec((1, LANES), lambda i: (0, 0)),
        ],
        out_specs=[pl.BlockSpec((tm, LANES), lambda i: (i, 0)), pl.BlockSpec((1, LANES), lambda i: (0, 0))],
        out_shape=[jax.ShapeDtypeStruct((t_tok, LANES), F32), jax.ShapeDtypeStruct((1, LANES), F32)],
        scratch_shapes=[pltpu.VMEM((1, LANES), F32)],
        compiler_params=_params(("arbitrary",)),
        name="moe_router",
    )(xn2, w_split, b_rt)


def _tile_rows(index):
    return pl.ds(pl.multiple_of(index * SUBLANES, SUBLANES), SUBLANES)


def _dispatch_kernel(dest_ref, start_ref, end_ref, x_ref, xs_ref, zero_ref, sem, zsem, *, n_blocks):
    step = pl.program_id(0)
    blk_rows = MOE_ROWS * SUBLANES

    @pl.when(step == 0)
    def _():
        zero_ref[...] = jnp.zeros(zero_ref.shape, F32)

        def fill(blk):
            return pltpu.make_async_copy(zero_ref, xs_ref.at[pl.ds(pl.multiple_of(blk * blk_rows, blk_rows),
                                                                  blk_rows), :], zsem.at[0])

        def expert_tail(op):
            def body(e, carry):
                @pl.when(end_ref[e] > start_ref[e])
                def _():
                    op(fill(end_ref[e] // MOE_ROWS - 1))
                return carry
            lax.fori_loop(0, N_EXPERTS, body, 0)

        def unused(op):
            def body(blk, carry):
                @pl.when(blk * MOE_ROWS >= end_ref[N_EXPERTS - 1])
                def _():
                    op(fill(blk))
                return carry
            lax.fori_loop(0, n_blocks, body, 0)

        for phase in (lambda c: c.start(), lambda c: c.wait()):
            expert_tail(phase)
            unused(phase)

    def issue(i, carry):
        tok = step * DMA_CHUNK + i
        for k in range(2):
            pltpu.make_async_copy(x_ref.at[_tile_rows(i), :], xs_ref.at[_tile_rows(dest_ref[2 * tok + k]), :],
                                  sem.at[0]).start()
        return carry

    lax.fori_loop(0, DMA_CHUNK, issue, 0, unroll=4)
    for _ in range(2):
        pltpu.make_async_copy(x_ref, xs_ref.at[pl.ds(0, DMA_CHUNK * SUBLANES), :], sem.at[0]).wait()


def _dispatch(dest, pad_start, pad_end, xn2, n_slots):
    t_tok = xn2.shape[0] // SUBLANES
    kern = functools.partial(_dispatch_kernel, n_blocks=n_slots // MOE_ROWS)
    grid_spec = pltpu.PrefetchScalarGridSpec(
        num_scalar_prefetch=3,
        grid=(t_tok // DMA_CHUNK,),
        in_specs=[pl.BlockSpec((DMA_CHUNK * SUBLANES, LANES), lambda i, *_: (i, 0))],
        out_specs=pl.BlockSpec(memory_space=pl.ANY),
        scratch_shapes=[pltpu.VMEM((MOE_ROWS * SUBLANES, LANES), F32), pltpu.SemaphoreType.DMA((1,)),
                        pltpu.SemaphoreType.DMA((1,))],
    )
    return pl.pallas_call(
        kern,
        grid_spec=grid_spec,
        out_shape=jax.ShapeDtypeStruct((n_slots * SUBLANES, LANES), F32),
        compiler_params=_params(("arbitrary",)),
        name="moe_dispatch",
    )(dest, pad_start, pad_end, xn2)


def _expert_kernel(be_ref, nxt_ref, nu_ref, x_ref, wg_hbm, wu_hbm, wd_hbm, o_ref,
                   wg_f, wu_f, wd_f, wg_s, wu_s, wd_s, sem, turn_ref, *, w_off):
    i = pl.program_id(0)
    used = i < nu_ref[0]

    def fetch(expert, slot):
        return [pltpu.make_async_copy(src.at[w_off + expert], dst.at[slot], sem.at[slot])
                for src, dst in ((wg_hbm, wg_f), (wu_hbm, wu_f), (wd_hbm, wd_f))]

    @pl.when(i == 0)
    def _():
        turn_ref[0] = 0
        for c in fetch(be_ref[0], 0):
            c.start()

    @pl.when(used & ((i == 0) | (be_ref[i] != be_ref[jnp.maximum(i - 1, 0)])))
    def _():
        slot = turn_ref[0] % 2
        for c in fetch(be_ref[i], slot):
            c.wait()
        nxt = nxt_ref[be_ref[i]]

        @pl.when(nxt >= 0)
        def _():
            for c in fetch(nxt, 1 - slot):
                c.start()

        wg_s[...] = wg_f[slot].astype(BF16)
        wu_s[...] = wu_f[slot].astype(BF16)
        wd_s[...] = wd_f[slot].astype(BF16)
        turn_ref[0] = turn_ref[0] + 1

    @pl.when(used)
    def _():
        x = _load_token_tiles(x_ref, MOE_ROWS).astype(BF16)
        gate = jnp.dot(x, wg_s[...], preferred_element_type=F32)
        up = jnp.dot(x, wu_s[...], preferred_element_type=F32)
        hdn = (gate * jax.nn.sigmoid(gate)) * up
        _store_token_tiles(o_ref, jnp.dot(hdn.astype(BF16), wd_s[...], preferred_element_type=F32))

    @pl.when(i >= nu_ref[0])
    def _():
        o_ref[...] = jnp.zeros(o_ref.shape, F32)


def _experts(block_expert, next_expert, n_used, xs, layer, w_gate, w_up, w_down):
    n_blocks = xs.shape[0] // (MOE_ROWS * SUBLANES)
    n_layers, n_e, d, de = w_gate.shape
    w_gate, w_up, w_down = (w.reshape(n_layers * n_e, *w.shape[2:]) for w in (w_gate, w_up, w_down))
    blk = pl.BlockSpec((MOE_ROWS * SUBLANES, LANES), lambda i, *_: (i, 0))
    hbm = pl.BlockSpec(memory_space=pl.ANY)
    grid_spec = pltpu.PrefetchScalarGridSpec(
        num_scalar_prefetch=3,
        grid=(n_blocks,),
        in_specs=[blk, hbm, hbm, hbm],
        out_specs=blk,
        scratch_shapes=[
            pltpu.VMEM((2, d, de), F32), pltpu.VMEM((2, d, de), F32), pltpu.VMEM((2, de, d), F32),
            pltpu.VMEM((d, de), BF16), pltpu.VMEM((d, de), BF16), pltpu.VMEM((de, d), BF16),
            pltpu.SemaphoreType.DMA((2,)), pltpu.SMEM((1,), jnp.int32),
        ],
    )
    return pl.pallas_call(
        functools.partial(_expert_kernel, w_off=layer * n_e),
        grid_spec=grid_spec,
        out_shape=jax.ShapeDtypeStruct(xs.shape, F32),
        compiler_params=_params(("arbitrary",)),
        name="moe_experts",
    )(block_expert, next_expert, n_used, xs, w_gate, w_up, w_down)


def _residual_kernel(dest_ref, h_ref, r_ref, gt_ref, gf_ref, ys_ref, o_ref, buf_ref, sem,
                     *, tm, n_i, row_off, final):
    step = pl.program_id(0) * n_i + pl.program_id(1)
    n_steps = pl.num_programs(0) * n_i
    slot = step % 2

    def gather(tile, into):
        base = row_off + tile * tm

        def issue(t, carry):
            for k in range(2):
                pltpu.make_async_copy(ys_ref.at[_tile_rows(dest_ref[2 * (base + t) + k]), :],
                                      buf_ref.at[into, k, _tile_rows(t), :], sem.at[into]).start()
            return carry

        lax.fori_loop(0, tm, issue, 0, unroll=4)

    @pl.when(step == 0)
    def _():
        gather(step, slot)

    @pl.when(step + 1 < n_steps)
    def _():
        gather(step + 1, 1 - slot)

    for k in range(2):
        pltpu.make_async_copy(ys_ref.at[pl.ds(0, tm * SUBLANES), :], buf_ref.at[slot, k], sem.at[slot]).wait()

    r = r_ref[...]
    y = (_load_token_tiles(buf_ref.at[slot, 0], tm) * r[:, R_W1:R_W1 + 1]
         + _load_token_tiles(buf_ref.at[slot, 1], tm) * r[:, R_W2:R_W2 + 1])
    h_new = h_ref[0] + gt_ref[0] * y
    if final:
        h_new = (h_new * lax.rsqrt(jnp.mean(h_new * h_new, axis=-1, keepdims=True) + EPS)) * gf_ref[...]
    o_ref[0] = h_new


def _residual(dest, h, ys, route, gt2, g_final, row_off, final):
    b, n, d = h.shape
    tm = 256
    n_i = n // tm
    off_blk = row_off // tm
    kern = functools.partial(_residual_kernel, tm=tm, n_i=n_i, row_off=row_off, final=final)
    grid_spec = pltpu.PrefetchScalarGridSpec(
        num_scalar_prefetch=1,
        grid=(b, n_i),
        in_specs=[
            pl.BlockSpec((1, tm, d), lambda bi, i, *_: (bi, i, 0)),
            pl.BlockSpec((tm, LANES), lambda bi, i, *_: (off_blk + bi * n_i + i, 0)),
            pl.BlockSpec((1, 1, d), lambda bi, i, *_: (bi, 0, 0)),
            pl.BlockSpec((1, d), lambda bi, i, *_: (0, 0)),
            pl.BlockSpec(memory_space=pl.ANY),
        ],
        out_specs=pl.BlockSpec((1, tm, d), lambda bi, i, *_: (bi, i, 0)),
        scratch_shapes=[pltpu.VMEM((2, 2, tm * SUBLANES, LANES), F32), pltpu.SemaphoreType.DMA((2,))],
    )
    return pl.pallas_call(
        kern,
        grid_spec=grid_spec,
        out_shape=jax.ShapeDtypeStruct((b, n, d), F32),
        compiler_params=_params(("arbitrary", "arbitrary")),
        name="moe_residual",
    )(dest, h, route, gt2, g_final, ys)


def _moe(xn2, w_rt, b_rt, layer, w_gate, w_up, w_down):
    t_tok = xn2.shape[0] // SUBLANES
    route, counts = _router(xn2, w_rt, b_rt)
    cnt = counts[0, N_GROUPS:N_GROUPS + N_EXPERTS].astype(jnp.int32)
    padded = (cnt + MOE_ROWS - 1) // MOE_ROWS * MOE_ROWS
    pad_end = jnp.cumsum(padded)
    pad_start = pad_end - padded
    pad_start = pad_start.astype(jnp.int32)
    ids = route[:, R_E1:R_RANK2 + 1].astype(jnp.int32)
    expert = ids[:, R_E1:R_E2 + 1].reshape(2 * t_tok)
    rank = ids[:, R_RANK1:R_RANK2 + 1].reshape(2 * t_tok)
    e_ids = jnp.arange(N_EXPERTS, dtype=jnp.int32)
    dest = rank + jnp.sum(jnp.where(expert[:, None] == e_ids[None, :], pad_start[None, :], 0), axis=1)
    n_blocks = -(-(2 * t_tok + N_EXPERTS * (MOE_ROWS - 1)) // MOE_ROWS)
    n_slots = n_blocks * MOE_ROWS
    starts = jnp.arange(n_blocks, dtype=jnp.int32) * MOE_ROWS
    block_expert = jnp.minimum(jnp.sum(starts[:, None] >= pad_end[None, :], axis=1), N_EXPERTS - 1).astype(jnp.int32)
    n_used = (pad_end[-1:] // MOE_ROWS).astype(jnp.int32)

    later = (padded > 0)[None, :] & (e_ids[None, :] > e_ids[:, None])
    next_expert = jnp.min(jnp.where(later, e_ids[None, :], N_EXPERTS), axis=1)
    next_expert = jnp.where(next_expert == N_EXPERTS, -1, next_expert).astype(jnp.int32)
    xs = _dispatch(dest, pad_start, pad_end.astype(jnp.int32), xn2, n_slots)
    return route, dest, _experts(block_expert, next_expert, n_used, xs, layer, w_gate, w_up, w_down)


def kernel(x, c, ctx, c_ctx, w_ada, b_ada, g_norm_mix, g_norm_ffn, w_in, da_lambda, da_subln_g, sg_ln_g, sg_ln_b, sg_w, sg_b, na_rpb, w_branch, w_out, moe_w_group, moe_b_group, moe_w_router, moe_b_router, moe_w_gate, moe_w_up, moe_w_down, g_final):
    b, n_lat, d = x.shape
    n_ctx = ctx.shape[1]
    depth = w_in.shape[0]
    rows = n_lat // GRID_W
    assert d == D_MODEL and n_lat % NA_QTOK == 0 and rows >= 2 * NA_QROWS and n_ctx % 256 == 0 and b <= 7
    tm_lat = 1024 if n_lat % 1024 == 0 else 512
    tm_mrg = 512
    tm_ctx = 256

    cos, sin = _rope_tables(n_lat)
    cond = jnp.zeros((8, d), F32).at[:b].set(c).at[b].set(c_ctx)
    mods = _ada(cond, w_ada, b_ada.reshape(depth, 1, 6 * d))

    h, hc = x, ctx
    for l in range(depth):
        last = l == depth - 1
        lam_init = 0.8 - 0.6 * math.exp(-0.3 * l)
        m_lat = mods[l, :b].reshape(b, 1, 6, d)
        m_ctx = jnp.broadcast_to(mods[l, b].reshape(1, 1, 6, d), (b, 1, 6, d))
        sh1, sc1, gt1, sh2, sc2, gt2 = (m_lat[:, :, i] for i in range(6))
        csh1, csc1, cgt1, csh2, csc2, cgt2 = (m_ctx[:, :, i] for i in range(6))

        w_in_l = w_in[l].astype(BF16)
        g_mix = g_norm_mix[l].reshape(1, d)
        g_ffn = g_norm_ffn[l].reshape(1, d)
        kv_zero = jnp.zeros((b, DA_HEADS, n_lat + n_ctx, HEAD_PAIR), BF16)
        p, q_hm, k_all, v_all = _norm_proj(h, g_mix, sh1, sc1, w_in_l, IN_COLS, tm_lat, rope_tables=(cos, sin),
                                           kv_all=(kv_zero, kv_zero))
        pc, *qc_hm, k_all, v_all = _norm_proj(hc, g_mix, csh1, csc1, w_in_l, KV_COLS if last else IN_COLS, tm_ctx,
                                              kv_all=(k_all, v_all), key_off=n_lat)

        g_sub = da_subln_g[l].reshape(1, 2 * DA_HEAD_DIM)
        y_a = _diff_attn(q_hm, k_all, v_all, da_lambda[l], g_sub, lam_init)

        y_c = _na_attn(p, pc, *_na_bias_tables(na_rpb[l], rows))

        wb = w_branch[l].astype(BF16)
        wo = w_out[l].astype(BF16)
        sgw = sg_w[l].astype(BF16)
        sgb = jnp.broadcast_to(sg_b[l][:, :, None], (SG_GROUPS, SG_CHUNK, SG_CHUNK))
        lng = sg_ln_g[l].reshape(1, SG_WIDTH)
        lnb = sg_ln_b[l].reshape(1, SG_WIDTH)
        t_lat = b * n_lat
        t_tok = t_lat if last else t_lat + b * n_ctx
        xn2_buf = None if last else jnp.zeros((t_tok * SUBLANES, LANES), F32)
        h, xn2 = _merge(h, y_a, p, y_c, wb, wo, sgw, sgb, lng, lnb, gt1, g_ffn, sh2, sc2, xn2_buf, t_tok, 0, tm_mrg)
        if not last:
            ya_c = _diff_attn(qc_hm[0], k_all[:, :, n_lat:], v_all[:, :, n_lat:], da_lambda[l], g_sub, lam_init)
            yc_c = _ctx_mha(pc)
            hc, xn2 = _merge(hc, ya_c, pc, yc_c, wb, wo, sgw, sgb, lng, lnb, cgt1, g_ffn, csh2, csc2,
                             xn2, t_tok, t_lat, tm_ctx)

        w_rt = jnp.zeros((d, LANES), F32).at[:, :N_GROUPS].set(moe_w_group[l]) \
            .at[:, N_GROUPS:N_GROUPS + N_EXPERTS].set(moe_w_router[l])
        b_rt = jnp.zeros((1, LANES), F32).at[0, :N_GROUPS].set(moe_b_group[l]) \
            .at[0, N_GROUPS:N_GROUPS + N_EXPERTS].set(moe_b_router[l])
        route, dest, ys = _moe(xn2, w_rt, b_rt, l, moe_w_gate, moe_w_up, moe_w_down)
        h = _residual(dest, h, ys, route, gt2, g_final.reshape(1, d), 0, last)
        if not last:
            hc = _residual(dest, hc, ys, route, cgt2, g_final.reshape(1, d), t_lat, False)
    return h
```

```python
import functools
import math

import numpy as np
import jax
import jax.numpy as jnp
from jax import lax
from jax.experimental import pallas as pl
from jax.experimental.pallas import tpu as pltpu

F32 = jnp.float32
BF16 = jnp.bfloat16

D_MODEL = 1024
GRID_W = 64
EPS = 1e-6
NEG_INF = -1e30
ROPE_THETA = 10000.0

DA_HEADS = 4
DA_HEAD_DIM = 64
NA_HEADS = 8
NA_ROWS = 8
NA_COLS = 16
SG_CHUNK = 128
SG_GROUPS = 4
SG_WIDTH = 512
BRANCH_WIDTH = 512
N_BRANCH = 3

OFF_KA = 0
OFF_VA = 512
OFF_KC = 1024
OFF_VC = 1536
KV_COLS = 2048
OFF_QA = 2048
OFF_QC = 2560
OFF_ZB = 3072
OFF_GATE = 4096
IN_COLS = 7168

N_GROUPS = 4
EXPERTS_PER_GROUP = 8
N_EXPERTS = 32

LANES = 128
HEAD_PAIR = LANES
VMEM_LIMIT = 56 * 1024 * 1024

MOE_ROWS = 256
DMA_CHUNK = 512

_CONTRACT_LAST = (((1,), (1,)), ((), ()))


def _params(sem):
    return pltpu.CompilerParams(dimension_semantics=sem, vmem_limit_bytes=VMEM_LIMIT)


DOT_ROWS = 256


def _dot_rows(a, b, contract_last=False):
    dims = _CONTRACT_LAST if contract_last else (((1,), (0,)), ((), ()))
    n = a.shape[0]
    if n <= DOT_ROWS:
        return lax.dot_general(a, b, dims, preferred_element_type=F32)
    return jnp.concatenate([lax.dot_general(a[r0:r0 + DOT_ROWS], b, dims, preferred_element_type=F32)
                            for r0 in range(0, n, DOT_ROWS)], axis=0)


def _ada_kernel(cond_ref, w_ref, b_ref, o_ref):
    c = cond_ref[...]
    c = c * jax.nn.sigmoid(c)
    o_ref[0] = jnp.dot(c, w_ref[0], preferred_element_type=F32, precision=lax.Precision.HIGHEST) + b_ref[0]


def _ada(cond, w_ada, b_ada):
    n_layers, d, d6 = w_ada.shape
    tn = 1024
    return pl.pallas_call(
        _ada_kernel,
        grid=(n_layers, d6 // tn),
        in_specs=[
            pl.BlockSpec((8, d), lambda l, j: (0, 0)),
            pl.BlockSpec((1, d, tn), lambda l, j: (l, 0, j)),
            pl.BlockSpec((1, 1, tn), lambda l, j: (l, 0, j)),
        ],
        out_specs=pl.BlockSpec((1, 8, tn), lambda l, j: (l, 0, j)),
        out_shape=jax.ShapeDtypeStruct((n_layers, 8, d6), F32),
        compiler_params=_params(("parallel", "parallel")),
        name="ada_mod",
    )(cond, w_ada, b_ada)


def _rms_mod(x, g, shift, scale):
    y = x * lax.rsqrt(jnp.mean(x * x, axis=-1, keepdims=True) + EPS)
    return (y * g) * (1.0 + scale) + shift


PROJ_TN = 1024
KV_TILE = OFF_KA // PROJ_TN
Q_TILE = OFF_QA // PROJ_TN


def _norm_proj_kernel(*refs, rope, with_q, aliased):
    n_in = 5 + (2 if rope else 0) + (2 if aliased else 0)
    h_ref, g_ref, sh_ref, sc_ref, w_ref = refs[:5]
    outs = refs[n_in:-1]
    o_ref, k_ref, v_ref = outs[0], outs[-2], outs[-1]
    xn_ref = refs[-1]
    j = pl.program_id(2)

    @pl.when(j == 0)
    def _():
        xn_ref[...] = _rms_mod(h_ref[0], g_ref[...], sh_ref[0], sc_ref[0]).astype(BF16)

    res = _dot_rows(xn_ref[...], w_ref[0].astype(BF16))
    o_ref[0] = res.astype(o_ref.dtype)

    if rope:
        cos, sin = refs[5][...], refs[6][...]
        seg = DA_HEAD_DIM // 4
        first = (lax.broadcasted_iota(jnp.int32, cos.shape, 1) % (2 * seg)) < seg

        def rotate(x):
            partner = jnp.where(first, pltpu.roll(x, LANES - seg, 1), pltpu.roll(x, seg, 1))
            return x * cos + partner * sin
    else:
        def rotate(x):
            return x

    @pl.when(j == KV_TILE)
    def _():
        for hd in range(DA_HEADS):
            k_ref[0, hd] = rotate(res[:, hd * HEAD_PAIR:(hd + 1) * HEAD_PAIR]).astype(BF16)
            v_ref[0, hd] = res[:, OFF_VA + hd * HEAD_PAIR:OFF_VA + (hd + 1) * HEAD_PAIR].astype(BF16)

    if with_q:
        @pl.when(j == Q_TILE)
        def _():
            for hd in range(DA_HEADS):
                outs[1][0, hd] = (rotate(res[:, hd * HEAD_PAIR:(hd + 1) * HEAD_PAIR]) * DA_Q_SCALE).astype(BF16)


def _norm_proj(h, g, shift, scale, w, layer, n_cols, tm, rope_tables=None, kv_all=None, n_keys=None, key_off=0):
    b, n, d = h.shape
    rope = rope_tables is not None
    aliased = kv_all is not None
    with_q = n_cols > OFF_QA
    off_blk = key_off // tm
    args = [h, g, shift, scale, w]
    in_specs = [
        pl.BlockSpec((1, tm, d), lambda bi, i, j: (bi, i, 0)),
        pl.BlockSpec((1, d), lambda bi, i, j: (0, 0)),
        pl.BlockSpec((1, 1, d), lambda bi, i, j: (bi, 0, 0)),
        pl.BlockSpec((1, 1, d), lambda bi, i, j: (bi, 0, 0)),
        pl.BlockSpec((1, d, PROJ_TN), lambda bi, i, j: (layer, 0, j)),
    ]
    if rope:
        args += list(rope_tables)
        in_specs += [pl.BlockSpec((tm, LANES), lambda bi, i, j: (i, 0))] * 2
    aliases = {}
    if aliased:
        n_keys = kv_all[0].shape[2]
        aliases = {len(args): 1 + with_q, len(args) + 1: 2 + with_q}
        args += list(kv_all)
        in_specs += [pl.BlockSpec(memory_space=pl.ANY)] * 2
    hm = lambda rows: jax.ShapeDtypeStruct((b, DA_HEADS, rows, HEAD_PAIR), BF16)
    q_spec = pl.BlockSpec((1, DA_HEADS, tm, HEAD_PAIR), lambda bi, i, j: (bi, 0, i, 0))
    kv_spec = pl.BlockSpec((1, DA_HEADS, tm, HEAD_PAIR), lambda bi, i, j: (bi, 0, off_blk + i, 0))
    out_specs = [pl.BlockSpec((1, tm, PROJ_TN), lambda bi, i, j: (bi, i, j))] + [q_spec] * with_q + [kv_spec] * 2
    out_shape = [jax.ShapeDtypeStruct((b, n, n_cols), BF16)] + [hm(n)] * with_q + [hm(n_keys)] * 2
    return pl.pallas_call(
        functools.partial(_norm_proj_kernel, rope=rope, with_q=with_q, aliased=aliased),
        grid=(b, n // tm, n_cols // PROJ_TN),
        in_specs=in_specs,
        out_specs=out_specs,
        out_shape=out_shape,
        input_output_aliases=aliases,
        scratch_shapes=[pltpu.VMEM((tm, d), BF16)],
        compiler_params=_params(("parallel", "parallel", "arbitrary")),
        name="norm_proj",
    )(*args)


def _rope_tables(n_tok):
    t = jnp.arange(n_tok, dtype=jnp.int32)
    row = (t // GRID_W).astype(F32)
    col = (t % GRID_W).astype(F32)
    half = DA_HEAD_DIM // 4
    inv = ROPE_THETA ** (-jnp.arange(half, dtype=F32) / half)
    ar = row[:, None] * inv
    ac = col[:, None] * inv
    ang = jnp.concatenate([ar, ar, ac, ac], axis=-1)
    sign = np.tile(np.concatenate([-np.ones(half), np.ones(half)]), 2).astype(np.float32)
    cos = jnp.cos(ang)
    sin = jnp.sin(ang) * sign
    return jnp.concatenate([cos, cos], axis=-1), jnp.concatenate([sin, sin], axis=-1)


DA_Q_SCALE = DA_HEAD_DIM ** -0.5 * math.log2(math.e)


def _split_pair(q):
    lo = lax.broadcasted_iota(jnp.int32, q.shape, 1) < (HEAD_PAIR // 2)
    zero = jnp.zeros_like(q)
    return jnp.concatenate([jnp.where(lo, q, zero), jnp.where(lo, zero, q)], axis=0)


def _diff_attn_kernel(lam_ref, g_ref, q_ref, k_ref, v_ref, o_ref, qs_ref, m_ref, l_ref, acc_ref,
                      *, tq, tk, lam_init):
    n_blk = k_ref.shape[2] // tk
    n_lt = tk // LANES
    qs_ref[...] = _split_pair(q_ref[0, 0])
    chunks = [slice(r0, r0 + DOT_ROWS) for r0 in range(0, 2 * tq, DOT_ROWS)]

    m_ref[...] = jnp.full(m_ref.shape, NEG_INF, F32)
    l_ref[...] = jnp.zeros(l_ref.shape, F32)
    acc_ref[...] = jnp.zeros(acc_ref.shape, F32)

    def key_block(j):
        keys = slice(0, tk) if n_blk == 1 else pl.ds(pl.multiple_of(j * tk, tk), tk)
        k = k_ref[0, 0, keys, :]
        v = v_ref[0, 0, keys, :]
        s = jnp.concatenate([lax.dot_general(qs_ref[c], k, _CONTRACT_LAST, preferred_element_type=F32)
                             for c in chunks], axis=0)
        tiles = [s[:, t * LANES:(t + 1) * LANES] for t in range(n_lt)]
        m_prev = m_ref[...]
        m_new = jnp.maximum(m_prev, jnp.max(functools.reduce(jnp.maximum, tiles), axis=-1, keepdims=True))
        alpha = jnp.exp2(m_prev - m_new)
        p_tiles = [jnp.exp2(t - m_new) for t in tiles]
        l_ref[...] = alpha * l_ref[...] + functools.reduce(jnp.add, p_tiles)
        p = jnp.concatenate(p_tiles, axis=1).astype(BF16)
        pv = jnp.concatenate([jnp.dot(p[c], v, preferred_element_type=F32) for c in chunks], axis=0)
        acc_ref[...] = alpha * acc_ref[...] + pv
        m_ref[...] = m_new

    if n_blk == 1:
        key_block(0)
    else:
        def body(j, carry):
            key_block(j)
            return carry
        lax.fori_loop(0, n_blk, body, 0)

    o = acc_ref[...] / jnp.sum(l_ref[...], axis=-1, keepdims=True)
    lm = lam_ref[...]
    lam = (jnp.exp(jnp.sum(lm[0:1] * lm[1:2], axis=-1, keepdims=True))
           - jnp.exp(jnp.sum(lm[2:3] * lm[3:4], axis=-1, keepdims=True)) + lam_init)
    d = o[:tq] - lam * o[tq:]
    y = d * lax.rsqrt(jnp.mean(d * d, axis=-1, keepdims=True) + EPS)
    o_ref[0] = ((y * g_ref[...]) * (1.0 - lam_init)).astype(o_ref.dtype)


def _pick(n, options):
    for o in options:
        if n % o == 0:
            return o
    raise ValueError(f"no tile in {options} divides {n}")


def _diff_attn(q, k, v, lam_params, g, lam_init):
    b, nh, nq, _ = q.shape
    nk = k.shape[2]
    tq = _pick(nq, (512, 256))
    tk = _pick(nk, (2816, 1408, 768, 512, 256))
    kern = functools.partial(_diff_attn_kernel, tq=tq, tk=tk, lam_init=lam_init)
    return pl.pallas_call(
        kern,
        grid=(b, nh, nq // tq),
        in_specs=[
            pl.BlockSpec((4, DA_HEAD_DIM), lambda bi, h, i: (0, 0)),
            pl.BlockSpec((1, HEAD_PAIR), lambda bi, h, i: (0, 0)),
            pl.BlockSpec((1, 1, tq, HEAD_PAIR), lambda bi, h, i: (bi, h, i, 0)),
            pl.BlockSpec((1, 1, nk, HEAD_PAIR), lambda bi, h, i: (bi, h, 0, 0)),
            pl.BlockSpec((1, 1, nk, HEAD_PAIR), lambda bi, h, i: (bi, h, 0, 0)),
        ],
        out_specs=pl.BlockSpec((1, tq, HEAD_PAIR), lambda bi, h, i: (bi, i, h)),
        out_shape=jax.ShapeDtypeStruct((b, nq, nh * HEAD_PAIR), BF16),
        scratch_shapes=[
            pltpu.VMEM((2 * tq, HEAD_PAIR), BF16),
            pltpu.VMEM((2 * tq, LANES), F32),
            pltpu.VMEM((2 * tq, LANES), F32),
            pltpu.VMEM((2 * tq, HEAD_PAIR), F32),
        ],
        compiler_params=_params(("parallel", "parallel", "parallel")),
        name="diff_attn",
    )(lam_params, g, q, k, v)


NA_QROWS = 8
NA_QTOK = NA_QROWS * GRID_W
NA_KBLK = 4 * GRID_W
NA_PIECES = 4
NA_WIN = NA_PIECES * NA_KBLK


def _na_bias_tables(rpb, rows):
    n_h = rpb.shape[0]
    n_kj = NA_PIECES * 4
    pad_r = n_kj - NA_ROWS
    pad_c = GRID_W - NA_COLS
    rp = jnp.pad(rpb * math.log2(math.e), ((0, 0), (pad_r, pad_r), (pad_c, pad_c)))
    a = jnp.stack([rp[:, :, GRID_W - 1 - qc:2 * GRID_W - 1 - qc] for qc in range(GRID_W)], axis=2)
    bias = jnp.stack([jnp.concatenate([a[:, kj - qi + 3 + pad_r] for kj in range(n_kj)], axis=-1)
                      for qi in range(NA_QROWS)], axis=1).reshape(n_h, NA_QTOK, NA_WIN)

    n_r = rows // NA_QROWS
    qi = np.arange(NA_QROWS)
    kj = np.arange(n_kj)
    c = np.arange(GRID_W)
    cstart = np.clip(c - NA_COLS // 2, 0, GRID_W - NA_COLS)
    col_ok = (c[None, :] >= cstart[:, None]) & (c[None, :] < cstart[:, None] + NA_COLS)
    row_ok = []
    for r_grp in (0, min(1, n_r - 1), n_r - 1):
        r = NA_QROWS * r_grp + qi
        rs = np.clip(r - NA_ROWS // 2, 0, rows - NA_ROWS)
        krow = NA_QROWS * r_grp - 4 + kj
        row_ok.append((krow[None, :] >= rs[:, None]) & (krow[None, :] < rs[:, None] + NA_ROWS)
                      & (krow[None, :] >= 0) & (krow[None, :] < rows))
    ok = jnp.asarray(np.stack(row_ok))[:, :, None, :, None] & jnp.asarray(col_ok)[None, None, :, None, :]
    mask = jnp.where(ok, 0.0, NEG_INF).astype(F32).reshape(3, NA_QTOK, NA_WIN)
    return bias, mask


def _pair_softmax_pv(qs, k_list, v_list, bias_list):
    n_rows = qs.shape[0]
    chunks = [slice(r0, r0 + DOT_ROWS) for r0 in range(0, n_rows, DOT_ROWS)]
    tiles_list = []
    for k, bias in zip(k_list, bias_list):
        s = jnp.concatenate([lax.dot_general(qs[c], k, _CONTRACT_LAST, preferred_element_type=F32)
                             for c in chunks], axis=0)
        tiles = [s[:, j * LANES:(j + 1) * LANES] for j in range(k.shape[0] // LANES)]
        tiles_list.append(tiles if bias is None else [t + bias(j) for j, t in enumerate(tiles)])
    all_tiles = [t for tiles in tiles_list for t in tiles]
    m = jnp.max(functools.reduce(jnp.maximum, all_tiles), axis=-1, keepdims=True)
    p_list = [[jnp.exp2(t - m) for t in tiles] for tiles in tiles_list]
    l = jnp.sum(functools.reduce(jnp.add, [t for tiles in p_list for t in tiles]), axis=-1, keepdims=True)
    o = None
    for tiles, v in zip(p_list, v_list):
        p = jnp.concatenate(tiles, axis=1).astype(BF16)
        pv = jnp.concatenate([jnp.dot(p[c], v, preferred_element_type=F32) for c in chunks], axis=0)
        o = pv if o is None else o + pv
    o = o / l
    t = qs.shape[0] // 2
    lo = lax.broadcasted_iota(jnp.int32, (t, HEAD_PAIR), 1) < (HEAD_PAIR // 2)
    return jnp.where(lo, o[:t], o[t:])


def _scaled_pair(q, head_dim):
    return _split_pair((q.astype(F32) * (head_dim ** -0.5 * math.log2(math.e))).astype(BF16))


def _na_kernel(q_ref, k0, k1, k2, k3, v0, v1, v2, v3, kc_ref, vc_ref, b_ref, mk_ref, o_ref):
    qs = _scaled_pair(q_ref[0], HEAD_PAIR // 2)
    k_win = jnp.concatenate([k0[0], k1[0], k2[0], k3[0]], axis=0)
    v_win = jnp.concatenate([v0[0], v1[0], v2[0], v3[0]], axis=0)
    def bias(j):
        lanes = slice(j * LANES, (j + 1) * LANES)
        window = mk_ref[0, :, lanes]
        return jnp.concatenate([b_ref[0, :, lanes] + window, b_ref[1, :, lanes] + window], axis=0)

    o = _pair_softmax_pv(qs, [k_win, kc_ref[0]], [v_win, vc_ref[0]], [bias, None])
    o_ref[0] = o.astype(o_ref.dtype)


def _na_attn(p, pc, bias, mask):
    b, n, _ = p.shape
    n_ctx = pc.shape[1]
    n_r = n // NA_QTOK
    n_kb = n // NA_KBLK
    n_hp = NA_HEADS // 2

    def case(r):
        return jnp.where(r == 0, 0, jnp.where(r == n_r - 1, 2, 1))

    def kv_spec(off, piece):
        return pl.BlockSpec(
            (1, NA_KBLK, HEAD_PAIR),
            lambda bi, hp, r: (bi, jnp.clip(2 * r - 1 + piece, 0, n_kb - 1), off // HEAD_PAIR + hp))

    in_specs = [pl.BlockSpec((1, NA_QTOK, HEAD_PAIR), lambda bi, hp, r: (bi, r, OFF_QC // HEAD_PAIR + hp))]
    in_specs += [kv_spec(OFF_KC, i) for i in range(NA_PIECES)]
    in_specs += [kv_spec(OFF_VC, i) for i in range(NA_PIECES)]
    in_specs += [
        pl.BlockSpec((1, n_ctx, HEAD_PAIR), lambda bi, hp, r: (bi, 0, OFF_KC // HEAD_PAIR + hp)),
        pl.BlockSpec((1, n_ctx, HEAD_PAIR), lambda bi, hp, r: (bi, 0, OFF_VC // HEAD_PAIR + hp)),
        pl.BlockSpec((2, NA_QTOK, NA_WIN), lambda bi, hp, r: (hp, 0, 0)),
        pl.BlockSpec((1, NA_QTOK, NA_WIN), lambda bi, hp, r: (case(r), 0, 0)),
    ]
    return pl.pallas_call(
        _na_kernel,
        grid=(b, n_hp, n_r),
        in_specs=in_specs,
        out_specs=pl.BlockSpec((1, NA_QTOK, HEAD_PAIR), lambda bi, hp, r: (bi, r, hp)),
        out_shape=jax.ShapeDtypeStruct((b, n, NA_HEADS * HEAD_PAIR // 2), BF16),
        compiler_params=_params(("parallel", "parallel", "arbitrary")),
        name="na_attn",
    )(p, *([p] * (2 * NA_PIECES)), pc, pc, bias, mask)


def _ctx_mha_kernel(q_ref, k_ref, v_ref, o_ref):
    qs = _scaled_pair(q_ref[0], HEAD_PAIR // 2)
    o_ref[0] = _pair_softmax_pv(qs, [k_ref[0]], [v_ref[0]], [None]).astype(o_ref.dtype)


def _ctx_mha(pc):
    b, n_ctx, _ = pc.shape
    n_hp = NA_HEADS // 2

    def spec(off):
        return pl.BlockSpec((1, n_ctx, HEAD_PAIR), lambda bi, hp: (bi, 0, off // HEAD_PAIR + hp))

    return pl.pallas_call(
        _ctx_mha_kernel,
        grid=(b, n_hp),
        in_specs=[spec(OFF_QC), spec(OFF_KC), spec(OFF_VC)],
        out_specs=pl.BlockSpec((1, n_ctx, HEAD_PAIR), lambda bi, hp: (bi, 0, hp)),
        out_shape=jax.ShapeDtypeStruct((b, n_ctx, NA_HEADS * HEAD_PAIR // 2), BF16),
        compiler_params=_params(("parallel", "parallel")),
        name="ctx_mha",
    )(pc, pc, pc)


def _gelu_tanh(x):
    return 0.5 * x * (1.0 + jnp.tanh(math.sqrt(2.0 / math.pi) * (x + 0.044715 * (x * x * x))))


SUBLANES = 8


def _store_token_tiles(ref, x):
    t = x.shape[0]
    for j in range(SUBLANES):
        ref[pl.ds(j, t, stride=SUBLANES), :] = x[:, j * LANES:(j + 1) * LANES]


def _load_token_tiles(ref, t):
    return jnp.concatenate([ref[pl.ds(j, t, stride=SUBLANES), :] for j in range(SUBLANES)], axis=1)


def _merge_kernel(h_ref, ya_ref, z_ref, yc_ref, g0_ref, g1_ref, g2_ref, wb_ref, wo_ref, sgw_ref, sgb_ref,
                  lng_ref, lnb_ref, gt_ref, gf_ref, sh_ref, sc_ref, ho_ref, xo_ref, *, tm):
    z = _gelu_tanh(z_ref[0].astype(F32))
    u = z[:, :SG_WIDTH]
    vv = z[:, SG_WIDTH:]
    mu = jnp.mean(vv, axis=-1, keepdims=True)
    var = jnp.mean(jnp.square(vv - mu), axis=-1, keepdims=True)
    vv = ((vv - mu) * lax.rsqrt(var + EPS)) * lng_ref[...] + lnb_ref[...]
    vv = vv.astype(BF16)
    gd = SG_WIDTH // SG_GROUPS
    chunks = []
    for c in range(tm // SG_CHUNK):
        rows = slice(c * SG_CHUNK, (c + 1) * SG_CHUNK)
        groups = []
        for g in range(SG_GROUPS):
            s = jnp.dot(sgw_ref[g], vv[rows, g * gd:(g + 1) * gd], preferred_element_type=F32) + sgb_ref[g]
            groups.append(s)
        chunks.append(jnp.concatenate(groups, axis=1))
    y_b = (u * jnp.concatenate(chunks, axis=0)).astype(BF16)

    merged = None
    for y, gate_ref, i in ((ya_ref[0], g0_ref, 0), (y_b, g1_ref, 1), (yc_ref[0], g2_ref, 2)):
        t = jax.nn.sigmoid(gate_ref[0].astype(F32)) * _dot_rows(y, wb_ref[i])
        merged = t if merged is None else merged + t
    out = _dot_rows(merged.astype(BF16), wo_ref[...])
    h_new = h_ref[0] + gt_ref[0] * out
    ho_ref[0] = h_new
    _store_token_tiles(xo_ref, _rms_mod(h_new, gf_ref[...], sh_ref[0], sc_ref[0]))


def _merge(h, y_a, p, y_c, wb, wo, sgw, sgb, lng, lnb, gt1, g_ffn, sh2, sc2, xn2_buf, xn2_rows, row_off, tm):
    b, n, d = h.shape
    n_i = n // tm
    off_blk = row_off // tm
    vec = pl.BlockSpec((1, 1, d), lambda bi, i: (bi, 0, 0))

    def col(width, off):
        return pl.BlockSpec((1, tm, width), lambda bi, i: (bi, i, off // width))

    kern = functools.partial(_merge_kernel, tm=tm)
    args = [h, y_a, p, y_c, p, p, p, wb, wo, sgw, sgb, lng, lnb, gt1, g_ffn, sh2, sc2]
    in_specs = [
        pl.BlockSpec((1, tm, d), lambda bi, i: (bi, i, 0)),
        col(BRANCH_WIDTH, 0),
        col(2 * SG_WIDTH, OFF_ZB),
        col(BRANCH_WIDTH, 0),
        col(d, OFF_GATE), col(d, OFF_GATE + d), col(d, OFF_GATE + 2 * d),
        pl.BlockSpec((N_BRANCH, BRANCH_WIDTH, d), lambda bi, i: (0, 0, 0)),
        pl.BlockSpec((d, d), lambda bi, i: (0, 0)),
        pl.BlockSpec((SG_GROUPS, SG_CHUNK, SG_CHUNK), lambda bi, i: (0, 0, 0)),
        pl.BlockSpec((SG_GROUPS, SG_CHUNK, SG_CHUNK), lambda bi, i: (0, 0, 0)),
        pl.BlockSpec((1, SG_WIDTH), lambda bi, i: (0, 0)),
        pl.BlockSpec((1, SG_WIDTH), lambda bi, i: (0, 0)),
        vec,
        pl.BlockSpec((1, d), lambda bi, i: (0, 0)),
        vec, vec,
    ]
    aliases = {}
    if xn2_buf is not None:
        args.append(xn2_buf)
        in_specs.append(pl.BlockSpec(memory_space=pl.ANY))
        aliases = {len(args) - 1: 1}
    assert d == SUBLANES * LANES
    xn2_shape = jax.ShapeDtypeStruct((xn2_rows * SUBLANES, LANES), F32)

    def body(*refs):
        n_in = 17
        kern(*refs[:n_in], *refs[len(refs) - 2:])

    return pl.pallas_call(
        body,
        grid=(b, n_i),
        in_specs=in_specs,
        out_specs=[
            pl.BlockSpec((1, tm, d), lambda bi, i: (bi, i, 0)),
            pl.BlockSpec((tm * SUBLANES, LANES), lambda bi, i: (off_blk + bi * n_i + i, 0)),
        ],
        out_shape=[jax.ShapeDtypeStruct((b, n, d), F32), xn2_shape],
        input_output_aliases=aliases,
        compiler_params=_params(("parallel", "parallel")),
        name="merge_branches",
    )(*args)


R_E1, R_E2, R_W1, R_W2, R_RANK1, R_RANK2 = range(6)


def _router_kernel(x_ref, w_ref, b_ref, o_ref, cnt_ref, run_ref, *, tm):
    @pl.when(pl.program_id(0) == 0)
    def _():
        run_ref[...] = jnp.zeros(run_ref.shape, F32)

    x = _load_token_tiles(x_ref, tm)
    x_hi = x.astype(BF16)
    x_lo = (x - x_hi.astype(F32)).astype(BF16)
    w_hi, w_lo = w_ref[0], w_ref[1]
    logits = (jnp.dot(x_hi, w_hi, preferred_element_type=F32) + jnp.dot(x_lo, w_hi, preferred_element_type=F32)
              + jnp.dot(x_hi, w_lo, preferred_element_type=F32)) + b_ref[...]
    lane = lax.broadcasted_iota(jnp.int32, logits.shape, 1)
    lane_f = lane.astype(F32)
    far = jnp.float32(1e9)

    def first_lane(mask):
        return jnp.min(jnp.where(mask, lane_f, far), axis=-1, keepdims=True)

    is_g = lane < N_GROUPS
    gl = jnp.where(is_g, logits, NEG_INF)
    g_max = jnp.max(gl, axis=-1, keepdims=True)
    g_idx = first_lane(is_g & (gl == g_max))
    g_sum = jnp.sum(jnp.where(is_g, jnp.exp(gl - g_max), 0.0), axis=-1, keepdims=True)
    g_w = 1.0 / g_sum

    e_lane = lane - N_GROUPS
    in_grp = (e_lane >= 0) & (e_lane < N_EXPERTS) & \
        ((e_lane // EXPERTS_PER_GROUP).astype(F32) == g_idx)
    el = jnp.where(in_grp, logits, NEG_INF)
    l1 = jnp.max(el, axis=-1, keepdims=True)
    i1 = first_lane(in_grp & (el == l1))
    rest = in_grp & (lane_f != i1)
    el2 = jnp.where(rest, logits, NEG_INF)
    l2 = jnp.max(el2, axis=-1, keepdims=True)
    i2 = first_lane(rest & (el2 == l2))
    t = jnp.exp(l2 - l1)
    w1 = g_w / (1.0 + t)
    w2 = g_w * t / (1.0 + t)

    oh1 = lane_f == i1
    oh2 = lane_f == i2
    oh = jnp.where(oh1 | oh2, 1.0, 0.0)
    row = lax.broadcasted_iota(jnp.int32, (tm, tm), 0)
    colm = lax.broadcasted_iota(jnp.int32, (tm, tm), 1)
    before = jnp.where(colm < row, 1.0, 0.0).astype(BF16)
    prior = jnp.dot(before, oh.astype(BF16), preferred_element_type=F32) + run_ref[...]
    rank1 = jnp.sum(jnp.where(oh1, prior, 0.0), axis=-1, keepdims=True)
    rank2 = jnp.sum(jnp.where(oh2, prior, 0.0), axis=-1, keepdims=True)
    run_new = run_ref[...] + jnp.sum(oh, axis=0, keepdims=True)
    run_ref[...] = run_new
    cnt_ref[...] = run_new

    slab = jnp.zeros(logits.shape, F32)
    for ln, val in ((R_E1, i1 - N_GROUPS), (R_E2, i2 - N_GROUPS), (R_W1, w1), (R_W2, w2),
                    (R_RANK1, rank1), (R_RANK2, rank2)):
        slab = jnp.where(lane == ln, val, slab)
    o_ref[...] = slab


def _router(xn2, w_rt, b_rt):
    t_tok = xn2.shape[0] // SUBLANES
    d = w_rt.shape[0]
    tm = 512
    w_hi = w_rt.astype(BF16)
    w_split = jnp.stack([w_hi, (w_rt - w_hi.astype(F32)).astype(BF16)])
    kern = functools.partial(_router_kernel, tm=tm)
    return pl.pallas_call(
        kern,
        grid=(t_tok // tm,),
        in_specs=[
            pl.BlockSpec((tm * SUBLANES, LANES), lambda i: (i, 0)),
            pl.BlockSpec((2, d, LANES), lambda i: (0, 0, 0)),
            pl.BlockSpec((1, LANES), lambda i: (0, 0)),
        ],
        out_specs=[pl.BlockSpec((tm, LANES), lambda i: (i, 0)), pl.BlockSpec((1, LANES), lambda i: (0, 0))],
        out_shape=[jax.ShapeDtypeStruct((t_tok, LANES), F32), jax.ShapeDtypeStruct((1, LANES), F32)],
        scratch_shapes=[pltpu.VMEM((1, LANES), F32)],
        compiler_params=_params(("arbitrary",)),
        name="moe_router",
    )(xn2, w_split, b_rt)


def _tile_rows(index):
    return pl.ds(pl.multiple_of(index * SUBLANES, SUBLANES), SUBLANES)


def _dispatch_kernel(dest_ref, start_ref, end_ref, x_ref, xs_ref, zero_ref, sem, zsem, *, n_blocks):
    step = pl.program_id(0)
    blk_rows = MOE_ROWS * SUBLANES

    @pl.when(step == 0)
    def _():
        zero_ref[...] = jnp.zeros(zero_ref.shape, F32)

        def fill(blk):
            return pltpu.make_async_copy(zero_ref, xs_ref.at[pl.ds(pl.multiple_of(blk * blk_rows, blk_rows),
                                                                  blk_rows), :], zsem.at[0])

        def expert_tail(op):
            def body(e, carry):
                @pl.when(end_ref[e] > start_ref[e])
                def _():
                    op(fill(end_ref[e] // MOE_ROWS - 1))
                return carry
            lax.fori_loop(0, N_EXPERTS, body, 0)

        def unused(op):
            def body(blk, carry):
                @pl.when(blk * MOE_ROWS >= end_ref[N_EXPERTS - 1])
                def _():
                    op(fill(blk))
                return carry
            lax.fori_loop(0, n_blocks, body, 0)

        for phase in (lambda c: c.start(), lambda c: c.wait()):
            expert_tail(phase)
            unused(phase)

    def issue(i, carry):
        tok = step * DMA_CHUNK + i
        for k in range(2):
            pltpu.make_async_copy(x_ref.at[_tile_rows(i), :], xs_ref.at[_tile_rows(dest_ref[2 * tok + k]), :],
                                  sem.at[0]).start()
        return carry

    lax.fori_loop(0, DMA_CHUNK, issue, 0, unroll=4)
    for _ in range(2):
        pltpu.make_async_copy(x_ref, xs_ref.at[pl.ds(0, DMA_CHUNK * SUBLANES), :], sem.at[0]).wait()


def _dispatch(dest, pad_start, pad_end, xn2, n_slots):
    t_tok = xn2.shape[0] // SUBLANES
    kern = functools.partial(_dispatch_kernel, n_blocks=n_slots // MOE_ROWS)
    grid_spec = pltpu.PrefetchScalarGridSpec(
        num_scalar_prefetch=3,
        grid=(t_tok // DMA_CHUNK,),
        in_specs=[pl.BlockSpec((DMA_CHUNK * SUBLANES, LANES), lambda i, *_: (i, 0))],
        out_specs=pl.BlockSpec(memory_space=pl.ANY),
        scratch_shapes=[pltpu.VMEM((MOE_ROWS * SUBLANES, LANES), F32), pltpu.SemaphoreType.DMA((1,)),
                        pltpu.SemaphoreType.DMA((1,))],
    )
    return pl.pallas_call(
        kern,
        grid_spec=grid_spec,
        out_shape=jax.ShapeDtypeStruct((n_slots * SUBLANES, LANES), F32),
        compiler_params=_params(("arbitrary",)),
        name="moe_dispatch",
    )(dest, pad_start, pad_end, xn2)


def _expert_kernel(be_ref, nxt_ref, nu_ref, x_ref, wg_hbm, wu_hbm, wd_hbm, o_ref,
                   wg_f, wu_f, wd_f, wg_s, wu_s, wd_s, sem, turn_ref, *, w_off):
    i = pl.program_id(0)
    used = i < nu_ref[0]

    def fetch(expert, slot):
        return [pltpu.make_async_copy(src.at[w_off + expert], dst.at[slot], sem.at[slot])
                for src, dst in ((wg_hbm, wg_f), (wu_hbm, wu_f), (wd_hbm, wd_f))]

    @pl.when(i == 0)
    def _():
        turn_ref[0] = 0
        for c in fetch(be_ref[0], 0):
            c.start()

    @pl.when(used & ((i == 0) | (be_ref[i] != be_ref[jnp.maximum(i - 1, 0)])))
    def _():
        slot = turn_ref[0] % 2
        for c in fetch(be_ref[i], slot):
            c.wait()
        nxt = nxt_ref[be_ref[i]]

        @pl.when(nxt >= 0)
        def _():
            for c in fetch(nxt, 1 - slot):
                c.start()

        wg_s[...] = wg_f[slot].astype(BF16)
        wu_s[...] = wu_f[slot].astype(BF16)
        wd_s[...] = wd_f[slot].astype(BF16)
        turn_ref[0] = turn_ref[0] + 1

    @pl.when(used)
    def _():
        x = _load_token_tiles(x_ref, MOE_ROWS).astype(BF16)
        gate = jnp.dot(x, wg_s[...], preferred_element_type=F32)
        up = jnp.dot(x, wu_s[...], preferred_element_type=F32)
        hdn = (gate * jax.nn.sigmoid(gate)) * up
        _store_token_tiles(o_ref, jnp.dot(hdn.astype(BF16), wd_s[...], preferred_element_type=F32))

    @pl.when(i >= nu_ref[0])
    def _():
        o_ref[...] = jnp.zeros(o_ref.shape, F32)


def _experts(block_expert, next_expert, n_used, xs, layer, w_gate, w_up, w_down):
    n_blocks = xs.shape[0] // (MOE_ROWS * SUBLANES)
    n_layers, n_e, d, de = w_gate.shape
    w_gate, w_up, w_down = (w.reshape(n_layers * n_e, *w.shape[2:]) for w in (w_gate, w_up, w_down))
    blk = pl.BlockSpec((MOE_ROWS * SUBLANES, LANES), lambda i, *_: (i, 0))
    hbm = pl.BlockSpec(memory_space=pl.ANY)
    grid_spec = pltpu.PrefetchScalarGridSpec(
        num_scalar_prefetch=3,
        grid=(n_blocks,),
        in_specs=[blk, hbm, hbm, hbm],
        out_specs=blk,
        scratch_shapes=[
            pltpu.VMEM((2, d, de), F32), pltpu.VMEM((2, d, de), F32), pltpu.VMEM((2, de, d), F32),
            pltpu.VMEM((d, de), BF16), pltpu.VMEM((d, de), BF16), pltpu.VMEM((de, d), BF16),
            pltpu.SemaphoreType.DMA((2,)), pltpu.SMEM((1,), jnp.int32),
        ],
    )
    return pl.pallas_call(
        functools.partial(_expert_kernel, w_off=layer * n_e),
        grid_spec=grid_spec,
        out_shape=jax.ShapeDtypeStruct(xs.shape, F32),
        compiler_params=_params(("arbitrary",)),
        name="moe_experts",
    )(block_expert, next_expert, n_used, xs, w_gate, w_up, w_down)


def _residual_kernel(dest_ref, h_ref, r_ref, gt_ref, gf_ref, ys_ref, o_ref, buf_ref, sem,
                     *, tm, n_i, row_off, final):
    step = pl.program_id(0) * n_i + pl.program_id(1)
    n_steps = pl.num_programs(0) * n_i
    slot = step % 2

    def gather(tile, into):
        base = row_off + tile * tm

        def issue(t, carry):
            for k in range(2):
                pltpu.make_async_copy(ys_ref.at[_tile_rows(dest_ref[2 * (base + t) + k]), :],
                                      buf_ref.at[into, k, _tile_rows(t), :], sem.at[into]).start()
            return carry

        lax.fori_loop(0, tm, issue, 0, unroll=4)

    @pl.when(step == 0)
    def _():
        gather(step, slot)

    @pl.when(step + 1 < n_steps)
    def _():
        gather(step + 1, 1 - slot)

    for k in range(2):
        pltpu.make_async_copy(ys_ref.at[pl.ds(0, tm * SUBLANES), :], buf_ref.at[slot, k], sem.at[slot]).wait()

    r = r_ref[...]
    y = (_load_token_tiles(buf_ref.at[slot, 0], tm) * r[:, R_W1:R_W1 + 1]
         + _load_token_tiles(buf_ref.at[slot, 1], tm) * r[:, R_W2:R_W2 + 1])
    h_new = h_ref[0] + gt_ref[0] * y
    if final:
        h_new = (h_new * lax.rsqrt(jnp.mean(h_new * h_new, axis=-1, keepdims=True) + EPS)) * gf_ref[...]
    o_ref[0] = h_new


def _residual(dest, h, ys, route, gt2, g_final, row_off, final):
    b, n, d = h.shape
    tm = 256
    n_i = n // tm
    off_blk = row_off // tm
    kern = functools.partial(_residual_kernel, tm=tm, n_i=n_i, row_off=row_off, final=final)
    grid_spec = pltpu.PrefetchScalarGridSpec(
        num_scalar_prefetch=1,
        grid=(b, n_i),
        in_specs=[
            pl.BlockSpec((1, tm, d), lambda bi, i, *_: (bi, i, 0)),
            pl.BlockSpec((tm, LANES), lambda bi, i, *_: (off_blk + bi * n_i + i, 0)),
            pl.BlockSpec((1, 1, d), lambda bi, i, *_: (bi, 0, 0)),
            pl.BlockSpec((1, d), lambda bi, i, *_: (0, 0)),
            pl.BlockSpec(memory_space=pl.ANY),
        ],
        out_specs=pl.BlockSpec((1, tm, d), lambda bi, i, *_: (bi, i, 0)),
        scratch_shapes=[pltpu.VMEM((2, 2, tm * SUBLANES, LANES), F32), pltpu.SemaphoreType.DMA((2,))],
    )
    return pl.pallas_call(
        kern,
        grid_spec=grid_spec,
        out_shape=jax.ShapeDtypeStruct((b, n, d), F32),
        compiler_params=_params(("arbitrary", "arbitrary")),
        name="moe_residual",
    )(dest, h, route, gt2, g_final, ys)


def _moe(xn2, w_rt, b_rt, layer, w_gate, w_up, w_down):
    t_tok = xn2.shape[0] // SUBLANES
    route, counts = _router(xn2, w_rt, b_rt)
    cnt = counts[0, N_GROUPS:N_GROUPS + N_EXPERTS].astype(jnp.int32)
    padded = (cnt + MOE_ROWS - 1) // MOE_ROWS * MOE_ROWS
    pad_end = jnp.cumsum(padded)
    pad_start = pad_end - padded
    pad_start = pad_start.astype(jnp.int32)
    ids = route[:, R_E1:R_RANK2 + 1].astype(jnp.int32)
    expert = ids[:, R_E1:R_E2 + 1].reshape(2 * t_tok)
    rank = ids[:, R_RANK1:R_RANK2 + 1].reshape(2 * t_tok)
    e_ids = jnp.arange(N_EXPERTS, dtype=jnp.int32)
    dest = rank + jnp.sum(jnp.where(expert[:, None] == e_ids[None, :], pad_start[None, :], 0), axis=1)
    n_blocks = -(-(2 * t_tok + N_EXPERTS * (MOE_ROWS - 1)) // MOE_ROWS)
    n_slots = n_blocks * MOE_ROWS
    starts = jnp.arange(n_blocks, dtype=jnp.int32) * MOE_ROWS
    block_expert = jnp.minimum(jnp.sum(starts[:, None] >= pad_end[None, :], axis=1), N_EXPERTS - 1).astype(jnp.int32)
    n_used = (pad_end[-1:] // MOE_ROWS).astype(jnp.int32)

    later = (padded > 0)[None, :] & (e_ids[None, :] > e_ids[:, None])
    next_expert = jnp.min(jnp.where(later, e_ids[None, :], N_EXPERTS), axis=1)
    next_expert = jnp.where(next_expert == N_EXPERTS, -1, next_expert).astype(jnp.int32)
    xs = _dispatch(dest, pad_start, pad_end.astype(jnp.int32), xn2, n_slots)
    return route, dest, _experts(block_expert, next_expert, n_used, xs, layer, w_gate, w_up, w_down)


def kernel(x, c, ctx, c_ctx, w_ada, b_ada, g_norm_mix, g_norm_ffn, w_in, da_lambda, da_subln_g, sg_ln_g, sg_ln_b, sg_w, sg_b, na_rpb, w_branch, w_out, moe_w_group, moe_b_group, moe_w_router, moe_b_router, moe_w_gate, moe_w_up, moe_w_down, g_final):
    b, n_lat, d = x.shape
    n_ctx = ctx.shape[1]
    depth = w_in.shape[0]
    rows = n_lat // GRID_W
    assert d == D_MODEL and n_lat % NA_QTOK == 0 and rows >= 2 * NA_QROWS and n_ctx % 256 == 0 and b <= 7
    tm_lat = 1024 if n_lat % 1024 == 0 else 512
    tm_mrg = 512
    tm_ctx = 256

    cos, sin = _rope_tables(n_lat)
    cond = jnp.zeros((8, d), F32).at[:b].set(c).at[b].set(c_ctx)
    mods = _ada(cond, w_ada, b_ada.reshape(depth, 1, 6 * d))

    h, hc = x, ctx
    for l in range(depth):
        last = l == depth - 1
        lam_init = 0.8 - 0.6 * math.exp(-0.3 * l)
        m_lat = mods[l, :b].reshape(b, 1, 6, d)
        m_ctx = jnp.broadcast_to(mods[l, b].reshape(1, 1, 6, d), (b, 1, 6, d))
        sh1, sc1, gt1, sh2, sc2, gt2 = (m_lat[:, :, i] for i in range(6))
        csh1, csc1, cgt1, csh2, csc2, cgt2 = (m_ctx[:, :, i] for i in range(6))

        g_mix = g_norm_mix[l].reshape(1, d)
        g_ffn = g_norm_ffn[l].reshape(1, d)
        kv_zero = jnp.zeros((b, DA_HEADS, n_lat + n_ctx, HEAD_PAIR), BF16)
        p, q_hm, k_all, v_all = _norm_proj(h, g_mix, sh1, sc1, w_in, l, IN_COLS, tm_lat, rope_tables=(cos, sin),
                                           kv_all=(kv_zero, kv_zero))
        pc, *qc_hm, k_all, v_all = _norm_proj(hc, g_mix, csh1, csc1, w_in, l, KV_COLS if last else IN_COLS, tm_ctx,
                                              kv_all=(k_all, v_all), key_off=n_lat)

        g_sub = da_subln_g[l].reshape(1, 2 * DA_HEAD_DIM)
        y_a = _diff_attn(q_hm, k_all, v_all, da_lambda[l], g_sub, lam_init)

        y_c = _na_attn(p, pc, *_na_bias_tables(na_rpb[l], rows))

        wb = w_branch[l].astype(BF16)
        wo = w_out[l].astype(BF16)
        sgw = sg_w[l].astype(BF16)
        sgb = jnp.broadcast_to(sg_b[l][:, :, None], (SG_GROUPS, SG_CHUNK, SG_CHUNK))
        lng = sg_ln_g[l].reshape(1, SG_WIDTH)
        lnb = sg_ln_b[l].reshape(1, SG_WIDTH)
        t_lat = b * n_lat
        t_tok = t_lat if last else t_lat + b * n_ctx
        xn2_buf = None if last else jnp.zeros((t_tok * SUBLANES, LANES), F32)
        h, xn2 = _merge(h, y_a, p, y_c, wb, wo, sgw, sgb, lng, lnb, gt1, g_ffn, sh2, sc2, xn2_buf, t_tok, 0, tm_mrg)
        if not last:
            ya_c = _diff_attn(qc_hm[0], k_all[:, :, n_lat:], v_all[:, :, n_lat:], da_lambda[l], g_sub, lam_init)
            yc_c = _ctx_mha(pc)
            hc, xn2 = _merge(hc, ya_c, pc, yc_c, wb, wo, sgw, sgb, lng, lnb, cgt1, g_ffn, csh2, csc2,
                             xn2, t_tok, t_lat, tm_ctx)

        w_rt = jnp.zeros((d, LANES), F32).at[:, :N_GROUPS].set(moe_w_group[l]) \
            .at[:, N_GROUPS:N_GROUPS + N_EXPERTS].set(moe_w_router[l])
        b_rt = jnp.zeros((1, LANES), F32).at[0, :N_GROUPS].set(moe_b_group[l]) \
            .at[0, N_GROUPS:N_GROUPS + N_EXPERTS].set(moe_b_router[l])
        route, dest, ys = _moe(xn2, w_rt, b_rt, l, moe_w_gate, moe_w_up, moe_w_down)
        h = _residual(dest, h, ys, route, gt2, g_final.reshape(1, d), 0, last)
        if not last:
            hc = _residual(dest, hc, ys, route, cgt2, g_final.reshape(1, d), t_lat, False)
    return h
```

```python
import functools
import math

import numpy as np
import jax
import jax.numpy as jnp
from jax import lax
from jax.experimental import pallas as pl
from jax.experimental.pallas import tpu as pltpu

F32 = jnp.float32
BF16 = jnp.bfloat16

D_MODEL = 1024
GRID_W = 64
EPS = 1e-6
NEG_INF = -1e30
ROPE_THETA = 10000.0

DA_HEADS = 4
DA_HEAD_DIM = 64
NA_HEADS = 8
NA_ROWS = 8
NA_COLS = 16
SG_CHUNK = 128
SG_GROUPS = 4
SG_WIDTH = 512
BRANCH_WIDTH = 512
N_BRANCH = 3

OFF_KA = 0
OFF_VA = 512
OFF_KC = 1024
OFF_VC = 1536
KV_COLS = 2048
OFF_QA = 2048
OFF_QC = 2560
OFF_ZB = 3072
OFF_GATE = 4096
IN_COLS = 7168

N_GROUPS = 4
EXPERTS_PER_GROUP = 8
N_EXPERTS = 32

LANES = 128
HEAD_PAIR = LANES
VMEM_LIMIT = 56 * 1024 * 1024

MOE_ROWS = 256
DMA_CHUNK = 512

_CONTRACT_LAST = (((1,), (1,)), ((), ()))


def _params(sem):
    return pltpu.CompilerParams(dimension_semantics=sem, vmem_limit_bytes=VMEM_LIMIT)


DOT_ROWS = 256


def _dot_rows(a, b, contract_last=False):
    dims = _CONTRACT_LAST if contract_last else (((1,), (0,)), ((), ()))
    n = a.shape[0]
    if n <= DOT_ROWS:
        return lax.dot_general(a, b, dims, preferred_element_type=F32)
    return jnp.concatenate([lax.dot_general(a[r0:r0 + DOT_ROWS], b, dims, preferred_element_type=F32)
                            for r0 in range(0, n, DOT_ROWS)], axis=0)


def _ada_kernel(cond_ref, w_ref, b_ref, o_ref):
    c = cond_ref[...]
    c = c * jax.nn.sigmoid(c)
    o_ref[0] = jnp.dot(c, w_ref[0], preferred_element_type=F32, precision=lax.Precision.HIGHEST) + b_ref[0]


def _ada(cond, w_ada, b_ada):
    n_layers, d, d6 = w_ada.shape
    tn = 1024
    return pl.pallas_call(
        _ada_kernel,
        grid=(n_layers, d6 // tn),
        in_specs=[
            pl.BlockSpec((8, d), lambda l, j: (0, 0)),
            pl.BlockSpec((1, d, tn), lambda l, j: (l, 0, j)),
            pl.BlockSpec((1, 1, tn), lambda l, j: (l, 0, j)),
        ],
        out_specs=pl.BlockSpec((1, 8, tn), lambda l, j: (l, 0, j)),
        out_shape=jax.ShapeDtypeStruct((n_layers, 8, d6), F32),
        compiler_params=_params(("parallel", "parallel")),
        name="ada_mod",
    )(cond, w_ada, b_ada)


def _rms_mod(x, g, shift, scale):
    y = x * lax.rsqrt(jnp.mean(x * x, axis=-1, keepdims=True) + EPS)
    return (y * g) * (1.0 + scale) + shift


PROJ_TN = 1024
KV_TILE = OFF_KA // PROJ_TN
Q_TILE = OFF_QA // PROJ_TN


def _norm_proj_kernel(*refs, rope, with_q, aliased):
    n_in = 5 + (2 if rope else 0) + (2 if aliased else 0)
    h_ref, g_ref, sh_ref, sc_ref, w_ref = refs[:5]
    outs = refs[n_in:-1]
    o_ref, k_ref, v_ref = outs[0], outs[-2], outs[-1]
    xn_ref = refs[-1]
    j = pl.program_id(2)

    @pl.when(j == 0)
    def _():
        xn_ref[...] = _rms_mod(h_ref[0], g_ref[...], sh_ref[0], sc_ref[0]).astype(BF16)

    res = _dot_rows(xn_ref[...], w_ref[0].astype(BF16))
    o_ref[0] = res.astype(o_ref.dtype)

    if rope:
        cos, sin = refs[5][...], refs[6][...]
        seg = DA_HEAD_DIM // 4
        first = (lax.broadcasted_iota(jnp.int32, cos.shape, 1) % (2 * seg)) < seg

        def rotate(x):
            partner = jnp.where(first, pltpu.roll(x, LANES - seg, 1), pltpu.roll(x, seg, 1))
            return x * cos + partner * sin
    else:
        def rotate(x):
            return x

    @pl.when(j == KV_TILE)
    def _():
        for hd in range(DA_HEADS):
            k_ref[0, hd] = rotate(res[:, hd * HEAD_PAIR:(hd + 1) * HEAD_PAIR]).astype(BF16)
            v_ref[0, hd] = res[:, OFF_VA + hd * HEAD_PAIR:OFF_VA + (hd + 1) * HEAD_PAIR].astype(BF16)

    if with_q:
        @pl.when(j == Q_TILE)
        def _():
            for hd in range(DA_HEADS):
                outs[1][0, hd] = (rotate(res[:, hd * HEAD_PAIR:(hd + 1) * HEAD_PAIR]) * DA_Q_SCALE).astype(BF16)


def _norm_proj(h, g, shift, scale, w, layer, n_cols, tm, rope_tables=None, kv_all=None, n_keys=None, key_off=0):
    b, n, d = h.shape
    rope = rope_tables is not None
    aliased = kv_all is not None
    with_q = n_cols > OFF_QA
    off_blk = key_off // tm
    args = [h, g, shift, scale, w]
    in_specs = [
        pl.BlockSpec((1, tm, d), lambda bi, i, j: (bi, i, 0)),
        pl.BlockSpec((1, d), lambda bi, i, j: (0, 0)),
        pl.BlockSpec((1, 1, d), lambda bi, i, j: (bi, 0, 0)),
        pl.BlockSpec((1, 1, d), lambda bi, i, j: (bi, 0, 0)),
        pl.BlockSpec((1, d, PROJ_TN), lambda bi, i, j: (layer, 0, j)),
    ]
    if rope:
        args += list(rope_tables)
        in_specs += [pl.BlockSpec((tm, LANES), lambda bi, i, j: (i, 0))] * 2
    aliases = {}
    if aliased:
        n_keys = kv_all[0].shape[2]
        aliases = {len(args): 1 + with_q, len(args) + 1: 2 + with_q}
        args += list(kv_all)
        in_specs += [pl.BlockSpec(memory_space=pl.ANY)] * 2
    hm = lambda rows: jax.ShapeDtypeStruct((b, DA_HEADS, rows, HEAD_PAIR), BF16)
    q_spec = pl.BlockSpec((1, DA_HEADS, tm, HEAD_PAIR), lambda bi, i, j: (bi, 0, i, 0))
    kv_spec = pl.BlockSpec((1, DA_HEADS, tm, HEAD_PAIR), lambda bi, i, j: (bi, 0, off_blk + i, 0))
    out_specs = [pl.BlockSpec((1, tm, PROJ_TN), lambda bi, i, j: (bi, i, j))] + [q_spec] * with_q + [kv_spec] * 2
    out_shape = [jax.ShapeDtypeStruct((b, n, n_cols), BF16)] + [hm(n)] * with_q + [hm(n_keys)] * 2
    return pl.pallas_call(
        functools.partial(_norm_proj_kernel, rope=rope, with_q=with_q, aliased=aliased),
        grid=(b, n // tm, n_cols // PROJ_TN),
        in_specs=in_specs,
        out_specs=out_specs,
        out_shape=out_shape,
        input_output_aliases=aliases,
        scratch_shapes=[pltpu.VMEM((tm, d), BF16)],
        compiler_params=_params(("parallel", "parallel", "arbitrary")),
        name="norm_proj",
    )(*args)


def _rope_tables(n_tok):
    t = jnp.arange(n_tok, dtype=jnp.int32)
    row = (t // GRID_W).astype(F32)
    col = (t % GRID_W).astype(F32)
    half = DA_HEAD_DIM // 4
    inv = ROPE_THETA ** (-jnp.arange(half, dtype=F32) / half)
    ar = row[:, None] * inv
    ac = col[:, None] * inv
    ang = jnp.concatenate([ar, ar, ac, ac], axis=-1)
    sign = np.tile(np.concatenate([-np.ones(half), np.ones(half)]), 2).astype(np.float32)
    cos = jnp.cos(ang)
    sin = jnp.sin(ang) * sign
    return jnp.concatenate([cos, cos], axis=-1), jnp.concatenate([sin, sin], axis=-1)


DA_Q_SCALE = DA_HEAD_DIM ** -0.5 * math.log2(math.e)


def _split_pair(q):
    lo = lax.broadcasted_iota(jnp.int32, q.shape, 1) < (HEAD_PAIR // 2)
    zero = jnp.zeros_like(q)
    return jnp.concatenate([jnp.where(lo, q, zero), jnp.where(lo, zero, q)], axis=0)


def _diff_attn_kernel(lam_ref, g_ref, q_ref, k_ref, v_ref, o_ref, qs_ref, m_ref, l_ref, acc_ref,
                      *, tq, tk, lam_init):
    n_blk = k_ref.shape[2] // tk
    n_lt = tk // LANES
    qs_ref[...] = _split_pair(q_ref[0, 0])
    chunks = [slice(r0, r0 + DOT_ROWS) for r0 in range(0, 2 * tq, DOT_ROWS)]

    m_ref[...] = jnp.full(m_ref.shape, NEG_INF, F32)
    l_ref[...] = jnp.zeros(l_ref.shape, F32)
    acc_ref[...] = jnp.zeros(acc_ref.shape, F32)

    def key_block(j):
        keys = slice(0, tk) if n_blk == 1 else pl.ds(pl.multiple_of(j * tk, tk), tk)
        k = k_ref[0, 0, keys, :]
        v = v_ref[0, 0, keys, :]
        s = jnp.concatenate([lax.dot_general(qs_ref[c], k, _CONTRACT_LAST, preferred_element_type=F32)
                             for c in chunks], axis=0)
        tiles = [s[:, t * LANES:(t + 1) * LANES] for t in range(n_lt)]
        m_prev = m_ref[...]
        m_new = jnp.maximum(m_prev, jnp.max(functools.reduce(jnp.maximum, tiles), axis=-1, keepdims=True))
        alpha = jnp.exp2(m_prev - m_new)
        p_tiles = [jnp.exp2(t - m_new) for t in tiles]
        l_ref[...] = alpha * l_ref[...] + functools.reduce(jnp.add, p_tiles)
        p = jnp.concatenate(p_tiles, axis=1).astype(BF16)
        pv = jnp.concatenate([jnp.dot(p[c], v, preferred_element_type=F32) for c in chunks], axis=0)
        acc_ref[...] = alpha * acc_ref[...] + pv
        m_ref[...] = m_new

    if n_blk == 1:
        key_block(0)
    else:
        def body(j, carry):
            key_block(j)
            return carry
        lax.fori_loop(0, n_blk, body, 0)

    o = acc_ref[...] / jnp.sum(l_ref[...], axis=-1, keepdims=True)
    lm = lam_ref[...]
    lam = (jnp.exp(jnp.sum(lm[0:1] * lm[1:2], axis=-1, keepdims=True))
           - jnp.exp(jnp.sum(lm[2:3] * lm[3:4], axis=-1, keepdims=True)) + lam_init)
    d = o[:tq] - lam * o[tq:]
    y = d * lax.rsqrt(jnp.mean(d * d, axis=-1, keepdims=True) + EPS)
    o_ref[0] = ((y * g_ref[...]) * (1.0 - lam_init)).astype(o_ref.dtype)


def _pick(n, options):
    for o in options:
        if n % o == 0:
            return o
    raise ValueError(f"no tile in {options} divides {n}")


def _diff_attn(q, k, v, lam_params, g, lam_init):
    b, nh, nq, _ = q.shape
    nk = k.shape[2]
    tq = _pick(nq, (512, 256))
    tk = _pick(nk, (2816, 1408, 768, 512, 256))
    kern = functools.partial(_diff_attn_kernel, tq=tq, tk=tk, lam_init=lam_init)
    return pl.pallas_call(
        kern,
        grid=(b, nh, nq // tq),
        in_specs=[
            pl.BlockSpec((4, DA_HEAD_DIM), lambda bi, h, i: (0, 0)),
            pl.BlockSpec((1, HEAD_PAIR), lambda bi, h, i: (0, 0)),
            pl.BlockSpec((1, 1, tq, HEAD_PAIR), lambda bi, h, i: (bi, h, i, 0)),
            pl.BlockSpec((1, 1, nk, HEAD_PAIR), lambda bi, h, i: (bi, h, 0, 0)),
            pl.BlockSpec((1, 1, nk, HEAD_PAIR), lambda bi, h, i: (bi, h, 0, 0)),
        ],
        out_specs=pl.BlockSpec((1, tq, HEAD_PAIR), lambda bi, h, i: (bi, i, h)),
        out_shape=jax.ShapeDtypeStruct((b, nq, nh * HEAD_PAIR), BF16),
        scratch_shapes=[
            pltpu.VMEM((2 * tq, HEAD_PAIR), BF16),
            pltpu.VMEM((2 * tq, LANES), F32),
            pltpu.VMEM((2 * tq, LANES), F32),
            pltpu.VMEM((2 * tq, HEAD_PAIR), F32),
        ],
        compiler_params=_params(("parallel", "parallel", "parallel")),
        name="diff_attn",
    )(lam_params, g, q, k, v)


NA_QROWS = 8
NA_QTOK = NA_QROWS * GRID_W
NA_KBLK = 4 * GRID_W
NA_PIECES = 4
NA_WIN = NA_PIECES * NA_KBLK


def _na_bias_tables(rpb, rows):
    n_h = rpb.shape[0]
    n_kj = NA_PIECES * 4
    pad_r = n_kj - NA_ROWS
    pad_c = GRID_W - NA_COLS
    rp = jnp.pad(rpb * math.log2(math.e), ((0, 0), (pad_r, pad_r), (pad_c, pad_c)))
    a = jnp.stack([rp[:, :, GRID_W - 1 - qc:2 * GRID_W - 1 - qc] for qc in range(GRID_W)], axis=2)
    bias = jnp.stack([jnp.concatenate([a[:, kj - qi + 3 + pad_r] for kj in range(n_kj)], axis=-1)
                      for qi in range(NA_QROWS)], axis=1).reshape(n_h, NA_QTOK, NA_WIN)

    n_r = rows // NA_QROWS
    qi = np.arange(NA_QROWS)
    kj = np.arange(n_kj)
    c = np.arange(GRID_W)
    cstart = np.clip(c - NA_COLS // 2, 0, GRID_W - NA_COLS)
    col_ok = (c[None, :] >= cstart[:, None]) & (c[None, :] < cstart[:, None] + NA_COLS)
    row_ok = []
    for r_grp in (0, min(1, n_r - 1), n_r - 1):
        r = NA_QROWS * r_grp + qi
        rs = np.clip(r - NA_ROWS // 2, 0, rows - NA_ROWS)
        krow = NA_QROWS * r_grp - 4 + kj
        row_ok.append((krow[None, :] >= rs[:, None]) & (krow[None, :] < rs[:, None] + NA_ROWS)
                      & (krow[None, :] >= 0) & (krow[None, :] < rows))
    ok = jnp.asarray(np.stack(row_ok))[:, :, None, :, None] & jnp.asarray(col_ok)[None, None, :, None, :]
    mask = jnp.where(ok, 0.0, NEG_INF).astype(F32).reshape(3, NA_QTOK, NA_WIN)
    return bias, mask


def _pair_softmax_pv(qs, k_list, v_list, bias_list):
    n_rows = qs.shape[0]
    chunks = [slice(r0, r0 + DOT_ROWS) for r0 in range(0, n_rows, DOT_ROWS)]
    tiles_list = []
    for k, bias in zip(k_list, bias_list):
        s = jnp.concatenate([lax.dot_general(qs[c], k, _CONTRACT_LAST, preferred_element_type=F32)
                             for c in chunks], axis=0)
        tiles = [s[:, j * LANES:(j + 1) * LANES] for j in range(k.shape[0] // LANES)]
        tiles_list.append(tiles if bias is None else [t + bias(j) for j, t in enumerate(tiles)])
    all_tiles = [t for tiles in tiles_list for t in tiles]
    m = jnp.max(functools.reduce(jnp.maximum, all_tiles), axis=-1, keepdims=True)
    p_list = [[jnp.exp2(t - m) for t in tiles] for tiles in tiles_list]
    l = jnp.sum(functools.reduce(jnp.add, [t for tiles in p_list for t in tiles]), axis=-1, keepdims=True)
    o = None
    for tiles, v in zip(p_list, v_list):
        p = jnp.concatenate(tiles, axis=1).astype(BF16)
        pv = jnp.concatenate([jnp.dot(p[c], v, preferred_element_type=F32) for c in chunks], axis=0)
        o = pv if o is None else o + pv
    o = o / l
    t = qs.shape[0] // 2
    lo = lax.broadcasted_iota(jnp.int32, (t, HEAD_PAIR), 1) < (HEAD_PAIR // 2)
    return jnp.where(lo, o[:t], o[t:])


def _scaled_pair(q, head_dim):
    return _split_pair((q.astype(F32) * (head_dim ** -0.5 * math.log2(math.e))).astype(BF16))


def _na_kernel(q_ref, k0, k1, k2, k3, v0, v1, v2, v3, kc_ref, vc_ref, b_ref, mk_ref, o_ref):
    qs = _scaled_pair(q_ref[0], HEAD_PAIR // 2)
    k_win = jnp.concatenate([k0[0], k1[0], k2[0], k3[0]], axis=0)
    v_win = jnp.concatenate([v0[0], v1[0], v2[0], v3[0]], axis=0)
    def bias(j):
        lanes = slice(j * LANES, (j + 1) * LANES)
        window = mk_ref[0, :, lanes]
        return jnp.concatenate([b_ref[0, :, lanes] + window, b_ref[1, :, lanes] + window], axis=0)

    o = _pair_softmax_pv(qs, [k_win, kc_ref[0]], [v_win, vc_ref[0]], [bias, None])
    o_ref[0] = o.astype(o_ref.dtype)


def _na_attn(p, pc, bias, mask, layer):
    b, n, _ = p.shape
    n_ctx = pc.shape[1]
    n_r = n // NA_QTOK
    n_kb = n // NA_KBLK
    n_hp = NA_HEADS // 2

    def case(r):
        return jnp.where(r == 0, 0, jnp.where(r == n_r - 1, 2, 1))

    def kv_spec(off, piece):
        return pl.BlockSpec(
            (1, NA_KBLK, HEAD_PAIR),
            lambda bi, hp, r: (bi, jnp.clip(2 * r - 1 + piece, 0, n_kb - 1), off // HEAD_PAIR + hp))

    in_specs = [pl.BlockSpec((1, NA_QTOK, HEAD_PAIR), lambda bi, hp, r: (bi, r, OFF_QC // HEAD_PAIR + hp))]
    in_specs += [kv_spec(OFF_KC, i) for i in range(NA_PIECES)]
    in_specs += [kv_spec(OFF_VC, i) for i in range(NA_PIECES)]
    in_specs += [
        pl.BlockSpec((1, n_ctx, HEAD_PAIR), lambda bi, hp, r: (bi, 0, OFF_KC // HEAD_PAIR + hp)),
        pl.BlockSpec((1, n_ctx, HEAD_PAIR), lambda bi, hp, r: (bi, 0, OFF_VC // HEAD_PAIR + hp)),
        pl.BlockSpec((2, NA_QTOK, NA_WIN), lambda bi, hp, r: (layer * n_hp + hp, 0, 0)),
        pl.BlockSpec((1, NA_QTOK, NA_WIN), lambda bi, hp, r: (case(r), 0, 0)),
    ]
    return pl.pallas_call(
        _na_kernel,
        grid=(b, n_hp, n_r),
        in_specs=in_specs,
        out_specs=pl.BlockSpec((1, NA_QTOK, HEAD_PAIR), lambda bi, hp, r: (bi, r, hp)),
        out_shape=jax.ShapeDtypeStruct((b, n, NA_HEADS * HEAD_PAIR // 2), BF16),
        compiler_params=_params(("parallel", "parallel", "arbitrary")),
        name="na_attn",
    )(p, *([p] * (2 * NA_PIECES)), pc, pc, bias, mask)


def _ctx_mha_kernel(q_ref, k_ref, v_ref, o_ref):
    qs = _scaled_pair(q_ref[0], HEAD_PAIR // 2)
    o_ref[0] = _pair_softmax_pv(qs, [k_ref[0]], [v_ref[0]], [None]).astype(o_ref.dtype)


def _ctx_mha(pc):
    b, n_ctx, _ = pc.shape
    n_hp = NA_HEADS // 2

    def spec(off):
        return pl.BlockSpec((1, n_ctx, HEAD_PAIR), lambda bi, hp: (bi, 0, off // HEAD_PAIR + hp))

    return pl.pallas_call(
        _ctx_mha_kernel,
        grid=(b, n_hp),
        in_specs=[spec(OFF_QC), spec(OFF_KC), spec(OFF_VC)],
        out_specs=pl.BlockSpec((1, n_ctx, HEAD_PAIR), lambda bi, hp: (bi, 0, hp)),
        out_shape=jax.ShapeDtypeStruct((b, n_ctx, NA_HEADS * HEAD_PAIR // 2), BF16),
        compiler_params=_params(("parallel", "parallel")),
        name="ctx_mha",
    )(pc, pc, pc)


def _gelu_tanh(x):
    return 0.5 * x * (1.0 + jnp.tanh(math.sqrt(2.0 / math.pi) * (x + 0.044715 * (x * x * x))))


SUBLANES = 8


def _store_token_tiles(ref, x):
    t = x.shape[0]
    for j in range(SUBLANES):
        ref[pl.ds(j, t, stride=SUBLANES), :] = x[:, j * LANES:(j + 1) * LANES]


def _load_token_tiles(ref, t):
    return jnp.concatenate([ref[pl.ds(j, t, stride=SUBLANES), :] for j in range(SUBLANES)], axis=1)


def _merge_kernel(h_ref, ya_ref, z_ref, yc_ref, g0_ref, g1_ref, g2_ref, wb_ref, wo_ref, sgw_ref, sgb_ref,
                  lng_ref, lnb_ref, gt_ref, gf_ref, sh_ref, sc_ref, ho_ref, xo_ref, *, tm):
    z = _gelu_tanh(z_ref[0].astype(F32))
    u = z[:, :SG_WIDTH]
    vv = z[:, SG_WIDTH:]
    mu = jnp.mean(vv, axis=-1, keepdims=True)
    var = jnp.mean(jnp.square(vv - mu), axis=-1, keepdims=True)
    vv = ((vv - mu) * lax.rsqrt(var + EPS)) * lng_ref[...] + lnb_ref[...]
    vv = vv.astype(BF16)
    gd = SG_WIDTH // SG_GROUPS
    chunks = []
    for c in range(tm // SG_CHUNK):
        rows = slice(c * SG_CHUNK, (c + 1) * SG_CHUNK)
        groups = []
        for g in range(SG_GROUPS):
            s = jnp.dot(sgw_ref[g], vv[rows, g * gd:(g + 1) * gd], preferred_element_type=F32) + sgb_ref[g]
            groups.append(s)
        chunks.append(jnp.concatenate(groups, axis=1))
    y_b = (u * jnp.concatenate(chunks, axis=0)).astype(BF16)

    merged = None
    for y, gate_ref, i in ((ya_ref[0], g0_ref, 0), (y_b, g1_ref, 1), (yc_ref[0], g2_ref, 2)):
        t = jax.nn.sigmoid(gate_ref[0].astype(F32)) * _dot_rows(y, wb_ref[i])
        merged = t if merged is None else merged + t
    out = _dot_rows(merged.astype(BF16), wo_ref[...])
    h_new = h_ref[0] + gt_ref[0] * out
    ho_ref[0] = h_new
    _store_token_tiles(xo_ref, _rms_mod(h_new, gf_ref[...], sh_ref[0], sc_ref[0]))


def _merge(h, y_a, p, y_c, wb, wo, sgw, sgb, lng, lnb, gt1, g_ffn, sh2, sc2, xn2_buf, xn2_rows, row_off, tm):
    b, n, d = h.shape
    n_i = n // tm
    off_blk = row_off // tm
    vec = pl.BlockSpec((1, 1, d), lambda bi, i: (bi, 0, 0))

    def col(width, off):
        return pl.BlockSpec((1, tm, width), lambda bi, i: (bi, i, off // width))

    kern = functools.partial(_merge_kernel, tm=tm)
    args = [h, y_a, p, y_c, p, p, p, wb, wo, sgw, sgb, lng, lnb, gt1, g_ffn, sh2, sc2]
    in_specs = [
        pl.BlockSpec((1, tm, d), lambda bi, i: (bi, i, 0)),
        col(BRANCH_WIDTH, 0),
        col(2 * SG_WIDTH, OFF_ZB),
        col(BRANCH_WIDTH, 0),
        col(d, OFF_GATE), col(d, OFF_GATE + d), col(d, OFF_GATE + 2 * d),
        pl.BlockSpec((N_BRANCH, BRANCH_WIDTH, d), lambda bi, i: (0, 0, 0)),
        pl.BlockSpec((d, d), lambda bi, i: (0, 0)),
        pl.BlockSpec((SG_GROUPS, SG_CHUNK, SG_CHUNK), lambda bi, i: (0, 0, 0)),
        pl.BlockSpec((SG_GROUPS, SG_CHUNK, SG_CHUNK), lambda bi, i: (0, 0, 0)),
        pl.BlockSpec((1, SG_WIDTH), lambda bi, i: (0, 0)),
        pl.BlockSpec((1, SG_WIDTH), lambda bi, i: (0, 0)),
        vec,
        pl.BlockSpec((1, d), lambda bi, i: (0, 0)),
        vec, vec,
    ]
    aliases = {}
    if xn2_buf is not None:
        args.append(xn2_buf)
        in_specs.append(pl.BlockSpec(memory_space=pl.ANY))
        aliases = {len(args) - 1: 1}
    assert d == SUBLANES * LANES
    xn2_shape = jax.ShapeDtypeStruct((xn2_rows * SUBLANES, LANES), F32)

    def body(*refs):
        n_in = 17
        kern(*refs[:n_in], *refs[len(refs) - 2:])

    return pl.pallas_call(
        body,
        grid=(b, n_i),
        in_specs=in_specs,
        out_specs=[
            pl.BlockSpec((1, tm, d), lambda bi, i: (bi, i, 0)),
            pl.BlockSpec((tm * SUBLANES, LANES), lambda bi, i: (off_blk + bi * n_i + i, 0)),
        ],
        out_shape=[jax.ShapeDtypeStruct((b, n, d), F32), xn2_shape],
        input_output_aliases=aliases,
        compiler_params=_params(("parallel", "parallel")),
        name="merge_branches",
    )(*args)


R_E1, R_E2, R_W1, R_W2, R_RANK1, R_RANK2 = range(6)


def _router_kernel(x_ref, w_ref, b_ref, o_ref, cnt_ref, run_ref, *, tm):
    @pl.when(pl.program_id(0) == 0)
    def _():
        run_ref[...] = jnp.zeros(run_ref.shape, F32)

    x = _load_token_tiles(x_ref, tm)
    x_hi = x.astype(BF16)
    x_lo = (x - x_hi.astype(F32)).astype(BF16)
    w_hi, w_lo = w_ref[0], w_ref[1]
    logits = (jnp.dot(x_hi, w_hi, preferred_element_type=F32) + jnp.dot(x_lo, w_hi, preferred_element_type=F32)
              + jnp.dot(x_hi, w_lo, preferred_element_type=F32)) + b_ref[...]
    lane = lax.broadcasted_iota(jnp.int32, logits.shape, 1)
    lane_f = lane.astype(F32)
    far = jnp.float32(1e9)

    def first_lane(mask):
        return jnp.min(jnp.where(mask, lane_f, far), axis=-1, keepdims=True)

    is_g = lane < N_GROUPS
    gl = jnp.where(is_g, logits, NEG_INF)
    g_max = jnp.max(gl, axis=-1, keepdims=True)
    g_idx = first_lane(is_g & (gl == g_max))
    g_sum = jnp.sum(jnp.where(is_g, jnp.exp(gl - g_max), 0.0), axis=-1, keepdims=True)
    g_w = 1.0 / g_sum

    e_lane = lane - N_GROUPS
    in_grp = (e_lane >= 0) & (e_lane < N_EXPERTS) & \
        ((e_lane // EXPERTS_PER_GROUP).astype(F32) == g_idx)
    el = jnp.where(in_grp, logits, NEG_INF)
    l1 = jnp.max(el, axis=-1, keepdims=True)
    i1 = first_lane(in_grp & (el == l1))
    rest = in_grp & (lane_f != i1)
    el2 = jnp.where(rest, logits, NEG_INF)
    l2 = jnp.max(el2, axis=-1, keepdims=True)
    i2 = first_lane(rest & (el2 == l2))
    t = jnp.exp(l2 - l1)
    w1 = g_w / (1.0 + t)
    w2 = g_w * t / (1.0 + t)

    oh1 = lane_f == i1
    oh2 = lane_f == i2
    oh = jnp.where(oh1 | oh2, 1.0, 0.0)
    row = lax.broadcasted_iota(jnp.int32, (tm, tm), 0)
    colm = lax.broadcasted_iota(jnp.int32, (tm, tm), 1)
    before = jnp.where(colm < row, 1.0, 0.0).astype(BF16)
    prior = jnp.dot(before, oh.astype(BF16), preferred_element_type=F32) + run_ref[...]
    rank1 = jnp.sum(jnp.where(oh1, prior, 0.0), axis=-1, keepdims=True)
    rank2 = jnp.sum(jnp.where(oh2, prior, 0.0), axis=-1, keepdims=True)
    run_new = run_ref[...] + jnp.sum(oh, axis=0, keepdims=True)
    run_ref[...] = run_new
    cnt_ref[...] = run_new

    slab = jnp.zeros(logits.shape, F32)
    for ln, val in ((R_E1, i1 - N_GROUPS), (R_E2, i2 - N_GROUPS), (R_W1, w1), (R_W2, w2),
                    (R_RANK1, rank1), (R_RANK2, rank2)):
        slab = jnp.where(lane == ln, val, slab)
    o_ref[...] = slab


def _router(xn2, w_rt, b_rt):
    t_tok = xn2.shape[0] // SUBLANES
    d = w_rt.shape[0]
    tm = 512
    w_hi = w_rt.astype(BF16)
    w_split = jnp.stack([w_hi, (w_rt - w_hi.astype(F32)).astype(BF16)])
    kern = functools.partial(_router_kernel, tm=tm)
    return pl.pallas_call(
        kern,
        grid=(t_tok // tm,),
        in_specs=[
            pl.BlockSpec((tm * SUBLANES, LANES), lambda i: (i, 0)),
            pl.BlockSpec((2, d, LANES), lambda i: (0, 0, 0)),
            pl.BlockSpec((1, LANES), lambda i: (0, 0)),
        ],
        out_specs=[pl.BlockSpec((tm, LANES), lambda i: (i, 0)), pl.BlockSpec((1, LANES), lambda i: (0, 0))],
        out_shape=[jax.ShapeDtypeStruct((t_tok, LANES), F32), jax.ShapeDtypeStruct((1, LANES), F32)],
        scratch_shapes=[pltpu.VMEM((1, LANES), F32)],
        compiler_params=_params(("arbitrary",)),
        name="moe_router",
    )(xn2, w_split, b_rt)


def _tile_rows(index):
    return pl.ds(pl.multiple_of(index * SUBLANES, SUBLANES), SUBLANES)


def _dispatch_kernel(dest_ref, start_ref, end_ref, x_ref, xs_ref, zero_ref, sem, zsem, *, n_blocks):
    step = pl.program_id(0)
    blk_rows = MOE_ROWS * SUBLANES

    @pl.when(step == 0)
    def _():
        zero_ref[...] = jnp.zeros(zero_ref.shape, F32)

        def fill(blk):
            return pltpu.make_async_copy(zero_ref, xs_ref.at[pl.ds(pl.multiple_of(blk * blk_rows, blk_rows),
                                                                  blk_rows), :], zsem.at[0])

        def expert_tail(op):
            def body(e, carry):
                @pl.when(end_ref[e] > start_ref[e])
                def _():
                    op(fill(end_ref[e] // MOE_ROWS - 1))
                return carry
            lax.fori_loop(0, N_EXPERTS, body, 0)

        def unused(op):
            def body(blk, carry):
                @pl.when(blk * MOE_ROWS >= end_ref[N_EXPERTS - 1])
                def _():
                    op(fill(blk))
                return carry
            lax.fori_loop(0, n_blocks, body, 0)

        for phase in (lambda c: c.start(), lambda c: c.wait()):
            expert_tail(phase)
            unused(phase)

    def issue(i, carry):
        tok = step * DMA_CHUNK + i
        for k in range(2):
            pltpu.make_async_copy(x_ref.at[_tile_rows(i), :], xs_ref.at[_tile_rows(dest_ref[2 * tok + k]), :],
                                  sem.at[0]).start()
        return carry

    lax.fori_loop(0, DMA_CHUNK, issue, 0, unroll=4)
    for _ in range(2):
        pltpu.make_async_copy(x_ref, xs_ref.at[pl.ds(0, DMA_CHUNK * SUBLANES), :], sem.at[0]).wait()


def _dispatch(dest, pad_start, pad_end, xn2, n_slots):
    t_tok = xn2.shape[0] // SUBLANES
    kern = functools.partial(_dispatch_kernel, n_blocks=n_slots // MOE_ROWS)
    grid_spec = pltpu.PrefetchScalarGridSpec(
        num_scalar_prefetch=3,
        grid=(t_tok // DMA_CHUNK,),
        in_specs=[pl.BlockSpec((DMA_CHUNK * SUBLANES, LANES), lambda i, *_: (i, 0))],
        out_specs=pl.BlockSpec(memory_space=pl.ANY),
        scratch_shapes=[pltpu.VMEM((MOE_ROWS * SUBLANES, LANES), F32), pltpu.SemaphoreType.DMA((1,)),
                        pltpu.SemaphoreType.DMA((1,))],
    )
    return pl.pallas_call(
        kern,
        grid_spec=grid_spec,
        out_shape=jax.ShapeDtypeStruct((n_slots * SUBLANES, LANES), F32),
        compiler_params=_params(("arbitrary",)),
        name="moe_dispatch",
    )(dest, pad_start, pad_end, xn2)


def _expert_kernel(be_ref, nxt_ref, nu_ref, x_ref, wg_hbm, wu_hbm, wd_hbm, o_ref,
                   wg_f, wu_f, wd_f, wg_s, wu_s, wd_s, sem, turn_ref, *, w_off):
    i = pl.program_id(0)
    used = i < nu_ref[0]

    def fetch(expert, slot):
        return [pltpu.make_async_copy(src.at[w_off + expert], dst.at[slot], sem.at[slot])
                for src, dst in ((wg_hbm, wg_f), (wu_hbm, wu_f), (wd_hbm, wd_f))]

    @pl.when(i == 0)
    def _():
        turn_ref[0] = 0
        for c in fetch(be_ref[0], 0):
            c.start()

    @pl.when(used & ((i == 0) | (be_ref[i] != be_ref[jnp.maximum(i - 1, 0)])))
    def _():
        slot = turn_ref[0] % 2
        for c in fetch(be_ref[i], slot):
            c.wait()
        nxt = nxt_ref[be_ref[i]]

        @pl.when(nxt >= 0)
        def _():
            for c in fetch(nxt, 1 - slot):
                c.start()

        wg_s[...] = wg_f[slot].astype(BF16)
        wu_s[...] = wu_f[slot].astype(BF16)
        wd_s[...] = wd_f[slot].astype(BF16)
        turn_ref[0] = turn_ref[0] + 1

    @pl.when(used)
    def _():
        x = _load_token_tiles(x_ref, MOE_ROWS).astype(BF16)
        gate = jnp.dot(x, wg_s[...], preferred_element_type=F32)
        up = jnp.dot(x, wu_s[...], preferred_element_type=F32)
        hdn = (gate * jax.nn.sigmoid(gate)) * up
        _store_token_tiles(o_ref, jnp.dot(hdn.astype(BF16), wd_s[...], preferred_element_type=F32))

    @pl.when(i >= nu_ref[0])
    def _():
        o_ref[...] = jnp.zeros(o_ref.shape, F32)


def _experts(block_expert, next_expert, n_used, xs, layer, w_gate, w_up, w_down):
    n_blocks = xs.shape[0] // (MOE_ROWS * SUBLANES)
    n_layers, n_e, d, de = w_gate.shape
    w_gate, w_up, w_down = (w.reshape(n_layers * n_e, *w.shape[2:]) for w in (w_gate, w_up, w_down))
    blk = pl.BlockSpec((MOE_ROWS * SUBLANES, LANES), lambda i, *_: (i, 0))
    hbm = pl.BlockSpec(memory_space=pl.ANY)
    grid_spec = pltpu.PrefetchScalarGridSpec(
        num_scalar_prefetch=3,
        grid=(n_blocks,),
        in_specs=[blk, hbm, hbm, hbm],
        out_specs=blk,
        scratch_shapes=[
            pltpu.VMEM((2, d, de), F32), pltpu.VMEM((2, d, de), F32), pltpu.VMEM((2, de, d), F32),
            pltpu.VMEM((d, de), BF16), pltpu.VMEM((d, de), BF16), pltpu.VMEM((de, d), BF16),
            pltpu.SemaphoreType.DMA((2,)), pltpu.SMEM((1,), jnp.int32),
        ],
    )
    return pl.pallas_call(
        functools.partial(_expert_kernel, w_off=layer * n_e),
        grid_spec=grid_spec,
        out_shape=jax.ShapeDtypeStruct(xs.shape, F32),
        compiler_params=_params(("arbitrary",)),
        name="moe_experts",
    )(block_expert, next_expert, n_used, xs, w_gate, w_up, w_down)


def _residual_kernel(dest_ref, h_ref, r_ref, gt_ref, gf_ref, ys_ref, o_ref, buf_ref, sem,
                     *, tm, n_i, row_off, final):
    step = pl.program_id(0) * n_i + pl.program_id(1)
    n_steps = pl.num_programs(0) * n_i
    slot = step % 2

    def gather(tile, into):
        base = row_off + tile * tm

        def issue(t, carry):
            for k in range(2):
                pltpu.make_async_copy(ys_ref.at[_tile_rows(dest_ref[2 * (base + t) + k]), :],
                                      buf_ref.at[into, k, _tile_rows(t), :], sem.at[into]).start()
            return carry

        lax.fori_loop(0, tm, issue, 0, unroll=4)

    @pl.when(step == 0)
    def _():
        gather(step, slot)

    @pl.when(step + 1 < n_steps)
    def _():
        gather(step + 1, 1 - slot)

    for k in range(2):
        pltpu.make_async_copy(ys_ref.at[pl.ds(0, tm * SUBLANES), :], buf_ref.at[slot, k], sem.at[slot]).wait()

    r = r_ref[...]
    y = (_load_token_tiles(buf_ref.at[slot, 0], tm) * r[:, R_W1:R_W1 + 1]
         + _load_token_tiles(buf_ref.at[slot, 1], tm) * r[:, R_W2:R_W2 + 1])
    h_new = h_ref[0] + gt_ref[0] * y
    if final:
        h_new = (h_new * lax.rsqrt(jnp.mean(h_new * h_new, axis=-1, keepdims=True) + EPS)) * gf_ref[...]
    o_ref[0] = h_new


def _residual(dest, h, ys, route, gt2, g_final, row_off, final):
    b, n, d = h.shape
    tm = 256
    n_i = n // tm
    off_blk = row_off // tm
    kern = functools.partial(_residual_kernel, tm=tm, n_i=n_i, row_off=row_off, final=final)
    grid_spec = pltpu.PrefetchScalarGridSpec(
        num_scalar_prefetch=1,
        grid=(b, n_i),
        in_specs=[
            pl.BlockSpec((1, tm, d), lambda bi, i, *_: (bi, i, 0)),
            pl.BlockSpec((tm, LANES), lambda bi, i, *_: (off_blk + bi * n_i + i, 0)),
            pl.BlockSpec((1, 1, d), lambda bi, i, *_: (bi, 0, 0)),
            pl.BlockSpec((1, d), lambda bi, i, *_: (0, 0)),
            pl.BlockSpec(memory_space=pl.ANY),
        ],
        out_specs=pl.BlockSpec((1, tm, d), lambda bi, i, *_: (bi, i, 0)),
        scratch_shapes=[pltpu.VMEM((2, 2, tm * SUBLANES, LANES), F32), pltpu.SemaphoreType.DMA((2,))],
    )
    return pl.pallas_call(
        kern,
        grid_spec=grid_spec,
        out_shape=jax.ShapeDtypeStruct((b, n, d), F32),
        compiler_params=_params(("arbitrary", "arbitrary")),
        name="moe_residual",
    )(dest, h, route, gt2, g_final, ys)


def _moe(xn2, w_rt, b_rt, layer, w_gate, w_up, w_down):
    t_tok = xn2.shape[0] // SUBLANES
    route, counts = _router(xn2, w_rt, b_rt)
    cnt = counts[0, N_GROUPS:N_GROUPS + N_EXPERTS].astype(jnp.int32)
    padded = (cnt + MOE_ROWS - 1) // MOE_ROWS * MOE_ROWS
    pad_end = jnp.cumsum(padded)
    pad_start = pad_end - padded
    pad_start = pad_start.astype(jnp.int32)
    ids = route[:, R_E1:R_RANK2 + 1].astype(jnp.int32)
    expert = ids[:, R_E1:R_E2 + 1].reshape(2 * t_tok)
    rank = ids[:, R_RANK1:R_RANK2 + 1].reshape(2 * t_tok)
    e_ids = jnp.arange(N_EXPERTS, dtype=jnp.int32)
    dest = rank + jnp.sum(jnp.where(expert[:, None] == e_ids[None, :], pad_start[None, :], 0), axis=1)
    n_blocks = -(-(2 * t_tok + N_EXPERTS * (MOE_ROWS - 1)) // MOE_ROWS)
    n_slots = n_blocks * MOE_ROWS
    starts = jnp.arange(n_blocks, dtype=jnp.int32) * MOE_ROWS
    block_expert = jnp.minimum(jnp.sum(starts[:, None] >= pad_end[None, :], axis=1), N_EXPERTS - 1).astype(jnp.int32)
    n_used = (pad_end[-1:] // MOE_ROWS).astype(jnp.int32)

    later = (padded > 0)[None, :] & (e_ids[None, :] > e_ids[:, None])
    next_expert = jnp.min(jnp.where(later, e_ids[None, :], N_EXPERTS), axis=1)
    next_expert = jnp.where(next_expert == N_EXPERTS, -1, next_expert).astype(jnp.int32)
    xs = _dispatch(dest, pad_start, pad_end.astype(jnp.int32), xn2, n_slots)
    return route, dest, _experts(block_expert, next_expert, n_used, xs, layer, w_gate, w_up, w_down)


def kernel(x, c, ctx, c_ctx, w_ada, b_ada, g_norm_mix, g_norm_ffn, w_in, da_lambda, da_subln_g, sg_ln_g, sg_ln_b, sg_w, sg_b, na_rpb, w_branch, w_out, moe_w_group, moe_b_group, moe_w_router, moe_b_router, moe_w_gate, moe_w_up, moe_w_down, g_final):
    b, n_lat, d = x.shape
    n_ctx = ctx.shape[1]
    depth = w_in.shape[0]
    rows = n_lat // GRID_W
    assert d == D_MODEL and n_lat % NA_QTOK == 0 and rows >= 2 * NA_QROWS and n_ctx % 256 == 0 and b <= 7
    tm_lat = 1024 if n_lat % 1024 == 0 else 512
    tm_mrg = 512
    tm_ctx = 256

    cos, sin = _rope_tables(n_lat)
    na_bias, na_mask = _na_bias_tables(na_rpb.reshape(depth * NA_HEADS, *na_rpb.shape[2:]), rows)
    cond = jnp.zeros((8, d), F32).at[:b].set(c).at[b].set(c_ctx)
    mods = _ada(cond, w_ada, b_ada.reshape(depth, 1, 6 * d))

    h, hc = x, ctx
    for l in range(depth):
        last = l == depth - 1
        lam_init = 0.8 - 0.6 * math.exp(-0.3 * l)
        m_lat = mods[l, :b].reshape(b, 1, 6, d)
        m_ctx = jnp.broadcast_to(mods[l, b].reshape(1, 1, 6, d), (b, 1, 6, d))
        sh1, sc1, gt1, sh2, sc2, gt2 = (m_lat[:, :, i] for i in range(6))
        csh1, csc1, cgt1, csh2, csc2, cgt2 = (m_ctx[:, :, i] for i in range(6))

        g_mix = g_norm_mix[l].reshape(1, d)
        g_ffn = g_norm_ffn[l].reshape(1, d)
        kv_zero = jnp.zeros((b, DA_HEADS, n_lat + n_ctx, HEAD_PAIR), BF16)
        p, q_hm, k_all, v_all = _norm_proj(h, g_mix, sh1, sc1, w_in, l, IN_COLS, tm_lat, rope_tables=(cos, sin),
                                           kv_all=(kv_zero, kv_zero))
        pc, *qc_hm, k_all, v_all = _norm_proj(hc, g_mix, csh1, csc1, w_in, l, KV_COLS if last else IN_COLS, tm_ctx,
                                              kv_all=(k_all, v_all), key_off=n_lat)

        g_sub = da_subln_g[l].reshape(1, 2 * DA_HEAD_DIM)
        y_a = _diff_attn(q_hm, k_all, v_all, da_lambda[l], g_sub, lam_init)

        y_c = _na_attn(p, pc, na_bias, na_mask, l)

        wb = w_branch[l].astype(BF16)
        wo = w_out[l].astype(BF16)
        sgw = sg_w[l].astype(BF16)
        sgb = jnp.broadcast_to(sg_b[l][:, :, None], (SG_GROUPS, SG_CHUNK, SG_CHUNK))
        lng = sg_ln_g[l].reshape(1, SG_WIDTH)
        lnb = sg_ln_b[l].reshape(1, SG_WIDTH)
        t_lat = b * n_lat
        t_tok = t_lat if last else t_lat + b * n_ctx
        xn2_buf = None if last else jnp.zeros((t_tok * SUBLANES, LANES), F32)
        h, xn2 = _merge(h, y_a, p, y_c, wb, wo, sgw, sgb, lng, lnb, gt1, g_ffn, sh2, sc2, xn2_buf, t_tok, 0, tm_mrg)
        if not last:
            ya_c = _diff_attn(qc_hm[0], k_all[:, :, n_lat:], v_all[:, :, n_lat:], da_lambda[l], g_sub, lam_init)
            yc_c = _ctx_mha(pc)
            hc, xn2 = _merge(hc, ya_c, pc, yc_c, wb, wo, sgw, sgb, lng, lnb, cgt1, g_ffn, csh2, csc2,
                             xn2, t_tok, t_lat, tm_ctx)

        w_rt = jnp.zeros((d, LANES), F32).at[:, :N_GROUPS].set(moe_w_group[l]) \
            .at[:, N_GROUPS:N_GROUPS + N_EXPERTS].set(moe_w_router[l])
        b_rt = jnp.zeros((1, LANES), F32).at[0, :N_GROUPS].set(moe_b_group[l]) \
            .at[0, N_GROUPS:N_GROUPS + N_EXPERTS].set(moe_b_router[l])
        route, dest, ys = _moe(xn2, w_rt, b_rt, l, moe_w_gate, moe_w_up, moe_w_down)
        h = _residual(dest, h, ys, route, gt2, g_final.reshape(1, d), 0, last)
        if not last:
            hc = _residual(dest, hc, ys, route, cgt2, g_final.reshape(1, d), t_lat, False)
    return h
```

```python
import functools
import math

import numpy as np
import jax
import jax.numpy as jnp
from jax import lax
from jax.experimental import pallas as pl
from jax.experimental.pallas import tpu as pltpu

F32 = jnp.float32
BF16 = jnp.bfloat16

D_MODEL = 1024
GRID_W = 64
EPS = 1e-6
NEG_INF = -1e30
ROPE_THETA = 10000.0

DA_HEADS = 4
DA_HEAD_DIM = 64
NA_HEADS = 8
NA_ROWS = 8
NA_COLS = 16
SG_CHUNK = 128
SG_GROUPS = 4
SG_WIDTH = 512
BRANCH_WIDTH = 512
N_BRANCH = 3

OFF_KA = 0
OFF_VA = 512
OFF_KC = 1024
OFF_VC = 1536
KV_COLS = 2048
OFF_QA = 2048
OFF_QC = 2560
OFF_ZB = 3072
OFF_GATE = 4096
IN_COLS = 7168

N_GROUPS = 4
EXPERTS_PER_GROUP = 8
N_EXPERTS = 32

LANES = 128
HEAD_PAIR = LANES
VMEM_LIMIT = 56 * 1024 * 1024

MOE_ROWS = 256
DMA_CHUNK = 512

_CONTRACT_LAST = (((1,), (1,)), ((), ()))


def _params(sem):
    return pltpu.CompilerParams(dimension_semantics=sem, vmem_limit_bytes=VMEM_LIMIT)


DOT_ROWS = 256


def _dot_rows(a, b, contract_last=False):
    dims = _CONTRACT_LAST if contract_last else (((1,), (0,)), ((), ()))
    n = a.shape[0]
    if n <= DOT_ROWS:
        return lax.dot_general(a, b, dims, preferred_element_type=F32)
    return jnp.concatenate([lax.dot_general(a[r0:r0 + DOT_ROWS], b, dims, preferred_element_type=F32)
                            for r0 in range(0, n, DOT_ROWS)], axis=0)


def _ada_kernel(cond_ref, w_ref, b_ref, o_ref):
    c = cond_ref[...]
    c = c * jax.nn.sigmoid(c)
    o_ref[0] = jnp.dot(c, w_ref[0], preferred_element_type=F32, precision=lax.Precision.HIGHEST) + b_ref[0]


def _ada(cond, w_ada, b_ada):
    n_layers, d, d6 = w_ada.shape
    tn = 1024
    return pl.pallas_call(
        _ada_kernel,
        grid=(n_layers, d6 // tn),
        in_specs=[
            pl.BlockSpec((8, d), lambda l, j: (0, 0)),
            pl.BlockSpec((1, d, tn), lambda l, j: (l, 0, j)),
            pl.BlockSpec((1, 1, tn), lambda l, j: (l, 0, j)),
        ],
        out_specs=pl.BlockSpec((1, 8, tn), lambda l, j: (l, 0, j)),
        out_shape=jax.ShapeDtypeStruct((n_layers, 8, d6), F32),
        compiler_params=_params(("parallel", "parallel")),
        name="ada_mod",
    )(cond, w_ada, b_ada)


def _rms_mod(x, g, shift, scale):
    y = x * lax.rsqrt(jnp.mean(x * x, axis=-1, keepdims=True) + EPS)
    return (y * g) * (1.0 + scale) + shift


PROJ_TN = 1024
KV_TILE = OFF_KA // PROJ_TN
Q_TILE = OFF_QA // PROJ_TN


def _norm_proj_kernel(*refs, rope, with_q, aliased):
    n_in = 5 + (2 if rope else 0) + (2 if aliased else 0)
    h_ref, g_ref, sh_ref, sc_ref, w_ref = refs[:5]
    outs = refs[n_in:-1]
    o_ref, k_ref, v_ref = outs[0], outs[-2], outs[-1]
    xn_ref = refs[-1]
    j = pl.program_id(2)

    @pl.when(j == 0)
    def _():
        xn_ref[...] = _rms_mod(h_ref[0], g_ref[...], sh_ref[0], sc_ref[0]).astype(BF16)

    res = _dot_rows(xn_ref[...], w_ref[0].astype(BF16))
    o_ref[0] = res.astype(o_ref.dtype)

    if rope:
        cos, sin = refs[5][...], refs[6][...]
        seg = DA_HEAD_DIM // 4
        first = (lax.broadcasted_iota(jnp.int32, cos.shape, 1) % (2 * seg)) < seg

        def rotate(x):
            partner = jnp.where(first, pltpu.roll(x, LANES - seg, 1), pltpu.roll(x, seg, 1))
            return x * cos + partner * sin
    else:
        def rotate(x):
            return x

    @pl.when(j == KV_TILE)
    def _():
        for hd in range(DA_HEADS):
            k_ref[0, hd] = rotate(res[:, hd * HEAD_PAIR:(hd + 1) * HEAD_PAIR]).astype(BF16)
            v_ref[0, hd] = res[:, OFF_VA + hd * HEAD_PAIR:OFF_VA + (hd + 1) * HEAD_PAIR].astype(BF16)

    if with_q:
        @pl.when(j == Q_TILE)
        def _():
            for hd in range(DA_HEADS):
                outs[1][0, hd] = (rotate(res[:, hd * HEAD_PAIR:(hd + 1) * HEAD_PAIR]) * DA_Q_SCALE).astype(BF16)


def _norm_proj(h, g, shift, scale, w, layer, n_cols, tm, rope_tables=None, kv_all=None, n_keys=None, key_off=0):
    b, n, d = h.shape
    rope = rope_tables is not None
    aliased = kv_all is not None
    with_q = n_cols > OFF_QA
    off_blk = key_off // tm
    args = [h, g, shift, scale, w]
    in_specs = [
        pl.BlockSpec((1, tm, d), lambda bi, i, j: (bi, i, 0)),
        pl.BlockSpec((1, d), lambda bi, i, j: (0, 0)),
        pl.BlockSpec((1, 1, d), lambda bi, i, j: (bi, 0, 0)),
        pl.BlockSpec((1, 1, d), lambda bi, i, j: (bi, 0, 0)),
        pl.BlockSpec((1, d, PROJ_TN), lambda bi, i, j: (layer, 0, j)),
    ]
    if rope:
        args += list(rope_tables)
        in_specs += [pl.BlockSpec((tm, LANES), lambda bi, i, j: (i, 0))] * 2
    aliases = {}
    if aliased:
        n_keys = kv_all[0].shape[2]
        aliases = {len(args): 1 + with_q, len(args) + 1: 2 + with_q}
        args += list(kv_all)
        in_specs += [pl.BlockSpec(memory_space=pl.ANY)] * 2
    hm = lambda rows: jax.ShapeDtypeStruct((b, DA_HEADS, rows, HEAD_PAIR), BF16)
    q_spec = pl.BlockSpec((1, DA_HEADS, tm, HEAD_PAIR), lambda bi, i, j: (bi, 0, i, 0))
    kv_spec = pl.BlockSpec((1, DA_HEADS, tm, HEAD_PAIR), lambda bi, i, j: (bi, 0, off_blk + i, 0))
    out_specs = [pl.BlockSpec((1, tm, PROJ_TN), lambda bi, i, j: (bi, i, j))] + [q_spec] * with_q + [kv_spec] * 2
    out_shape = [jax.ShapeDtypeStruct((b, n, n_cols), BF16)] + [hm(n)] * with_q + [hm(n_keys)] * 2
    return pl.pallas_call(
        functools.partial(_norm_proj_kernel, rope=rope, with_q=with_q, aliased=aliased),
        grid=(b, n // tm, n_cols // PROJ_TN),
        in_specs=in_specs,
        out_specs=out_specs,
        out_shape=out_shape,
        input_output_aliases=aliases,
        scratch_shapes=[pltpu.VMEM((tm, d), BF16)],
        compiler_params=_params(("parallel", "parallel", "arbitrary")),
        name="norm_proj",
    )(*args)


def _rope_tables(n_tok):
    t = jnp.arange(n_tok, dtype=jnp.int32)
    row = (t // GRID_W).astype(F32)
    col = (t % GRID_W).astype(F32)
    half = DA_HEAD_DIM // 4
    inv = ROPE_THETA ** (-jnp.arange(half, dtype=F32) / half)
    ar = row[:, None] * inv
    ac = col[:, None] * inv
    ang = jnp.concatenate([ar, ar, ac, ac], axis=-1)
    sign = np.tile(np.concatenate([-np.ones(half), np.ones(half)]), 2).astype(np.float32)
    cos = jnp.cos(ang)
    sin = jnp.sin(ang) * sign
    return jnp.concatenate([cos, cos], axis=-1), jnp.concatenate([sin, sin], axis=-1)


DA_Q_SCALE = DA_HEAD_DIM ** -0.5 * math.log2(math.e)


def _split_pair(q):
    lo = lax.broadcasted_iota(jnp.int32, q.shape, 1) < (HEAD_PAIR // 2)
    zero = jnp.zeros_like(q)
    return jnp.concatenate([jnp.where(lo, q, zero), jnp.where(lo, zero, q)], axis=0)


def _diff_attn_kernel(lam_ref, g_ref, q_ref, k_ref, v_ref, o_ref, qs_ref, m_ref, l_ref, acc_ref,
                      *, tq, tk, lam_init):
    n_blk = k_ref.shape[2] // tk
    n_lt = tk // LANES
    qs_ref[...] = _split_pair(q_ref[0, 0])
    chunks = [slice(r0, r0 + DOT_ROWS) for r0 in range(0, 2 * tq, DOT_ROWS)]

    m_ref[...] = jnp.full(m_ref.shape, NEG_INF, F32)
    l_ref[...] = jnp.zeros(l_ref.shape, F32)
    acc_ref[...] = jnp.zeros(acc_ref.shape, F32)

    def key_block(j):
        keys = slice(0, tk) if n_blk == 1 else pl.ds(pl.multiple_of(j * tk, tk), tk)
        k = k_ref[0, 0, keys, :]
        v = v_ref[0, 0, keys, :]
        s = jnp.concatenate([lax.dot_general(qs_ref[c], k, _CONTRACT_LAST, preferred_element_type=F32)
                             for c in chunks], axis=0)
        tiles = [s[:, t * LANES:(t + 1) * LANES] for t in range(n_lt)]
        m_prev = m_ref[...]
        m_new = jnp.maximum(m_prev, jnp.max(functools.reduce(jnp.maximum, tiles), axis=-1, keepdims=True))
        alpha = jnp.exp2(m_prev - m_new)
        p_tiles = [jnp.exp2(t - m_new) for t in tiles]
        l_ref[...] = alpha * l_ref[...] + functools.reduce(jnp.add, p_tiles)
        p = jnp.concatenate(p_tiles, axis=1).astype(BF16)
        pv = jnp.concatenate([jnp.dot(p[c], v, preferred_element_type=F32) for c in chunks], axis=0)
        acc_ref[...] = alpha * acc_ref[...] + pv
        m_ref[...] = m_new

    if n_blk == 1:
        key_block(0)
    else:
        def body(j, carry):
            key_block(j)
            return carry
        lax.fori_loop(0, n_blk, body, 0)

    o = acc_ref[...] / jnp.sum(l_ref[...], axis=-1, keepdims=True)
    lm = lam_ref[...]
    lam = (jnp.exp(jnp.sum(lm[0:1] * lm[1:2], axis=-1, keepdims=True))
           - jnp.exp(jnp.sum(lm[2:3] * lm[3:4], axis=-1, keepdims=True)) + lam_init)
    d = o[:tq] - lam * o[tq:]
    y = d * lax.rsqrt(jnp.mean(d * d, axis=-1, keepdims=True) + EPS)
    o_ref[0] = ((y * g_ref[...]) * (1.0 - lam_init)).astype(o_ref.dtype)


def _pick(n, options):
    for o in options:
        if n % o == 0:
            return o
    raise ValueError(f"no tile in {options} divides {n}")


def _diff_attn(q, k, v, lam_params, g, lam_init):
    b, nh, nq, _ = q.shape
    nk = k.shape[2]
    tq = _pick(nq, (512, 256))
    tk = _pick(nk, (2816, 1408, 768, 512, 256))
    kern = functools.partial(_diff_attn_kernel, tq=tq, tk=tk, lam_init=lam_init)
    return pl.pallas_call(
        kern,
        grid=(b, nh, nq // tq),
        in_specs=[
            pl.BlockSpec((4, DA_HEAD_DIM), lambda bi, h, i: (0, 0)),
            pl.BlockSpec((1, HEAD_PAIR), lambda bi, h, i: (0, 0)),
            pl.BlockSpec((1, 1, tq, HEAD_PAIR), lambda bi, h, i: (bi, h, i, 0)),
            pl.BlockSpec((1, 1, nk, HEAD_PAIR), lambda bi, h, i: (bi, h, 0, 0)),
            pl.BlockSpec((1, 1, nk, HEAD_PAIR), lambda bi, h, i: (bi, h, 0, 0)),
        ],
        out_specs=pl.BlockSpec((1, tq, HEAD_PAIR), lambda bi, h, i: (bi, i, h)),
        out_shape=jax.ShapeDtypeStruct((b, nq, nh * HEAD_PAIR), BF16),
        scratch_shapes=[
            pltpu.VMEM((2 * tq, HEAD_PAIR), BF16),
            pltpu.VMEM((2 * tq, LANES), F32),
            pltpu.VMEM((2 * tq, LANES), F32),
            pltpu.VMEM((2 * tq, HEAD_PAIR), F32),
        ],
        compiler_params=_params(("parallel", "parallel", "parallel")),
        name="diff_attn",
    )(lam_params, g, q, k, v)


NA_QROWS = 8
NA_QTOK = NA_QROWS * GRID_W
NA_KBLK = 4 * GRID_W
NA_PIECES = 4
NA_WIN = NA_PIECES * NA_KBLK


def _na_bias_tables(rpb, rows):
    n_h = rpb.shape[0]
    n_kj = NA_PIECES * 4
    pad_r = n_kj - NA_ROWS
    pad_c = GRID_W - NA_COLS
    rp = jnp.pad(rpb * math.log2(math.e), ((0, 0), (pad_r, pad_r), (pad_c, pad_c)))
    a = jnp.stack([rp[:, :, GRID_W - 1 - qc:2 * GRID_W - 1 - qc] for qc in range(GRID_W)], axis=2)
    bias = jnp.stack([jnp.concatenate([a[:, kj - qi + 3 + pad_r] for kj in range(n_kj)], axis=-1)
                      for qi in range(NA_QROWS)], axis=1).reshape(n_h, NA_QTOK, NA_WIN)

    n_r = rows // NA_QROWS
    qi = np.arange(NA_QROWS)
    kj = np.arange(n_kj)
    c = np.arange(GRID_W)
    cstart = np.clip(c - NA_COLS // 2, 0, GRID_W - NA_COLS)
    col_ok = (c[None, :] >= cstart[:, None]) & (c[None, :] < cstart[:, None] + NA_COLS)
    row_ok = []
    for r_grp in (0, min(1, n_r - 1), n_r - 1):
        r = NA_QROWS * r_grp + qi
        rs = np.clip(r - NA_ROWS // 2, 0, rows - NA_ROWS)
        krow = NA_QROWS * r_grp - 4 + kj
        row_ok.append((krow[None, :] >= rs[:, None]) & (krow[None, :] < rs[:, None] + NA_ROWS)
                      & (krow[None, :] >= 0) & (krow[None, :] < rows))
    ok = jnp.asarray(np.stack(row_ok))[:, :, None, :, None] & jnp.asarray(col_ok)[None, None, :, None, :]
    mask = jnp.where(ok, 0.0, NEG_INF).astype(F32).reshape(3, NA_QTOK, NA_WIN)
    return bias, mask


def _pair_softmax_pv(qs, k_list, v_list, bias_list):
    n_rows = qs.shape[0]
    chunks = [slice(r0, r0 + DOT_ROWS) for r0 in range(0, n_rows, DOT_ROWS)]
    tiles_list = []
    for k, bias in zip(k_list, bias_list):
        s = jnp.concatenate([lax.dot_general(qs[c], k, _CONTRACT_LAST, preferred_element_type=F32)
                             for c in chunks], axis=0)
        tiles = [s[:, j * LANES:(j + 1) * LANES] for j in range(k.shape[0] // LANES)]
        tiles_list.append(tiles if bias is None else [t + bias(j) for j, t in enumerate(tiles)])
    all_tiles = [t for tiles in tiles_list for t in tiles]
    m = jnp.max(functools.reduce(jnp.maximum, all_tiles), axis=-1, keepdims=True)
    p_list = [[jnp.exp2(t - m) for t in tiles] for tiles in tiles_list]
    l = jnp.sum(functools.reduce(jnp.add, [t for tiles in p_list for t in tiles]), axis=-1, keepdims=True)
    o = None
    for tiles, v in zip(p_list, v_list):
        p = jnp.concatenate(tiles, axis=1).astype(BF16)
        pv = jnp.concatenate([jnp.dot(p[c], v, preferred_element_type=F32) for c in chunks], axis=0)
        o = pv if o is None else o + pv
    o = o / l
    t = qs.shape[0] // 2
    lo = lax.broadcasted_iota(jnp.int32, (t, HEAD_PAIR), 1) < (HEAD_PAIR // 2)
    return jnp.where(lo, o[:t], o[t:])


def _scaled_pair(q, head_dim):
    return _split_pair((q.astype(F32) * (head_dim ** -0.5 * math.log2(math.e))).astype(BF16))


def _na_kernel(q_ref, k0, k1, k2, k3, v0, v1, v2, v3, kc_ref, vc_ref, b_ref, mk_ref, o_ref):
    qs = _scaled_pair(q_ref[0], HEAD_PAIR // 2)
    k_win = jnp.concatenate([k0[0], k1[0], k2[0], k3[0]], axis=0)
    v_win = jnp.concatenate([v0[0], v1[0], v2[0], v3[0]], axis=0)
    def bias(j):
        lanes = slice(j * LANES, (j + 1) * LANES)
        window = mk_ref[0, :, lanes]
        return jnp.concatenate([b_ref[0, :, lanes] + window, b_ref[1, :, lanes] + window], axis=0)

    o = _pair_softmax_pv(qs, [k_win, kc_ref[0]], [v_win, vc_ref[0]], [bias, None])
    o_ref[0] = o.astype(o_ref.dtype)


def _na_attn(p, pc, bias, mask):
    b, n, _ = p.shape
    n_ctx = pc.shape[1]
    n_r = n // NA_QTOK
    n_kb = n // NA_KBLK
    n_hp = NA_HEADS // 2

    def case(r):
        return jnp.where(r == 0, 0, jnp.where(r == n_r - 1, 2, 1))

    def kv_spec(off, piece):
        return pl.BlockSpec(
            (1, NA_KBLK, HEAD_PAIR),
            lambda bi, hp, r: (bi, jnp.clip(2 * r - 1 + piece, 0, n_kb - 1), off // HEAD_PAIR + hp))

    in_specs = [pl.BlockSpec((1, NA_QTOK, HEAD_PAIR), lambda bi, hp, r: (bi, r, OFF_QC // HEAD_PAIR + hp))]
    in_specs += [kv_spec(OFF_KC, i) for i in range(NA_PIECES)]
    in_specs += [kv_spec(OFF_VC, i) for i in range(NA_PIECES)]
    in_specs += [
        pl.BlockSpec((1, n_ctx, HEAD_PAIR), lambda bi, hp, r: (bi, 0, OFF_KC // HEAD_PAIR + hp)),
        pl.BlockSpec((1, n_ctx, HEAD_PAIR), lambda bi, hp, r: (bi, 0, OFF_VC // HEAD_PAIR + hp)),
        pl.BlockSpec((2, NA_QTOK, NA_WIN), lambda bi, hp, r: (hp, 0, 0)),
        pl.BlockSpec((1, NA_QTOK, NA_WIN), lambda bi, hp, r: (case(r), 0, 0)),
    ]
    return pl.pallas_call(
        _na_kernel,
        grid=(b, n_hp, n_r),
        in_specs=in_specs,
        out_specs=pl.BlockSpec((1, NA_QTOK, HEAD_PAIR), lambda bi, hp, r: (bi, r, hp)),
        out_shape=jax.ShapeDtypeStruct((b, n, NA_HEADS * HEAD_PAIR // 2), BF16),
        compiler_params=_params(("parallel", "parallel", "arbitrary")),
        name="na_attn",
    )(p, *([p] * (2 * NA_PIECES)), pc, pc, bias, mask)


def _ctx_mha_kernel(q_ref, k_ref, v_ref, o_ref):
    qs = _scaled_pair(q_ref[0], HEAD_PAIR // 2)
    o_ref[0] = _pair_softmax_pv(qs, [k_ref[0]], [v_ref[0]], [None]).astype(o_ref.dtype)


def _ctx_mha(pc):
    b, n_ctx, _ = pc.shape
    n_hp = NA_HEADS // 2

    def spec(off):
        return pl.BlockSpec((1, n_ctx, HEAD_PAIR), lambda bi, hp: (bi, 0, off // HEAD_PAIR + hp))

    return pl.pallas_call(
        _ctx_mha_kernel,
        grid=(b, n_hp),
        in_specs=[spec(OFF_QC), spec(OFF_KC), spec(OFF_VC)],
        out_specs=pl.BlockSpec((1, n_ctx, HEAD_PAIR), lambda bi, hp: (bi, 0, hp)),
        out_shape=jax.ShapeDtypeStruct((b, n_ctx, NA_HEADS * HEAD_PAIR // 2), BF16),
        compiler_params=_params(("parallel", "parallel")),
        name="ctx_mha",
    )(pc, pc, pc)


def _gelu_tanh(x):
    return 0.5 * x * (1.0 + jnp.tanh(math.sqrt(2.0 / math.pi) * (x + 0.044715 * (x * x * x))))


SUBLANES = 8


def _store_token_tiles(ref, x):
    t = x.shape[0]
    for j in range(SUBLANES):
        ref[pl.ds(j, t, stride=SUBLANES), :] = x[:, j * LANES:(j + 1) * LANES]


def _load_token_tiles(ref, t):
    return jnp.concatenate([ref[pl.ds(j, t, stride=SUBLANES), :] for j in range(SUBLANES)], axis=1)


def _merge_kernel(h_ref, ya_ref, z_ref, yc_ref, g0_ref, g1_ref, g2_ref, wb_ref, wo_ref, sgw_ref, sgb_ref,
                  lng_ref, lnb_ref, gt_ref, gf_ref, sh_ref, sc_ref, ho_ref, xo_ref, *, tm):
    z = _gelu_tanh(z_ref[0].astype(F32))
    u = z[:, :SG_WIDTH]
    vv = z[:, SG_WIDTH:]
    mu = jnp.mean(vv, axis=-1, keepdims=True)
    var = jnp.mean(jnp.square(vv - mu), axis=-1, keepdims=True)
    vv = ((vv - mu) * lax.rsqrt(var + EPS)) * lng_ref[...] + lnb_ref[...]
    vv = vv.astype(BF16)
    gd = SG_WIDTH // SG_GROUPS
    chunks = []
    for c in range(tm // SG_CHUNK):
        rows = slice(c * SG_CHUNK, (c + 1) * SG_CHUNK)
        groups = []
        for g in range(SG_GROUPS):
            s = jnp.dot(sgw_ref[g], vv[rows, g * gd:(g + 1) * gd], preferred_element_type=F32) + sgb_ref[g]
            groups.append(s)
        chunks.append(jnp.concatenate(groups, axis=1))
    y_b = (u * jnp.concatenate(chunks, axis=0)).astype(BF16)

    merged = None
    for y, gate_ref, i in ((ya_ref[0], g0_ref, 0), (y_b, g1_ref, 1), (yc_ref[0], g2_ref, 2)):
        t = jax.nn.sigmoid(gate_ref[0].astype(F32)) * _dot_rows(y, wb_ref[i])
        merged = t if merged is None else merged + t
    out = _dot_rows(merged.astype(BF16), wo_ref[...])
    h_new = h_ref[0] + gt_ref[0] * out
    ho_ref[0] = h_new
    _store_token_tiles(xo_ref, _rms_mod(h_new, gf_ref[...], sh_ref[0], sc_ref[0]))


def _merge(h, y_a, p, y_c, wb, wo, sgw, sgb, lng, lnb, gt1, g_ffn, sh2, sc2, xn2_buf, xn2_rows, row_off, tm):
    b, n, d = h.shape
    n_i = n // tm
    off_blk = row_off // tm
    vec = pl.BlockSpec((1, 1, d), lambda bi, i: (bi, 0, 0))

    def col(width, off):
        return pl.BlockSpec((1, tm, width), lambda bi, i: (bi, i, off // width))

    kern = functools.partial(_merge_kernel, tm=tm)
    args = [h, y_a, p, y_c, p, p, p, wb, wo, sgw, sgb, lng, lnb, gt1, g_ffn, sh2, sc2]
    in_specs = [
        pl.BlockSpec((1, tm, d), lambda bi, i: (bi, i, 0)),
        col(BRANCH_WIDTH, 0),
        col(2 * SG_WIDTH, OFF_ZB),
        col(BRANCH_WIDTH, 0),
        col(d, OFF_GATE), col(d, OFF_GATE + d), col(d, OFF_GATE + 2 * d),
        pl.BlockSpec((N_BRANCH, BRANCH_WIDTH, d), lambda bi, i: (0, 0, 0)),
        pl.BlockSpec((d, d), lambda bi, i: (0, 0)),
        pl.BlockSpec((SG_GROUPS, SG_CHUNK, SG_CHUNK), lambda bi, i: (0, 0, 0)),
        pl.BlockSpec((SG_GROUPS, SG_CHUNK, SG_CHUNK), lambda bi, i: (0, 0, 0)),
        pl.BlockSpec((1, SG_WIDTH), lambda bi, i: (0, 0)),
        pl.BlockSpec((1, SG_WIDTH), lambda bi, i: (0, 0)),
        vec,
        pl.BlockSpec((1, d), lambda bi, i: (0, 0)),
        vec, vec,
    ]
    aliases = {}
    if xn2_buf is not None:
        args.append(xn2_buf)
        in_specs.append(pl.BlockSpec(memory_space=pl.ANY))
        aliases = {len(args) - 1: 1}
    assert d == SUBLANES * LANES
    xn2_shape = jax.ShapeDtypeStruct((xn2_rows * SUBLANES, LANES), F32)

    def body(*refs):
        n_in = 17
        kern(*refs[:n_in], *refs[len(refs) - 2:])

    return pl.pallas_call(
        body,
        grid=(b, n_i),
        in_specs=in_specs,
        out_specs=[
            pl.BlockSpec((1, tm, d), lambda bi, i: (bi, i, 0)),
            pl.BlockSpec((tm * SUBLANES, LANES), lambda bi, i: (off_blk + bi * n_i + i, 0)),
        ],
        out_shape=[jax.ShapeDtypeStruct((b, n, d), F32), xn2_shape],
        input_output_aliases=aliases,
        compiler_params=_params(("parallel", "parallel")),
        name="merge_branches",
    )(*args)


R_E1, R_E2, R_W1, R_W2, R_RANK1, R_RANK2 = range(6)


def _router_kernel(x_ref, w_ref, b_ref, o_ref, cnt_ref, run_ref, *, tm):
    @pl.when(pl.program_id(0) == 0)
    def _():
        run_ref[...] = jnp.zeros(run_ref.shape, F32)

    x = _load_token_tiles(x_ref, tm)
    x_hi = x.astype(BF16)
    x_lo = (x - x_hi.astype(F32)).astype(BF16)
    w_hi, w_lo = w_ref[0], w_ref[1]
    logits = (jnp.dot(x_hi, w_hi, preferred_element_type=F32) + jnp.dot(x_lo, w_hi, preferred_element_type=F32)
              + jnp.dot(x_hi, w_lo, preferred_element_type=F32)) + b_ref[...]
    lane = lax.broadcasted_iota(jnp.int32, logits.shape, 1)
    lane_f = lane.astype(F32)
    far = jnp.float32(1e9)

    def first_lane(mask):
        return jnp.min(jnp.where(mask, lane_f, far), axis=-1, keepdims=True)

    is_g = lane < N_GROUPS
    gl = jnp.where(is_g, logits, NEG_INF)
    g_max = jnp.max(gl, axis=-1, keepdims=True)
    g_idx = first_lane(is_g & (gl == g_max))
    g_sum = jnp.sum(jnp.where(is_g, jnp.exp(gl - g_max), 0.0), axis=-1, keepdims=True)
    g_w = 1.0 / g_sum

    e_lane = lane - N_GROUPS
    in_grp = (e_lane >= 0) & (e_lane < N_EXPERTS) & \
        ((e_lane // EXPERTS_PER_GROUP).astype(F32) == g_idx)
    el = jnp.where(in_grp, logits, NEG_INF)
    l1 = jnp.max(el, axis=-1, keepdims=True)
    i1 = first_lane(in_grp & (el == l1))
    rest = in_grp & (lane_f != i1)
    el2 = jnp.where(rest, logits, NEG_INF)
    l2 = jnp.max(el2, axis=-1, keepdims=True)
    i2 = first_lane(rest & (el2 == l2))
    t = jnp.exp(l2 - l1)
    w1 = g_w / (1.0 + t)
    w2 = g_w * t / (1.0 + t)

    oh1 = lane_f == i1
    oh2 = lane_f == i2
    oh = jnp.where(oh1 | oh2, 1.0, 0.0)
    row = lax.broadcasted_iota(jnp.int32, (tm, tm), 0)
    colm = lax.broadcasted_iota(jnp.int32, (tm, tm), 1)
    before = jnp.where(colm < row, 1.0, 0.0).astype(BF16)
    prior = jnp.dot(before, oh.astype(BF16), preferred_element_type=F32) + run_ref[...]
    rank1 = jnp.sum(jnp.where(oh1, prior, 0.0), axis=-1, keepdims=True)
    rank2 = jnp.sum(jnp.where(oh2, prior, 0.0), axis=-1, keepdims=True)
    run_new = run_ref[...] + jnp.sum(oh, axis=0, keepdims=True)
    run_ref[...] = run_new
    cnt_ref[...] = run_new

    slab = jnp.zeros(logits.shape, F32)
    for ln, val in ((R_E1, i1 - N_GROUPS), (R_E2, i2 - N_GROUPS), (R_W1, w1), (R_W2, w2),
                    (R_RANK1, rank1), (R_RANK2, rank2)):
        slab = jnp.where(lane == ln, val, slab)
    o_ref[...] = slab


def _router(xn2, w_rt, b_rt):
    t_tok = xn2.shape[0] // SUBLANES
    d = w_rt.shape[0]
    tm = 512
    w_hi = w_rt.astype(BF16)
    w_split = jnp.stack([w_hi, (w_rt - w_hi.astype(F32)).astype(BF16)])
    kern = functools.partial(_router_kernel, tm=tm)
    return pl.pallas_call(
        kern,
        grid=(t_tok // tm,),
        in_specs=[
            pl.BlockSpec((tm * SUBLANES, LANES), lambda i: (i, 0)),
            pl.BlockSpec((2, d, LANES), lambda i: (0, 0, 0)),
            pl.BlockSpec((1, LANES), lambda i: (0, 0)),
        ],
        out_specs=[pl.BlockSpec((tm, LANES), lambda i: (i, 0)), pl.BlockSpec((1, LANES), lambda i: (0, 0))],
        out_shape=[jax.ShapeDtypeStruct((t_tok, LANES), F32), jax.ShapeDtypeStruct((1, LANES), F32)],
        scratch_shapes=[pltpu.VMEM((1, LANES), F32)],
        compiler_params=_params(("arbitrary",)),
        name="moe_router",
    )(xn2, w_split, b_rt)


def _tile_rows(index):
    return pl.ds(pl.multiple_of(index * SUBLANES, SUBLANES), SUBLANES)


def _dispatch_kernel(dest_ref, start_ref, end_ref, x_ref, xs_ref, zero_ref, sem, zsem, *, n_blocks):
    step = pl.program_id(0)
    blk_rows = MOE_ROWS * SUBLANES

    @pl.when(step == 0)
    def _():
        zero_ref[...] = jnp.zeros(zero_ref.shape, F32)

        def fill(blk):
            return pltpu.make_async_copy(zero_ref, xs_ref.at[pl.ds(pl.multiple_of(blk * blk_rows, blk_rows),
                                                                  blk_rows), :], zsem.at[0])

        def expert_tail(op):
            def body(e, carry):
                @pl.when(end_ref[e] > start_ref[e])
                def _():
                    op(fill(end_ref[e] // MOE_ROWS - 1))
                return carry
            lax.fori_loop(0, N_EXPERTS, body, 0)

        def unused(op):
            def body(blk, carry):
                @pl.when(blk * MOE_ROWS >= end_ref[N_EXPERTS - 1])
                def _():
                    op(fill(blk))
                return carry
            lax.fori_loop(0, n_blocks, body, 0)

        for phase in (lambda c: c.start(), lambda c: c.wait()):
            expert_tail(phase)
            unused(phase)

    def issue(i, carry):
        tok = step * DMA_CHUNK + i
        for k in range(2):
            pltpu.make_async_copy(x_ref.at[_tile_rows(i), :], xs_ref.at[_tile_rows(dest_ref[2 * tok + k]), :],
                                  sem.at[0]).start(priority=k)
        return carry

    lax.fori_loop(0, DMA_CHUNK, issue, 0, unroll=4)
    for _ in range(2):
        pltpu.make_async_copy(x_ref, xs_ref.at[pl.ds(0, DMA_CHUNK * SUBLANES), :], sem.at[0]).wait()


def _dispatch(dest, pad_start, pad_end, xn2, n_slots):
    t_tok = xn2.shape[0] // SUBLANES
    kern = functools.partial(_dispatch_kernel, n_blocks=n_slots // MOE_ROWS)
    grid_spec = pltpu.PrefetchScalarGridSpec(
        num_scalar_prefetch=3,
        grid=(t_tok // DMA_CHUNK,),
        in_specs=[pl.BlockSpec((DMA_CHUNK * SUBLANES, LANES), lambda i, *_: (i, 0))],
        out_specs=pl.BlockSpec(memory_space=pl.ANY),
        scratch_shapes=[pltpu.VMEM((MOE_ROWS * SUBLANES, LANES), F32), pltpu.SemaphoreType.DMA((1,)),
                        pltpu.SemaphoreType.DMA((1,))],
    )
    return pl.pallas_call(
        kern,
        grid_spec=grid_spec,
        out_shape=jax.ShapeDtypeStruct((n_slots * SUBLANES, LANES), F32),
        compiler_params=_params(("arbitrary",)),
        name="moe_dispatch",
    )(dest, pad_start, pad_end, xn2)


def _expert_kernel(be_ref, nxt_ref, nu_ref, x_ref, wg_hbm, wu_hbm, wd_hbm, o_ref,
                   wg_f, wu_f, wd_f, wg_s, wu_s, wd_s, sem, turn_ref, *, w_off):
    i = pl.program_id(0)
    used = i < nu_ref[0]

    def fetch(expert, slot):
        return [pltpu.make_async_copy(src.at[w_off + expert], dst.at[slot], sem.at[slot])
                for src, dst in ((wg_hbm, wg_f), (wu_hbm, wu_f), (wd_hbm, wd_f))]

    @pl.when(i == 0)
    def _():
        turn_ref[0] = 0
        for c in fetch(be_ref[0], 0):
            c.start()

    @pl.when(used & ((i == 0) | (be_ref[i] != be_ref[jnp.maximum(i - 1, 0)])))
    def _():
        slot = turn_ref[0] % 2
        for c in fetch(be_ref[i], slot):
            c.wait()
        nxt = nxt_ref[be_ref[i]]

        @pl.when(nxt >= 0)
        def _():
            for c in fetch(nxt, 1 - slot):
                c.start()

        wg_s[...] = wg_f[slot].astype(BF16)
        wu_s[...] = wu_f[slot].astype(BF16)
        wd_s[...] = wd_f[slot].astype(BF16)
        turn_ref[0] = turn_ref[0] + 1

    @pl.when(used)
    def _():
        x = _load_token_tiles(x_ref, MOE_ROWS).astype(BF16)
        gate = jnp.dot(x, wg_s[...], preferred_element_type=F32)
        up = jnp.dot(x, wu_s[...], preferred_element_type=F32)
        hdn = (gate * jax.nn.sigmoid(gate)) * up
        _store_token_tiles(o_ref, jnp.dot(hdn.astype(BF16), wd_s[...], preferred_element_type=F32))

    @pl.when(i >= nu_ref[0])
    def _():
        o_ref[...] = jnp.zeros(o_ref.shape, F32)


def _experts(block_expert, next_expert, n_used, xs, layer, w_gate, w_up, w_down):
    n_blocks = xs.shape[0] // (MOE_ROWS * SUBLANES)
    n_layers, n_e, d, de = w_gate.shape
    w_gate, w_up, w_down = (w.reshape(n_layers * n_e, *w.shape[2:]) for w in (w_gate, w_up, w_down))
    blk = pl.BlockSpec((MOE_ROWS * SUBLANES, LANES), lambda i, *_: (i, 0))
    hbm = pl.BlockSpec(memory_space=pl.ANY)
    grid_spec = pltpu.PrefetchScalarGridSpec(
        num_scalar_prefetch=3,
        grid=(n_blocks,),
        in_specs=[blk, hbm, hbm, hbm],
        out_specs=blk,
        scratch_shapes=[
            pltpu.VMEM((2, d, de), F32), pltpu.VMEM((2, d, de), F32), pltpu.VMEM((2, de, d), F32),
            pltpu.VMEM((d, de), BF16), pltpu.VMEM((d, de), BF16), pltpu.VMEM((de, d), BF16),
            pltpu.SemaphoreType.DMA((2,)), pltpu.SMEM((1,), jnp.int32),
        ],
    )
    return pl.pallas_call(
        functools.partial(_expert_kernel, w_off=layer * n_e),
        grid_spec=grid_spec,
        out_shape=jax.ShapeDtypeStruct(xs.shape, F32),
        compiler_params=_params(("arbitrary",)),
        name="moe_experts",
    )(block_expert, next_expert, n_used, xs, w_gate, w_up, w_down)


def _residual_kernel(dest_ref, h_ref, r_ref, gt_ref, gf_ref, ys_ref, o_ref, buf_ref, sem,
                     *, tm, n_i, row_off, final):
    step = pl.program_id(0) * n_i + pl.program_id(1)
    n_steps = pl.num_programs(0) * n_i
    slot = step % 2

    def gather(tile, into):
        base = row_off + tile * tm

        def issue(t, carry):
            for k in range(2):
                pltpu.make_async_copy(ys_ref.at[_tile_rows(dest_ref[2 * (base + t) + k]), :],
                                      buf_ref.at[into, k, _tile_rows(t), :], sem.at[into]).start(priority=k)
            return carry

        lax.fori_loop(0, tm, issue, 0, unroll=4)

    @pl.when(step == 0)
    def _():
        gather(step, slot)

    @pl.when(step + 1 < n_steps)
    def _():
        gather(step + 1, 1 - slot)

    for k in range(2):
        pltpu.make_async_copy(ys_ref.at[pl.ds(0, tm * SUBLANES), :], buf_ref.at[slot, k], sem.at[slot]).wait()

    r = r_ref[...]
    y = (_load_token_tiles(buf_ref.at[slot, 0], tm) * r[:, R_W1:R_W1 + 1]
         + _load_token_tiles(buf_ref.at[slot, 1], tm) * r[:, R_W2:R_W2 + 1])
    h_new = h_ref[0] + gt_ref[0] * y
    if final:
        h_new = (h_new * lax.rsqrt(jnp.mean(h_new * h_new, axis=-1, keepdims=True) + EPS)) * gf_ref[...]
    o_ref[0] = h_new


def _residual(dest, h, ys, route, gt2, g_final, row_off, final):
    b, n, d = h.shape
    tm = 256
    n_i = n // tm
    off_blk = row_off // tm
    kern = functools.partial(_residual_kernel, tm=tm, n_i=n_i, row_off=row_off, final=final)
    grid_spec = pltpu.PrefetchScalarGridSpec(
        num_scalar_prefetch=1,
        grid=(b, n_i),
        in_specs=[
            pl.BlockSpec((1, tm, d), lambda bi, i, *_: (bi, i, 0)),
            pl.BlockSpec((tm, LANES), lambda bi, i, *_: (off_blk + bi * n_i + i, 0)),
            pl.BlockSpec((1, 1, d), lambda bi, i, *_: (bi, 0, 0)),
            pl.BlockSpec((1, d), lambda bi, i, *_: (0, 0)),
            pl.BlockSpec(memory_space=pl.ANY),
        ],
        out_specs=pl.BlockSpec((1, tm, d), lambda bi, i, *_: (bi, i, 0)),
        scratch_shapes=[pltpu.VMEM((2, 2, tm * SUBLANES, LANES), F32), pltpu.SemaphoreType.DMA((2,))],
    )
    return pl.pallas_call(
        kern,
        grid_spec=grid_spec,
        out_shape=jax.ShapeDtypeStruct((b, n, d), F32),
        compiler_params=_params(("arbitrary", "arbitrary")),
        name="moe_residual",
    )(dest, h, route, gt2, g_final, ys)


def _moe(xn2, w_rt, b_rt, layer, w_gate, w_up, w_down):
    t_tok = xn2.shape[0] // SUBLANES
    route, counts = _router(xn2, w_rt, b_rt)
    cnt = counts[0, N_GROUPS:N_GROUPS + N_EXPERTS].astype(jnp.int32)
    padded = (cnt + MOE_ROWS - 1) // MOE_ROWS * MOE_ROWS
    pad_end = jnp.cumsum(padded)
    pad_start = pad_end - padded
    pad_start = pad_start.astype(jnp.int32)
    ids = route[:, R_E1:R_RANK2 + 1].astype(jnp.int32)
    expert = ids[:, R_E1:R_E2 + 1].reshape(2 * t_tok)
    rank = ids[:, R_RANK1:R_RANK2 + 1].reshape(2 * t_tok)
    e_ids = jnp.arange(N_EXPERTS, dtype=jnp.int32)
    dest = rank + jnp.sum(jnp.where(expert[:, None] == e_ids[None, :], pad_start[None, :], 0), axis=1)
    n_blocks = -(-(2 * t_tok + N_EXPERTS * (MOE_ROWS - 1)) // MOE_ROWS)
    n_slots = n_blocks * MOE_ROWS
    starts = jnp.arange(n_blocks, dtype=jnp.int32) * MOE_ROWS
    block_expert = jnp.minimum(jnp.sum(starts[:, None] >= pad_end[None, :], axis=1), N_EXPERTS - 1).astype(jnp.int32)
    n_used = (pad_end[-1:] // MOE_ROWS).astype(jnp.int32)

    later = (padded > 0)[None, :] & (e_ids[None, :] > e_ids[:, None])
    next_expert = jnp.min(jnp.where(later, e_ids[None, :], N_EXPERTS), axis=1)
    next_expert = jnp.where(next_expert == N_EXPERTS, -1, next_expert).astype(jnp.int32)
    xs = _dispatch(dest, pad_start, pad_end.astype(jnp.int32), xn2, n_slots)
    return route, dest, _experts(block_expert, next_expert, n_used, xs, layer, w_gate, w_up, w_down)


def kernel(x, c, ctx, c_ctx, w_ada, b_ada, g_norm_mix, g_norm_ffn, w_in, da_lambda, da_subln_g, sg_ln_g, sg_ln_b, sg_w, sg_b, na_rpb, w_branch, w_out, moe_w_group, moe_b_group, moe_w_router, moe_b_router, moe_w_gate, moe_w_up, moe_w_down, g_final):
    b, n_lat, d = x.shape
    n_ctx = ctx.shape[1]
    depth = w_in.shape[0]
    rows = n_lat // GRID_W
    assert d == D_MODEL and n_lat % NA_QTOK == 0 and rows >= 2 * NA_QROWS and n_ctx % 256 == 0 and b <= 7
    tm_lat = 1024 if n_lat % 1024 == 0 else 512
    tm_mrg = 512
    tm_ctx = 256

    cos, sin = _rope_tables(n_lat)
    cond = jnp.zeros((8, d), F32).at[:b].set(c).at[b].set(c_ctx)
    mods = _ada(cond, w_ada, b_ada.reshape(depth, 1, 6 * d))

    h, hc = x, ctx
    for l in range(depth):
        last = l == depth - 1
        lam_init = 0.8 - 0.6 * math.exp(-0.3 * l)
        m_lat = mods[l, :b].reshape(b, 1, 6, d)
        m_ctx = jnp.broadcast_to(mods[l, b].reshape(1, 1, 6, d), (b, 1, 6, d))
        sh1, sc1, gt1, sh2, sc2, gt2 = (m_lat[:, :, i] for i in range(6))
        csh1, csc1, cgt1, csh2, csc2, cgt2 = (m_ctx[:, :, i] for i in range(6))

        g_mix = g_norm_mix[l].reshape(1, d)
        g_ffn = g_norm_ffn[l].reshape(1, d)
        kv_zero = jnp.zeros((b, DA_HEADS, n_lat + n_ctx, HEAD_PAIR), BF16)
        p, q_hm, k_all, v_all = _norm_proj(h, g_mix, sh1, sc1, w_in, l, IN_COLS, tm_lat, rope_tables=(cos, sin),
                                           kv_all=(kv_zero, kv_zero))
        pc, *qc_hm, k_all, v_all = _norm_proj(hc, g_mix, csh1, csc1, w_in, l, KV_COLS if last else IN_COLS, tm_ctx,
                                              kv_all=(k_all, v_all), key_off=n_lat)

        g_sub = da_subln_g[l].reshape(1, 2 * DA_HEAD_DIM)
        y_a = _diff_attn(q_hm, k_all, v_all, da_lambda[l], g_sub, lam_init)

        y_c = _na_attn(p, pc, *_na_bias_tables(na_rpb[l], rows))

        wb = w_branch[l].astype(BF16)
        wo = w_out[l].astype(BF16)
        sgw = sg_w[l].astype(BF16)
        sgb = jnp.broadcast_to(sg_b[l][:, :, None], (SG_GROUPS, SG_CHUNK, SG_CHUNK))
        lng = sg_ln_g[l].reshape(1, SG_WIDTH)
        lnb = sg_ln_b[l].reshape(1, SG_WIDTH)
        t_lat = b * n_lat
        t_tok = t_lat if last else t_lat + b * n_ctx
        xn2_buf = None if last else jnp.zeros((t_tok * SUBLANES, LANES), F32)
        h, xn2 = _merge(h, y_a, p, y_c, wb, wo, sgw, sgb, lng, lnb, gt1, g_ffn, sh2, sc2, xn2_buf, t_tok, 0, tm_mrg)
        if not last:
            ya_c = _diff_attn(qc_hm[0], k_all[:, :, n_lat:], v_all[:, :, n_lat:], da_lambda[l], g_sub, lam_init)
            yc_c = _ctx_mha(pc)
            hc, xn2 = _merge(hc, ya_c, pc, yc_c, wb, wo, sgw, sgb, lng, lnb, cgt1, g_ffn, csh2, csc2,
                             xn2, t_tok, t_lat, tm_ctx)

        w_rt = jnp.zeros((d, LANES), F32).at[:, :N_GROUPS].set(moe_w_group[l]) \
            .at[:, N_GROUPS:N_GROUPS + N_EXPERTS].set(moe_w_router[l])
        b_rt = jnp.zeros((1, LANES), F32).at[0, :N_GROUPS].set(moe_b_group[l]) \
            .at[0, N_GROUPS:N_GROUPS + N_EXPERTS].set(moe_b_router[l])
        route, dest, ys = _moe(xn2, w_rt, b_rt, l, moe_w_gate, moe_w_up, moe_w_down)
        h = _residual(dest, h, ys, route, gt2, g_final.reshape(1, d), 0, last)
        if not last:
            hc = _residual(dest, hc, ys, route, cgt2, g_final.reshape(1, d), t_lat, False)
    return h
```

```python
import functools
import math

import numpy as np
import jax
import jax.numpy as jnp
from jax import lax
from jax.experimental import pallas as pl
from jax.experimental.pallas import tpu as pltpu

F32 = jnp.float32
BF16 = jnp.bfloat16

D_MODEL = 1024
GRID_W = 64
EPS = 1e-6
NEG_INF = -1e30
ROPE_THETA = 10000.0

DA_HEADS = 4
DA_HEAD_DIM = 64
NA_HEADS = 8
NA_ROWS = 8
NA_COLS = 16
SG_CHUNK = 128
SG_GROUPS = 4
SG_WIDTH = 512
BRANCH_WIDTH = 512
N_BRANCH = 3

OFF_KA = 0
OFF_VA = 512
OFF_KC = 1024
OFF_VC = 1536
KV_COLS = 2048
OFF_QA = 2048
OFF_QC = 2560
OFF_ZB = 3072
OFF_GATE = 4096
IN_COLS = 7168

N_GROUPS = 4
EXPERTS_PER_GROUP = 8
N_EXPERTS = 32

LANES = 128
HEAD_PAIR = LANES
VMEM_LIMIT = 56 * 1024 * 1024

MOE_ROWS = 256
DMA_CHUNK = 512

_CONTRACT_LAST = (((1,), (1,)), ((), ()))


def _params(sem):
    return pltpu.CompilerParams(dimension_semantics=sem, vmem_limit_bytes=VMEM_LIMIT)


DOT_ROWS = 256


def _dot_rows(a, b, contract_last=False):
    dims = _CONTRACT_LAST if contract_last else (((1,), (0,)), ((), ()))
    n = a.shape[0]
    if n <= DOT_ROWS:
        return lax.dot_general(a, b, dims, preferred_element_type=F32)
    return jnp.concatenate([lax.dot_general(a[r0:r0 + DOT_ROWS], b, dims, preferred_element_type=F32)
                            for r0 in range(0, n, DOT_ROWS)], axis=0)


def _ada_kernel(cond_ref, w_ref, b_ref, o_ref):
    c = cond_ref[...]
    c = c * jax.nn.sigmoid(c)
    o_ref[0] = jnp.dot(c, w_ref[0], preferred_element_type=F32, precision=lax.Precision.HIGHEST) + b_ref[0]


def _ada(cond, w_ada, b_ada):
    n_layers, d, d6 = w_ada.shape
    tn = 1024
    return pl.pallas_call(
        _ada_kernel,
        grid=(n_layers, d6 // tn),
        in_specs=[
            pl.BlockSpec((8, d), lambda l, j: (0, 0)),
            pl.BlockSpec((1, d, tn), lambda l, j: (l, 0, j)),
            pl.BlockSpec((1, 1, tn), lambda l, j: (l, 0, j)),
        ],
        out_specs=pl.BlockSpec((1, 8, tn), lambda l, j: (l, 0, j)),
        out_shape=jax.ShapeDtypeStruct((n_layers, 8, d6), F32),
        compiler_params=_params(("parallel", "parallel")),
        name="ada_mod",
    )(cond, w_ada, b_ada)


def _rms_mod(x, g, shift, scale):
    y = x * lax.rsqrt(jnp.mean(x * x, axis=-1, keepdims=True) + EPS)
    return (y * g) * (1.0 + scale) + shift


PROJ_TN = 1024
KV_TILE = OFF_KA // PROJ_TN
Q_TILE = OFF_QA // PROJ_TN


def _norm_proj_kernel(*refs, rope, with_q, aliased):
    n_in = 5 + (2 if rope else 0) + (2 if aliased else 0)
    h_ref, g_ref, sh_ref, sc_ref, w_ref = refs[:5]
    outs = refs[n_in:-1]
    o_ref, k_ref, v_ref = outs[0], outs[-2], outs[-1]
    xn_ref = refs[-1]
    j = pl.program_id(2)

    @pl.when(j == 0)
    def _():
        xn_ref[...] = _rms_mod(h_ref[0], g_ref[...], sh_ref[0], sc_ref[0]).astype(BF16)

    res = _dot_rows(xn_ref[...], w_ref[0].astype(BF16))
    o_ref[0] = res.astype(o_ref.dtype)

    if rope:
        cos, sin = refs[5][...], refs[6][...]
        seg = DA_HEAD_DIM // 4
        first = (lax.broadcasted_iota(jnp.int32, cos.shape, 1) % (2 * seg)) < seg

        def rotate(x):
            partner = jnp.where(first, pltpu.roll(x, LANES - seg, 1), pltpu.roll(x, seg, 1))
            return x * cos + partner * sin
    else:
        def rotate(x):
            return x

    @pl.when(j == KV_TILE)
    def _():
        for hd in range(DA_HEADS):
            k_ref[0, hd] = rotate(res[:, hd * HEAD_PAIR:(hd + 1) * HEAD_PAIR]).astype(BF16)
            v_ref[0, hd] = res[:, OFF_VA + hd * HEAD_PAIR:OFF_VA + (hd + 1) * HEAD_PAIR].astype(BF16)

    if with_q:
        @pl.when(j == Q_TILE)
        def _():
            for hd in range(DA_HEADS):
                outs[1][0, hd] = (rotate(res[:, hd * HEAD_PAIR:(hd + 1) * HEAD_PAIR]) * DA_Q_SCALE).astype(BF16)


def _norm_proj(h, g, shift, scale, w, layer, n_cols, tm, rope_tables=None, kv_all=None, n_keys=None, key_off=0):
    b, n, d = h.shape
    rope = rope_tables is not None
    aliased = kv_all is not None
    with_q = n_cols > OFF_QA
    off_blk = key_off // tm
    args = [h, g, shift, scale, w]
    in_specs = [
        pl.BlockSpec((1, tm, d), lambda bi, i, j: (bi, i, 0)),
        pl.BlockSpec((1, d), lambda bi, i, j: (0, 0)),
        pl.BlockSpec((1, 1, d), lambda bi, i, j: (bi, 0, 0)),
        pl.BlockSpec((1, 1, d), lambda bi, i, j: (bi, 0, 0)),
        pl.BlockSpec((1, d, PROJ_TN), lambda bi, i, j: (layer, 0, j)),
    ]
    if rope:
        args += list(rope_tables)
        in_specs += [pl.BlockSpec((tm, LANES), lambda bi, i, j: (i, 0))] * 2
    aliases = {}
    if aliased:
        n_keys = kv_all[0].shape[2]
        aliases = {len(args): 1 + with_q, len(args) + 1: 2 + with_q}
        args += list(kv_all)
        in_specs += [pl.BlockSpec(memory_space=pl.ANY)] * 2
    hm = lambda rows: jax.ShapeDtypeStruct((b, DA_HEADS, rows, HEAD_PAIR), BF16)
    q_spec = pl.BlockSpec((1, DA_HEADS, tm, HEAD_PAIR), lambda bi, i, j: (bi, 0, i, 0))
    kv_spec = pl.BlockSpec((1, DA_HEADS, tm, HEAD_PAIR), lambda bi, i, j: (bi, 0, off_blk + i, 0))
    out_specs = [pl.BlockSpec((1, tm, PROJ_TN), lambda bi, i, j: (bi, i, j))] + [q_spec] * with_q + [kv_spec] * 2
    out_shape = [jax.ShapeDtypeStruct((b, n, n_cols), BF16)] + [hm(n)] * with_q + [hm(n_keys)] * 2
    return pl.pallas_call(
        functools.partial(_norm_proj_kernel, rope=rope, with_q=with_q, aliased=aliased),
        grid=(b, n // tm, n_cols // PROJ_TN),
        in_specs=in_specs,
        out_specs=out_specs,
        out_shape=out_shape,
        input_output_aliases=aliases,
        scratch_shapes=[pltpu.VMEM((tm, d), BF16)],
        compiler_params=_params(("parallel", "parallel", "arbitrary")),
        name="norm_proj",
    )(*args)


def _rope_tables(n_tok):
    t = jnp.arange(n_tok, dtype=jnp.int32)
    row = (t // GRID_W).astype(F32)
    col = (t % GRID_W).astype(F32)
    half = DA_HEAD_DIM // 4
    inv = ROPE_THETA ** (-jnp.arange(half, dtype=F32) / half)
    ar = row[:, None] * inv
    ac = col[:, None] * inv
    ang = jnp.concatenate([ar, ar, ac, ac], axis=-1)
    sign = np.tile(np.concatenate([-np.ones(half), np.ones(half)]), 2).astype(np.float32)
    cos = jnp.cos(ang)
    sin = jnp.sin(ang) * sign
    return jnp.concatenate([cos, cos], axis=-1), jnp.concatenate([sin, sin], axis=-1)


DA_Q_SCALE = DA_HEAD_DIM ** -0.5 * math.log2(math.e)


def _split_pair(q):
    lo = lax.broadcasted_iota(jnp.int32, q.shape, 1) < (HEAD_PAIR // 2)
    zero = jnp.zeros_like(q)
    return jnp.concatenate([jnp.where(lo, q, zero), jnp.where(lo, zero, q)], axis=0)


def _diff_attn_kernel(lam_ref, g_ref, q_ref, k_ref, v_ref, o_ref, qs_ref, m_ref, l_ref, acc_ref,
                      *, tq, tk, lam_init):
    n_blk = k_ref.shape[2] // tk
    n_lt = tk // LANES
    qs_ref[...] = _split_pair(q_ref[0, 0])
    chunks = [slice(r0, r0 + DOT_ROWS) for r0 in range(0, 2 * tq, DOT_ROWS)]

    m_ref[...] = jnp.full(m_ref.shape, NEG_INF, F32)
    l_ref[...] = jnp.zeros(l_ref.shape, F32)
    acc_ref[...] = jnp.zeros(acc_ref.shape, F32)

    def key_block(j):
        keys = slice(0, tk) if n_blk == 1 else pl.ds(pl.multiple_of(j * tk, tk), tk)
        k = k_ref[0, 0, keys, :]
        v = v_ref[0, 0, keys, :]
        s = jnp.concatenate([lax.dot_general(qs_ref[c], k, _CONTRACT_LAST, preferred_element_type=F32)
                             for c in chunks], axis=0)
        tiles = [s[:, t * LANES:(t + 1) * LANES] for t in range(n_lt)]
        m_prev = m_ref[...]
        m_new = jnp.maximum(m_prev, jnp.max(functools.reduce(jnp.maximum, tiles), axis=-1, keepdims=True))
        alpha = jnp.exp2(m_prev - m_new)
        p_tiles = [jnp.exp2(t - m_new) for t in tiles]
        l_ref[...] = alpha * l_ref[...] + functools.reduce(jnp.add, p_tiles)
        p = jnp.concatenate(p_tiles, axis=1).astype(BF16)
        pv = jnp.concatenate([jnp.dot(p[c], v, preferred_element_type=F32) for c in chunks], axis=0)
        acc_ref[...] = alpha * acc_ref[...] + pv
        m_ref[...] = m_new

    if n_blk == 1:
        key_block(0)
    else:
        def body(j, carry):
            key_block(j)
            return carry
        lax.fori_loop(0, n_blk, body, 0)

    o = acc_ref[...] / jnp.sum(l_ref[...], axis=-1, keepdims=True)
    lm = lam_ref[...]
    lam = (jnp.exp(jnp.sum(lm[0:1] * lm[1:2], axis=-1, keepdims=True))
           - jnp.exp(jnp.sum(lm[2:3] * lm[3:4], axis=-1, keepdims=True)) + lam_init)
    d = o[:tq] - lam * o[tq:]
    y = d * lax.rsqrt(jnp.mean(d * d, axis=-1, keepdims=True) + EPS)
    o_ref[0] = ((y * g_ref[...]) * (1.0 - lam_init)).astype(o_ref.dtype)


def _pick(n, options):
    for o in options:
        if n % o == 0:
            return o
    raise ValueError(f"no tile in {options} divides {n}")


def _diff_attn(q, k, v, lam_params, g, lam_init):
    b, nh, nq, _ = q.shape
    nk = k.shape[2]
    tq = _pick(nq, (512, 256))
    tk = _pick(nk, (2816, 1408, 768, 512, 256))
    kern = functools.partial(_diff_attn_kernel, tq=tq, tk=tk, lam_init=lam_init)
    return pl.pallas_call(
        kern,
        grid=(b, nh, nq // tq),
        in_specs=[
            pl.BlockSpec((4, DA_HEAD_DIM), lambda bi, h, i: (0, 0)),
            pl.BlockSpec((1, HEAD_PAIR), lambda bi, h, i: (0, 0)),
            pl.BlockSpec((1, 1, tq, HEAD_PAIR), lambda bi, h, i: (bi, h, i, 0)),
            pl.BlockSpec((1, 1, nk, HEAD_PAIR), lambda bi, h, i: (bi, h, 0, 0)),
            pl.BlockSpec((1, 1, nk, HEAD_PAIR), lambda bi, h, i: (bi, h, 0, 0)),
        ],
        out_specs=pl.BlockSpec((1, tq, HEAD_PAIR), lambda bi, h, i: (bi, i, h)),
        out_shape=jax.ShapeDtypeStruct((b, nq, nh * HEAD_PAIR), BF16),
        scratch_shapes=[
            pltpu.VMEM((2 * tq, HEAD_PAIR), BF16),
            pltpu.VMEM((2 * tq, LANES), F32),
            pltpu.VMEM((2 * tq, LANES), F32),
            pltpu.VMEM((2 * tq, HEAD_PAIR), F32),
        ],
        compiler_params=_params(("parallel", "parallel", "parallel")),
        name="diff_attn",
    )(lam_params, g, q, k, v)


NA_QROWS = 8
NA_QTOK = NA_QROWS * GRID_W
NA_KBLK = 4 * GRID_W
NA_PIECES = 4
NA_WIN = NA_PIECES * NA_KBLK


def _na_bias_tables(rpb, rows):
    n_h = rpb.shape[0]
    n_kj = NA_PIECES * 4
    pad_r = n_kj - NA_ROWS
    pad_c = GRID_W - NA_COLS
    rp = jnp.pad(rpb * math.log2(math.e), ((0, 0), (pad_r, pad_r), (pad_c, pad_c)))
    a = jnp.stack([rp[:, :, GRID_W - 1 - qc:2 * GRID_W - 1 - qc] for qc in range(GRID_W)], axis=2)
    bias = jnp.stack([jnp.concatenate([a[:, kj - qi + 3 + pad_r] for kj in range(n_kj)], axis=-1)
                      for qi in range(NA_QROWS)], axis=1).reshape(n_h, NA_QTOK, NA_WIN)

    n_r = rows // NA_QROWS
    qi = np.arange(NA_QROWS)
    kj = np.arange(n_kj)
    c = np.arange(GRID_W)
    cstart = np.clip(c - NA_COLS // 2, 0, GRID_W - NA_COLS)
    col_ok = (c[None, :] >= cstart[:, None]) & (c[None, :] < cstart[:, None] + NA_COLS)
    row_ok = []
    for r_grp in (0, min(1, n_r - 1), n_r - 1):
        r = NA_QROWS * r_grp + qi
        rs = np.clip(r - NA_ROWS // 2, 0, rows - NA_ROWS)
        krow = NA_QROWS * r_grp - 4 + kj
        row_ok.append((krow[None, :] >= rs[:, None]) & (krow[None, :] < rs[:, None] + NA_ROWS)
                      & (krow[None, :] >= 0) & (krow[None, :] < rows))
    ok = jnp.asarray(np.stack(row_ok))[:, :, None, :, None] & jnp.asarray(col_ok)[None, None, :, None, :]
    mask = jnp.where(ok, 0.0, NEG_INF).astype(F32).reshape(3, NA_QTOK, NA_WIN)
    return bias, mask


def _pair_softmax_pv(qs, k_list, v_list, bias_list):
    n_rows = qs.shape[0]
    chunks = [slice(r0, r0 + DOT_ROWS) for r0 in range(0, n_rows, DOT_ROWS)]
    tiles_list = []
    for k, bias in zip(k_list, bias_list):
        s = jnp.concatenate([lax.dot_general(qs[c], k, _CONTRACT_LAST, preferred_element_type=F32)
                             for c in chunks], axis=0)
        tiles = [s[:, j * LANES:(j + 1) * LANES] for j in range(k.shape[0] // LANES)]
        tiles_list.append(tiles if bias is None else [t + bias(j) for j, t in enumerate(tiles)])
    all_tiles = [t for tiles in tiles_list for t in tiles]
    m = jnp.max(functools.reduce(jnp.maximum, all_tiles), axis=-1, keepdims=True)
    p_list = [[jnp.exp2(t - m) for t in tiles] for tiles in tiles_list]
    l = jnp.sum(functools.reduce(jnp.add, [t for tiles in p_list for t in tiles]), axis=-1, keepdims=True)
    o = None
    for tiles, v in zip(p_list, v_list):
        p = jnp.concatenate(tiles, axis=1).astype(BF16)
        pv = jnp.concatenate([jnp.dot(p[c], v, preferred_element_type=F32) for c in chunks], axis=0)
        o = pv if o is None else o + pv
    o = o / l
    t = qs.shape[0] // 2
    lo = lax.broadcasted_iota(jnp.int32, (t, HEAD_PAIR), 1) < (HEAD_PAIR // 2)
    return jnp.where(lo, o[:t], o[t:])


def _scaled_pair(q, head_dim):
    return _split_pair((q.astype(F32) * (head_dim ** -0.5 * math.log2(math.e))).astype(BF16))


def _na_kernel(q_ref, k0, k1, k2, k3, v0, v1, v2, v3, kc_ref, vc_ref, b_ref, mk_ref, o_ref):
    qs = _scaled_pair(q_ref[0], HEAD_PAIR // 2)
    k_win = jnp.concatenate([k0[0], k1[0], k2[0], k3[0]], axis=0)
    v_win = jnp.concatenate([v0[0], v1[0], v2[0], v3[0]], axis=0)
    def bias(j):
        lanes = slice(j * LANES, (j + 1) * LANES)
        window = mk_ref[0, :, lanes]
        return jnp.concatenate([b_ref[0, :, lanes] + window, b_ref[1, :, lanes] + window], axis=0)

    o = _pair_softmax_pv(qs, [k_win, kc_ref[0]], [v_win, vc_ref[0]], [bias, None])
    o_ref[0] = o.astype(o_ref.dtype)


def _na_attn(p, pc, bias, mask):
    b, n, _ = p.shape
    n_ctx = pc.shape[1]
    n_r = n // NA_QTOK
    n_kb = n // NA_KBLK
    n_hp = NA_HEADS // 2

    def case(r):
        return jnp.where(r == 0, 0, jnp.where(r == n_r - 1, 2, 1))

    def kv_spec(off, piece):
        return pl.BlockSpec(
            (1, NA_KBLK, HEAD_PAIR),
            lambda bi, hp, r: (bi, jnp.clip(2 * r - 1 + piece, 0, n_kb - 1), off // HEAD_PAIR + hp))

    in_specs = [pl.BlockSpec((1, NA_QTOK, HEAD_PAIR), lambda bi, hp, r: (bi, r, OFF_QC // HEAD_PAIR + hp))]
    in_specs += [kv_spec(OFF_KC, i) for i in range(NA_PIECES)]
    in_specs += [kv_spec(OFF_VC, i) for i in range(NA_PIECES)]
    in_specs += [
        pl.BlockSpec((1, n_ctx, HEAD_PAIR), lambda bi, hp, r: (bi, 0, OFF_KC // HEAD_PAIR + hp)),
        pl.BlockSpec((1, n_ctx, HEAD_PAIR), lambda bi, hp, r: (bi, 0, OFF_VC // HEAD_PAIR + hp)),
        pl.BlockSpec((2, NA_QTOK, NA_WIN), lambda bi, hp, r: (hp, 0, 0)),
        pl.BlockSpec((1, NA_QTOK, NA_WIN), lambda bi, hp, r: (case(r), 0, 0)),
    ]
    return pl.pallas_call(
        _na_kernel,
        grid=(b, n_hp, n_r),
        in_specs=in_specs,
        out_specs=pl.BlockSpec((1, NA_QTOK, HEAD_PAIR), lambda bi, hp, r: (bi, r, hp)),
        out_shape=jax.ShapeDtypeStruct((b, n, NA_HEADS * HEAD_PAIR // 2), BF16),
        compiler_params=_params(("parallel", "parallel", "arbitrary")),
        name="na_attn",
    )(p, *([p] * (2 * NA_PIECES)), pc, pc, bias, mask)


def _ctx_mha_kernel(q_ref, k_ref, v_ref, o_ref):
    qs = _scaled_pair(q_ref[0], HEAD_PAIR // 2)
    o_ref[0] = _pair_softmax_pv(qs, [k_ref[0]], [v_ref[0]], [None]).astype(o_ref.dtype)


def _ctx_mha(pc):
    b, n_ctx, _ = pc.shape
    n_hp = NA_HEADS // 2

    def spec(off):
        return pl.BlockSpec((1, n_ctx, HEAD_PAIR), lambda bi, hp: (bi, 0, off // HEAD_PAIR + hp))

    return pl.pallas_call(
        _ctx_mha_kernel,
        grid=(b, n_hp),
        in_specs=[spec(OFF_QC), spec(OFF_KC), spec(OFF_VC)],
        out_specs=pl.BlockSpec((1, n_ctx, HEAD_PAIR), lambda bi, hp: (bi, 0, hp)),
        out_shape=jax.ShapeDtypeStruct((b, n_ctx, NA_HEADS * HEAD_PAIR // 2), BF16),
        compiler_params=_params(("parallel", "parallel")),
        name="ctx_mha",
    )(pc, pc, pc)


def _gelu_tanh(x):
    return 0.5 * x * (1.0 + jnp.tanh(math.sqrt(2.0 / math.pi) * (x + 0.044715 * (x * x * x))))


SUBLANES = 8


def _store_token_tiles(ref, x):
    t = x.shape[0]
    for j in range(SUBLANES):
        ref[pl.ds(j, t, stride=SUBLANES), :] = x[:, j * LANES:(j + 1) * LANES]


def _load_token_tiles(ref, t):
    return jnp.concatenate([ref[pl.ds(j, t, stride=SUBLANES), :] for j in range(SUBLANES)], axis=1)


def _merge_kernel(h_ref, ya_ref, z_ref, yc_ref, g0_ref, g1_ref, g2_ref, wb_ref, wo_ref, sgw_ref, sgb_ref,
                  lng_ref, lnb_ref, gt_ref, gf_ref, sh_ref, sc_ref, ho_ref, xo_ref, *, tm):
    z = _gelu_tanh(z_ref[0].astype(F32))
    u = z[:, :SG_WIDTH]
    vv = z[:, SG_WIDTH:]
    mu = jnp.mean(vv, axis=-1, keepdims=True)
    var = jnp.mean(jnp.square(vv - mu), axis=-1, keepdims=True)
    vv = ((vv - mu) * lax.rsqrt(var + EPS)) * lng_ref[...] + lnb_ref[...]
    vv = vv.astype(BF16)
    gd = SG_WIDTH // SG_GROUPS
    chunks = []
    for c in range(tm // SG_CHUNK):
        rows = slice(c * SG_CHUNK, (c + 1) * SG_CHUNK)
        groups = []
        for g in range(SG_GROUPS):
            s = jnp.dot(sgw_ref[g], vv[rows, g * gd:(g + 1) * gd], preferred_element_type=F32) + sgb_ref[g]
            groups.append(s)
        chunks.append(jnp.concatenate(groups, axis=1))
    y_b = (u * jnp.concatenate(chunks, axis=0)).astype(BF16)

    merged = None
    for y, gate_ref, i in ((ya_ref[0], g0_ref, 0), (y_b, g1_ref, 1), (yc_ref[0], g2_ref, 2)):
        t = jax.nn.sigmoid(gate_ref[0].astype(F32)) * _dot_rows(y, wb_ref[i])
        merged = t if merged is None else merged + t
    out = _dot_rows(merged.astype(BF16), wo_ref[...])
    h_new = h_ref[0] + gt_ref[0] * out
    ho_ref[0] = h_new
    _store_token_tiles(xo_ref, _rms_mod(h_new, gf_ref[...], sh_ref[0], sc_ref[0]))


def _merge(h, y_a, p, y_c, wb, wo, sgw, sgb, lng, lnb, gt1, g_ffn, sh2, sc2, xn2_buf, xn2_rows, row_off, tm):
    b, n, d = h.shape
    n_i = n // tm
    off_blk = row_off // tm
    vec = pl.BlockSpec((1, 1, d), lambda bi, i: (bi, 0, 0))

    def col(width, off):
        return pl.BlockSpec((1, tm, width), lambda bi, i: (bi, i, off // width))

    kern = functools.partial(_merge_kernel, tm=tm)
    args = [h, y_a, p, y_c, p, p, p, wb, wo, sgw, sgb, lng, lnb, gt1, g_ffn, sh2, sc2]
    in_specs = [
        pl.BlockSpec((1, tm, d), lambda bi, i: (bi, i, 0)),
        col(BRANCH_WIDTH, 0),
        col(2 * SG_WIDTH, OFF_ZB),
        col(BRANCH_WIDTH, 0),
        col(d, OFF_GATE), col(d, OFF_GATE + d), col(d, OFF_GATE + 2 * d),
        pl.BlockSpec((N_BRANCH, BRANCH_WIDTH, d), lambda bi, i: (0, 0, 0)),
        pl.BlockSpec((d, d), lambda bi, i: (0, 0)),
        pl.BlockSpec((SG_GROUPS, SG_CHUNK, SG_CHUNK), lambda bi, i: (0, 0, 0)),
        pl.BlockSpec((SG_GROUPS, SG_CHUNK, SG_CHUNK), lambda bi, i: (0, 0, 0)),
        pl.BlockSpec((1, SG_WIDTH), lambda bi, i: (0, 0)),
        pl.BlockSpec((1, SG_WIDTH), lambda bi, i: (0, 0)),
        vec,
        pl.BlockSpec((1, d), lambda bi, i: (0, 0)),
        vec, vec,
    ]
    aliases = {}
    if xn2_buf is not None:
        args.append(xn2_buf)
        in_specs.append(pl.BlockSpec(memory_space=pl.ANY))
        aliases = {len(args) - 1: 1}
    assert d == SUBLANES * LANES
    xn2_shape = jax.ShapeDtypeStruct((xn2_rows * SUBLANES, LANES), F32)

    def body(*refs):
        n_in = 17
        kern(*refs[:n_in], *refs[len(refs) - 2:])

    return pl.pallas_call(
        body,
        grid=(b, n_i),
        in_specs=in_specs,
        out_specs=[
            pl.BlockSpec((1, tm, d), lambda bi, i: (bi, i, 0)),
            pl.BlockSpec((tm * SUBLANES, LANES), lambda bi, i: (off_blk + bi * n_i + i, 0)),
        ],
        out_shape=[jax.ShapeDtypeStruct((b, n, d), F32), xn2_shape],
        input_output_aliases=aliases,
        compiler_params=_params(("parallel", "parallel")),
        name="merge_branches",
    )(*args)


R_E1, R_E2, R_W1, R_W2, R_RANK1, R_RANK2 = range(6)


def _router_kernel(x_ref, w_ref, b_ref, o_ref, cnt_ref, run_ref, *, tm):
    @pl.when(pl.program_id(0) == 0)
    def _():
        run_ref[...] = jnp.zeros(run_ref.shape, F32)

    x = _load_token_tiles(x_ref, tm)
    x_hi = x.astype(BF16)
    x_lo = (x - x_hi.astype(F32)).astype(BF16)
    w_hi, w_lo = w_ref[0], w_ref[1]
    logits = (jnp.dot(x_hi, w_hi, preferred_element_type=F32) + jnp.dot(x_lo, w_hi, preferred_element_type=F32)
              + jnp.dot(x_hi, w_lo, preferred_element_type=F32)) + b_ref[...]
    lane = lax.broadcasted_iota(jnp.int32, logits.shape, 1)
    lane_f = lane.astype(F32)
    far = jnp.float32(1e9)

    def first_lane(mask):
        return jnp.min(jnp.where(mask, lane_f, far), axis=-1, keepdims=True)

    is_g = lane < N_GROUPS
    gl = jnp.where(is_g, logits, NEG_INF)
    g_max = jnp.max(gl, axis=-1, keepdims=True)
    g_idx = first_lane(is_g & (gl == g_max))
    g_sum = jnp.sum(jnp.where(is_g, jnp.exp(gl - g_max), 0.0), axis=-1, keepdims=True)
    g_w = 1.0 / g_sum

    e_lane = lane - N_GROUPS
    in_grp = (e_lane >= 0) & (e_lane < N_EXPERTS) & \
        ((e_lane // EXPERTS_PER_GROUP).astype(F32) == g_idx)
    el = jnp.where(in_grp, logits, NEG_INF)
    l1 = jnp.max(el, axis=-1, keepdims=True)
    i1 = first_lane(in_grp & (el == l1))
    rest = in_grp & (lane_f != i1)
    el2 = jnp.where(rest, logits, NEG_INF)
    l2 = jnp.max(el2, axis=-1, keepdims=True)
    i2 = first_lane(rest & (el2 == l2))
    t = jnp.exp(l2 - l1)
    w1 = g_w / (1.0 + t)
    w2 = g_w * t / (1.0 + t)

    oh1 = lane_f == i1
    oh2 = lane_f == i2
    oh = jnp.where(oh1 | oh2, 1.0, 0.0)
    row = lax.broadcasted_iota(jnp.int32, (tm, tm), 0)
    colm = lax.broadcasted_iota(jnp.int32, (tm, tm), 1)
    before = jnp.where(colm < row, 1.0, 0.0).astype(BF16)
    prior = jnp.dot(before, oh.astype(BF16), preferred_element_type=F32) + run_ref[...]
    rank1 = jnp.sum(jnp.where(oh1, prior, 0.0), axis=-1, keepdims=True)
    rank2 = jnp.sum(jnp.where(oh2, prior, 0.0), axis=-1, keepdims=True)
    run_new = run_ref[...] + jnp.sum(oh, axis=0, keepdims=True)
    run_ref[...] = run_new
    cnt_ref[...] = run_new

    slab = jnp.zeros(logits.shape, F32)
    for ln, val in ((R_E1, i1 - N_GROUPS), (R_E2, i2 - N_GROUPS), (R_W1, w1), (R_W2, w2),
                    (R_RANK1, rank1), (R_RANK2, rank2)):
        slab = jnp.where(lane == ln, val, slab)
    o_ref[...] = slab


def _router(xn2, w_rt, b_rt):
    t_tok = xn2.shape[0] // SUBLANES
    d = w_rt.shape[0]
    tm = 512
    w_hi = w_rt.astype(BF16)
    w_split = jnp.stack([w_hi, (w_rt - w_hi.astype(F32)).astype(BF16)])
    kern = functools.partial(_router_kernel, tm=tm)
    return pl.pallas_call(
        kern,
        grid=(t_tok // tm,),
        in_specs=[
            pl.BlockSpec((tm * SUBLANES, LANES), lambda i: (i, 0)),
            pl.BlockSpec((2, d, LANES), lambda i: (0, 0, 0)),
            pl.BlockSpec((1, LANES), lambda i: (0, 0)),
        ],
        out_specs=[pl.BlockSpec((tm, LANES), lambda i: (i, 0)), pl.BlockSpec((1, LANES), lambda i: (0, 0))],
        out_shape=[jax.ShapeDtypeStruct((t_tok, LANES), F32), jax.ShapeDtypeStruct((1, LANES), F32)],
        scratch_shapes=[pltpu.VMEM((1, LANES), F32)],
        compiler_params=_params(("arbitrary",)),
        name="moe_router",
    )(xn2, w_split, b_rt)


def _tile_rows(index):
    return pl.ds(pl.multiple_of(index * SUBLANES, SUBLANES), SUBLANES)


def _dispatch_kernel(dest_ref, start_ref, end_ref, x_ref, xs_ref, zero_ref, sem, zsem, *, n_blocks):
    step = pl.program_id(0)
    blk_rows = MOE_ROWS * SUBLANES

    @pl.when(step == 0)
    def _():
        zero_ref[...] = jnp.zeros(zero_ref.shape, F32)

        def fill(blk):
            return pltpu.make_async_copy(zero_ref, xs_ref.at[pl.ds(pl.multiple_of(blk * blk_rows, blk_rows),
                                                                  blk_rows), :], zsem.at[0])

        def expert_tail(op):
            def body(e, carry):
                @pl.when(end_ref[e] > start_ref[e])
                def _():
                    op(fill(end_ref[e] // MOE_ROWS - 1))
                return carry
            lax.fori_loop(0, N_EXPERTS, body, 0)

        def unused(op):
            def body(blk, carry):
                @pl.when(blk * MOE_ROWS >= end_ref[N_EXPERTS - 1])
                def _():
                    op(fill(blk))
                return carry
            lax.fori_loop(0, n_blocks, body, 0)

        for phase in (lambda c: c.start(), lambda c: c.wait()):
            expert_tail(phase)
            unused(phase)

    def issue(i, carry):
        tok = step * DMA_CHUNK + i
        for k in range(2):
            pltpu.make_async_copy(x_ref.at[_tile_rows(i), :], xs_ref.at[_tile_rows(dest_ref[2 * tok + k]), :],
                                  sem.at[0]).start(priority=k)
        return carry

    lax.fori_loop(0, DMA_CHUNK, issue, 0, unroll=4)
    for _ in range(2):
        pltpu.make_async_copy(x_ref, xs_ref.at[pl.ds(0, DMA_CHUNK * SUBLANES), :], sem.at[0]).wait()


def _dispatch(dest, pad_start, pad_end, xn2, n_slots):
    t_tok = xn2.shape[0] // SUBLANES
    kern = functools.partial(_dispatch_kernel, n_blocks=n_slots // MOE_ROWS)
    grid_spec = pltpu.PrefetchScalarGridSpec(
        num_scalar_prefetch=3,
        grid=(t_tok // DMA_CHUNK,),
        in_specs=[pl.BlockSpec((DMA_CHUNK * SUBLANES, LANES), lambda i, *_: (i, 0))],
        out_specs=pl.BlockSpec(memory_space=pl.ANY),
        scratch_shapes=[pltpu.VMEM((MOE_ROWS * SUBLANES, LANES), F32), pltpu.SemaphoreType.DMA((1,)),
                        pltpu.SemaphoreType.DMA((1,))],
    )
    return pl.pallas_call(
        kern,
        grid_spec=grid_spec,
        out_shape=jax.ShapeDtypeStruct((n_slots * SUBLANES, LANES), F32),
        compiler_params=_params(("arbitrary",)),
        name="moe_dispatch",
    )(dest, pad_start, pad_end, xn2)


def _expert_kernel(be_ref, nxt_ref, nu_ref, x_ref, wg_hbm, wu_hbm, wd_hbm, o_ref,
                   wg_f, wu_f, wd_f, wg_s, wu_s, wd_s, sem, turn_ref, *, w_off):
    i = pl.program_id(0)
    used = i < nu_ref[0]

    def fetch(expert, slot):
        return [pltpu.make_async_copy(src.at[w_off + expert], dst.at[slot], sem.at[slot])
                for src, dst in ((wg_hbm, wg_f), (wu_hbm, wu_f), (wd_hbm, wd_f))]

    @pl.when(i == 0)
    def _():
        turn_ref[0] = 0
        for c in fetch(be_ref[0], 0):
            c.start(priority=1)

    @pl.when(used & ((i == 0) | (be_ref[i] != be_ref[jnp.maximum(i - 1, 0)])))
    def _():
        slot = turn_ref[0] % 2
        for c in fetch(be_ref[i], slot):
            c.wait()
        nxt = nxt_ref[be_ref[i]]

        @pl.when(nxt >= 0)
        def _():
            for c in fetch(nxt, 1 - slot):
                c.start(priority=1)

        wg_s[...] = wg_f[slot].astype(BF16)
        wu_s[...] = wu_f[slot].astype(BF16)
        wd_s[...] = wd_f[slot].astype(BF16)
        turn_ref[0] = turn_ref[0] + 1

    @pl.when(used)
    def _():
        x = _load_token_tiles(x_ref, MOE_ROWS).astype(BF16)
        gate = jnp.dot(x, wg_s[...], preferred_element_type=F32)
        up = jnp.dot(x, wu_s[...], preferred_element_type=F32)
        hdn = (gate * jax.nn.sigmoid(gate)) * up
        _store_token_tiles(o_ref, jnp.dot(hdn.astype(BF16), wd_s[...], preferred_element_type=F32))

    @pl.when(i >= nu_ref[0])
    def _():
        o_ref[...] = jnp.zeros(o_ref.shape, F32)


def _experts(block_expert, next_expert, n_used, xs, layer, w_gate, w_up, w_down):
    n_blocks = xs.shape[0] // (MOE_ROWS * SUBLANES)
    n_layers, n_e, d, de = w_gate.shape
    w_gate, w_up, w_down = (w.reshape(n_layers * n_e, *w.shape[2:]) for w in (w_gate, w_up, w_down))
    blk = pl.BlockSpec((MOE_ROWS * SUBLANES, LANES), lambda i, *_: (i, 0))
    hbm = pl.BlockSpec(memory_space=pl.ANY)
    grid_spec = pltpu.PrefetchScalarGridSpec(
        num_scalar_prefetch=3,
        grid=(n_blocks,),
        in_specs=[blk, hbm, hbm, hbm],
        out_specs=blk,
        scratch_shapes=[
            pltpu.VMEM((2, d, de), F32), pltpu.VMEM((2, d, de), F32), pltpu.VMEM((2, de, d), F32),
            pltpu.VMEM((d, de), BF16), pltpu.VMEM((d, de), BF16), pltpu.VMEM((de, d), BF16),
            pltpu.SemaphoreType.DMA((2,)), pltpu.SMEM((1,), jnp.int32),
        ],
    )
    return pl.pallas_call(
        functools.partial(_expert_kernel, w_off=layer * n_e),
        grid_spec=grid_spec,
        out_shape=jax.ShapeDtypeStruct(xs.shape, F32),
        compiler_params=_params(("arbitrary",)),
        name="moe_experts",
    )(block_expert, next_expert, n_used, xs, w_gate, w_up, w_down)


def _residual_kernel(dest_ref, h_ref, r_ref, gt_ref, gf_ref, ys_ref, o_ref, buf_ref, sem,
                     *, tm, n_i, row_off, final):
    step = pl.program_id(0) * n_i + pl.program_id(1)
    n_steps = pl.num_programs(0) * n_i
    slot = step % 2

    def gather(tile, into):
        base = row_off + tile * tm

        def issue(t, carry):
            for k in range(2):
                pltpu.make_async_copy(ys_ref.at[_tile_rows(dest_ref[2 * (base + t) + k]), :],
                                      buf_ref.at[into, k, _tile_rows(t), :], sem.at[into]).start(priority=k)
            return carry

        lax.fori_loop(0, tm, issue, 0, unroll=4)

    @pl.when(step == 0)
    def _():
        gather(step, slot)

    @pl.when(step + 1 < n_steps)
    def _():
        gather(step + 1, 1 - slot)

    for k in range(2):
        pltpu.make_async_copy(ys_ref.at[pl.ds(0, tm * SUBLANES), :], buf_ref.at[slot, k], sem.at[slot]).wait()

    r = r_ref[...]
    y = (_load_token_tiles(buf_ref.at[slot, 0], tm) * r[:, R_W1:R_W1 + 1]
         + _load_token_tiles(buf_ref.at[slot, 1], tm) * r[:, R_W2:R_W2 + 1])
    h_new = h_ref[0] + gt_ref[0] * y
    if final:
        h_new = (h_new * lax.rsqrt(jnp.mean(h_new * h_new, axis=-1, keepdims=True) + EPS)) * gf_ref[...]
    o_ref[0] = h_new


def _residual(dest, h, ys, route, gt2, g_final, row_off, final):
    b, n, d = h.shape
    tm = 256
    n_i = n // tm
    off_blk = row_off // tm
    kern = functools.partial(_residual_kernel, tm=tm, n_i=n_i, row_off=row_off, final=final)
    grid_spec = pltpu.PrefetchScalarGridSpec(
        num_scalar_prefetch=1,
        grid=(b, n_i),
        in_specs=[
            pl.BlockSpec((1, tm, d), lambda bi, i, *_: (bi, i, 0)),
            pl.BlockSpec((tm, LANES), lambda bi, i, *_: (off_blk + bi * n_i + i, 0)),
            pl.BlockSpec((1, 1, d), lambda bi, i, *_: (bi, 0, 0)),
            pl.BlockSpec((1, d), lambda bi, i, *_: (0, 0)),
            pl.BlockSpec(memory_space=pl.ANY),
        ],
        out_specs=pl.BlockSpec((1, tm, d), lambda bi, i, *_: (bi, i, 0)),
        scratch_shapes=[pltpu.VMEM((2, 2, tm * SUBLANES, LANES), F32), pltpu.SemaphoreType.DMA((2,))],
    )
    return pl.pallas_call(
        kern,
        grid_spec=grid_spec,
        out_shape=jax.ShapeDtypeStruct((b, n, d), F32),
        compiler_params=_params(("arbitrary", "arbitrary")),
        name="moe_residual",
    )(dest, h, route, gt2, g_final, ys)


def _moe(xn2, w_rt, b_rt, layer, w_gate, w_up, w_down):
    t_tok = xn2.shape[0] // SUBLANES
    route, counts = _router(xn2, w_rt, b_rt)
    cnt = counts[0, N_GROUPS:N_GROUPS + N_EXPERTS].astype(jnp.int32)
    padded = (cnt + MOE_ROWS - 1) // MOE_ROWS * MOE_ROWS
    pad_end = jnp.cumsum(padded)
    pad_start = pad_end - padded
    pad_start = pad_start.astype(jnp.int32)
    ids = route[:, R_E1:R_RANK2 + 1].astype(jnp.int32)
    expert = ids[:, R_E1:R_E2 + 1].reshape(2 * t_tok)
    rank = ids[:, R_RANK1:R_RANK2 + 1].reshape(2 * t_tok)
    e_ids = jnp.arange(N_EXPERTS, dtype=jnp.int32)
    dest = rank + jnp.sum(jnp.where(expert[:, None] == e_ids[None, :], pad_start[None, :], 0), axis=1)
    n_blocks = -(-(2 * t_tok + N_EXPERTS * (MOE_ROWS - 1)) // MOE_ROWS)
    n_slots = n_blocks * MOE_ROWS
    starts = jnp.arange(n_blocks, dtype=jnp.int32) * MOE_ROWS
    block_expert = jnp.minimum(jnp.sum(starts[:, None] >= pad_end[None, :], axis=1), N_EXPERTS - 1).astype(jnp.int32)
    n_used = (pad_end[-1:] // MOE_ROWS).astype(jnp.int32)

    later = (padded > 0)[None, :] & (e_ids[None, :] > e_ids[:, None])
    next_expert = jnp.min(jnp.where(later, e_ids[None, :], N_EXPERTS), axis=1)
    next_expert = jnp.where(next_expert == N_EXPERTS, -1, next_expert).astype(jnp.int32)
    xs = _dispatch(dest, pad_start, pad_end.astype(jnp.int32), xn2, n_slots)
    return route, dest, _experts(block_expert, next_expert, n_used, xs, layer, w_gate, w_up, w_down)


def kernel(x, c, ctx, c_ctx, w_ada, b_ada, g_norm_mix, g_norm_ffn, w_in, da_lambda, da_subln_g, sg_ln_g, sg_ln_b, sg_w, sg_b, na_rpb, w_branch, w_out, moe_w_group, moe_b_group, moe_w_router, moe_b_router, moe_w_gate, moe_w_up, moe_w_down, g_final):
    b, n_lat, d = x.shape
    n_ctx = ctx.shape[1]
    depth = w_in.shape[0]
    rows = n_lat // GRID_W
    assert d == D_MODEL and n_lat % NA_QTOK == 0 and rows >= 2 * NA_QROWS and n_ctx % 256 == 0 and b <= 7
    tm_lat = 1024 if n_lat % 1024 == 0 else 512
    tm_mrg = 512
    tm_ctx = 256

    cos, sin = _rope_tables(n_lat)
    cond = jnp.zeros((8, d), F32).at[:b].set(c).at[b].set(c_ctx)
    mods = _ada(cond, w_ada, b_ada.reshape(depth, 1, 6 * d))

    h, hc = x, ctx
    for l in range(depth):
        last = l == depth - 1
        lam_init = 0.8 - 0.6 * math.exp(-0.3 * l)
        m_lat = mods[l, :b].reshape(b, 1, 6, d)
        m_ctx = jnp.broadcast_to(mods[l, b].reshape(1, 1, 6, d), (b, 1, 6, d))
        sh1, sc1, gt1, sh2, sc2, gt2 = (m_lat[:, :, i] for i in range(6))
        csh1, csc1, cgt1, csh2, csc2, cgt2 = (m_ctx[:, :, i] for i in range(6))

        g_mix = g_norm_mix[l].reshape(1, d)
        g_ffn = g_norm_ffn[l].reshape(1, d)
        kv_zero = jnp.zeros((b, DA_HEADS, n_lat + n_ctx, HEAD_PAIR), BF16)
        p, q_hm, k_all, v_all = _norm_proj(h, g_mix, sh1, sc1, w_in, l, IN_COLS, tm_lat, rope_tables=(cos, sin),
                                           kv_all=(kv_zero, kv_zero))
        pc, *qc_hm, k_all, v_all = _norm_proj(hc, g_mix, csh1, csc1, w_in, l, KV_COLS if last else IN_COLS, tm_ctx,
                                              kv_all=(k_all, v_all), key_off=n_lat)

        g_sub = da_subln_g[l].reshape(1, 2 * DA_HEAD_DIM)
        y_a = _diff_attn(q_hm, k_all, v_all, da_lambda[l], g_sub, lam_init)

        y_c = _na_attn(p, pc, *_na_bias_tables(na_rpb[l], rows))

        wb = w_branch[l].astype(BF16)
        wo = w_out[l].astype(BF16)
        sgw = sg_w[l].astype(BF16)
        sgb = jnp.broadcast_to(sg_b[l][:, :, None], (SG_GROUPS, SG_CHUNK, SG_CHUNK))
        lng = sg_ln_g[l].reshape(1, SG_WIDTH)
        lnb = sg_ln_b[l].reshape(1, SG_WIDTH)
        t_lat = b * n_lat
        t_tok = t_lat if last else t_lat + b * n_ctx
        xn2_buf = None if last else jnp.zeros((t_tok * SUBLANES, LANES), F32)
        h, xn2 = _merge(h, y_a, p, y_c, wb, wo, sgw, sgb, lng, lnb, gt1, g_ffn, sh2, sc2, xn2_buf, t_tok, 0, tm_mrg)
        if not last:
            ya_c = _diff_attn(qc_hm[0], k_all[:, :, n_lat:], v_all[:, :, n_lat:], da_lambda[l], g_sub, lam_init)
            yc_c = _ctx_mha(pc)
            hc, xn2 = _merge(hc, ya_c, pc, yc_c, wb, wo, sgw, sgb, lng, lnb, cgt1, g_ffn, csh2, csc2,
                             xn2, t_tok, t_lat, tm_ctx)

        w_rt = jnp.zeros((d, LANES), F32).at[:, :N_GROUPS].set(moe_w_group[l]) \
            .at[:, N_GROUPS:N_GROUPS + N_EXPERTS].set(moe_w_router[l])
        b_rt = jnp.zeros((1, LANES), F32).at[0, :N_GROUPS].set(moe_b_group[l]) \
            .at[0, N_GROUPS:N_GROUPS + N_EXPERTS].set(moe_b_router[l])
        route, dest, ys = _moe(xn2, w_rt, b_rt, l, moe_w_gate, moe_w_up, moe_w_down)
        h = _residual(dest, h, ys, route, gt2, g_final.reshape(1, d), 0, last)
        if not last:
            hc = _residual(dest, hc, ys, route, cgt2, g_final.reshape(1, d), t_lat, False)
    return h
```
